```python
import jax, jax.numpy as jnp
from jax import lax
import numpy as np

D_MODEL = 1024
BATCH = 2
SEQ = 8192
DEPTH = 2

GRID_W = 64
CTX_LEN = 256
HEAD_DIM = 64
CONV_W = 256
CONV_GROUPS = 4
CONV_K = 3
RET_HEADS = 6
RET_W = RET_HEADS * HEAD_DIM
NA_HEADS = 6
NA_W = NA_HEADS * HEAD_DIM
MIX_W = CONV_W + RET_W + NA_W
RET_CHUNK = 128
WIN_H = 8
WIN_W = 16
ROPE_BASE = 10000.0
LN_EPS = 1e-5
DEEPNORM_ALPHA = (2 * DEPTH) ** 0.25
DEEPNORM_BETA = (8 * DEPTH) ** -0.25

PROJ_SPLITS = (RET_W, RET_W, NA_W, NA_W,
               RET_W, RET_W, NA_W, NA_W,
               CONV_W, CONV_W, CONV_W, CONV_W)
PROJ_W = sum(PROJ_SPLITS)
KV_W = 2 * RET_W + 2 * NA_W

kernel_name = 'hybrid_conv_retention_natten_dit'

F32 = jnp.float32


def split_cols(p, sizes):
    idx = np.cumsum(sizes)[:-1].tolist()
    return jnp.split(p, idx, axis=-1)


def to_heads(t, n_heads):
    b, l, _ = t.shape
    return t.reshape(b, l, n_heads, HEAD_DIM).transpose(0, 2, 1, 3)


def from_heads(t):
    b, h, l, d = t.shape
    return t.transpose(0, 2, 1, 3).reshape(b, l, h * d)


def layer_norm(x, g, b):
    xf = x.astype(F32)
    mu = jnp.mean(xf, axis=-1, keepdims=True)
    var = jnp.mean(jnp.square(xf - mu), axis=-1, keepdims=True)
    return ((xf - mu) * lax.rsqrt(var + LN_EPS) * g + b).astype(x.dtype)


def head_norm(o):
    of = o.astype(F32)
    mu = jnp.mean(of, axis=-1, keepdims=True)
    var = jnp.mean(jnp.square(of - mu), axis=-1, keepdims=True)
    return (of - mu) * lax.rsqrt(var + LN_EPS)


def axial_rope_angles(n):
    t = jnp.arange(n)
    row = (t // GRID_W).astype(F32)
    col = (t % GRID_W).astype(F32)
    nf = HEAD_DIM // 4
    inv = ROPE_BASE ** (-jnp.arange(nf, dtype=F32) / nf)
    return row[:, None] * inv, col[:, None] * inv


def rotate(xp, ang):
    a, b = jnp.split(xp, 2, axis=-1)
    cos, sin = jnp.cos(ang), jnp.sin(ang)
    return jnp.concatenate([a * cos - b * sin, a * sin + b * cos], axis=-1)


def apply_axial_rope(x, ang_r, ang_c):
    xr, xc = jnp.split(x, 2, axis=-1)
    return jnp.concatenate([rotate(xr, ang_r), rotate(xc, ang_c)], axis=-1).astype(x.dtype)


def short_conv_branch(h, bg, cg, z, w, b):
    u = cg * h
    up = jnp.pad(u, ((0, 0), (1, 1), (0, 0)))
    conv = up[:, :-2] * w[0] + up[:, 1:-1] * w[1] + up[:, 2:] * w[2] + b
    return bg * conv * jax.nn.silu(z)


def ret_final_state(k, v, lg):
    l = k.shape[2]
    w = jnp.exp((l - 1 - jnp.arange(l, dtype=F32))[None, :] * lg[:, None])
    return jnp.einsum('bhld,hl,bhle->bhde', k, w, v)


def chunk_retention(q, k, v, lg, s0, inclusive):
    b, h, l, dk = q.shape
    dv = v.shape[-1]
    c = min(RET_CHUNK, l)
    nc = l // c
    qc = q.reshape(b, h, nc, c, dk)
    kc = k.reshape(b, h, nc, c, dk)
    vc = v.reshape(b, h, nc, c, dv)
    i = jnp.arange(c, dtype=F32)
    diff = i[:, None] - i[None, :]
    mask = diff >= 0 if inclusive else diff > 0
    dmat = jnp.where(mask[None], jnp.exp(jnp.where(mask, diff, 0.0)[None] * lg[:, None, None]), 0.0)
    scores = jnp.einsum('bhnid,bhnjd->bhnij', qc, kc) * dmat[None, :, None]
    o_inner = jnp.einsum('bhnij,bhnje->bhnie', scores, vc)
    wk = jnp.exp((c - 1 - i)[None, :] * lg[:, None])
    u = jnp.einsum('bhnjd,hj,bhnje->bhnde', kc, wk, vc)
    g_chunk = jnp.exp(c * lg)[:, None, None]

    def step(s, u_n):
        return g_chunk * s + u_n, s

    _, s_prev = lax.scan(step, s0, jnp.moveaxis(u, 2, 0))
    s_prev = jnp.moveaxis(s_prev, 0, 2)
    wq = jnp.exp((i + 1)[None, :] * lg[:, None])
    o_cross = jnp.einsum('bhnid,hi,bhnde->bhnie', qc, wq, s_prev)
    return (o_inner + o_cross).reshape(b, h, l, dv)


def bidir_retention(q, k, v, lg_f, lg_b, s_f, s_b):
    fwd = chunk_retention(q, k, v, lg_f, s_f, True)
    bwd = chunk_retention(jnp.flip(q, 2), jnp.flip(k, 2), jnp.flip(v, 2), lg_b, s_b, False)
    return fwd + jnp.flip(bwd, 2)


def retention_branch(q, k, v, g, kc, vc, qc, gc, decay_logit, need_ctx):
    n = q.shape[1]
    kscale = HEAD_DIM ** -0.5
    ang_r, ang_c = axial_rope_angles(n)
    qh = apply_axial_rope(to_heads(q, RET_HEADS), ang_r, ang_c)
    kh = apply_axial_rope(to_heads(k, RET_HEADS), ang_r, ang_c) * kscale
    vh = to_heads(v, RET_HEADS)
    kch = to_heads(kc, RET_HEADS) * kscale
    vch = to_heads(vc, RET_HEADS)
    lg = jax.nn.log_sigmoid(decay_logit.astype(F32))
    lg_f, lg_b = lg[0], lg[1]
    s_f = ret_final_state(kch, vch, lg_f)
    s_b = ret_final_state(jnp.flip(kch, 2), jnp.flip(vch, 2), lg_b)
    o = bidir_retention(qh, kh, vh, lg_f, lg_b, s_f, s_b)
    y = from_heads(head_norm(o)) * jax.nn.silu(g)
    if not need_ctx:
        return y, None
    qch = to_heads(qc, RET_HEADS)
    zero = jnp.zeros_like(s_f)
    oc = bidir_retention(qch, kch, vch, lg_f, lg_b, zero, zero)
    yc = from_heads(head_norm(oc)) * jax.nn.silu(gc)
    return y, yc


def neighbourhood_branch(q, k, v, g, kc, vc, qc, gc, rpb, need_ctx):
    b, n, _ = q.shape
    rows = n // GRID_W
    win_h = min(WIN_H, rows)
    n_loc = win_h * WIN_W
    scale = HEAD_DIM ** -0.5

    def grid(t):
        return to_heads(t, NA_HEADS).reshape(b, NA_HEADS, rows, GRID_W, HEAD_DIM)

    qg, kg, vg = grid(q), grid(k), grid(v)
    kch, vch = to_heads(kc, NA_HEADS), to_heads(vc, NA_HEADS)
    r = np.arange(rows)
    row_start = np.clip(r - win_h // 2, 0, rows - win_h)
    d_row = row_start[:, None] + np.arange(win_h)[None, :] - r[:, None]
    cpos = np.arange(GRID_W)
    col_idx = np.clip(cpos - WIN_W // 2, 0, GRID_W - WIN_W)[:, None] + np.arange(WIN_W)[None, :]
    d_col = col_idx - cpos[:, None]

    def row_block(args):
        q_r, rs, dr = args
        k_band = lax.dynamic_slice_in_dim(kg, rs, win_h, axis=2)
        v_band = lax.dynamic_slice_in_dim(vg, rs, win_h, axis=2)
        k_win = k_band[:, :, :, col_idx]
        v_win = v_band[:, :, :, col_idx]
        s_loc = jnp.einsum('bhcd,bhrckd->bhcrk', q_r, k_win).astype(F32) * scale
        bias = rpb[:, dr[:, None, None] + (WIN_H - 1), d_col[None] + (WIN_W - 1)]
        s_loc = s_loc + bias.transpose(0, 2, 1, 3)[None].astype(F32)
        s_ctx = jnp.einsum('bhcd,bhjd->bhcj', q_r, kch).astype(F32) * scale
        p = jax.nn.softmax(jnp.concatenate([s_loc.reshape(b, NA_HEADS, GRID_W, n_loc), s_ctx], axis=-1), axis=-1)
        p_loc = p[..., :n_loc].reshape(b, NA_HEADS, GRID_W, win_h, WIN_W).astype(v.dtype)
        p_ctx = p[..., n_loc:].astype(v.dtype)
        return (jnp.einsum('bhcrk,bhrckd->bhcd', p_loc, v_win)
                + jnp.einsum('bhcj,bhjd->bhcd', p_ctx, vch))

    o = lax.map(row_block, (jnp.moveaxis(qg, 2, 0),
                            jnp.asarray(row_start, jnp.int32),
                            jnp.asarray(d_row, jnp.int32)))
    y = o.transpose(1, 0, 3, 2, 4).reshape(b, n, NA_W) * jax.nn.silu(g)
    if not need_ctx:
        return y, None
    qch = to_heads(qc, NA_HEADS)
    s = jnp.einsum('bhid,bhjd->bhij', qch, kch).astype(F32) * scale
    oc = jnp.einsum('bhij,bhjd->bhid', jax.nn.softmax(s, axis=-1).astype(vc.dtype), vch)
    yc = from_heads(oc) * jax.nn.silu(gc)
    return y, yc


def hybrid_layer(x, xc, c_act, cc_act, w_mod, b_mod, w_in, conv_w, conv_b,
                 ret_decay, na_rpb, w_out, ln_g, ln_b, need_ctx):
    shift, scale, gate = jnp.split(c_act @ w_mod + b_mod, 3, axis=-1)
    h = x * (1 + scale[:, None]) + shift[:, None]
    rk, rv, nk, nv, rq, rg, nq, ng, ch, cb, ccg, cz = split_cols(h @ w_in, PROJ_SPLITS)
    n_mod = 3 if need_ctx else 2
    mod_c = jnp.split(cc_act @ w_mod[:, :n_mod * D_MODEL] + b_mod[:n_mod * D_MODEL], n_mod)
    hc = xc * (1 + mod_c[1]) + mod_c[0]
    if need_ctx:
        pc = split_cols(hc @ w_in, PROJ_SPLITS)
    else:
        pc = split_cols(hc @ w_in[:, :KV_W], PROJ_SPLITS[:4]) + [None] * 8
    rkc, rvc, nkc, nvc, rqc, rgc, nqc, ngc, chc, cbc, ccc, czc = pc

    y_conv = short_conv_branch(ch, cb, ccg, cz, conv_w, conv_b)
    y_ret, yc_ret = retention_branch(rq, rk, rv, rg, rkc, rvc, rqc, rgc, ret_decay, need_ctx)
    y_na, yc_na = neighbourhood_branch(nq, nk, nv, ng, nkc, nvc, nqc, ngc, na_rpb, need_ctx)
    y = jnp.concatenate([y_conv, y_ret, y_na], axis=-1) @ w_out
    x_new = layer_norm(DEEPNORM_ALPHA * x + gate[:, None] * y, ln_g, ln_b)
    if not need_ctx:
        return x_new, None
    yc_conv = short_conv_branch(chc, cbc, ccc, czc, conv_w, conv_b)
    yc = jnp.concatenate([yc_conv, yc_ret, yc_na], axis=-1) @ w_out
    xc_new = layer_norm(DEEPNORM_ALPHA * xc + mod_c[2] * yc, ln_g, ln_b)
    return x_new, xc_new


def setup_inputs(seed: int = 0) -> dict:
    key = jax.random.key(seed)
    ks = jax.random.split(key, 14)
    nrm = jax.random.normal
    x = nrm(ks[0], (BATCH, SEQ, D_MODEL), F32)
    c = nrm(ks[1], (BATCH, D_MODEL), F32)
    ctx = nrm(ks[2], (BATCH, CTX_LEN, D_MODEL), F32)
    c_ctx = nrm(ks[3], (D_MODEL,), F32)
    w_mod = nrm(ks[4], (DEPTH, D_MODEL, 3 * D_MODEL), F32) * (0.5 * D_MODEL ** -0.5)
    b_mod = 0.01 * nrm(ks[5], (DEPTH, 3 * D_MODEL), F32)
    w_in = nrm(ks[6], (DEPTH, D_MODEL, PROJ_W), F32) * (D_MODEL ** -0.5)
    conv_w = nrm(ks[7], (DEPTH, CONV_K, CONV_W), F32) * (CONV_K ** -0.5)
    conv_b = 0.01 * nrm(ks[8], (DEPTH, CONV_W), F32)
    gamma = 1.0 - 2.0 ** (-5.0 - np.arange(RET_HEADS))
    decay_init = np.log(gamma / (1.0 - gamma)).astype(np.float32)
    ret_decay = jnp.asarray(decay_init)[None, None, :] + 0.1 * nrm(ks[9], (DEPTH, 2, RET_HEADS), F32)
    na_rpb = 0.05 * nrm(ks[10], (DEPTH, NA_HEADS, 2 * WIN_H - 1, 2 * WIN_W - 1), F32)
    w_out = nrm(ks[11], (DEPTH, MIX_W, D_MODEL), F32) * (DEEPNORM_BETA * MIX_W ** -0.5)
    ln_g = 1.0 + 0.01 * nrm(ks[12], (DEPTH, D_MODEL), F32)
    ln_b = 0.01 * nrm(ks[13], (DEPTH, D_MODEL), F32)
    return {'x': x, 'c': c, 'ctx': ctx, 'c_ctx': c_ctx, 'w_mod': w_mod, 'b_mod': b_mod,
            'w_in': w_in, 'conv_w': conv_w, 'conv_b': conv_b, 'ret_decay': ret_decay,
            'na_rpb': na_rpb, 'w_out': w_out, 'ln_g': ln_g, 'ln_b': ln_b}


def reference(x, c, ctx, c_ctx, w_mod, b_mod, w_in, conv_w, conv_b, ret_decay, na_rpb,
              w_out, ln_g, ln_b):
    c_act = jax.nn.silu(c)
    cc_act = jax.nn.silu(c_ctx)
    xc = ctx
    for l in range(DEPTH):
        x, xc = hybrid_layer(x, xc, c_act, cc_act, w_mod[l], b_mod[l], w_in[l], conv_w[l], conv_b[l],
                             ret_decay[l], na_rpb[l], w_out[l], ln_g[l], ln_b[l],
                             need_ctx=(l < DEPTH - 1))
    return x
```

```python
import functools

import numpy as np
import jax
import jax.numpy as jnp
from jax import lax
from jax.experimental import pallas as pl
from jax.experimental.pallas import tpu as pltpu

D_MODEL = 1024
DEPTH = 2
GRID_W = 64
HEAD_DIM = 64
CONV_W = 256
RET_HEADS = 6
RET_W = RET_HEADS * HEAD_DIM
NA_HEADS = 6
NA_W = NA_HEADS * HEAD_DIM
MIX_W = CONV_W + RET_W + NA_W
RET_CHUNK = 128
WIN_H = 8
WIN_W = 16
ROPE_BASE = 10000.0
LN_EPS = 1e-5
DEEPNORM_ALPHA = (2 * DEPTH) ** 0.25
PROJ_SPLITS = (RET_W, RET_W, NA_W, NA_W, RET_W, RET_W, NA_W, NA_W, CONV_W, CONV_W, CONV_W, CONV_W)
PROJ_W = sum(PROJ_SPLITS)
KV_W = 2 * RET_W + 2 * NA_W

LANES = 128
PAIRS = RET_HEADS // 2
CB_RK, CB_RV, CB_NK, CB_NV, CB_RQ, CB_RG, CB_NQ, CB_NG = 0, 3, 6, 9, 12, 15, 18, 21
CB_CH, CB_CB, CB_CC, CB_CZ = 12, 13, 14, 15
QK_SCALE = HEAD_DIM ** -0.5
NEG = -1e30
NA_ROWS = 4
NA_TOK = NA_ROWS * GRID_W
RET_BLOCK = 1024
ROW_TILE = 512

F32 = jnp.float32
BF16 = jnp.bfloat16


def _silu(v):
    return v * jax.nn.sigmoid(v)


def _dot(a, b):
    return jnp.dot(a, b, preferred_element_type=F32)


def _dot_nt(a, b):
    return lax.dot_general(a, b, (((1,), (1,)), ((), ())), preferred_element_type=F32)


def _dot_tn(a, b):
    return lax.dot_general(a, b, (((0,), (0,)), ((), ())), preferred_element_type=F32)


def _lane_is_head0(shape):
    return lax.broadcasted_iota(jnp.int32, shape, len(shape) - 1) < HEAD_DIM


def _mod_kernel(act_ref, w_ref, b_ref, o_ref):
    a = _silu(act_ref[...])
    w = w_ref[0]
    bias = b_ref[0]
    for r in range(3):
        o_ref[0, r:r + 1, :] = jnp.sum(a[:, r:r + 1] * w, axis=0, keepdims=True) + bias
    o_ref[0, 3:8, :] = jnp.zeros((5, w.shape[1]), F32)


def _modulation(act_t, w_mod, b_mod):
    tn = 512
    n = w_mod.shape[-1]
    return pl.pallas_call(
        _mod_kernel,
        grid=(DEPTH, n // tn),
        in_specs=[pl.BlockSpec((D_MODEL, 8), lambda l, j: (0, 0)),
                  pl.BlockSpec((1, D_MODEL, tn), lambda l, j: (l, 0, j)),
                  pl.BlockSpec((1, 1, tn), lambda l, j: (l, 0, j))],
        out_specs=pl.BlockSpec((1, 8, tn), lambda l, j: (l, 0, j)),
        out_shape=jax.ShapeDtypeStruct((DEPTH, 8, n), F32),
        name="modulation",
    )(act_t, w_mod, b_mod.reshape(DEPTH, 1, n))


def _rope_tables(n):
    t = jnp.arange(n)
    row = (t // GRID_W).astype(F32)
    col = (t % GRID_W).astype(F32)
    nf = HEAD_DIM // 4
    inv = ROPE_BASE ** (-jnp.arange(nf, dtype=F32) / nf)
    ang_r, ang_c = row[:, None] * inv, col[:, None] * inv
    ang = jnp.concatenate([ang_r, ang_r, ang_c, ang_c], axis=-1)
    ang = jnp.concatenate([ang, ang], axis=-1)
    first_half = (np.arange(LANES) % 32) < 16
    cos, sin = jnp.cos(ang), jnp.sin(ang)
    s_up = jnp.where(first_half[None], -sin, 0.0)
    s_dn = jnp.where(first_half[None], 0.0, sin)
    return cos, s_up, s_dn


def _proj_kernel(x_ref, shift_ref, scale_ref, w_ref, *rest, rope, n_cols):
    if rope:
        cos_ref, up_ref, dn_ref, o_ref = rest
    else:
        (o_ref,) = rest
    h = (x_ref[0] * (1.0 + scale_ref[0]) + shift_ref[0]).astype(BF16)
    off = 0
    for seg, width in enumerate(PROJ_SPLITS):
        if off >= n_cols:
            break
        acc = _dot(h, w_ref[:, off:off + width])
        if seg in (0, 4):
            if rope:
                cos, up, dn = cos_ref[...], up_ref[...], dn_ref[...]
                tiles = []
                for j in range(width // LANES):
                    v = acc[:, j * LANES:(j + 1) * LANES]
                    tiles.append(v * cos + pltpu.roll(v, LANES - 16, axis=1) * up
                                 + pltpu.roll(v, 16, axis=1) * dn)
                acc = jnp.concatenate(tiles, axis=1)
            if seg == 0:
                acc = acc * QK_SCALE
        o_ref[0, :, off:off + width] = acc.astype(BF16)
        off += width


def _projection(x, shift, scale, w_bf, tables, n_cols):
    b, n, _ = x.shape
    tm = min(ROW_TILE, n)
    rope = tables is not None
    in_specs = [pl.BlockSpec((1, tm, D_MODEL), lambda bi, i: (bi, i, 0)),
                pl.BlockSpec((1, 1, D_MODEL), lambda bi, i: (bi, 0, 0)),
                pl.BlockSpec((1, 1, D_MODEL), lambda bi, i: (bi, 0, 0)),
                pl.BlockSpec((D_MODEL, n_cols), lambda bi, i: (0, 0))]
    args = [x, shift, scale, w_bf]
    if rope:
        in_specs += [pl.BlockSpec((tm, LANES), lambda bi, i: (i, 0))] * 3
        args += list(tables)
    return pl.pallas_call(
        functools.partial(_proj_kernel, rope=rope, n_cols=n_cols),
        grid=(b, n // tm),
        in_specs=in_specs,
        out_specs=pl.BlockSpec((1, tm, n_cols), lambda bi, i: (bi, i, 0)),
        out_shape=jax.ShapeDtypeStruct((b, n, n_cols), BF16),
        compiler_params=pltpu.CompilerParams(dimension_semantics=("parallel", "parallel")),
        name="projection_rope" if rope else "projection_ctx",
    )(*args)


def _log_sigmoid(v):
    return jnp.minimum(v, 0.0) - jnp.log1p(jnp.exp(-jnp.abs(v)))


def _block_diag(m):
    r = lax.broadcasted_iota(jnp.int32, m.shape, 0) < HEAD_DIM
    c = lax.broadcasted_iota(jnp.int32, m.shape, 1) < HEAD_DIM
    return jnp.where(r == c, m, 0.0)


def _ret_kernel(*refs, forward, init_state, n_chunks):
    refs = list(refs)
    dec_ref, q_ref, k_ref, v_ref = refs[:4]
    refs = refs[4:]
    if init_state:
        kc_ref, vc_ref = refs[:2]
        refs = refs[2:]
    if forward:
        g_ref, ob_ref = refs[:2]
        refs = refs[2:]
    o_ref, s_ref, wq_ref, wk_ref, d_ref, gc_ref = refs
    c = RET_CHUNK
    head0 = _lane_is_head0((c, LANES))

    @pl.when(pl.program_id(2) == 0)
    def _():
        lg = _log_sigmoid(dec_ref[0])
        i = lax.broadcasted_iota(jnp.int32, (c, LANES), 0).astype(F32)
        if forward:
            wq_ref[...] = jnp.exp((i + 1.0) * lg)
            wk_ref[...] = jnp.exp((c - 1.0 - i) * lg)
        else:
            wq_ref[...] = jnp.exp((c - i) * lg)
            wk_ref[...] = jnp.exp(i * lg)
        gc_ref[...] = jnp.exp(float(c) * lg)
        jj = lax.broadcasted_iota(jnp.int32, (c, c), 1).astype(F32)
        ii = lax.broadcasted_iota(jnp.int32, (c, c), 0).astype(F32)
        diff = ii - jj if forward else jj - ii
        mask = diff >= 0 if forward else diff > 0
        for hh in range(2):
            lg_h = lg[:, hh * HEAD_DIM:hh * HEAD_DIM + 1]
            d_ref[:, hh * c:(hh + 1) * c] = jnp.where(mask, jnp.exp(jnp.where(mask, diff, 0.0) * lg_h), 0.0)
        if init_state:
            lc = kc_ref.shape[1]
            m = lax.broadcasted_iota(jnp.int32, (lc, LANES), 0).astype(F32)
            wc = jnp.exp((lc - 1.0 - m) * lg) if forward else jnp.exp(m * lg)
            kcw = (kc_ref[0].astype(F32) * wc).astype(BF16)
            s_ref[...] = _block_diag(_dot_tn(kcw, vc_ref[0]))
        else:
            s_ref[...] = jnp.zeros_like(s_ref)

    wq, wk, dcat, gc = wq_ref[...], wk_ref[...], d_ref[...], gc_ref[...]
    order = range(n_chunks) if forward else range(n_chunks - 1, -1, -1)
    for n in order:
        rows = pl.ds(n * c, c)
        q, k, v = q_ref[0, rows, :], k_ref[0, rows, :], v_ref[0, rows, :]
        zero = jnp.zeros_like(k)
        kcat = jnp.concatenate([jnp.where(head0, k, zero), jnp.where(head0, zero, k)], axis=0)
        vcat = jnp.concatenate([jnp.where(head0, v, zero), jnp.where(head0, zero, v)], axis=0)
        scores = _dot_nt(q, kcat) * dcat
        s_prev = s_ref[...]
        o = _dot(scores.astype(BF16), vcat) + _dot((q.astype(F32) * wq).astype(BF16), s_prev.astype(BF16))
        s_ref[...] = s_prev * gc + _block_diag(_dot_tn((k.astype(F32) * wk).astype(BF16), v))
        if forward:
            o = o + ob_ref[0, rows, :]
            inv_n = 1.0 / HEAD_DIM
            mu = jnp.where(head0,
                           jnp.sum(jnp.where(head0, o, 0.0), axis=-1, keepdims=True),
                           jnp.sum(jnp.where(head0, 0.0, o), axis=-1, keepdims=True)) * inv_n
            dlt = o - mu
            sq = dlt * dlt
            var = jnp.where(head0,
                            jnp.sum(jnp.where(head0, sq, 0.0), axis=-1, keepdims=True),
                            jnp.sum(jnp.where(head0, 0.0, sq), axis=-1, keepdims=True)) * inv_n
            y = dlt * lax.rsqrt(var + LN_EPS) * _silu(g_ref[0, rows, :].astype(F32))
            o_ref[0, rows, :] = y.astype(o_ref.dtype)
        else:
            o_ref[0, rows, :] = o


def _retention_pass(p, pc, dec_lanes, o_bwd, forward, init_state):
    b, n, _ = p.shape
    tb = min(RET_BLOCK, n)
    nblk = n // tb

    def tok(i):
        return i if forward else nblk - 1 - i

    def col(cb):
        return pl.BlockSpec((1, tb, LANES), lambda bi, pi, i: (bi, tok(i), cb + pi))

    in_specs = [pl.BlockSpec((1, 1, LANES), lambda bi, pi, i: (pi, 0, 0)), col(CB_RQ), col(CB_RK), col(CB_RV)]
    args = [dec_lanes, p, p, p]
    if init_state:
        lc = pc.shape[1]
        in_specs += [pl.BlockSpec((1, lc, LANES), lambda bi, pi, i: (bi, 0, CB_RK + pi)),
                     pl.BlockSpec((1, lc, LANES), lambda bi, pi, i: (bi, 0, CB_RV + pi))]
        args += [pc, pc]
    if forward:
        in_specs += [col(CB_RG), col(0)]
        args += [p, o_bwd]
    c = RET_CHUNK
    return pl.pallas_call(
        functools.partial(_ret_kernel, forward=forward, init_state=init_state, n_chunks=tb // c),
        grid=(b, PAIRS, nblk),
        in_specs=in_specs,
        out_specs=col(0),
        out_shape=jax.ShapeDtypeStruct((b, n, RET_W), BF16 if forward else F32),
        scratch_shapes=[pltpu.VMEM((LANES, LANES), F32),
                        pltpu.VMEM((c, LANES), F32),
                        pltpu.VMEM((c, LANES), F32),
                        pltpu.VMEM((c, 2 * c), F32),
                        pltpu.VMEM((1, LANES), F32)],
        compiler_params=pltpu.CompilerParams(dimension_semantics=("parallel", "parallel", "arbitrary")),
        name="retention_fwd" if forward else "retention_bwd",
    )(*args)


def _retention(p, pc, ret_decay, init_state):
    dec = jnp.repeat(ret_decay.astype(F32), HEAD_DIM, axis=-1).reshape(2, PAIRS, 1, LANES)
    o_bwd = _retention_pass(p, pc, dec[1], None, forward=False, init_state=init_state)
    return _retention_pass(p, pc, dec[0], o_bwd, forward=True, init_state=init_state)


def _na_bias_tables(rpb, rows):
    nblk = rows // NA_ROWS
    win_h = min(WIN_H, rows)
    tables = []
    for blk in (0, 1, nblk - 1):
        r = blk * NA_ROWS + np.arange(NA_ROWS)
        rs = np.clip(r - win_h // 2, 0, rows - win_h)
        kblk = np.clip(blk + np.arange(-1, 2), 0, nblk - 1)
        kr = (kblk[:, None] * NA_ROWS + np.arange(NA_ROWS)[None, :])
        dup = np.array([blk - 1 < 0, False, blk + 1 > nblk - 1])
        row_ok = (kr[None] >= rs[:, None, None]) & (kr[None] < rs[:, None, None] + win_h) & ~dup[None, :, None]
        d_row = kr[None] - r[:, None, None]
        qc = np.arange(GRID_W)
        cs = np.clip(qc - WIN_W // 2, 0, GRID_W - WIN_W)
        kc = np.arange(GRID_W)
        col_ok = (kc[None] >= cs[:, None]) & (kc[None] < cs[:, None] + WIN_W)
        d_col = kc[None] - qc[:, None]
        ok = row_ok[:, None, :, :, None] & col_ok[None, :, None, None, :]
        i_row = np.broadcast_to(np.clip(d_row + WIN_H - 1, 0, 2 * WIN_H - 2)[:, None, :, :, None], ok.shape)
        i_col = np.broadcast_to(np.clip(d_col + WIN_W - 1, 0, 2 * WIN_W - 2)[None, :, None, None, :], ok.shape)
        shape2d = (NA_TOK, 3 * NA_TOK)
        vals = rpb[:, i_row.reshape(shape2d), i_col.reshape(shape2d)].astype(F32)
        tables.append(jnp.where(ok.reshape(shape2d)[None], vals, NEG))
    return jnp.stack(tables, axis=0)


def _na_kernel(q_ref, kp_ref, kc_ref, kn_ref, vp_ref, vc_ref, vn_ref, kx_ref, vx_ref, g_ref, bias_ref, o_ref):
    q = q_ref[0]
    head0 = _lane_is_head0(q.shape)
    zero = jnp.zeros_like(q)
    k_loc = (kp_ref[0], kc_ref[0], kn_ref[0])
    v_loc = (vp_ref[0], vc_ref[0], vn_ref[0])
    kx, vx = kx_ref[0], vx_ref[0]
    outs = []
    for hh in range(2):
        qh = jnp.where(head0, q, zero) if hh == 0 else jnp.where(head0, zero, q)
        s_loc = [_dot_nt(qh, k_loc[j]) * QK_SCALE + bias_ref[0, hh, :, j * NA_TOK:(j + 1) * NA_TOK] for j in range(3)]
        s_ctx = _dot_nt(qh, kx) * QK_SCALE
        m = jnp.max(s_ctx, axis=-1, keepdims=True)
        for s in s_loc:
            m = jnp.maximum(m, jnp.max(s, axis=-1, keepdims=True))
        p_ctx = jnp.exp(s_ctx - m)
        den = jnp.sum(p_ctx, axis=-1, keepdims=True)
        acc = _dot(p_ctx.astype(BF16), vx)
        for j in range(3):
            p = jnp.exp(s_loc[j] - m)
            den = den + jnp.sum(p, axis=-1, keepdims=True)
            acc = acc + _dot(p.astype(BF16), v_loc[j])
        outs.append(acc / den)
    o = jnp.where(head0, outs[0], outs[1])
    o_ref[0] = (o * _silu(g_ref[0].astype(F32))).astype(o_ref.dtype)


def _neighbourhood(p, pc, bias):
    b, n, _ = p.shape
    nblk = n // NA_TOK

    def blk(cb, shift):
        return pl.BlockSpec((1, NA_TOK, LANES),
                            lambda bi, pi, i: (bi, jnp.clip(i + shift, 0, nblk - 1), cb + pi))

    lc = pc.shape[1]

    def ctx(cb):
        return pl.BlockSpec((1, lc, LANES), lambda bi, pi, i: (bi, 0, cb + pi))

    def bias_index(bi, pi, i):
        kind = jnp.where(i == 0, 0, jnp.where(i == nblk - 1, 2, 1))
        return (kind, pi, 0, 0)

    return pl.pallas_call(
        _na_kernel,
        grid=(b, PAIRS, nblk),
        in_specs=[blk(CB_NQ, 0), blk(CB_NK, -1), blk(CB_NK, 0), blk(CB_NK, 1),
                  blk(CB_NV, -1), blk(CB_NV, 0), blk(CB_NV, 1), ctx(CB_NK), ctx(CB_NV), blk(CB_NG, 0),
                  pl.BlockSpec((1, 2, NA_TOK, 3 * NA_TOK), bias_index)],
        out_specs=pl.BlockSpec((1, NA_TOK, LANES), lambda bi, pi, i: (bi, i, pi)),
        out_shape=jax.ShapeDtypeStruct((b, n, NA_W), BF16),
        compiler_params=pltpu.CompilerParams(dimension_semantics=("parallel", "parallel", "arbitrary")),
        name="neighbourhood",
    )(p, p, p, p, p, p, p, pc, pc, p, bias)


def _ctx_attn_kernel(q_ref, k_ref, v_ref, g_ref, o_ref):
    q, k, v = q_ref[0], k_ref[0], v_ref[0]
    head0 = _lane_is_head0(q.shape)
    zero = jnp.zeros_like(q)
    outs = []
    for hh in range(2):
        qh = jnp.where(head0, q, zero) if hh == 0 else jnp.where(head0, zero, q)
        s = _dot_nt(qh, k) * QK_SCALE
        p = jnp.exp(s - jnp.max(s, axis=-1, keepdims=True))
        outs.append(_dot(p.astype(BF16), v) / jnp.sum(p, axis=-1, keepdims=True))
    o = jnp.where(head0, outs[0], outs[1])
    o_ref[0] = (o * _silu(g_ref[0].astype(F32))).astype(o_ref.dtype)


def _ctx_attention(pc):
    b, lc, _ = pc.shape

    def col(cb):
        return pl.BlockSpec((1, lc, LANES), lambda bi, pi: (bi, 0, cb + pi))

    return pl.pallas_call(
        _ctx_attn_kernel,
        grid=(b, PAIRS),
        in_specs=[col(CB_NQ), col(CB_NK), col(CB_NV), col(CB_NG)],
        out_specs=col(0),
        out_shape=jax.ShapeDtypeStruct((b, lc, NA_W), BF16),
        name="context_attention",
    )(pc, pc, pc, pc)


def _out_kernel(x_ref, gate_ref, ch_ref, cb_ref, cc_ref, cz_ref, hp_ref, cp_ref, hn_ref, cn_ref,
                yr_ref, yn_ref, cw_ref, cbias_ref, w_ref, lg_ref, lb_ref, o_ref):
    i = pl.program_id(1)
    last = pl.num_programs(1) - 1
    u = cc_ref[0].astype(F32) * ch_ref[0].astype(F32)
    tm = u.shape[0]
    halo = hp_ref.shape[1]
    u_before = cp_ref[0, halo - 1:halo, :].astype(F32) * hp_ref[0, halo - 1:halo, :].astype(F32)
    u_after = cn_ref[0, 0:1, :].astype(F32) * hn_ref[0, 0:1, :].astype(F32)
    u_before = jnp.where(i == 0, 0.0, u_before)
    u_after = jnp.where(i == last, 0.0, u_after)
    row = lax.broadcasted_iota(jnp.int32, u.shape, 0)
    u_prev = jnp.where(row == 0, u_before, pltpu.roll(u, 1, axis=0))
    u_next = jnp.where(row == tm - 1, u_after, pltpu.roll(u, tm - 1, axis=0))
    cw = cw_ref[...]
    conv = u_prev * cw[0:1] + u * cw[1:2] + u_next * cw[2:3] + cbias_ref[...]
    y_conv = cb_ref[0].astype(F32) * conv * _silu(cz_ref[0].astype(F32))
    y = jnp.concatenate([y_conv.astype(BF16), yr_ref[0], yn_ref[0]], axis=-1)
    z = DEEPNORM_ALPHA * x_ref[0] + gate_ref[0] * _dot(y, w_ref[...])
    mu = jnp.mean(z, axis=-1, keepdims=True)
    dlt = z - mu
    var = jnp.mean(dlt * dlt, axis=-1, keepdims=True)
    o_ref[0] = dlt * lax.rsqrt(var + LN_EPS) * lg_ref[...] + lb_ref[...]


def _output(x, gate, p, y_ret, y_na, conv_w, conv_b, w_out_bf, ln_g, ln_b):
    b, n, _ = x.shape
    tm = min(ROW_TILE, n)
    halo = 16
    per = tm // halo
    nh = n // halo

    def conv(cb):
        return pl.BlockSpec((1, tm, CONV_W), lambda bi, i: (bi, i, cb))

    def before(cb):
        return pl.BlockSpec((1, halo, CONV_W), lambda bi, i: (bi, jnp.maximum(i * per - 1, 0), cb))

    def after(cb):
        return pl.BlockSpec((1, halo, CONV_W), lambda bi, i: (bi, jnp.minimum((i + 1) * per, nh - 1), cb))

    def const(shape):
        return pl.BlockSpec(shape, lambda bi, i: (0,) * len(shape))

    return pl.pallas_call(
        _out_kernel,
        grid=(b, n // tm),
        in_specs=[pl.BlockSpec((1, tm, D_MODEL), lambda bi, i: (bi, i, 0)),
                  pl.BlockSpec((1, 1, D_MODEL), lambda bi, i: (bi, 0, 0)),
                  conv(CB_CH), conv(CB_CB), conv(CB_CC), conv(CB_CZ),
                  before(CB_CH), before(CB_CC), after(CB_CH), after(CB_CC),
                  pl.BlockSpec((1, tm, RET_W), lambda bi, i: (bi, i, 0)),
                  pl.BlockSpec((1, tm, NA_W), lambda bi, i: (bi, i, 0)),
                  const((3, CONV_W)), const((1, CONV_W)), const((MIX_W, D_MODEL)),
                  const((1, D_MODEL)), const((1, D_MODEL))],
        out_specs=pl.BlockSpec((1, tm, D_MODEL), lambda bi, i: (bi, i, 0)),
        out_shape=jax.ShapeDtypeStruct((b, n, D_MODEL), F32),
        compiler_params=pltpu.CompilerParams(dimension_semantics=("parallel", "parallel")),
        name="output",
    )(x, gate, p, p, p, p, p, p, p, p, y_ret, y_na, conv_w, conv_b.reshape(1, CONV_W), w_out_bf,
      ln_g.reshape(1, D_MODEL), ln_b.reshape(1, D_MODEL))


def kernel(x, c, ctx, c_ctx, w_mod, b_mod, w_in, conv_w, conv_b, ret_decay, na_rpb, w_out, ln_g, ln_b):
    b, n, d = x.shape
    act_t = jnp.pad(jnp.concatenate([c, c_ctx[None]], axis=0).T, ((0, 0), (0, 5)))
    mod = _modulation(act_t, w_mod, b_mod)
    tables = _rope_tables(n)
    w_in_bf = w_in.astype(BF16)
    w_out_bf = w_out.astype(BF16)
    xc = ctx
    for l in range(DEPTH):
        need_ctx = l < DEPTH - 1
        shift, scale, gate = (mod[l, :b, None, j * d:(j + 1) * d] for j in range(3))
        shift_c, scale_c, gate_c = (jnp.broadcast_to(mod[l, b, j * d:(j + 1) * d], (b, 1, d)) for j in range(3))
        p = _projection(x, shift, scale, w_in_bf[l], tables, PROJ_W)
        pc = _projection(xc, shift_c, scale_c, w_in_bf[l], None, PROJ_W if need_ctx else KV_W)
        y_ret = _retention(p, pc, ret_decay[l], init_state=True)
        y_na = _neighbourhood(p, pc, _na_bias_tables(na_rpb[l], n // GRID_W))
        x_new = _output(x, gate, p, y_ret, y_na, conv_w[l], conv_b[l], w_out_bf[l], ln_g[l], ln_b[l])
        if need_ctx:
            yc_ret = _retention(pc, None, ret_decay[l], init_state=False)
            yc_na = _ctx_attention(pc)
            xc = _output(xc, gate_c, pc, yc_ret, yc_na, conv_w[l], conv_b[l], w_out_bf[l], ln_g[l], ln_b[l])
        x = x_new
    return x
```

```python
import functools

import numpy as np
import jax
import jax.numpy as jnp
from jax import lax
from jax.experimental import pallas as pl
from jax.experimental.pallas import tpu as pltpu

D_MODEL = 1024
DEPTH = 2
GRID_W = 64
HEAD_DIM = 64
CONV_W = 256
RET_HEADS = 6
RET_W = RET_HEADS * HEAD_DIM
NA_HEADS = 6
NA_W = NA_HEADS * HEAD_DIM
MIX_W = CONV_W + RET_W + NA_W
RET_CHUNK = 128
WIN_H = 8
WIN_W = 16
ROPE_BASE = 10000.0
LN_EPS = 1e-5
DEEPNORM_ALPHA = (2 * DEPTH) ** 0.25
PROJ_SPLITS = (RET_W, RET_W, NA_W, NA_W, RET_W, RET_W, NA_W, NA_W, CONV_W, CONV_W, CONV_W, CONV_W)
PROJ_W = sum(PROJ_SPLITS)
KV_W = 2 * RET_W + 2 * NA_W

LANES = 128
PAIRS = RET_HEADS // 2
CB_RK, CB_RV, CB_NK, CB_NV, CB_RQ, CB_RG, CB_NQ, CB_NG = 0, 3, 6, 9, 12, 15, 18, 21
CB_CH, CB_CB, CB_CC, CB_CZ = 12, 13, 14, 15
QK_SCALE = HEAD_DIM ** -0.5
NEG = -1e30
NA_ROWS = 4
NA_TOK = NA_ROWS * GRID_W
RET_BLOCK = 1024
ROW_TILE = 512

F32 = jnp.float32
BF16 = jnp.bfloat16


def _silu(v):
    return v * jax.nn.sigmoid(v)


def _dot(a, b):
    return jnp.dot(a, b, preferred_element_type=F32)


def _dot_nt(a, b):
    return lax.dot_general(a, b, (((1,), (1,)), ((), ())), preferred_element_type=F32)


def _dot_tn(a, b):
    return lax.dot_general(a, b, (((0,), (0,)), ((), ())), preferred_element_type=F32)


def _lane_is_head0(shape):
    return lax.broadcasted_iota(jnp.int32, shape, len(shape) - 1) < HEAD_DIM


def _mod_kernel(act_ref, w_ref, b_ref, o_ref):
    a = _silu(act_ref[...])
    w = w_ref[0]
    bias = b_ref[0]
    for r in range(3):
        o_ref[0, r:r + 1, :] = jnp.sum(a[:, r:r + 1] * w, axis=0, keepdims=True) + bias
    o_ref[0, 3:8, :] = jnp.zeros((5, w.shape[1]), F32)


def _modulation(act_t, w_mod, b_mod):
    tn = 512
    n = w_mod.shape[-1]
    return pl.pallas_call(
        _mod_kernel,
        grid=(DEPTH, n // tn),
        in_specs=[pl.BlockSpec((D_MODEL, 8), lambda l, j: (0, 0)),
                  pl.BlockSpec((1, D_MODEL, tn), lambda l, j: (l, 0, j)),
                  pl.BlockSpec((1, 1, tn), lambda l, j: (l, 0, j))],
        out_specs=pl.BlockSpec((1, 8, tn), lambda l, j: (l, 0, j)),
        out_shape=jax.ShapeDtypeStruct((DEPTH, 8, n), F32),
        name="modulation",
    )(act_t, w_mod, b_mod.reshape(DEPTH, 1, n))


def _rope_tables(n):
    t = jnp.arange(n)
    row = (t // GRID_W).astype(F32)
    col = (t % GRID_W).astype(F32)
    nf = HEAD_DIM // 4
    inv = ROPE_BASE ** (-jnp.arange(nf, dtype=F32) / nf)
    ang_r, ang_c = row[:, None] * inv, col[:, None] * inv
    ang = jnp.concatenate([ang_r, ang_r, ang_c, ang_c], axis=-1)
    ang = jnp.concatenate([ang, ang], axis=-1)
    first_half = (np.arange(LANES) % 32) < 16
    cos, sin = jnp.cos(ang), jnp.sin(ang)
    s_up = jnp.where(first_half[None], -sin, 0.0)
    s_dn = jnp.where(first_half[None], 0.0, sin)
    return cos, s_up, s_dn


def _proj_kernel(x_ref, shift_ref, scale_ref, w_ref, *rest, rope, n_cols):
    if rope:
        cos_ref, up_ref, dn_ref, o_ref = rest
    else:
        (o_ref,) = rest
    h = (x_ref[0] * (1.0 + scale_ref[0]) + shift_ref[0]).astype(BF16)
    off = 0
    for seg, width in enumerate(PROJ_SPLITS):
        if off >= n_cols:
            break
        acc = _dot(h, w_ref[:, off:off + width])
        if seg in (0, 4):
            if rope:
                cos, up, dn = cos_ref[...], up_ref[...], dn_ref[...]
                tiles = []
                for j in range(width // LANES):
                    v = acc[:, j * LANES:(j + 1) * LANES]
                    tiles.append(v * cos + pltpu.roll(v, LANES - 16, axis=1) * up
                                 + pltpu.roll(v, 16, axis=1) * dn)
                acc = jnp.concatenate(tiles, axis=1)
            if seg == 0:
                acc = acc * QK_SCALE
        o_ref[0, :, off:off + width] = acc.astype(BF16)
        off += width


def _projection(x, shift, scale, w_bf, tables, n_cols):
    b, n, _ = x.shape
    tm = min(ROW_TILE, n)
    rope = tables is not None
    in_specs = [pl.BlockSpec((1, tm, D_MODEL), lambda bi, i: (bi, i, 0)),
                pl.BlockSpec((1, 1, D_MODEL), lambda bi, i: (bi, 0, 0)),
                pl.BlockSpec((1, 1, D_MODEL), lambda bi, i: (bi, 0, 0)),
                pl.BlockSpec((D_MODEL, n_cols), lambda bi, i: (0, 0))]
    args = [x, shift, scale, w_bf]
    if rope:
        in_specs += [pl.BlockSpec((tm, LANES), lambda bi, i: (i, 0))] * 3
        args += list(tables)
    return pl.pallas_call(
        functools.partial(_proj_kernel, rope=rope, n_cols=n_cols),
        grid=(b, n // tm),
        in_specs=in_specs,
        out_specs=pl.BlockSpec((1, tm, n_cols), lambda bi, i: (bi, i, 0)),
        out_shape=jax.ShapeDtypeStruct((b, n, n_cols), BF16),
        compiler_params=pltpu.CompilerParams(dimension_semantics=("parallel", "parallel")),
        name="projection_rope" if rope else "projection_ctx",
    )(*args)


def _log_sigmoid(v):
    return jnp.minimum(v, 0.0) - jnp.log1p(jnp.exp(-jnp.abs(v)))


def _block_diag(m):
    r = lax.broadcasted_iota(jnp.int32, m.shape, 0) < HEAD_DIM
    c = lax.broadcasted_iota(jnp.int32, m.shape, 1) < HEAD_DIM
    return jnp.where(r == c, m, 0.0)


def _ret_kernel(*refs, forward, init_state, n_chunks):
    refs = list(refs)
    dec_ref, q_ref, k_ref, v_ref = refs[:4]
    refs = refs[4:]
    if init_state:
        kc_ref, vc_ref = refs[:2]
        refs = refs[2:]
    if forward:
        g_ref, ob_ref = refs[:2]
        refs = refs[2:]
    o_ref, s_ref, wq_ref, wk_ref, d_ref, gc_ref = refs
    c = RET_CHUNK
    head0 = _lane_is_head0((c, LANES))

    @pl.when(pl.program_id(2) == 0)
    def _():
        lg = _log_sigmoid(dec_ref[0])
        i = lax.broadcasted_iota(jnp.int32, (c, LANES), 0).astype(F32)
        if forward:
            wq_ref[...] = jnp.exp((i + 1.0) * lg)
            wk_ref[...] = jnp.exp((c - 1.0 - i) * lg)
        else:
            wq_ref[...] = jnp.exp((c - i) * lg)
            wk_ref[...] = jnp.exp(i * lg)
        gc_ref[...] = jnp.exp(float(c) * lg)
        jj = lax.broadcasted_iota(jnp.int32, (c, c), 1).astype(F32)
        ii = lax.broadcasted_iota(jnp.int32, (c, c), 0).astype(F32)
        diff = ii - jj if forward else jj - ii
        mask = diff >= 0 if forward else diff > 0
        for hh in range(2):
            lg_h = lg[:, hh * HEAD_DIM:hh * HEAD_DIM + 1]
            d_ref[:, hh * c:(hh + 1) * c] = jnp.where(mask, jnp.exp(jnp.where(mask, diff, 0.0) * lg_h), 0.0)
        if init_state:
            lc = kc_ref.shape[1]
            m = lax.broadcasted_iota(jnp.int32, (lc, LANES), 0).astype(F32)
            wc = jnp.exp((lc - 1.0 - m) * lg) if forward else jnp.exp(m * lg)
            kcw = (kc_ref[0].astype(F32) * wc).astype(BF16)
            s_ref[...] = _block_diag(_dot_tn(kcw, vc_ref[0]))
        else:
            s_ref[...] = jnp.zeros_like(s_ref)

    wq, wk, dcat, gc = wq_ref[...], wk_ref[...], d_ref[...], gc_ref[...]
    order = range(n_chunks) if forward else range(n_chunks - 1, -1, -1)
    for n in order:
        rows = pl.ds(n * c, c)
        q, k, v = q_ref[0, rows, :], k_ref[0, rows, :], v_ref[0, rows, :]
        zero = jnp.zeros_like(k)
        kcat = jnp.concatenate([jnp.where(head0, k, zero), jnp.where(head0, zero, k)], axis=0)
        vcat = jnp.concatenate([jnp.where(head0, v, zero), jnp.where(head0, zero, v)], axis=0)
        scores = _dot_nt(q, kcat) * dcat
        s_prev = s_ref[...]
        o = _dot(scores.astype(BF16), vcat) + _dot((q.astype(F32) * wq).astype(BF16), s_prev.astype(BF16))
        s_ref[...] = s_prev * gc + _block_diag(_dot_tn((k.astype(F32) * wk).astype(BF16), v))
        if forward:
            o = o + ob_ref[0, rows, :]
            inv_n = 1.0 / HEAD_DIM
            mu = jnp.where(head0,
                           jnp.sum(jnp.where(head0, o, 0.0), axis=-1, keepdims=True),
                           jnp.sum(jnp.where(head0, 0.0, o), axis=-1, keepdims=True)) * inv_n
            dlt = o - mu
            sq = dlt * dlt
            var = jnp.where(head0,
                            jnp.sum(jnp.where(head0, sq, 0.0), axis=-1, keepdims=True),
                            jnp.sum(jnp.where(head0, 0.0, sq), axis=-1, keepdims=True)) * inv_n
            y = dlt * lax.rsqrt(var + LN_EPS) * _silu(g_ref[0, rows, :].astype(F32))
            o_ref[0, rows, :] = y.astype(o_ref.dtype)
        else:
            o_ref[0, rows, :] = o


def _retention_pass(p, pc, dec_lanes, o_bwd, forward, init_state):
    b, n, _ = p.shape
    tb = min(RET_BLOCK, n)
    nblk = n // tb

    def tok(i):
        return i if forward else nblk - 1 - i

    def col(cb):
        return pl.BlockSpec((1, tb, LANES), lambda bi, pi, i: (bi, tok(i), cb + pi))

    in_specs = [pl.BlockSpec((1, 1, LANES), lambda bi, pi, i: (pi, 0, 0)), col(CB_RQ), col(CB_RK), col(CB_RV)]
    args = [dec_lanes, p, p, p]
    if init_state:
        lc = pc.shape[1]
        in_specs += [pl.BlockSpec((1, lc, LANES), lambda bi, pi, i: (bi, 0, CB_RK + pi)),
                     pl.BlockSpec((1, lc, LANES), lambda bi, pi, i: (bi, 0, CB_RV + pi))]
        args += [pc, pc]
    if forward:
        in_specs += [col(CB_RG), col(0)]
        args += [p, o_bwd]
    c = RET_CHUNK
    return pl.pallas_call(
        functools.partial(_ret_kernel, forward=forward, init_state=init_state, n_chunks=tb // c),
        grid=(b, PAIRS, nblk),
        in_specs=in_specs,
        out_specs=col(0),
        out_shape=jax.ShapeDtypeStruct((b, n, RET_W), BF16 if forward else F32),
        scratch_shapes=[pltpu.VMEM((LANES, LANES), F32),
                        pltpu.VMEM((c, LANES), F32),
                        pltpu.VMEM((c, LANES), F32),
                        pltpu.VMEM((c, 2 * c), F32),
                        pltpu.VMEM((1, LANES), F32)],
        compiler_params=pltpu.CompilerParams(dimension_semantics=("parallel", "parallel", "arbitrary")),
        name="retention_fwd" if forward else "retention_bwd",
    )(*args)


def _retention(p, pc, ret_decay, init_state):
    dec = jnp.repeat(ret_decay.astype(F32), HEAD_DIM, axis=-1).reshape(2, PAIRS, 1, LANES)
    o_bwd = _retention_pass(p, pc, dec[1], None, forward=False, init_state=init_state)
    return _retention_pass(p, pc, dec[0], o_bwd, forward=True, init_state=init_state)


def _na_bias_plan(rows):
    nblk = rows // NA_ROWS
    win_h = min(WIN_H, rows)
    plan = np.full((3, NA_ROWS, 3, NA_ROWS), -1, np.int64)
    for kind, blk in enumerate((0, 1, nblk - 1)):
        for qr in range(NA_ROWS):
            r = blk * NA_ROWS + qr
            rs = min(max(r - win_h // 2, 0), rows - win_h)
            for kb in range(3):
                if not 0 <= blk + kb - 1 <= nblk - 1:
                    continue
                for krl in range(NA_ROWS):
                    kr = (blk + kb - 1) * NA_ROWS + krl
                    if rs <= kr < rs + win_h:
                        plan[kind, qr, kb, krl] = kr - r + WIN_H - 1
    return plan


def _na_bias_kernel(rpb_ref, o_ref, t_ref, *, plan):
    h = pl.program_id(0)
    n_dr, n_dc = 2 * WIN_H - 1, 2 * WIN_W - 1
    qc = lax.broadcasted_iota(jnp.int32, (GRID_W, LANES), 0)
    lane = lax.broadcasted_iota(jnp.int32, (GRID_W, LANES), 1)
    kc = lane % GRID_W
    d_col = kc - qc + (WIN_W - 1)
    cs = jnp.clip(qc - WIN_W // 2, 0, GRID_W - WIN_W)
    col_ok = (kc >= cs) & (kc < cs + WIN_W)
    neg = jnp.full((GRID_W, LANES), NEG, F32)
    for dr in range(n_dr):
        t = neg
        for dc in range(n_dc):
            t = jnp.where(d_col == dc, rpb_ref[(h * n_dr + dr) * n_dc + dc], t)
        t_ref[dr] = jnp.where(col_ok, t, neg)
    first = lane < GRID_W
    for kind in range(3):
        for qr in range(NA_ROWS):
            for kb in range(3):
                for pr in range(NA_ROWS // 2):
                    ia, ib = (int(plan[kind, qr, kb, 2 * pr + s]) for s in range(2))
                    a = t_ref[ia] if ia >= 0 else neg
                    b = t_ref[ib] if ib >= 0 else neg
                    col0 = kb * NA_TOK + pr * LANES
                    o_ref[kind, 0, qr * GRID_W:(qr + 1) * GRID_W, col0:col0 + LANES] = jnp.where(first, a, b)


def _na_bias_tables(rpb, rows):
    n_dr, n_dc = 2 * WIN_H - 1, 2 * WIN_W - 1
    return pl.pallas_call(
        functools.partial(_na_bias_kernel, plan=_na_bias_plan(rows)),
        grid=(NA_HEADS,),
        in_specs=[pl.BlockSpec(memory_space=pltpu.SMEM)],
        out_specs=pl.BlockSpec((3, 1, NA_TOK, 3 * NA_TOK), lambda h: (0, h, 0, 0)),
        out_shape=jax.ShapeDtypeStruct((3, NA_HEADS, NA_TOK, 3 * NA_TOK), F32),
        scratch_shapes=[pltpu.VMEM((n_dr, GRID_W, LANES), F32)],
        name="na_bias",
    )(rpb.astype(F32).reshape(NA_HEADS * n_dr * n_dc))


def _na_kernel(q_ref, kp_ref, kc_ref, kn_ref, vp_ref, vc_ref, vn_ref, kx_ref, vx_ref, g_ref, bias_ref, o_ref):
    q = q_ref[0]
    head0 = _lane_is_head0(q.shape)
    zero = jnp.zeros_like(q)
    k_loc = (kp_ref[0], kc_ref[0], kn_ref[0])
    v_loc = (vp_ref[0], vc_ref[0], vn_ref[0])
    kx, vx = kx_ref[0], vx_ref[0]
    outs = []
    for hh in range(2):
        qh = jnp.where(head0, q, zero) if hh == 0 else jnp.where(head0, zero, q)
        s_loc = [_dot_nt(qh, k_loc[j]) * QK_SCALE + bias_ref[0, hh, :, j * NA_TOK:(j + 1) * NA_TOK] for j in range(3)]
        s_ctx = _dot_nt(qh, kx) * QK_SCALE
        m = jnp.max(s_ctx, axis=-1, keepdims=True)
        for s in s_loc:
            m = jnp.maximum(m, jnp.max(s, axis=-1, keepdims=True))
        p_ctx = jnp.exp(s_ctx - m)
        den = jnp.sum(p_ctx, axis=-1, keepdims=True)
        acc = _dot(p_ctx.astype(BF16), vx)
        for j in range(3):
            p = jnp.exp(s_loc[j] - m)
            den = den + jnp.sum(p, axis=-1, keepdims=True)
            acc = acc + _dot(p.astype(BF16), v_loc[j])
        outs.append(acc / den)
    o = jnp.where(head0, outs[0], outs[1])
    o_ref[0] = (o * _silu(g_ref[0].astype(F32))).astype(o_ref.dtype)


def _neighbourhood(p, pc, bias):
    b, n, _ = p.shape
    nblk = n // NA_TOK

    def blk(cb, shift):
        return pl.BlockSpec((1, NA_TOK, LANES),
                            lambda bi, pi, i: (bi, jnp.clip(i + shift, 0, nblk - 1), cb + pi))

    lc = pc.shape[1]

    def ctx(cb):
        return pl.BlockSpec((1, lc, LANES), lambda bi, pi, i: (bi, 0, cb + pi))

    def bias_index(bi, pi, i):
        kind = jnp.where(i == 0, 0, jnp.where(i == nblk - 1, 2, 1))
        return (kind, pi, 0, 0)

    return pl.pallas_call(
        _na_kernel,
        grid=(b, PAIRS, nblk),
        in_specs=[blk(CB_NQ, 0), blk(CB_NK, -1), blk(CB_NK, 0), blk(CB_NK, 1),
                  blk(CB_NV, -1), blk(CB_NV, 0), blk(CB_NV, 1), ctx(CB_NK), ctx(CB_NV), blk(CB_NG, 0),
                  pl.BlockSpec((1, 2, NA_TOK, 3 * NA_TOK), bias_index)],
        out_specs=pl.BlockSpec((1, NA_TOK, LANES), lambda bi, pi, i: (bi, i, pi)),
        out_shape=jax.ShapeDtypeStruct((b, n, NA_W), BF16),
        compiler_params=pltpu.CompilerParams(dimension_semantics=("parallel", "parallel", "arbitrary")),
        name="neighbourhood",
    )(p, p, p, p, p, p, p, pc, pc, p, bias)


def _ctx_attn_kernel(q_ref, k_ref, v_ref, g_ref, o_ref):
    q, k, v = q_ref[0], k_ref[0], v_ref[0]
    head0 = _lane_is_head0(q.shape)
    zero = jnp.zeros_like(q)
    outs = []
    for hh in range(2):
        qh = jnp.where(head0, q, zero) if hh == 0 else jnp.where(head0, zero, q)
        s = _dot_nt(qh, k) * QK_SCALE
        p = jnp.exp(s - jnp.max(s, axis=-1, keepdims=True))
        outs.append(_dot(p.astype(BF16), v) / jnp.sum(p, axis=-1, keepdims=True))
    o = jnp.where(head0, outs[0], outs[1])
    o_ref[0] = (o * _silu(g_ref[0].astype(F32))).astype(o_ref.dtype)


def _ctx_attention(pc):
    b, lc, _ = pc.shape

    def col(cb):
        return pl.BlockSpec((1, lc, LANES), lambda bi, pi: (bi, 0, cb + pi))

    return pl.pallas_call(
        _ctx_attn_kernel,
        grid=(b, PAIRS),
        in_specs=[col(CB_NQ), col(CB_NK), col(CB_NV), col(CB_NG)],
        out_specs=col(0),
        out_shape=jax.ShapeDtypeStruct((b, lc, NA_W), BF16),
        name="context_attention",
    )(pc, pc, pc, pc)


def _out_kernel(x_ref, gate_ref, ch_ref, cb_ref, cc_ref, cz_ref, hp_ref, cp_ref, hn_ref, cn_ref,
                yr_ref, yn_ref, cw_ref, cbias_ref, w_ref, lg_ref, lb_ref, o_ref):
    i = pl.program_id(1)
    last = pl.num_programs(1) - 1
    u = cc_ref[0].astype(F32) * ch_ref[0].astype(F32)
    tm = u.shape[0]
    halo = hp_ref.shape[1]
    u_before = cp_ref[0, halo - 1:halo, :].astype(F32) * hp_ref[0, halo - 1:halo, :].astype(F32)
    u_after = cn_ref[0, 0:1, :].astype(F32) * hn_ref[0, 0:1, :].astype(F32)
    u_before = jnp.where(i == 0, 0.0, u_before)
    u_after = jnp.where(i == last, 0.0, u_after)
    row = lax.broadcasted_iota(jnp.int32, u.shape, 0)
    u_prev = jnp.where(row == 0, u_before, pltpu.roll(u, 1, axis=0))
    u_next = jnp.where(row == tm - 1, u_after, pltpu.roll(u, tm - 1, axis=0))
    cw = cw_ref[...]
    conv = u_prev * cw[0:1] + u * cw[1:2] + u_next * cw[2:3] + cbias_ref[...]
    y_conv = cb_ref[0].astype(F32) * conv * _silu(cz_ref[0].astype(F32))
    y = jnp.concatenate([y_conv.astype(BF16), yr_ref[0], yn_ref[0]], axis=-1)
    z = DEEPNORM_ALPHA * x_ref[0] + gate_ref[0] * _dot(y, w_ref[...])
    mu = jnp.mean(z, axis=-1, keepdims=True)
    dlt = z - mu
    var = jnp.mean(dlt * dlt, axis=-1, keepdims=True)
    o_ref[0] = dlt * lax.rsqrt(var + LN_EPS) * lg_ref[...] + lb_ref[...]


def _output(x, gate, p, y_ret, y_na, conv_w, conv_b, w_out_bf, ln_g, ln_b):
    b, n, _ = x.shape
    tm = min(ROW_TILE, n)
    halo = 16
    per = tm // halo
    nh = n // halo

    def conv(cb):
        return pl.BlockSpec((1, tm, CONV_W), lambda bi, i: (bi, i, cb))

    def before(cb):
        return pl.BlockSpec((1, halo, CONV_W), lambda bi, i: (bi, jnp.maximum(i * per - 1, 0), cb))

    def after(cb):
        return pl.BlockSpec((1, halo, CONV_W), lambda bi, i: (bi, jnp.minimum((i + 1) * per, nh - 1), cb))

    def const(shape):
        return pl.BlockSpec(shape, lambda bi, i: (0,) * len(shape))

    return pl.pallas_call(
        _out_kernel,
        grid=(b, n // tm),
        in_specs=[pl.BlockSpec((1, tm, D_MODEL), lambda bi, i: (bi, i, 0)),
                  pl.BlockSpec((1, 1, D_MODEL), lambda bi, i: (bi, 0, 0)),
                  conv(CB_CH), conv(CB_CB), conv(CB_CC), conv(CB_CZ),
                  before(CB_CH), before(CB_CC), after(CB_CH), after(CB_CC),
                  pl.BlockSpec((1, tm, RET_W), lambda bi, i: (bi, i, 0)),
                  pl.BlockSpec((1, tm, NA_W), lambda bi, i: (bi, i, 0)),
                  const((3, CONV_W)), const((1, CONV_W)), const((MIX_W, D_MODEL)),
                  const((1, D_MODEL)), const((1, D_MODEL))],
        out_specs=pl.BlockSpec((1, tm, D_MODEL), lambda bi, i: (bi, i, 0)),
        out_shape=jax.ShapeDtypeStruct((b, n, D_MODEL), F32),
        compiler_params=pltpu.CompilerParams(dimension_semantics=("parallel", "parallel")),
        name="output",
    )(x, gate, p, p, p, p, p, p, p, p, y_ret, y_na, conv_w, conv_b.reshape(1, CONV_W), w_out_bf,
      ln_g.reshape(1, D_MODEL), ln_b.reshape(1, D_MODEL))


def kernel(x, c, ctx, c_ctx, w_mod, b_mod, w_in, conv_w, conv_b, ret_decay, na_rpb, w_out, ln_g, ln_b):
    b, n, d = x.shape
    act_t = jnp.pad(jnp.concatenate([c, c_ctx[None]], axis=0).T, ((0, 0), (0, 5)))
    mod = _modulation(act_t, w_mod, b_mod)
    tables = _rope_tables(n)
    w_in_bf = w_in.astype(BF16)
    w_out_bf = w_out.astype(BF16)
    xc = ctx
    for l in range(DEPTH):
        need_ctx = l < DEPTH - 1
        shift, scale, gate = (mod[l, :b, None, j * d:(j + 1) * d] for j in range(3))
        shift_c, scale_c, gate_c = (jnp.broadcast_to(mod[l, b, j * d:(j + 1) * d], (b, 1, d)) for j in range(3))
        p = _projection(x, shift, scale, w_in_bf[l], tables, PROJ_W)
        pc = _projection(xc, shift_c, scale_c, w_in_bf[l], None, PROJ_W if need_ctx else KV_W)
        y_ret = _retention(p, pc, ret_decay[l], init_state=True)
        y_na = _neighbourhood(p, pc, _na_bias_tables(na_rpb[l], n // GRID_W))
        x_new = _output(x, gate, p, y_ret, y_na, conv_w[l], conv_b[l], w_out_bf[l], ln_g[l], ln_b[l])
        if need_ctx:
            yc_ret = _retention(pc, None, ret_decay[l], init_state=False)
            yc_na = _ctx_attention(pc)
            xc = _output(xc, gate_c, pc, yc_ret, yc_na, conv_w[l], conv_b[l], w_out_bf[l], ln_g[l], ln_b[l])
        x = x_new
    return x
```

```python
import functools

import numpy as np
import jax
import jax.numpy as jnp
from jax import lax
from jax.experimental import pallas as pl
from jax.experimental.pallas import tpu as pltpu

D_MODEL = 1024
DEPTH = 2
GRID_W = 64
HEAD_DIM = 64
CONV_W = 256
RET_HEADS = 6
RET_W = RET_HEADS * HEAD_DIM
NA_HEADS = 6
NA_W = NA_HEADS * HEAD_DIM
MIX_W = CONV_W + RET_W + NA_W
RET_CHUNK = 128
WIN_H = 8
WIN_W = 16
ROPE_BASE = 10000.0
LN_EPS = 1e-5
DEEPNORM_ALPHA = (2 * DEPTH) ** 0.25
PROJ_SPLITS = (RET_W, RET_W, NA_W, NA_W, RET_W, RET_W, NA_W, NA_W, CONV_W, CONV_W, CONV_W, CONV_W)
PROJ_W = sum(PROJ_SPLITS)
KV_W = 2 * RET_W + 2 * NA_W

LANES = 128
PAIRS = RET_HEADS // 2
CB_RK, CB_RV, CB_NK, CB_NV, CB_RQ, CB_RG, CB_NQ, CB_NG = 0, 3, 6, 9, 12, 15, 18, 21
CB_CH, CB_CB, CB_CC, CB_CZ = 12, 13, 14, 15
QK_SCALE = HEAD_DIM ** -0.5
NEG = -1e30
LOG2E = 1.4426950408889634
NA_ROWS = 4
NA_TOK = NA_ROWS * GRID_W
ROW_TILE = 512

F32 = jnp.float32
BF16 = jnp.bfloat16


def _silu(v):
    return v * jax.nn.sigmoid(v)


def _dot(a, b):
    return jnp.dot(a, b, preferred_element_type=F32)


def _dot_nt(a, b):
    return lax.dot_general(a, b, (((1,), (1,)), ((), ())), preferred_element_type=F32)


def _dot_tn(a, b):
    return lax.dot_general(a, b, (((0,), (0,)), ((), ())), preferred_element_type=F32)


def _lane_is_head0(shape):
    return lax.broadcasted_iota(jnp.int32, shape, len(shape) - 1) < HEAD_DIM


def _mod_kernel(act_ref, w_ref, b_ref, o_ref):
    a = _silu(act_ref[...])
    w = w_ref[0]
    bias = b_ref[0]
    for r in range(3):
        o_ref[0, r:r + 1, :] = jnp.sum(a[:, r:r + 1] * w, axis=0, keepdims=True) + bias
    o_ref[0, 3:8, :] = jnp.zeros((5, w.shape[1]), F32)


def _modulation(act_t, w_mod, b_mod):
    tn = 512
    n = w_mod.shape[-1]
    return pl.pallas_call(
        _mod_kernel,
        grid=(DEPTH, n // tn),
        in_specs=[pl.BlockSpec((D_MODEL, 8), lambda l, j: (0, 0)),
                  pl.BlockSpec((1, D_MODEL, tn), lambda l, j: (l, 0, j)),
                  pl.BlockSpec((1, 1, tn), lambda l, j: (l, 0, j))],
        out_specs=pl.BlockSpec((1, 8, tn), lambda l, j: (l, 0, j)),
        out_shape=jax.ShapeDtypeStruct((DEPTH, 8, n), F32),
        name="modulation",
    )(act_t, w_mod, b_mod.reshape(DEPTH, 1, n))


def _rope_tables(n):
    t = jnp.arange(n)
    row = (t // GRID_W).astype(F32)
    col = (t % GRID_W).astype(F32)
    nf = HEAD_DIM // 4
    inv = ROPE_BASE ** (-jnp.arange(nf, dtype=F32) / nf)
    ang_r, ang_c = row[:, None] * inv, col[:, None] * inv
    ang = jnp.concatenate([ang_r, ang_r, ang_c, ang_c], axis=-1)
    ang = jnp.concatenate([ang, ang], axis=-1)
    first_half = (np.arange(LANES) % 32) < 16
    cos, sin = jnp.cos(ang), jnp.sin(ang)
    s_up = jnp.where(first_half[None], -sin, 0.0)
    s_dn = jnp.where(first_half[None], 0.0, sin)
    return cos, s_up, s_dn


def _proj_kernel(x_ref, shift_ref, scale_ref, w_ref, *rest, rope, n_cols):
    if rope:
        cos_ref, up_ref, dn_ref, o_ref = rest
    else:
        (o_ref,) = rest
    h = (x_ref[0] * (1.0 + scale_ref[0]) + shift_ref[0]).astype(BF16)
    off = 0
    for seg, width in enumerate(PROJ_SPLITS):
        if off >= n_cols:
            break
        acc = _dot(h, w_ref[:, off:off + width])
        if seg in (0, 4):
            if rope:
                cos, up, dn = cos_ref[...], up_ref[...], dn_ref[...]
                tiles = []
                for j in range(width // LANES):
                    v = acc[:, j * LANES:(j + 1) * LANES]
                    tiles.append(v * cos + pltpu.roll(v, LANES - 16, axis=1) * up
                                 + pltpu.roll(v, 16, axis=1) * dn)
                acc = jnp.concatenate(tiles, axis=1)
            if seg == 0:
                acc = acc * QK_SCALE
        if seg == 6:
            acc = acc * (QK_SCALE * LOG2E)
        o_ref[0, :, off:off + width] = acc.astype(BF16)
        off += width


def _projection(x, shift, scale, w_bf, tables, n_cols):
    b, n, _ = x.shape
    tm = min(ROW_TILE, n)
    rope = tables is not None
    in_specs = [pl.BlockSpec((1, tm, D_MODEL), lambda bi, i: (bi, i, 0)),
                pl.BlockSpec((1, 1, D_MODEL), lambda bi, i: (bi, 0, 0)),
                pl.BlockSpec((1, 1, D_MODEL), lambda bi, i: (bi, 0, 0)),
                pl.BlockSpec((D_MODEL, n_cols), lambda bi, i: (0, 0))]
    args = [x, shift, scale, w_bf]
    if rope:
        in_specs += [pl.BlockSpec((tm, LANES), lambda bi, i: (i, 0))] * 3
        args += list(tables)
    return pl.pallas_call(
        functools.partial(_proj_kernel, rope=rope, n_cols=n_cols),
        grid=(b, n // tm),
        in_specs=in_specs,
        out_specs=pl.BlockSpec((1, tm, n_cols), lambda bi, i: (bi, i, 0)),
        out_shape=jax.ShapeDtypeStruct((b, n, n_cols), BF16),
        compiler_params=pltpu.CompilerParams(dimension_semantics=("parallel", "parallel")),
        name="projection_rope" if rope else "projection_ctx",
    )(*args)


def _log_sigmoid(v):
    return jnp.minimum(v, 0.0) - jnp.log1p(jnp.exp(-jnp.abs(v)))


def _block_diag(m):
    r = lax.broadcasted_iota(jnp.int32, m.shape, 0) < HEAD_DIM
    c = lax.broadcasted_iota(jnp.int32, m.shape, 1) < HEAD_DIM
    return jnp.where(r == c, m, 0.0)


def _ret_kernel(*refs, init_state):
    refs = list(refs)
    dec_ref, q_ref, k_ref, v_ref, g_ref = refs[:5]
    refs = refs[5:]
    if init_state:
        kc_ref, vc_ref = refs[:2]
        refs = refs[2:]
    o_ref, s_ref, wq_ref, wk_ref, d_ref = refs
    c = RET_CHUNK
    n_chunks = q_ref.shape[1] // c
    head0 = _lane_is_head0((c, LANES))
    lg_f = _log_sigmoid(dec_ref[0, 0])
    lg_b = _log_sigmoid(dec_ref[1, 0])
    i = lax.broadcasted_iota(jnp.int32, (c, LANES), 0).astype(F32)
    wq_ref[:, :LANES] = jnp.exp((i + 1.0) * lg_f)
    wq_ref[:, LANES:] = jnp.exp((c - i) * lg_b)
    wk_ref[:, :LANES] = jnp.exp((c - 1.0 - i) * lg_f)
    wk_ref[:, LANES:] = jnp.exp(i * lg_b)
    gc_f = jnp.exp(float(c) * lg_f)
    gc_b = jnp.exp(float(c) * lg_b)
    diff = (lax.broadcasted_iota(jnp.int32, (c, c), 0) - lax.broadcasted_iota(jnp.int32, (c, c), 1)).astype(F32)
    lower = diff >= 0
    for hh in range(2):
        lf = lg_f[:, hh * HEAD_DIM:hh * HEAD_DIM + 1]
        lb = lg_b[:, hh * HEAD_DIM:hh * HEAD_DIM + 1]
        d_ref[:, hh * c:(hh + 1) * c] = jnp.where(lower, jnp.exp(jnp.where(lower, diff, 0.0) * lf),
                                                  jnp.exp(jnp.where(lower, 0.0, -diff) * lb))
    if init_state:
        lc = kc_ref.shape[1]
        m = lax.broadcasted_iota(jnp.int32, (lc, LANES), 0).astype(F32)
        kcf = kc_ref[0].astype(F32)
        s_f0 = _block_diag(_dot_tn((kcf * jnp.exp((lc - 1.0 - m) * lg_f)).astype(BF16), vc_ref[0]))
        s_b0 = _block_diag(_dot_tn((kcf * jnp.exp(m * lg_b)).astype(BF16), vc_ref[0]))
    else:
        s_f0 = s_b0 = jnp.zeros((LANES, LANES), F32)

    def chunk(n):
        return pl.ds(pl.multiple_of(n * c, c), c)

    def fwd_state(n, s):
        s_ref[n, :LANES, :] = s.astype(BF16)
        kw = (k_ref[0, chunk(n), :].astype(F32) * wk_ref[:, :LANES]).astype(BF16)
        return s * gc_f + _block_diag(_dot_tn(kw, v_ref[0, chunk(n), :]))

    def bwd_state(t, s):
        n = n_chunks - 1 - t
        s_ref[n, LANES:, :] = s.astype(BF16)
        kw = (k_ref[0, chunk(n), :].astype(F32) * wk_ref[:, LANES:]).astype(BF16)
        return s * gc_b + _block_diag(_dot_tn(kw, v_ref[0, chunk(n), :]))

    lax.fori_loop(0, n_chunks, fwd_state, s_f0)
    lax.fori_loop(0, n_chunks, bwd_state, s_b0)

    def body(n, carry):
        rows = chunk(n)
        q, k, v = q_ref[0, rows, :], k_ref[0, rows, :], v_ref[0, rows, :]
        zero = jnp.zeros_like(k)
        kcat = jnp.concatenate([jnp.where(head0, k, zero), jnp.where(head0, zero, k)], axis=0)
        vcat = jnp.concatenate([jnp.where(head0, v, zero), jnp.where(head0, zero, v)], axis=0)
        scores = _dot_nt(q, kcat) * d_ref[...]
        qf = q.astype(F32)
        qw = jnp.concatenate([qf * wq_ref[:, :LANES], qf * wq_ref[:, LANES:]], axis=1).astype(BF16)
        o = _dot(scores.astype(BF16), vcat) + _dot(qw, s_ref[n])
        inv_n = 1.0 / HEAD_DIM
        mu = jnp.where(head0,
                       jnp.sum(jnp.where(head0, o, 0.0), axis=-1, keepdims=True),
                       jnp.sum(jnp.where(head0, 0.0, o), axis=-1, keepdims=True)) * inv_n
        dlt = o - mu
        sq = dlt * dlt
        var = jnp.where(head0,
                        jnp.sum(jnp.where(head0, sq, 0.0), axis=-1, keepdims=True),
                        jnp.sum(jnp.where(head0, 0.0, sq), axis=-1, keepdims=True)) * inv_n
        y = dlt * lax.rsqrt(var + LN_EPS) * _silu(g_ref[0, rows, :].astype(F32))
        o_ref[0, rows, :] = y.astype(o_ref.dtype)
        return carry

    lax.fori_loop(0, n_chunks, body, 0, unroll=2)


def _retention(p, pc, ret_decay, init_state):
    b, n, _ = p.shape
    dec = jnp.repeat(ret_decay.astype(F32), HEAD_DIM, axis=-1).reshape(2, PAIRS, 1, LANES)

    def col(cb):
        return pl.BlockSpec((1, n, LANES), lambda bi, pi: (bi, 0, cb + pi))

    in_specs = [pl.BlockSpec((2, 1, 1, LANES), lambda bi, pi: (0, pi, 0, 0)),
                col(CB_RQ), col(CB_RK), col(CB_RV), col(CB_RG)]
    args = [dec, p, p, p, p]
    if init_state:
        lc = pc.shape[1]
        in_specs += [pl.BlockSpec((1, lc, LANES), lambda bi, pi: (bi, 0, CB_RK + pi)),
                     pl.BlockSpec((1, lc, LANES), lambda bi, pi: (bi, 0, CB_RV + pi))]
        args += [pc, pc]
    c = RET_CHUNK
    return pl.pallas_call(
        functools.partial(_ret_kernel, init_state=init_state),
        grid=(b, PAIRS),
        in_specs=in_specs,
        out_specs=col(0),
        out_shape=jax.ShapeDtypeStruct((b, n, RET_W), BF16),
        scratch_shapes=[pltpu.VMEM((n // c, 2 * LANES, LANES), BF16),
                        pltpu.VMEM((c, 2 * LANES), F32),
                        pltpu.VMEM((c, 2 * LANES), F32),
                        pltpu.VMEM((c, 2 * c), F32)],
        compiler_params=pltpu.CompilerParams(dimension_semantics=("parallel", "parallel")),
        name="retention",
    )(*args)


def _na_bias_plan(rows):
    nblk = rows // NA_ROWS
    win_h = min(WIN_H, rows)
    plan = np.full((3, NA_ROWS, 3, NA_ROWS), -1, np.int64)
    for kind, blk in enumerate((0, 1, nblk - 1)):
        for qr in range(NA_ROWS):
            r = blk * NA_ROWS + qr
            rs = min(max(r - win_h // 2, 0), rows - win_h)
            for kb in range(3):
                if not 0 <= blk + kb - 1 <= nblk - 1:
                    continue
                for krl in range(NA_ROWS):
                    kr = (blk + kb - 1) * NA_ROWS + krl
                    if rs <= kr < rs + win_h:
                        plan[kind, qr, kb, krl] = kr - r + WIN_H - 1
    return plan


def _na_bias_kernel(rpb_ref, o_ref, t_ref, *, plan):
    h = pl.program_id(0)
    n_dr, n_dc = 2 * WIN_H - 1, 2 * WIN_W - 1
    qc = lax.broadcasted_iota(jnp.int32, (GRID_W, LANES), 0)
    lane = lax.broadcasted_iota(jnp.int32, (GRID_W, LANES), 1)
    kc = lane % GRID_W
    d_col = kc - qc + (WIN_W - 1)
    cs = jnp.clip(qc - WIN_W // 2, 0, GRID_W - WIN_W)
    col_ok = (kc >= cs) & (kc < cs + WIN_W)
    neg = jnp.full((GRID_W, LANES), NEG, F32)
    for dr in range(n_dr):
        t = neg
        for dc in range(n_dc):
            t = jnp.where(d_col == dc, rpb_ref[(h * n_dr + dr) * n_dc + dc] * LOG2E, t)
        t_ref[dr] = jnp.where(col_ok, t, neg)
    first = lane < GRID_W
    for kind in range(3):
        for qr in range(NA_ROWS):
            for kb in range(3):
                for pr in range(NA_ROWS // 2):
                    ia, ib = (int(plan[kind, qr, kb, 2 * pr + s]) for s in range(2))
                    a = t_ref[ia] if ia >= 0 else neg
                    b = t_ref[ib] if ib >= 0 else neg
                    col0 = kb * NA_TOK + pr * LANES
                    o_ref[kind, 0, qr * GRID_W:(qr + 1) * GRID_W, col0:col0 + LANES] = jnp.where(first, a, b)


def _na_bias_tables(rpb, rows):
    n_dr, n_dc = 2 * WIN_H - 1, 2 * WIN_W - 1
    return pl.pallas_call(
        functools.partial(_na_bias_kernel, plan=_na_bias_plan(rows)),
        grid=(NA_HEADS,),
        in_specs=[pl.BlockSpec(memory_space=pltpu.SMEM)],
        out_specs=pl.BlockSpec((3, 1, NA_TOK, 3 * NA_TOK), lambda h: (0, h, 0, 0)),
        out_shape=jax.ShapeDtypeStruct((3, NA_HEADS, NA_TOK, 3 * NA_TOK), F32),
        scratch_shapes=[pltpu.VMEM((n_dr, GRID_W, LANES), F32)],
        name="na_bias",
    )(rpb.astype(F32).reshape(NA_HEADS * n_dr * n_dc))


def _na_kernel(q_ref, kp_ref, kc_ref, kn_ref, vp_ref, vc_ref, vn_ref, kx_ref, vx_ref, g_ref, bias_ref, o_ref):
    head0 = _lane_is_head0((NA_TOK, LANES))
    for pi in range(PAIRS):
        cols = slice(pi * LANES, (pi + 1) * LANES)
        q = q_ref[0, :, cols]
        zero = jnp.zeros_like(q)
        keys = (kp_ref[0, :, cols], kc_ref[0, :, cols], kn_ref[0, :, cols], kx_ref[0, :, cols])
        vals = (vp_ref[0, :, cols], vc_ref[0, :, cols], vn_ref[0, :, cols], vx_ref[0, :, cols])
        probs, inv = [], []
        for hh in range(2):
            qh = jnp.where(head0, q, zero) if hh == 0 else jnp.where(head0, zero, q)
            s = [_dot_nt(qh, keys[j]) + bias_ref[0, 2 * pi + hh, :, j * NA_TOK:(j + 1) * NA_TOK] for j in range(3)]
            s.append(_dot_nt(qh, keys[3]))
            m = jnp.max(s[0], axis=-1, keepdims=True)
            for sj in s[1:]:
                m = jnp.maximum(m, jnp.max(sj, axis=-1, keepdims=True))
            den = None
            for sj in s:
                pj = jnp.exp2(sj - m)
                dj = jnp.sum(pj, axis=-1, keepdims=True)
                den = dj if den is None else den + dj
                probs.append(pj.astype(BF16))
            inv.append(1.0 / den)
        v_cat = jnp.concatenate([jnp.where(head0, v, zero) for v in vals]
                                + [jnp.where(head0, zero, v) for v in vals], axis=0)
        o = _dot(jnp.concatenate(probs, axis=1), v_cat) * jnp.where(head0, inv[0], inv[1])
        o_ref[0, :, cols] = (o * _silu(g_ref[0, :, cols].astype(F32))).astype(o_ref.dtype)


def _neighbourhood(p, pc, bias):
    b, n, _ = p.shape
    nblk = n // NA_TOK
    grp = NA_W // LANES

    def blk(cb, shift):
        return pl.BlockSpec((1, NA_TOK, NA_W),
                            lambda bi, i: (bi, jnp.clip(i + shift, 0, nblk - 1), cb // grp))

    lc = pc.shape[1]

    def ctx(cb):
        return pl.BlockSpec((1, lc, NA_W), lambda bi, i: (bi, 0, cb // grp))

    def bias_index(bi, i):
        kind = jnp.where(i == 0, 0, jnp.where(i == nblk - 1, 2, 1))
        return (kind, 0, 0, 0)

    return pl.pallas_call(
        _na_kernel,
        grid=(b, nblk),
        in_specs=[blk(CB_NQ, 0), blk(CB_NK, -1), blk(CB_NK, 0), blk(CB_NK, 1),
                  blk(CB_NV, -1), blk(CB_NV, 0), blk(CB_NV, 1), ctx(CB_NK), ctx(CB_NV), blk(CB_NG, 0),
                  pl.BlockSpec((1, NA_HEADS, NA_TOK, 3 * NA_TOK), bias_index)],
        out_specs=pl.BlockSpec((1, NA_TOK, NA_W), lambda bi, i: (bi, i, 0)),
        out_shape=jax.ShapeDtypeStruct((b, n, NA_W), BF16),
        compiler_params=pltpu.CompilerParams(dimension_semantics=("parallel", "arbitrary")),
        name="neighbourhood",
    )(p, p, p, p, p, p, p, pc, pc, p, bias)


def _ctx_attn_kernel(q_ref, k_ref, v_ref, g_ref, o_ref):
    q, k, v = q_ref[0], k_ref[0], v_ref[0]
    head0 = _lane_is_head0(q.shape)
    zero = jnp.zeros_like(q)
    outs = []
    for hh in range(2):
        qh = jnp.where(head0, q, zero) if hh == 0 else jnp.where(head0, zero, q)
        s = _dot_nt(qh, k)
        p = jnp.exp2(s - jnp.max(s, axis=-1, keepdims=True))
        outs.append(_dot(p.astype(BF16), v) / jnp.sum(p, axis=-1, keepdims=True))
    o = jnp.where(head0, outs[0], outs[1])
    o_ref[0] = (o * _silu(g_ref[0].astype(F32))).astype(o_ref.dtype)


def _ctx_attention(pc):
    b, lc, _ = pc.shape

    def col(cb):
        return pl.BlockSpec((1, lc, LANES), lambda bi, pi: (bi, 0, cb + pi))

    return pl.pallas_call(
        _ctx_attn_kernel,
        grid=(b, PAIRS),
        in_specs=[col(CB_NQ), col(CB_NK), col(CB_NV), col(CB_NG)],
        out_specs=col(0),
        out_shape=jax.ShapeDtypeStruct((b, lc, NA_W), BF16),
        name="context_attention",
    )(pc, pc, pc, pc)


def _out_kernel(x_ref, gate_ref, ch_ref, cb_ref, cc_ref, cz_ref, hp_ref, cp_ref, hn_ref, cn_ref,
                yr_ref, yn_ref, cw_ref, cbias_ref, w_ref, lg_ref, lb_ref, o_ref):
    i = pl.program_id(1)
    last = pl.num_programs(1) - 1
    u = cc_ref[0].astype(F32) * ch_ref[0].astype(F32)
    tm = u.shape[0]
    halo = hp_ref.shape[1]
    u_before = cp_ref[0, halo - 1:halo, :].astype(F32) * hp_ref[0, halo - 1:halo, :].astype(F32)
    u_after = cn_ref[0, 0:1, :].astype(F32) * hn_ref[0, 0:1, :].astype(F32)
    u_before = jnp.where(i == 0, 0.0, u_before)
    u_after = jnp.where(i == last, 0.0, u_after)
    row = lax.broadcasted_iota(jnp.int32, u.shape, 0)
    u_prev = jnp.where(row == 0, u_before, pltpu.roll(u, 1, axis=0))
    u_next = jnp.where(row == tm - 1, u_after, pltpu.roll(u, tm - 1, axis=0))
    cw = cw_ref[...]
    conv = u_prev * cw[0:1] + u * cw[1:2] + u_next * cw[2:3] + cbias_ref[...]
    y_conv = cb_ref[0].astype(F32) * conv * _silu(cz_ref[0].astype(F32))
    y = jnp.concatenate([y_conv.astype(BF16), yr_ref[0], yn_ref[0]], axis=-1)
    z = DEEPNORM_ALPHA * x_ref[0] + gate_ref[0] * _dot(y, w_ref[...])
    mu = jnp.mean(z, axis=-1, keepdims=True)
    dlt = z - mu
    var = jnp.mean(dlt * dlt, axis=-1, keepdims=True)
    o_ref[0] = dlt * lax.rsqrt(var + LN_EPS) * lg_ref[...] + lb_ref[...]


def _output(x, gate, p, y_ret, y_na, conv_w, conv_b, w_out_bf, ln_g, ln_b):
    b, n, _ = x.shape
    tm = min(ROW_TILE, n)
    halo = 16
    per = tm // halo
    nh = n // halo

    def conv(cb):
        return pl.BlockSpec((1, tm, CONV_W), lambda bi, i: (bi, i, cb))

    def before(cb):
        return pl.BlockSpec((1, halo, CONV_W), lambda bi, i: (bi, jnp.maximum(i * per - 1, 0), cb))

    def after(cb):
        return pl.BlockSpec((1, halo, CONV_W), lambda bi, i: (bi, jnp.minimum((i + 1) * per, nh - 1), cb))

    def const(shape):
        return pl.BlockSpec(shape, lambda bi, i: (0,) * len(shape))

    return pl.pallas_call(
        _out_kernel,
        grid=(b, n // tm),
        in_specs=[pl.BlockSpec((1, tm, D_MODEL), lambda bi, i: (bi, i, 0)),
                  pl.BlockSpec((1, 1, D_MODEL), lambda bi, i: (bi, 0, 0)),
                  conv(CB_CH), conv(CB_CB), conv(CB_CC), conv(CB_CZ),
                  before(CB_CH), before(CB_CC), after(CB_CH), after(CB_CC),
                  pl.BlockSpec((1, tm, RET_W), lambda bi, i: (bi, i, 0)),
                  pl.BlockSpec((1, tm, NA_W), lambda bi, i: (bi, i, 0)),
                  const((3, CONV_W)), const((1, CONV_W)), const((MIX_W, D_MODEL)),
                  const((1, D_MODEL)), const((1, D_MODEL))],
        out_specs=pl.BlockSpec((1, tm, D_MODEL), lambda bi, i: (bi, i, 0)),
        out_shape=jax.ShapeDtypeStruct((b, n, D_MODEL), F32),
        compiler_params=pltpu.CompilerParams(dimension_semantics=("parallel", "parallel")),
        name="output",
    )(x, gate, p, p, p, p, p, p, p, p, y_ret, y_na, conv_w, conv_b.reshape(1, CONV_W), w_out_bf,
      ln_g.reshape(1, D_MODEL), ln_b.reshape(1, D_MODEL))


def kernel(x, c, ctx, c_ctx, w_mod, b_mod, w_in, conv_w, conv_b, ret_decay, na_rpb, w_out, ln_g, ln_b):
    b, n, d = x.shape
    act_t = jnp.pad(jnp.concatenate([c, c_ctx[None]], axis=0).T, ((0, 0), (0, 5)))
    mod = _modulation(act_t, w_mod, b_mod)
    tables = _rope_tables(n)
    w_in_bf = w_in.astype(BF16)
    w_out_bf = w_out.astype(BF16)
    xc = ctx
    for l in range(DEPTH):
        need_ctx = l < DEPTH - 1
        shift, scale, gate = (mod[l, :b, None, j * d:(j + 1) * d] for j in range(3))
        shift_c, scale_c, gate_c = (jnp.broadcast_to(mod[l, b, j * d:(j + 1) * d], (b, 1, d)) for j in range(3))
        p = _projection(x, shift, scale, w_in_bf[l], tables, PROJ_W)
        pc = _projection(xc, shift_c, scale_c, w_in_bf[l], None, PROJ_W if need_ctx else KV_W)
        y_ret = _retention(p, pc, ret_decay[l], init_state=True)
        y_na = _neighbourhood(p, pc, _na_bias_tables(na_rpb[l], n // GRID_W))
        x_new = _output(x, gate, p, y_ret, y_na, conv_w[l], conv_b[l], w_out_bf[l], ln_g[l], ln_b[l])
        if need_ctx:
            yc_ret = _retention(pc, None, ret_decay[l], init_state=False)
            yc_na = _ctx_attention(pc)
            xc = _output(xc, gate_c, pc, yc_ret, yc_na, conv_w[l], conv_b[l], w_out_bf[l], ln_g[l], ln_b[l])
        x = x_new
    return x
```

```python
import functools

import numpy as np
import jax
import jax.numpy as jnp
from jax import lax
from jax.experimental import pallas as pl
from jax.experimental.pallas import tpu as pltpu

D_MODEL = 1024
DEPTH = 2
GRID_W = 64
HEAD_DIM = 64
CONV_W = 256
RET_HEADS = 6
RET_W = RET_HEADS * HEAD_DIM
NA_HEADS = 6
NA_W = NA_HEADS * HEAD_DIM
MIX_W = CONV_W + RET_W + NA_W
RET_CHUNK = 128
WIN_H = 8
WIN_W = 16
ROPE_BASE = 10000.0
LN_EPS = 1e-5
DEEPNORM_ALPHA = (2 * DEPTH) ** 0.25
PROJ_SPLITS = (RET_W, RET_W, NA_W, NA_W, RET_W, RET_W, NA_W, NA_W, CONV_W, CONV_W, CONV_W, CONV_W)
PROJ_W = sum(PROJ_SPLITS)
KV_W = 2 * RET_W + 2 * NA_W

LANES = 128
PAIRS = RET_HEADS // 2
CB_RK, CB_RV, CB_NK, CB_NV, CB_RQ, CB_RG, CB_NQ, CB_NG = 0, 3, 6, 9, 12, 15, 18, 21
CB_CH, CB_CB, CB_CC, CB_CZ = 12, 13, 14, 15
QK_SCALE = HEAD_DIM ** -0.5
NEG = -1e30
LOG2E = 1.4426950408889634
NA_ROWS = 4
NA_TOK = NA_ROWS * GRID_W
RET_GROUP = 8
ROW_TILE = 512
PROJ_CHUNK = 512

F32 = jnp.float32
BF16 = jnp.bfloat16


def _silu(v):
    return v * jax.nn.sigmoid(v)


def _dot(a, b):
    return jnp.dot(a, b, preferred_element_type=F32)


def _dot_nt(a, b):
    return lax.dot_general(a, b, (((1,), (1,)), ((), ())), preferred_element_type=F32)


def _dot_tn(a, b):
    return lax.dot_general(a, b, (((0,), (0,)), ((), ())), preferred_element_type=F32)


def _lane_is_head0(shape):
    return lax.broadcasted_iota(jnp.int32, shape, len(shape) - 1) < HEAD_DIM


def _mod_kernel(act_ref, w_ref, b_ref, o_ref):
    a = _silu(act_ref[...])
    w = w_ref[0]
    bias = b_ref[0]
    for r in range(3):
        o_ref[0, r:r + 1, :] = jnp.sum(a[:, r:r + 1] * w, axis=0, keepdims=True) + bias
    o_ref[0, 3:8, :] = jnp.zeros((5, w.shape[1]), F32)


def _modulation(act_t, w_mod, b_mod):
    tn = 512
    n = w_mod.shape[-1]
    return pl.pallas_call(
        _mod_kernel,
        grid=(DEPTH, n // tn),
        in_specs=[pl.BlockSpec((D_MODEL, 8), lambda l, j: (0, 0)),
                  pl.BlockSpec((1, D_MODEL, tn), lambda l, j: (l, 0, j)),
                  pl.BlockSpec((1, 1, tn), lambda l, j: (l, 0, j))],
        out_specs=pl.BlockSpec((1, 8, tn), lambda l, j: (l, 0, j)),
        out_shape=jax.ShapeDtypeStruct((DEPTH, 8, n), F32),
        name="modulation",
    )(act_t, w_mod, b_mod.reshape(DEPTH, 1, n))


def _rope_tables(n):
    t = jnp.arange(n)
    row = (t // GRID_W).astype(F32)
    col = (t % GRID_W).astype(F32)
    nf = HEAD_DIM // 4
    inv = ROPE_BASE ** (-jnp.arange(nf, dtype=F32) / nf)
    ang_r, ang_c = row[:, None] * inv, col[:, None] * inv
    ang = jnp.concatenate([ang_r, ang_r, ang_c, ang_c], axis=-1)
    ang = jnp.concatenate([ang, ang], axis=-1)
    first_half = (np.arange(LANES) % 32) < 16
    cos, sin = jnp.cos(ang), jnp.sin(ang)
    s_up = jnp.where(first_half[None], -sin, 0.0)
    s_dn = jnp.where(first_half[None], 0.0, sin)
    return cos, s_up, s_dn


def _proj_kernel(x_ref, shift_ref, scale_ref, w_ref, *rest, rope, n_cols):
    if rope:
        cos_ref, up_ref, dn_ref, o_ref = rest
    else:
        (o_ref,) = rest
    h = (x_ref[0] * (1.0 + scale_ref[0]) + shift_ref[0]).astype(BF16)
    rotated = tuple(range(CB_RK, CB_RK + PAIRS)) + tuple(range(CB_RQ, CB_RQ + PAIRS))
    col_scale = {cb: QK_SCALE for cb in range(CB_RK, CB_RK + PAIRS)}
    col_scale.update({cb: QK_SCALE * LOG2E for cb in range(CB_NQ, CB_NQ + PAIRS)})
    for off in range(0, n_cols, PROJ_CHUNK):
        acc = _dot(h, w_ref[:, off:off + PROJ_CHUNK])
        tiles = []
        for j in range(PROJ_CHUNK // LANES):
            cb = off // LANES + j
            v = acc[:, j * LANES:(j + 1) * LANES]
            if rope and cb in rotated:
                v = (v * cos_ref[...] + pltpu.roll(v, LANES - 16, axis=1) * up_ref[...]
                     + pltpu.roll(v, 16, axis=1) * dn_ref[...])
            if cb in col_scale:
                v = v * col_scale[cb]
            tiles.append(v.astype(BF16))
        o_ref[0, :, off:off + PROJ_CHUNK] = jnp.concatenate(tiles, axis=1)


def _projection(x, shift, scale, w_bf, tables, n_cols):
    b, n, _ = x.shape
    tm = min(ROW_TILE, n)
    rope = tables is not None
    in_specs = [pl.BlockSpec((1, tm, D_MODEL), lambda bi, i: (bi, i, 0)),
                pl.BlockSpec((1, 1, D_MODEL), lambda bi, i: (bi, 0, 0)),
                pl.BlockSpec((1, 1, D_MODEL), lambda bi, i: (bi, 0, 0)),
                pl.BlockSpec((D_MODEL, n_cols), lambda bi, i: (0, 0))]
    args = [x, shift, scale, w_bf]
    if rope:
        in_specs += [pl.BlockSpec((tm, LANES), lambda bi, i: (i, 0))] * 3
        args += list(tables)
    return pl.pallas_call(
        functools.partial(_proj_kernel, rope=rope, n_cols=n_cols),
        grid=(b, n // tm),
        in_specs=in_specs,
        out_specs=pl.BlockSpec((1, tm, n_cols), lambda bi, i: (bi, i, 0)),
        out_shape=jax.ShapeDtypeStruct((b, n, n_cols), BF16),
        compiler_params=pltpu.CompilerParams(dimension_semantics=("parallel", "parallel")),
        name="projection_rope" if rope else "projection_ctx",
    )(*args)


def _log_sigmoid(v):
    return jnp.minimum(v, 0.0) - jnp.log1p(jnp.exp(-jnp.abs(v)))


def _block_diag(m):
    r = lax.broadcasted_iota(jnp.int32, m.shape, 0) < HEAD_DIM
    c = lax.broadcasted_iota(jnp.int32, m.shape, 1) < HEAD_DIM
    return jnp.where(r == c, m, 0.0)


def _head_mean(x, avg2):
    hi = x.astype(BF16)
    lo = (x - hi.astype(F32)).astype(BF16)
    return _dot(jnp.concatenate([hi, lo], axis=1), avg2)


def _ret_kernel(*refs, init_state, group):
    refs = list(refs)
    dec_ref, q_ref, k_ref, v_ref, g_ref = refs[:5]
    refs = refs[5:]
    if init_state:
        kc_ref, vc_ref = refs[:2]
        refs = refs[2:]
    o_ref, s_ref, u_ref, wq_ref, wkt_ref, d_ref = refs
    c = RET_CHUNK
    n_chunks = q_ref.shape[1] // c
    n_groups = n_chunks // group
    head0 = _lane_is_head0((c, LANES))
    lg_f = _log_sigmoid(dec_ref[0, 0])
    lg_b = _log_sigmoid(dec_ref[1, 0])
    i = lax.broadcasted_iota(jnp.int32, (c, LANES), 0).astype(F32)
    wq_ref[:, :LANES] = jnp.exp((i + 1.0) * lg_f)
    wq_ref[:, LANES:] = jnp.exp((c - i) * lg_b)
    wkt_ref[:LANES, :] = jnp.exp((c - 1.0 - i) * lg_f).T
    wkt_ref[LANES:, :] = jnp.exp(i * lg_b).T
    gc_f = jnp.exp(float(c) * lg_f)
    gc_b = jnp.exp(float(c) * lg_b)
    diff = (lax.broadcasted_iota(jnp.int32, (c, c), 0) - lax.broadcasted_iota(jnp.int32, (c, c), 1)).astype(F32)
    lower = diff >= 0
    for hh in range(2):
        lf = lg_f[:, hh * HEAD_DIM:hh * HEAD_DIM + 1]
        lb = lg_b[:, hh * HEAD_DIM:hh * HEAD_DIM + 1]
        d_ref[:, hh * c:(hh + 1) * c] = jnp.where(lower, jnp.exp(jnp.where(lower, diff, 0.0) * lf),
                                                  jnp.exp(jnp.where(lower, 0.0, -diff) * lb))
    if init_state:
        lc = kc_ref.shape[1]
        m = lax.broadcasted_iota(jnp.int32, (lc, LANES), 0).astype(F32)
        kcf = kc_ref[0].astype(F32)
        s_f0 = _block_diag(_dot_tn((kcf * jnp.exp((lc - 1.0 - m) * lg_f)).astype(BF16), vc_ref[0]))
        s_b0 = _block_diag(_dot_tn((kcf * jnp.exp(m * lg_b)).astype(BF16), vc_ref[0]))
    else:
        s_f0 = s_b0 = jnp.zeros((LANES, LANES), F32)
    r2 = lax.broadcasted_iota(jnp.int32, (2 * LANES, LANES), 0) % LANES < HEAD_DIM
    c2 = lax.broadcasted_iota(jnp.int32, (2 * LANES, LANES), 1) < HEAD_DIM
    diag2 = r2 == c2
    avg2 = jnp.where(diag2, 1.0 / HEAD_DIM, 0.0).astype(BF16)

    def chunk(n):
        return pl.ds(pl.multiple_of(n * c, c), c)

    def increments(gi, carry):
        for j in range(group):
            n = gi * group + j
            kt = k_ref[0, chunk(n), :].astype(F32).T
            lhs = jnp.concatenate([kt * wkt_ref[:LANES, :], kt * wkt_ref[LANES:, :]], axis=0).astype(BF16)
            u_ref[n] = jnp.where(diag2, _dot(lhs, v_ref[0, chunk(n), :]), 0.0)
        return carry

    lax.fori_loop(0, n_groups, increments, 0)

    def fwd_scan(gi, s):
        for j in range(group):
            n = gi * group + j
            s_ref[n, :LANES, :] = s.astype(BF16)
            s = s * gc_f + u_ref[n, :LANES, :]
        return s

    def bwd_scan(gi, s):
        for j in range(group):
            n = n_chunks - 1 - (gi * group + j)
            s_ref[n, LANES:, :] = s.astype(BF16)
            s = s * gc_b + u_ref[n, LANES:, :]
        return s

    lax.fori_loop(0, n_groups, fwd_scan, s_f0)
    lax.fori_loop(0, n_groups, bwd_scan, s_b0)

    def body(gi, carry):
        outs = []
        for j in range(group):
            n = gi * group + j
            rows = chunk(n)
            q, k, v = q_ref[0, rows, :], k_ref[0, rows, :], v_ref[0, rows, :]
            zero = jnp.zeros_like(k)
            kcat = jnp.concatenate([jnp.where(head0, k, zero), jnp.where(head0, zero, k)], axis=0)
            vcat = jnp.concatenate([jnp.where(head0, v, zero), jnp.where(head0, zero, v)], axis=0)
            scores = _dot_nt(q, kcat) * d_ref[...]
            qf = q.astype(F32)
            qw = jnp.concatenate([qf * wq_ref[:, :LANES], qf * wq_ref[:, LANES:]], axis=1).astype(BF16)
            outs.append(_dot(scores.astype(BF16), vcat) + _dot(qw, s_ref[n]))
        o = jnp.concatenate(outs, axis=0)
        rows = pl.ds(pl.multiple_of(gi * (group * c), group * c), group * c)
        dlt = o - _head_mean(o, avg2)
        var = _head_mean(dlt * dlt, avg2)
        y = dlt * lax.rsqrt(var + LN_EPS) * _silu(g_ref[0, rows, :].astype(F32))
        o_ref[0, rows, :] = y.astype(o_ref.dtype)
        return carry

    lax.fori_loop(0, n_groups, body, 0)


def _retention(p, pc, ret_decay, init_state):
    b, n, _ = p.shape
    dec = jnp.repeat(ret_decay.astype(F32), HEAD_DIM, axis=-1).reshape(2, PAIRS, 1, LANES)

    def col(cb):
        return pl.BlockSpec((1, n, LANES), lambda bi, pi: (bi, 0, cb + pi))

    in_specs = [pl.BlockSpec((2, 1, 1, LANES), lambda bi, pi: (0, pi, 0, 0)),
                col(CB_RQ), col(CB_RK), col(CB_RV), col(CB_RG)]
    args = [dec, p, p, p, p]
    if init_state:
        lc = pc.shape[1]
        in_specs += [pl.BlockSpec((1, lc, LANES), lambda bi, pi: (bi, 0, CB_RK + pi)),
                     pl.BlockSpec((1, lc, LANES), lambda bi, pi: (bi, 0, CB_RV + pi))]
        args += [pc, pc]
    c = RET_CHUNK
    return pl.pallas_call(
        functools.partial(_ret_kernel, init_state=init_state, group=min(RET_GROUP, n // c)),
        grid=(b, PAIRS),
        in_specs=in_specs,
        out_specs=col(0),
        out_shape=jax.ShapeDtypeStruct((b, n, RET_W), BF16),
        scratch_shapes=[pltpu.VMEM((n // c, 2 * LANES, LANES), BF16),
                        pltpu.VMEM((n // c, 2 * LANES, LANES), F32),
                        pltpu.VMEM((c, 2 * LANES), F32),
                        pltpu.VMEM((2 * LANES, c), F32),
                        pltpu.VMEM((c, 2 * c), F32)],
        compiler_params=pltpu.CompilerParams(dimension_semantics=("parallel", "parallel")),
        name="retention",
    )(*args)


def _na_bias_plan(rows):
    nblk = rows // NA_ROWS
    win_h = min(WIN_H, rows)
    plan = np.full((3, NA_ROWS, 3, NA_ROWS), -1, np.int64)
    for kind, blk in enumerate((0, 1, nblk - 1)):
        for qr in range(NA_ROWS):
            r = blk * NA_ROWS + qr
            rs = min(max(r - win_h // 2, 0), rows - win_h)
            for kb in range(3):
                if not 0 <= blk + kb - 1 <= nblk - 1:
                    continue
                for krl in range(NA_ROWS):
                    kr = (blk + kb - 1) * NA_ROWS + krl
                    if rs <= kr < rs + win_h:
                        plan[kind, qr, kb, krl] = kr - r + WIN_H - 1
    return plan


def _na_bias_kernel(rpb_ref, o_ref, t_ref, *, plan):
    h = pl.program_id(0)
    n_dr, n_dc = 2 * WIN_H - 1, 2 * WIN_W - 1
    qc = lax.broadcasted_iota(jnp.int32, (GRID_W, LANES), 0)
    lane = lax.broadcasted_iota(jnp.int32, (GRID_W, LANES), 1)
    kc = lane % GRID_W
    d_col = kc - qc + (WIN_W - 1)
    cs = jnp.clip(qc - WIN_W // 2, 0, GRID_W - WIN_W)
    col_ok = (kc >= cs) & (kc < cs + WIN_W)
    neg = jnp.full((GRID_W, LANES), NEG, F32)
    for dr in range(n_dr):
        t = neg
        for dc in range(n_dc):
            t = jnp.where(d_col == dc, rpb_ref[(h * n_dr + dr) * n_dc + dc] * LOG2E, t)
        t_ref[dr] = jnp.where(col_ok, t, neg)
    first = lane < GRID_W
    for kind in range(3):
        for qr in range(NA_ROWS):
            for kb in range(3):
                for pr in range(NA_ROWS // 2):
                    ia, ib = (int(plan[kind, qr, kb, 2 * pr + s]) for s in range(2))
                    a = t_ref[ia] if ia >= 0 else neg
                    b = t_ref[ib] if ib >= 0 else neg
                    col0 = kb * NA_TOK + pr * LANES
                    o_ref[kind, 0, qr * GRID_W:(qr + 1) * GRID_W, col0:col0 + LANES] = jnp.where(first, a, b)


def _na_bias_tables(rpb, rows):
    n_dr, n_dc = 2 * WIN_H - 1, 2 * WIN_W - 1
    return pl.pallas_call(
        functools.partial(_na_bias_kernel, plan=_na_bias_plan(rows)),
        grid=(NA_HEADS,),
        in_specs=[pl.BlockSpec(memory_space=pltpu.SMEM)],
        out_specs=pl.BlockSpec((3, 1, NA_TOK, 3 * NA_TOK), lambda h: (0, h, 0, 0)),
        out_shape=jax.ShapeDtypeStruct((3, NA_HEADS, NA_TOK, 3 * NA_TOK), F32),
        scratch_shapes=[pltpu.VMEM((n_dr, GRID_W, LANES), F32)],
        name="na_bias",
    )(rpb.astype(F32).reshape(NA_HEADS * n_dr * n_dc))


def _fold_lanes(blocks, op):
    tiles = [blk[:, j:j + LANES] for blk in blocks for j in range(0, blk.shape[1], LANES)]
    acc = tiles[0]
    for t in tiles[1:]:
        acc = op(acc, t)
    return acc


def _na_kernel(q_ref, kp_ref, kc_ref, kn_ref, vp_ref, vc_ref, vn_ref, kx_ref, vx_ref, g_ref, bias_ref, o_ref):
    head0 = _lane_is_head0((NA_TOK, LANES))
    for pi in range(PAIRS):
        cols = slice(pi * LANES, (pi + 1) * LANES)
        q = q_ref[0, :, cols]
        zero = jnp.zeros_like(q)
        keys = (kp_ref[0, :, cols], kc_ref[0, :, cols], kn_ref[0, :, cols], kx_ref[0, :, cols])
        vals = (vp_ref[0, :, cols], vc_ref[0, :, cols], vn_ref[0, :, cols], vx_ref[0, :, cols])
        probs, inv = [], []
        for hh in range(2):
            qh = jnp.where(head0, q, zero) if hh == 0 else jnp.where(head0, zero, q)
            s = [_dot_nt(qh, keys[j]) + bias_ref[0, 2 * pi + hh, :, j * NA_TOK:(j + 1) * NA_TOK] for j in range(3)]
            s.append(_dot_nt(qh, keys[3]))
            m = jnp.max(_fold_lanes(s, jnp.maximum), axis=-1, keepdims=True)
            p = [jnp.exp2(sj - m) for sj in s]
            inv.append(1.0 / jnp.sum(_fold_lanes(p, jnp.add), axis=-1, keepdims=True))
            probs += [pj.astype(BF16) for pj in p]
        v_cat = jnp.concatenate([jnp.where(head0, v, zero) for v in vals]
                                + [jnp.where(head0, zero, v) for v in vals], axis=0)
        o = _dot(jnp.concatenate(probs, axis=1), v_cat) * jnp.where(head0, inv[0], inv[1])
        o_ref[0, :, cols] = (o * _silu(g_ref[0, :, cols].astype(F32))).astype(o_ref.dtype)


def _neighbourhood(p, pc, bias):
    b, n, _ = p.shape
    nblk = n // NA_TOK
    grp = NA_W // LANES

    def blk(cb, shift):
        return pl.BlockSpec((1, NA_TOK, NA_W),
                            lambda bi, i: (bi, jnp.clip(i + shift, 0, nblk - 1), cb // grp))

    lc = pc.shape[1]

    def ctx(cb):
        return pl.BlockSpec((1, lc, NA_W), lambda bi, i: (bi, 0, cb // grp))

    def bias_index(bi, i):
        kind = jnp.where(i == 0, 0, jnp.where(i == nblk - 1, 2, 1))
        return (kind, 0, 0, 0)

    return pl.pallas_call(
        _na_kernel,
        grid=(b, nblk),
        in_specs=[blk(CB_NQ, 0), blk(CB_NK, -1), blk(CB_NK, 0), blk(CB_NK, 1),
                  blk(CB_NV, -1), blk(CB_NV, 0), blk(CB_NV, 1), ctx(CB_NK), ctx(CB_NV), blk(CB_NG, 0),
                  pl.BlockSpec((1, NA_HEADS, NA_TOK, 3 * NA_TOK), bias_index)],
        out_specs=pl.BlockSpec((1, NA_TOK, NA_W), lambda bi, i: (bi, i, 0)),
        out_shape=jax.ShapeDtypeStruct((b, n, NA_W), BF16),
        compiler_params=pltpu.CompilerParams(dimension_semantics=("parallel", "arbitrary")),
        name="neighbourhood",
    )(p, p, p, p, p, p, p, pc, pc, p, bias)


def _ctx_attn_kernel(q_ref, k_ref, v_ref, g_ref, o_ref):
    q, k, v = q_ref[0], k_ref[0], v_ref[0]
    head0 = _lane_is_head0(q.shape)
    zero = jnp.zeros_like(q)
    outs = []
    for hh in range(2):
        qh = jnp.where(head0, q, zero) if hh == 0 else jnp.where(head0, zero, q)
        s = _dot_nt(qh, k)
        p = jnp.exp2(s - jnp.max(s, axis=-1, keepdims=True))
        outs.append(_dot(p.astype(BF16), v) / jnp.sum(p, axis=-1, keepdims=True))
    o = jnp.where(head0, outs[0], outs[1])
    o_ref[0] = (o * _silu(g_ref[0].astype(F32))).astype(o_ref.dtype)


def _ctx_attention(pc):
    b, lc, _ = pc.shape

    def col(cb):
        return pl.BlockSpec((1, lc, LANES), lambda bi, pi: (bi, 0, cb + pi))

    return pl.pallas_call(
        _ctx_attn_kernel,
        grid=(b, PAIRS),
        in_specs=[col(CB_NQ), col(CB_NK), col(CB_NV), col(CB_NG)],
        out_specs=col(0),
        out_shape=jax.ShapeDtypeStruct((b, lc, NA_W), BF16),
        name="context_attention",
    )(pc, pc, pc, pc)


def _out_kernel(x_ref, gate_ref, ch_ref, cb_ref, cc_ref, cz_ref, hp_ref, cp_ref, hn_ref, cn_ref,
                yr_ref, yn_ref, cw_ref, cbias_ref, w_ref, lg_ref, lb_ref, o_ref):
    i = pl.program_id(1)
    last = pl.num_programs(1) - 1
    u = cc_ref[0].astype(F32) * ch_ref[0].astype(F32)
    tm = u.shape[0]
    halo = hp_ref.shape[1]
    u_before = cp_ref[0, halo - 1:halo, :].astype(F32) * hp_ref[0, halo - 1:halo, :].astype(F32)
    u_after = cn_ref[0, 0:1, :].astype(F32) * hn_ref[0, 0:1, :].astype(F32)
    u_before = jnp.where(i == 0, 0.0, u_before)
    u_after = jnp.where(i == last, 0.0, u_after)
    row = lax.broadcasted_iota(jnp.int32, u.shape, 0)
    u_prev = jnp.where(row == 0, u_before, pltpu.roll(u, 1, axis=0))
    u_next = jnp.where(row == tm - 1, u_after, pltpu.roll(u, tm - 1, axis=0))
    cw = cw_ref[...]
    conv = u_prev * cw[0:1] + u * cw[1:2] + u_next * cw[2:3] + cbias_ref[...]
    y_conv = cb_ref[0].astype(F32) * conv * _silu(cz_ref[0].astype(F32))
    y = jnp.concatenate([y_conv.astype(BF16), yr_ref[0], yn_ref[0]], axis=-1)
    z = DEEPNORM_ALPHA * x_ref[0] + gate_ref[0] * _dot(y, w_ref[...])
    mu = jnp.mean(z, axis=-1, keepdims=True)
    dlt = z - mu
    var = jnp.mean(dlt * dlt, axis=-1, keepdims=True)
    o_ref[0] = dlt * lax.rsqrt(var + LN_EPS) * lg_ref[...] + lb_ref[...]


def _output(x, gate, p, y_ret, y_na, conv_w, conv_b, w_out_bf, ln_g, ln_b):
    b, n, _ = x.shape
    tm = min(ROW_TILE, n)
    halo = 16
    per = tm // halo
    nh = n // halo

    def conv(cb):
        return pl.BlockSpec((1, tm, CONV_W), lambda bi, i: (bi, i, cb))

    def before(cb):
        return pl.BlockSpec((1, halo, CONV_W), lambda bi, i: (bi, jnp.maximum(i * per - 1, 0), cb))

    def after(cb):
        return pl.BlockSpec((1, halo, CONV_W), lambda bi, i: (bi, jnp.minimum((i + 1) * per, nh - 1), cb))

    def const(shape):
        return pl.BlockSpec(shape, lambda bi, i: (0,) * len(shape))

    return pl.pallas_call(
        _out_kernel,
        grid=(b, n // tm),
        in_specs=[pl.BlockSpec((1, tm, D_MODEL), lambda bi, i: (bi, i, 0)),
                  pl.BlockSpec((1, 1, D_MODEL), lambda bi, i: (bi, 0, 0)),
                  conv(CB_CH), conv(CB_CB), conv(CB_CC), conv(CB_CZ),
                  before(CB_CH), before(CB_CC), after(CB_CH), after(CB_CC),
                  pl.BlockSpec((1, tm, RET_W), lambda bi, i: (bi, i, 0)),
                  pl.BlockSpec((1, tm, NA_W), lambda bi, i: (bi, i, 0)),
                  const((3, CONV_W)), const((1, CONV_W)), const((MIX_W, D_MODEL)),
                  const((1, D_MODEL)), const((1, D_MODEL))],
        out_specs=pl.BlockSpec((1, tm, D_MODEL), lambda bi, i: (bi, i, 0)),
        out_shape=jax.ShapeDtypeStruct((b, n, D_MODEL), F32),
        compiler_params=pltpu.CompilerParams(dimension_semantics=("parallel", "parallel")),
        name="output",
    )(x, gate, p, p, p, p, p, p, p, p, y_ret, y_na, conv_w, conv_b.reshape(1, CONV_W), w_out_bf,
      ln_g.reshape(1, D_MODEL), ln_b.reshape(1, D_MODEL))


def kernel(x, c, ctx, c_ctx, w_mod, b_mod, w_in, conv_w, conv_b, ret_decay, na_rpb, w_out, ln_g, ln_b):
    b, n, d = x.shape
    act_t = jnp.pad(jnp.concatenate([c, c_ctx[None]], axis=0).T, ((0, 0), (0, 5)))
    mod = _modulation(act_t, w_mod, b_mod)
    tables = _rope_tables(n)
    w_in_bf = w_in.astype(BF16)
    w_out_bf = w_out.astype(BF16)
    xc = ctx
    for l in range(DEPTH):
        need_ctx = l < DEPTH - 1
        shift, scale, gate = (mod[l, :b, None, j * d:(j + 1) * d] for j in range(3))
        shift_c, scale_c, gate_c = (jnp.broadcast_to(mod[l, b, j * d:(j + 1) * d], (b, 1, d)) for j in range(3))
        p = _projection(x, shift, scale, w_in_bf[l], tables, PROJ_W)
        pc = _projection(xc, shift_c, scale_c, w_in_bf[l], None, PROJ_W if need_ctx else KV_W)
        y_ret = _retention(p, pc, ret_decay[l], init_state=True)
        y_na = _neighbourhood(p, pc, _na_bias_tables(na_rpb[l], n // GRID_W))
        x_new = _output(x, gate, p, y_ret, y_na, conv_w[l], conv_b[l], w_out_bf[l], ln_g[l], ln_b[l])
        if need_ctx:
            yc_ret = _retention(pc, None, ret_decay[l], init_state=False)
            yc_na = _ctx_attention(pc)
            xc = _output(xc, gate_c, pc, yc_ret, yc_na, conv_w[l], conv_b[l], w_out_bf[l], ln_g[l], ln_b[l])
        x = x_new
    return x
```

```python
import functools

import numpy as np
import jax
import jax.numpy as jnp
from jax import lax
from jax.experimental import pallas as pl
from jax.experimental.pallas import tpu as pltpu

D_MODEL = 1024
DEPTH = 2
GRID_W = 64
HEAD_DIM = 64
CONV_W = 256
RET_HEADS = 6
RET_W = RET_HEADS * HEAD_DIM
NA_HEADS = 6
NA_W = NA_HEADS * HEAD_DIM
MIX_W = CONV_W + RET_W + NA_W
RET_CHUNK = 128
WIN_H = 8
WIN_W = 16
ROPE_BASE = 10000.0
LN_EPS = 1e-5
DEEPNORM_ALPHA = (2 * DEPTH) ** 0.25
PROJ_SPLITS = (RET_W, RET_W, NA_W, NA_W, RET_W, RET_W, NA_W, NA_W, CONV_W, CONV_W, CONV_W, CONV_W)
PROJ_W = sum(PROJ_SPLITS)
KV_W = 2 * RET_W + 2 * NA_W

LANES = 128
PAIRS = RET_HEADS // 2
CB_RK, CB_RV, CB_NK, CB_NV, CB_RQ, CB_RG, CB_NQ, CB_NG = 0, 3, 6, 9, 12, 15, 18, 21
CB_CH, CB_CB, CB_CC, CB_CZ = 12, 13, 14, 15
QK_SCALE = HEAD_DIM ** -0.5
NEG = -1e30
LOG2E = 1.4426950408889634
NA_ROWS = 4
NA_TOK = NA_ROWS * GRID_W
RET_GROUP = 8
ROW_TILE = 512
PROJ_CHUNK = 512

F32 = jnp.float32
BF16 = jnp.bfloat16


def _silu(v):
    return v * jax.nn.sigmoid(v)


def _dot(a, b):
    return jnp.dot(a, b, preferred_element_type=F32)


def _dot_nt(a, b):
    return lax.dot_general(a, b, (((1,), (1,)), ((), ())), preferred_element_type=F32)


def _dot_tn(a, b):
    return lax.dot_general(a, b, (((0,), (0,)), ((), ())), preferred_element_type=F32)


def _lane_is_head0(shape):
    return lax.broadcasted_iota(jnp.int32, shape, len(shape) - 1) < HEAD_DIM


def _mod_kernel(act_ref, w_ref, b_ref, o_ref):
    a = _silu(act_ref[...])
    w = w_ref[0]
    bias = b_ref[0]
    for r in range(3):
        o_ref[0, r:r + 1, :] = jnp.sum(a[:, r:r + 1] * w, axis=0, keepdims=True) + bias
    o_ref[0, 3:8, :] = jnp.zeros((5, w.shape[1]), F32)


def _modulation(act_t, w_mod, b_mod):
    tn = 512
    n = w_mod.shape[-1]
    return pl.pallas_call(
        _mod_kernel,
        grid=(DEPTH, n // tn),
        in_specs=[pl.BlockSpec((D_MODEL, 8), lambda l, j: (0, 0)),
                  pl.BlockSpec((1, D_MODEL, tn), lambda l, j: (l, 0, j)),
                  pl.BlockSpec((1, 1, tn), lambda l, j: (l, 0, j))],
        out_specs=pl.BlockSpec((1, 8, tn), lambda l, j: (l, 0, j)),
        out_shape=jax.ShapeDtypeStruct((DEPTH, 8, n), F32),
        name="modulation",
    )(act_t, w_mod, b_mod.reshape(DEPTH, 1, n))


def _rope_tables(n):
    rows = n // GRID_W
    nf = HEAD_DIM // 4
    inv = ROPE_BASE ** (-jnp.arange(nf, dtype=F32) / nf)
    ang_r = jnp.arange(rows).astype(F32)[:, None] * inv
    ang_c = jnp.arange(GRID_W).astype(F32)[:, None] * inv
    first_half = (np.arange(LANES) % 32) < 16

    def expand(by_row, by_col):
        r = jnp.broadcast_to(by_row[:, None, :], (rows, GRID_W, nf))
        c = jnp.broadcast_to(by_col[None, :, :], (rows, GRID_W, nf))
        return jnp.concatenate([r, r, c, c] * 2, axis=-1).reshape(n, LANES)

    cos = expand(jnp.cos(ang_r), jnp.cos(ang_c))
    sin = expand(jnp.sin(ang_r), jnp.sin(ang_c))
    s_up = jnp.where(first_half[None], -sin, 0.0)
    s_dn = jnp.where(first_half[None], 0.0, sin)
    return cos, s_up, s_dn


def _proj_kernel(x_ref, shift_ref, scale_ref, w_ref, *rest, rope, n_cols):
    if rope:
        cos_ref, up_ref, dn_ref, o_ref = rest
    else:
        (o_ref,) = rest
    h = (x_ref[0] * (1.0 + scale_ref[0]) + shift_ref[0]).astype(BF16)
    rotated = tuple(range(CB_RK, CB_RK + PAIRS)) + tuple(range(CB_RQ, CB_RQ + PAIRS))
    col_scale = {cb: QK_SCALE for cb in range(CB_RK, CB_RK + PAIRS)}
    col_scale.update({cb: QK_SCALE * LOG2E for cb in range(CB_NQ, CB_NQ + PAIRS)})
    for off in range(0, n_cols, PROJ_CHUNK):
        acc = _dot(h, w_ref[:, off:off + PROJ_CHUNK])
        tiles = []
        for j in range(PROJ_CHUNK // LANES):
            cb = off // LANES + j
            v = acc[:, j * LANES:(j + 1) * LANES]
            if rope and cb in rotated:
                v = (v * cos_ref[...] + pltpu.roll(v, LANES - 16, axis=1) * up_ref[...]
                     + pltpu.roll(v, 16, axis=1) * dn_ref[...])
            if cb in col_scale:
                v = v * col_scale[cb]
            tiles.append(v.astype(BF16))
        o_ref[0, :, off:off + PROJ_CHUNK] = jnp.concatenate(tiles, axis=1)


def _projection(x, shift, scale, w_bf, layer, tables, n_cols):
    b, n, _ = x.shape
    tm = min(ROW_TILE, n)
    rope = tables is not None
    in_specs = [pl.BlockSpec((1, tm, D_MODEL), lambda bi, i: (bi, i, 0)),
                pl.BlockSpec((1, 1, D_MODEL), lambda bi, i: (bi, 0, 0)),
                pl.BlockSpec((1, 1, D_MODEL), lambda bi, i: (bi, 0, 0)),
                pl.BlockSpec((None, D_MODEL, n_cols), lambda bi, i: (layer, 0, 0))]
    args = [x, shift, scale, w_bf]
    if rope:
        in_specs += [pl.BlockSpec((tm, LANES), lambda bi, i: (i, 0))] * 3
        args += list(tables)
    return pl.pallas_call(
        functools.partial(_proj_kernel, rope=rope, n_cols=n_cols),
        grid=(b, n // tm),
        in_specs=in_specs,
        out_specs=pl.BlockSpec((1, tm, n_cols), lambda bi, i: (bi, i, 0)),
        out_shape=jax.ShapeDtypeStruct((b, n, n_cols), BF16),
        compiler_params=pltpu.CompilerParams(dimension_semantics=("parallel", "parallel")),
        name="projection_rope" if rope else "projection_ctx",
    )(*args)


def _log_sigmoid(v):
    return jnp.minimum(v, 0.0) - jnp.log1p(jnp.exp(-jnp.abs(v)))


def _block_diag(m):
    r = lax.broadcasted_iota(jnp.int32, m.shape, 0) < HEAD_DIM
    c = lax.broadcasted_iota(jnp.int32, m.shape, 1) < HEAD_DIM
    return jnp.where(r == c, m, 0.0)


def _head_mean(x, avg2):
    hi = x.astype(BF16)
    lo = (x - hi.astype(F32)).astype(BF16)
    return _dot(jnp.concatenate([hi, lo], axis=1), avg2)


def _ret_kernel(*refs, init_state, group):
    refs = list(refs)
    dec_ref, q_ref, k_ref, v_ref, g_ref = refs[:5]
    refs = refs[5:]
    if init_state:
        kc_ref, vc_ref = refs[:2]
        refs = refs[2:]
    o_ref, s_ref, u_ref, wq_ref, wkt_ref, d_ref = refs
    c = RET_CHUNK
    n_chunks = q_ref.shape[1] // c
    n_groups = n_chunks // group
    head0 = _lane_is_head0((c, LANES))
    lg_f = _log_sigmoid(dec_ref[0, 0])
    lg_b = _log_sigmoid(dec_ref[1, 0])
    i = lax.broadcasted_iota(jnp.int32, (c, LANES), 0).astype(F32)
    wq_ref[:, :LANES] = jnp.exp((i + 1.0) * lg_f)
    wq_ref[:, LANES:] = jnp.exp((c - i) * lg_b)
    wkt_ref[:LANES, :] = jnp.exp((c - 1.0 - i) * lg_f).T
    wkt_ref[LANES:, :] = jnp.exp(i * lg_b).T
    gc_f = jnp.exp(float(c) * lg_f)
    gc_b = jnp.exp(float(c) * lg_b)
    diff = (lax.broadcasted_iota(jnp.int32, (c, c), 0) - lax.broadcasted_iota(jnp.int32, (c, c), 1)).astype(F32)
    lower = diff >= 0
    for hh in range(2):
        lf = lg_f[:, hh * HEAD_DIM:hh * HEAD_DIM + 1]
        lb = lg_b[:, hh * HEAD_DIM:hh * HEAD_DIM + 1]
        d_ref[:, hh * c:(hh + 1) * c] = jnp.where(lower, jnp.exp(jnp.where(lower, diff, 0.0) * lf),
                                                  jnp.exp(jnp.where(lower, 0.0, -diff) * lb))
    if init_state:
        lc = kc_ref.shape[1]
        m = lax.broadcasted_iota(jnp.int32, (lc, LANES), 0).astype(F32)
        kcf = kc_ref[0].astype(F32)
        s_f0 = _block_diag(_dot_tn((kcf * jnp.exp((lc - 1.0 - m) * lg_f)).astype(BF16), vc_ref[0]))
        s_b0 = _block_diag(_dot_tn((kcf * jnp.exp(m * lg_b)).astype(BF16), vc_ref[0]))
    else:
        s_f0 = s_b0 = jnp.zeros((LANES, LANES), F32)
    r2 = lax.broadcasted_iota(jnp.int32, (2 * LANES, LANES), 0) % LANES < HEAD_DIM
    c2 = lax.broadcasted_iota(jnp.int32, (2 * LANES, LANES), 1) < HEAD_DIM
    diag2 = r2 == c2
    avg2 = jnp.where(diag2, 1.0 / HEAD_DIM, 0.0).astype(BF16)

    def chunk(n):
        return pl.ds(pl.multiple_of(n * c, c), c)

    def increments(gi, carry):
        for j in range(group):
            n = gi * group + j
            kt = k_ref[0, chunk(n), :].astype(F32).T
            lhs = jnp.concatenate([kt * wkt_ref[:LANES, :], kt * wkt_ref[LANES:, :]], axis=0).astype(BF16)
            u_ref[n] = jnp.where(diag2, _dot(lhs, v_ref[0, chunk(n), :]), 0.0)
        return carry

    lax.fori_loop(0, n_groups, increments, 0)

    def fwd_scan(gi, s):
        for j in range(group):
            n = gi * group + j
            s_ref[n, :LANES, :] = s.astype(BF16)
            s = s * gc_f + u_ref[n, :LANES, :]
        return s

    def bwd_scan(gi, s):
        for j in range(group):
            n = n_chunks - 1 - (gi * group + j)
            s_ref[n, LANES:, :] = s.astype(BF16)
            s = s * gc_b + u_ref[n, LANES:, :]
        return s

    lax.fori_loop(0, n_groups, fwd_scan, s_f0)
    lax.fori_loop(0, n_groups, bwd_scan, s_b0)

    def body(gi, carry):
        outs = []
        for j in range(group):
            n = gi * group + j
            rows = chunk(n)
            q, k, v = q_ref[0, rows, :], k_ref[0, rows, :], v_ref[0, rows, :]
            zero = jnp.zeros_like(k)
            kcat = jnp.concatenate([jnp.where(head0, k, zero), jnp.where(head0, zero, k)], axis=0)
            vcat = jnp.concatenate([jnp.where(head0, v, zero), jnp.where(head0, zero, v)], axis=0)
            scores = _dot_nt(q, kcat) * d_ref[...]
            qf = q.astype(F32)
            qw = jnp.concatenate([qf * wq_ref[:, :LANES], qf * wq_ref[:, LANES:]], axis=1).astype(BF16)
            outs.append(_dot(scores.astype(BF16), vcat) + _dot(qw, s_ref[n]))
        o = jnp.concatenate(outs, axis=0)
        rows = pl.ds(pl.multiple_of(gi * (group * c), group * c), group * c)
        dlt = o - _head_mean(o, avg2)
        var = _head_mean(dlt * dlt, avg2)
        y = dlt * lax.rsqrt(var + LN_EPS) * _silu(g_ref[0, rows, :].astype(F32))
        o_ref[0, rows, :] = y.astype(o_ref.dtype)
        return carry

    lax.fori_loop(0, n_groups, body, 0)


def _retention(p, pc, ret_decay, init_state):
    b, n, _ = p.shape
    dec = jnp.repeat(ret_decay.astype(F32), HEAD_DIM, axis=-1).reshape(2, PAIRS, 1, LANES)

    def col(cb):
        return pl.BlockSpec((1, n, LANES), lambda bi, pi: (bi, 0, cb + pi))

    in_specs = [pl.BlockSpec((2, 1, 1, LANES), lambda bi, pi: (0, pi, 0, 0)),
                col(CB_RQ), col(CB_RK), col(CB_RV), col(CB_RG)]
    args = [dec, p, p, p, p]
    if init_state:
        lc = pc.shape[1]
        in_specs += [pl.BlockSpec((1, lc, LANES), lambda bi, pi: (bi, 0, CB_RK + pi)),
                     pl.BlockSpec((1, lc, LANES), lambda bi, pi: (bi, 0, CB_RV + pi))]
        args += [pc, pc]
    c = RET_CHUNK
    return pl.pallas_call(
        functools.partial(_ret_kernel, init_state=init_state, group=min(RET_GROUP, n // c)),
        grid=(b, PAIRS),
        in_specs=in_specs,
        out_specs=col(0),
        out_shape=jax.ShapeDtypeStruct((b, n, RET_W), BF16),
        scratch_shapes=[pltpu.VMEM((n // c, 2 * LANES, LANES), BF16),
                        pltpu.VMEM((n // c, 2 * LANES, LANES), F32),
                        pltpu.VMEM((c, 2 * LANES), F32),
                        pltpu.VMEM((2 * LANES, c), F32),
                        pltpu.VMEM((c, 2 * c), F32)],
        compiler_params=pltpu.CompilerParams(dimension_semantics=("parallel", "parallel")),
        name="retention",
    )(*args)


def _na_bias_plan(rows):
    nblk = rows // NA_ROWS
    win_h = min(WIN_H, rows)
    plan = np.full((3, NA_ROWS, 3, NA_ROWS), -1, np.int64)
    for kind, blk in enumerate((0, 1, nblk - 1)):
        for qr in range(NA_ROWS):
            r = blk * NA_ROWS + qr
            rs = min(max(r - win_h // 2, 0), rows - win_h)
            for kb in range(3):
                if not 0 <= blk + kb - 1 <= nblk - 1:
                    continue
                for krl in range(NA_ROWS):
                    kr = (blk + kb - 1) * NA_ROWS + krl
                    if rs <= kr < rs + win_h:
                        plan[kind, qr, kb, krl] = kr - r + WIN_H - 1
    return plan


def _na_bias_kernel(rpb_ref, o_ref, t_ref, *, plan):
    h = pl.program_id(0)
    n_dr, n_dc = 2 * WIN_H - 1, 2 * WIN_W - 1
    qc = lax.broadcasted_iota(jnp.int32, (GRID_W, LANES), 0)
    lane = lax.broadcasted_iota(jnp.int32, (GRID_W, LANES), 1)
    kc = lane % GRID_W
    d_col = kc - qc + (WIN_W - 1)
    cs = jnp.clip(qc - WIN_W // 2, 0, GRID_W - WIN_W)
    col_ok = (kc >= cs) & (kc < cs + WIN_W)
    neg = jnp.full((GRID_W, LANES), NEG, F32)
    for dr in range(n_dr):
        t = neg
        for dc in range(n_dc):
            t = jnp.where(d_col == dc, rpb_ref[(h * n_dr + dr) * n_dc + dc] * LOG2E, t)
        t_ref[dr] = jnp.where(col_ok, t, neg)
    first = lane < GRID_W
    for kind in range(3):
        for qr in range(NA_ROWS):
            for kb in range(3):
                for pr in range(NA_ROWS // 2):
                    ia, ib = (int(plan[kind, qr, kb, 2 * pr + s]) for s in range(2))
                    a = t_ref[ia] if ia >= 0 else neg
                    b = t_ref[ib] if ib >= 0 else neg
                    col0 = kb * NA_TOK + pr * LANES
                    o_ref[kind, 0, qr * GRID_W:(qr + 1) * GRID_W, col0:col0 + LANES] = jnp.where(first, a, b)


def _na_bias_tables(rpb, rows):
    n_dr, n_dc = 2 * WIN_H - 1, 2 * WIN_W - 1
    return pl.pallas_call(
        functools.partial(_na_bias_kernel, plan=_na_bias_plan(rows)),
        grid=(NA_HEADS,),
        in_specs=[pl.BlockSpec(memory_space=pltpu.SMEM)],
        out_specs=pl.BlockSpec((3, 1, NA_TOK, 3 * NA_TOK), lambda h: (0, h, 0, 0)),
        out_shape=jax.ShapeDtypeStruct((3, NA_HEADS, NA_TOK, 3 * NA_TOK), F32),
        scratch_shapes=[pltpu.VMEM((n_dr, GRID_W, LANES), F32)],
        name="na_bias",
    )(rpb.astype(F32).reshape(NA_HEADS * n_dr * n_dc))


def _fold_lanes(blocks, op):
    tiles = [blk[:, j:j + LANES] for blk in blocks for j in range(0, blk.shape[1], LANES)]
    acc = tiles[0]
    for t in tiles[1:]:
        acc = op(acc, t)
    return acc


def _na_kernel(q_ref, kp_ref, kc_ref, kn_ref, vp_ref, vc_ref, vn_ref, kx_ref, vx_ref, g_ref, bias_ref, o_ref):
    head0 = _lane_is_head0((NA_TOK, LANES))
    for pi in range(PAIRS):
        cols = slice(pi * LANES, (pi + 1) * LANES)
        q = q_ref[0, :, cols]
        zero = jnp.zeros_like(q)
        keys = (kp_ref[0, :, cols], kc_ref[0, :, cols], kn_ref[0, :, cols], kx_ref[0, :, cols])
        vals = (vp_ref[0, :, cols], vc_ref[0, :, cols], vn_ref[0, :, cols], vx_ref[0, :, cols])
        probs, inv = [], []
        for hh in range(2):
            qh = jnp.where(head0, q, zero) if hh == 0 else jnp.where(head0, zero, q)
            s = [_dot_nt(qh, keys[j]) + bias_ref[0, 2 * pi + hh, :, j * NA_TOK:(j + 1) * NA_TOK] for j in range(3)]
            s.append(_dot_nt(qh, keys[3]))
            m = jnp.max(_fold_lanes(s, jnp.maximum), axis=-1, keepdims=True)
            p = [jnp.exp2(sj - m) for sj in s]
            inv.append(1.0 / jnp.sum(_fold_lanes(p, jnp.add), axis=-1, keepdims=True))
            probs += [pj.astype(BF16) for pj in p]
        v_cat = jnp.concatenate([jnp.where(head0, v, zero) for v in vals]
                                + [jnp.where(head0, zero, v) for v in vals], axis=0)
        o = _dot(jnp.concatenate(probs, axis=1), v_cat) * jnp.where(head0, inv[0], inv[1])
        o_ref[0, :, cols] = (o * _silu(g_ref[0, :, cols].astype(F32))).astype(o_ref.dtype)


def _neighbourhood(p, pc, bias):
    b, n, _ = p.shape
    nblk = n // NA_TOK
    grp = NA_W // LANES

    def blk(cb, shift):
        return pl.BlockSpec((1, NA_TOK, NA_W),
                            lambda bi, i: (bi, jnp.clip(i + shift, 0, nblk - 1), cb // grp))

    lc = pc.shape[1]

    def ctx(cb):
        return pl.BlockSpec((1, lc, NA_W), lambda bi, i: (bi, 0, cb // grp))

    def bias_index(bi, i):
        kind = jnp.where(i == 0, 0, jnp.where(i == nblk - 1, 2, 1))
        return (kind, 0, 0, 0)

    return pl.pallas_call(
        _na_kernel,
        grid=(b, nblk),
        in_specs=[blk(CB_NQ, 0), blk(CB_NK, -1), blk(CB_NK, 0), blk(CB_NK, 1),
                  blk(CB_NV, -1), blk(CB_NV, 0), blk(CB_NV, 1), ctx(CB_NK), ctx(CB_NV), blk(CB_NG, 0),
                  pl.BlockSpec((1, NA_HEADS, NA_TOK, 3 * NA_TOK), bias_index)],
        out_specs=pl.BlockSpec((1, NA_TOK, NA_W), lambda bi, i: (bi, i, 0)),
        out_shape=jax.ShapeDtypeStruct((b, n, NA_W), BF16),
        compiler_params=pltpu.CompilerParams(dimension_semantics=("parallel", "arbitrary")),
        name="neighbourhood",
    )(p, p, p, p, p, p, p, pc, pc, p, bias)


def _ctx_attn_kernel(q_ref, k_ref, v_ref, g_ref, o_ref):
    q, k, v = q_ref[0], k_ref[0], v_ref[0]
    head0 = _lane_is_head0(q.shape)
    zero = jnp.zeros_like(q)
    outs = []
    for hh in range(2):
        qh = jnp.where(head0, q, zero) if hh == 0 else jnp.where(head0, zero, q)
        s = _dot_nt(qh, k)
        p = jnp.exp2(s - jnp.max(s, axis=-1, keepdims=True))
        outs.append(_dot(p.astype(BF16), v) / jnp.sum(p, axis=-1, keepdims=True))
    o = jnp.where(head0, outs[0], outs[1])
    o_ref[0] = (o * _silu(g_ref[0].astype(F32))).astype(o_ref.dtype)


def _ctx_attention(pc):
    b, lc, _ = pc.shape

    def col(cb):
        return pl.BlockSpec((1, lc, LANES), lambda bi, pi: (bi, 0, cb + pi))

    return pl.pallas_call(
        _ctx_attn_kernel,
        grid=(b, PAIRS),
        in_specs=[col(CB_NQ), col(CB_NK), col(CB_NV), col(CB_NG)],
        out_specs=col(0),
        out_shape=jax.ShapeDtypeStruct((b, lc, NA_W), BF16),
        name="context_attention",
    )(pc, pc, pc, pc)


def _out_kernel(x_ref, gate_ref, ch_ref, cb_ref, cc_ref, cz_ref, hp_ref, cp_ref, hn_ref, cn_ref,
                yr_ref, yn_ref, cw_ref, cbias_ref, w_ref, lg_ref, lb_ref, o_ref):
    i = pl.program_id(1)
    last = pl.num_programs(1) - 1
    u = cc_ref[0].astype(F32) * ch_ref[0].astype(F32)
    tm = u.shape[0]
    halo = hp_ref.shape[1]
    u_before = cp_ref[0, halo - 1:halo, :].astype(F32) * hp_ref[0, halo - 1:halo, :].astype(F32)
    u_after = cn_ref[0, 0:1, :].astype(F32) * hn_ref[0, 0:1, :].astype(F32)
    u_before = jnp.where(i == 0, 0.0, u_before)
    u_after = jnp.where(i == last, 0.0, u_after)
    row = lax.broadcasted_iota(jnp.int32, u.shape, 0)
    u_prev = jnp.where(row == 0, u_before, pltpu.roll(u, 1, axis=0))
    u_next = jnp.where(row == tm - 1, u_after, pltpu.roll(u, tm - 1, axis=0))
    cw = cw_ref[...]
    conv = u_prev * cw[0:1] + u * cw[1:2] + u_next * cw[2:3] + cbias_ref[...]
    y_conv = cb_ref[0].astype(F32) * conv * _silu(cz_ref[0].astype(F32))
    y = jnp.concatenate([y_conv.astype(BF16), yr_ref[0], yn_ref[0]], axis=-1)
    z = DEEPNORM_ALPHA * x_ref[0] + gate_ref[0] * _dot(y, w_ref[...])
    mu = jnp.mean(z, axis=-1, keepdims=True)
    dlt = z - mu
    var = jnp.mean(dlt * dlt, axis=-1, keepdims=True)
    o_ref[0] = dlt * lax.rsqrt(var + LN_EPS) * lg_ref[...] + lb_ref[...]


def _output(x, gate, p, y_ret, y_na, conv_w, conv_b, w_out_bf, layer, ln_g, ln_b):
    b, n, _ = x.shape
    tm = min(ROW_TILE, n)
    halo = 16
    per = tm // halo
    nh = n // halo

    def conv(cb):
        return pl.BlockSpec((1, tm, CONV_W), lambda bi, i: (bi, i, cb))

    def before(cb):
        return pl.BlockSpec((1, halo, CONV_W), lambda bi, i: (bi, jnp.maximum(i * per - 1, 0), cb))

    def after(cb):
        return pl.BlockSpec((1, halo, CONV_W), lambda bi, i: (bi, jnp.minimum((i + 1) * per, nh - 1), cb))

    def const(shape):
        return pl.BlockSpec(shape, lambda bi, i: (0,) * len(shape))

    return pl.pallas_call(
        _out_kernel,
        grid=(b, n // tm),
        in_specs=[pl.BlockSpec((1, tm, D_MODEL), lambda bi, i: (bi, i, 0)),
                  pl.BlockSpec((1, 1, D_MODEL), lambda bi, i: (bi, 0, 0)),
                  conv(CB_CH), conv(CB_CB), conv(CB_CC), conv(CB_CZ),
                  before(CB_CH), before(CB_CC), after(CB_CH), after(CB_CC),
                  pl.BlockSpec((1, tm, RET_W), lambda bi, i: (bi, i, 0)),
                  pl.BlockSpec((1, tm, NA_W), lambda bi, i: (bi, i, 0)),
                  const((3, CONV_W)), const((1, CONV_W)),
                  pl.BlockSpec((None, MIX_W, D_MODEL), lambda bi, i: (layer, 0, 0)),
                  const((1, D_MODEL)), const((1, D_MODEL))],
        out_specs=pl.BlockSpec((1, tm, D_MODEL), lambda bi, i: (bi, i, 0)),
        out_shape=jax.ShapeDtypeStruct((b, n, D_MODEL), F32),
        compiler_params=pltpu.CompilerParams(dimension_semantics=("parallel", "parallel")),
        name="output",
    )(x, gate, p, p, p, p, p, p, p, p, y_ret, y_na, conv_w, conv_b.reshape(1, CONV_W), w_out_bf,
      ln_g.reshape(1, D_MODEL), ln_b.reshape(1, D_MODEL))


def kernel(x, c, ctx, c_ctx, w_mod, b_mod, w_in, conv_w, conv_b, ret_decay, na_rpb, w_out, ln_g, ln_b):
    b, n, d = x.shape
    act_t = jnp.pad(jnp.concatenate([c, c_ctx[None]], axis=0).T, ((0, 0), (0, 5)))
    mod = _modulation(act_t, w_mod, b_mod)
    tables = _rope_tables(n)
    w_in_bf = w_in.astype(BF16)
    w_out_bf = w_out.astype(BF16)
    xc = ctx
    for l in range(DEPTH):
        need_ctx = l < DEPTH - 1
        shift, scale, gate = (mod[l, :b, None, j * d:(j + 1) * d] for j in range(3))
        shift_c, scale_c, gate_c = (jnp.broadcast_to(mod[l, b, j * d:(j + 1) * d], (b, 1, d)) for j in range(3))
        p = _projection(x, shift, scale, w_in_bf, l, tables, PROJ_W)
        pc = _projection(xc, shift_c, scale_c, w_in_bf, l, None, PROJ_W if need_ctx else KV_W)
        y_ret = _retention(p, pc, ret_decay[l], init_state=True)
        y_na = _neighbourhood(p, pc, _na_bias_tables(na_rpb[l], n // GRID_W))
        x_new = _output(x, gate, p, y_ret, y_na, conv_w[l], conv_b[l], w_out_bf, l, ln_g[l], ln_b[l])
        if need_ctx:
            yc_ret = _retention(pc, None, ret_decay[l], init_state=False)
            yc_na = _ctx_attention(pc)
            xc = _output(xc, gate_c, pc, yc_ret, yc_na, conv_w[l], conv_b[l], w_out_bf, l, ln_g[l], ln_b[l])
        x = x_new
    return x
```

```python
import functools

import numpy as np
import jax
import jax.numpy as jnp
from jax import lax
from jax.experimental import pallas as pl
from jax.experimental.pallas import tpu as pltpu

D_MODEL = 1024
DEPTH = 2
GRID_W = 64
HEAD_DIM = 64
CONV_W = 256
RET_HEADS = 6
RET_W = RET_HEADS * HEAD_DIM
NA_HEADS = 6
NA_W = NA_HEADS * HEAD_DIM
MIX_W = CONV_W + RET_W + NA_W
RET_CHUNK = 128
WIN_H = 8
WIN_W = 16
ROPE_BASE = 10000.0
LN_EPS = 1e-5
DEEPNORM_ALPHA = (2 * DEPTH) ** 0.25
PROJ_SPLITS = (RET_W, RET_W, NA_W, NA_W, RET_W, RET_W, NA_W, NA_W, CONV_W, CONV_W, CONV_W, CONV_W)
PROJ_W = sum(PROJ_SPLITS)
KV_W = 2 * RET_W + 2 * NA_W

LANES = 128
PAIRS = RET_HEADS // 2
CB_RK, CB_RV, CB_NK, CB_NV, CB_RQ, CB_RG, CB_NQ, CB_NG = 0, 3, 6, 9, 12, 15, 18, 21
CB_CH, CB_CB, CB_CC, CB_CZ = 12, 13, 14, 15
QK_SCALE = HEAD_DIM ** -0.5
NEG = -1e30
LOG2E = 1.4426950408889634
NA_ROWS = 4
NA_TOK = NA_ROWS * GRID_W
RET_GROUP = 8
ROW_TILE = 512
PROJ_CHUNK = 512

F32 = jnp.float32
BF16 = jnp.bfloat16


def _silu(v):
    return v * jax.nn.sigmoid(v)


def _dot(a, b):
    return jnp.dot(a, b, preferred_element_type=F32)


def _dot_nt(a, b):
    return lax.dot_general(a, b, (((1,), (1,)), ((), ())), preferred_element_type=F32)


def _dot_tn(a, b):
    return lax.dot_general(a, b, (((0,), (0,)), ((), ())), preferred_element_type=F32)


def _lane_is_head0(shape):
    return lax.broadcasted_iota(jnp.int32, shape, len(shape) - 1) < HEAD_DIM


def _mod_kernel(act_ref, w_ref, b_ref, o_ref):
    a = _silu(act_ref[...])
    w = w_ref[0]
    bias = b_ref[0]
    for r in range(3):
        o_ref[0, r:r + 1, :] = jnp.sum(a[:, r:r + 1] * w, axis=0, keepdims=True) + bias
    o_ref[0, 3:8, :] = jnp.zeros((5, w.shape[1]), F32)


def _modulation(act_t, w_mod, b_mod):
    tn = 1536
    n = w_mod.shape[-1]
    return pl.pallas_call(
        _mod_kernel,
        grid=(DEPTH, n // tn),
        in_specs=[pl.BlockSpec((D_MODEL, 8), lambda l, j: (0, 0)),
                  pl.BlockSpec((1, D_MODEL, tn), lambda l, j: (l, 0, j)),
                  pl.BlockSpec((1, 1, tn), lambda l, j: (l, 0, j))],
        out_specs=pl.BlockSpec((1, 8, tn), lambda l, j: (l, 0, j)),
        out_shape=jax.ShapeDtypeStruct((DEPTH, 8, n), F32),
        name="modulation",
    )(act_t, w_mod, b_mod.reshape(DEPTH, 1, n))


def _rope_tables(n):
    rows = n // GRID_W
    nf = HEAD_DIM // 4
    inv = ROPE_BASE ** (-jnp.arange(nf, dtype=F32) / nf)
    lane = np.arange(LANES)
    by_row = ((lane % HEAD_DIM) < HEAD_DIM // 2)[None]
    first = ((lane % (2 * nf)) < nf)[None]
    ang_r = jnp.tile(jnp.arange(rows).astype(F32)[:, None] * inv, (1, LANES // nf))
    ang_c = jnp.tile(jnp.arange(GRID_W).astype(F32)[:, None] * inv, (1, LANES // nf))
    parts = []
    for ang, own in ((ang_r, by_row), (ang_c, ~by_row)):
        cos, sin = jnp.cos(ang), jnp.sin(ang)
        parts.append((jnp.where(own, cos, 0.0), jnp.where(own & first, -sin, 0.0), jnp.where(own & ~first, sin, 0.0)))
    return parts[0] + parts[1]


def _proj_kernel(x_ref, shift_ref, scale_ref, w_ref, *rest, rope, n_cols):
    if rope:
        *table_refs, o_ref = rest
        cos, up, dn = (jnp.concatenate([r_ref[g:g + 1, :] + c_ref[...] for g in range(r_ref.shape[0])], axis=0)
                       for r_ref, c_ref in zip(table_refs[:3], table_refs[3:]))
    else:
        (o_ref,) = rest
    h = (x_ref[0] * (1.0 + scale_ref[0]) + shift_ref[0]).astype(BF16)
    rotated = tuple(range(CB_RK, CB_RK + PAIRS)) + tuple(range(CB_RQ, CB_RQ + PAIRS))
    col_scale = {cb: QK_SCALE for cb in range(CB_RK, CB_RK + PAIRS)}
    col_scale.update({cb: QK_SCALE * LOG2E for cb in range(CB_NQ, CB_NQ + PAIRS)})
    for off in range(0, n_cols, PROJ_CHUNK):
        acc = _dot(h, w_ref[:, off:off + PROJ_CHUNK])
        tiles = []
        for j in range(PROJ_CHUNK // LANES):
            cb = off // LANES + j
            v = acc[:, j * LANES:(j + 1) * LANES]
            if rope and cb in rotated:
                v = v * cos + pltpu.roll(v, LANES - 16, axis=1) * up + pltpu.roll(v, 16, axis=1) * dn
            if cb in col_scale:
                v = v * col_scale[cb]
            tiles.append(v.astype(BF16))
        o_ref[0, :, off:off + PROJ_CHUNK] = jnp.concatenate(tiles, axis=1)


def _projection(x, shift, scale, w_bf, layer, tables, n_cols):
    b, n, _ = x.shape
    tm = min(ROW_TILE, n)
    rope = tables is not None
    in_specs = [pl.BlockSpec((1, tm, D_MODEL), lambda bi, i: (bi, i, 0)),
                pl.BlockSpec((1, 1, D_MODEL), lambda bi, i: (bi, 0, 0)),
                pl.BlockSpec((1, 1, D_MODEL), lambda bi, i: (bi, 0, 0)),
                pl.BlockSpec((None, D_MODEL, n_cols), lambda bi, i: (layer, 0, 0))]
    args = [x, shift, scale, w_bf]
    if rope:
        in_specs += ([pl.BlockSpec((tm // GRID_W, LANES), lambda bi, i: (i, 0))] * 3
                     + [pl.BlockSpec((GRID_W, LANES), lambda bi, i: (0, 0))] * 3)
        args += list(tables)
    return pl.pallas_call(
        functools.partial(_proj_kernel, rope=rope, n_cols=n_cols),
        grid=(b, n // tm),
        in_specs=in_specs,
        out_specs=pl.BlockSpec((1, tm, n_cols), lambda bi, i: (bi, i, 0)),
        out_shape=jax.ShapeDtypeStruct((b, n, n_cols), BF16),
        compiler_params=pltpu.CompilerParams(dimension_semantics=("parallel", "parallel")),
        name="projection_rope" if rope else "projection_ctx",
    )(*args)


def _log_sigmoid(v):
    return jnp.minimum(v, 0.0) - jnp.log1p(jnp.exp(-jnp.abs(v)))


def _block_diag(m):
    r = lax.broadcasted_iota(jnp.int32, m.shape, 0) < HEAD_DIM
    c = lax.broadcasted_iota(jnp.int32, m.shape, 1) < HEAD_DIM
    return jnp.where(r == c, m, 0.0)


def _head_mean(x, avg2):
    hi = x.astype(BF16)
    lo = (x - hi.astype(F32)).astype(BF16)
    return _dot(jnp.concatenate([hi, lo], axis=1), avg2)


def _ret_kernel(*refs, init_state, group):
    refs = list(refs)
    dec_ref, q_ref, k_ref, v_ref, g_ref = refs[:5]
    refs = refs[5:]
    if init_state:
        kc_ref, vc_ref = refs[:2]
        refs = refs[2:]
    o_ref, s_ref, u_ref, wq_ref, wkt_ref, d_ref = refs
    c = RET_CHUNK
    n_chunks = q_ref.shape[1] // c
    n_groups = n_chunks // group
    head0 = _lane_is_head0((c, LANES))
    lg_f = _log_sigmoid(dec_ref[0, 0])
    lg_b = _log_sigmoid(dec_ref[1, 0])
    i = lax.broadcasted_iota(jnp.int32, (c, LANES), 0).astype(F32)
    wq_ref[:, :LANES] = jnp.exp((i + 1.0) * lg_f)
    wq_ref[:, LANES:] = jnp.exp((c - i) * lg_b)
    wkt_ref[:LANES, :] = jnp.exp((c - 1.0 - i) * lg_f).T
    wkt_ref[LANES:, :] = jnp.exp(i * lg_b).T
    gc_f = jnp.exp(float(c) * lg_f)
    gc_b = jnp.exp(float(c) * lg_b)
    diff = (lax.broadcasted_iota(jnp.int32, (c, c), 0) - lax.broadcasted_iota(jnp.int32, (c, c), 1)).astype(F32)
    lower = diff >= 0
    for hh in range(2):
        lf = lg_f[:, hh * HEAD_DIM:hh * HEAD_DIM + 1]
        lb = lg_b[:, hh * HEAD_DIM:hh * HEAD_DIM + 1]
        d_ref[:, hh * c:(hh + 1) * c] = jnp.where(lower, jnp.exp(jnp.where(lower, diff, 0.0) * lf),
                                                  jnp.exp(jnp.where(lower, 0.0, -diff) * lb))
    if init_state:
        lc = kc_ref.shape[1]
        m = lax.broadcasted_iota(jnp.int32, (lc, LANES), 0).astype(F32)
        kcf = kc_ref[0].astype(F32)
        s_f0 = _block_diag(_dot_tn((kcf * jnp.exp((lc - 1.0 - m) * lg_f)).astype(BF16), vc_ref[0]))
        s_b0 = _block_diag(_dot_tn((kcf * jnp.exp(m * lg_b)).astype(BF16), vc_ref[0]))
    else:
        s_f0 = s_b0 = jnp.zeros((LANES, LANES), F32)
    r2 = lax.broadcasted_iota(jnp.int32, (2 * LANES, LANES), 0) % LANES < HEAD_DIM
    c2 = lax.broadcasted_iota(jnp.int32, (2 * LANES, LANES), 1) < HEAD_DIM
    diag2 = r2 == c2
    avg2 = jnp.where(diag2, 1.0 / HEAD_DIM, 0.0).astype(BF16)

    def chunk(n):
        return pl.ds(pl.multiple_of(n * c, c), c)

    def increments(gi, carry):
        for j in range(group):
            n = gi * group + j
            kt = k_ref[0, chunk(n), :].astype(F32).T
            lhs = jnp.concatenate([kt * wkt_ref[:LANES, :], kt * wkt_ref[LANES:, :]], axis=0).astype(BF16)
            u_ref[n] = jnp.where(diag2, _dot(lhs, v_ref[0, chunk(n), :]), 0.0)
        return carry

    lax.fori_loop(0, n_groups, increments, 0)

    def fwd_scan(gi, s):
        for j in range(group):
            n = gi * group + j
            s_ref[n, :LANES, :] = s.astype(BF16)
            s = s * gc_f + u_ref[n, :LANES, :]
        return s

    def bwd_scan(gi, s):
        for j in range(group):
            n = n_chunks - 1 - (gi * group + j)
            s_ref[n, LANES:, :] = s.astype(BF16)
            s = s * gc_b + u_ref[n, LANES:, :]
        return s

    lax.fori_loop(0, n_groups, fwd_scan, s_f0)
    lax.fori_loop(0, n_groups, bwd_scan, s_b0)

    def body(gi, carry):
        outs = []
        for j in range(group):
            n = gi * group + j
            rows = chunk(n)
            q, k, v = q_ref[0, rows, :], k_ref[0, rows, :], v_ref[0, rows, :]
            zero = jnp.zeros_like(k)
            kcat = jnp.concatenate([jnp.where(head0, k, zero), jnp.where(head0, zero, k)], axis=0)
            vcat = jnp.concatenate([jnp.where(head0, v, zero), jnp.where(head0, zero, v)], axis=0)
            scores = _dot_nt(q, kcat) * d_ref[...]
            qf = q.astype(F32)
            qw = jnp.concatenate([qf * wq_ref[:, :LANES], qf * wq_ref[:, LANES:]], axis=1).astype(BF16)
            outs.append(_dot(scores.astype(BF16), vcat) + _dot(qw, s_ref[n]))
        o = jnp.concatenate(outs, axis=0)
        rows = pl.ds(pl.multiple_of(gi * (group * c), group * c), group * c)
        dlt = o - _head_mean(o, avg2)
        var = _head_mean(dlt * dlt, avg2)
        y = dlt * lax.rsqrt(var + LN_EPS) * _silu(g_ref[0, rows, :].astype(F32))
        o_ref[0, rows, :] = y.astype(o_ref.dtype)
        return carry

    lax.fori_loop(0, n_groups, body, 0)


def _retention(p, pc, ret_decay, init_state):
    b, n, _ = p.shape
    dec = jnp.repeat(ret_decay.astype(F32), HEAD_DIM, axis=-1).reshape(2, PAIRS, 1, LANES)

    def col(cb):
        return pl.BlockSpec((1, n, LANES), lambda bi, pi: (bi, 0, cb + pi))

    in_specs = [pl.BlockSpec((2, 1, 1, LANES), lambda bi, pi: (0, pi, 0, 0)),
                col(CB_RQ), col(CB_RK), col(CB_RV), col(CB_RG)]
    args = [dec, p, p, p, p]
    if init_state:
        lc = pc.shape[1]
        in_specs += [pl.BlockSpec((1, lc, LANES), lambda bi, pi: (bi, 0, CB_RK + pi)),
                     pl.BlockSpec((1, lc, LANES), lambda bi, pi: (bi, 0, CB_RV + pi))]
        args += [pc, pc]
    c = RET_CHUNK
    return pl.pallas_call(
        functools.partial(_ret_kernel, init_state=init_state, group=min(RET_GROUP, n // c)),
        grid=(b, PAIRS),
        in_specs=in_specs,
        out_specs=col(0),
        out_shape=jax.ShapeDtypeStruct((b, n, RET_W), BF16),
        scratch_shapes=[pltpu.VMEM((n // c, 2 * LANES, LANES), BF16),
                        pltpu.VMEM((n // c, 2 * LANES, LANES), F32),
                        pltpu.VMEM((c, 2 * LANES), F32),
                        pltpu.VMEM((2 * LANES, c), F32),
                        pltpu.VMEM((c, 2 * c), F32)],
        compiler_params=pltpu.CompilerParams(dimension_semantics=("parallel", "parallel")),
        name="retention",
    )(*args)


def _na_bias_plan(rows):
    nblk = rows // NA_ROWS
    win_h = min(WIN_H, rows)
    plan = np.full((3, NA_ROWS, 3, NA_ROWS), -1, np.int64)
    for kind, blk in enumerate((0, 1, nblk - 1)):
        for qr in range(NA_ROWS):
            r = blk * NA_ROWS + qr
            rs = min(max(r - win_h // 2, 0), rows - win_h)
            for kb in range(3):
                if not 0 <= blk + kb - 1 <= nblk - 1:
                    continue
                for krl in range(NA_ROWS):
                    kr = (blk + kb - 1) * NA_ROWS + krl
                    if rs <= kr < rs + win_h:
                        plan[kind, qr, kb, krl] = kr - r + WIN_H - 1
    return plan


def _na_bias_kernel(rpb_ref, o_ref, t_ref, *, plan):
    h = pl.program_id(0)
    n_dr, n_dc = 2 * WIN_H - 1, 2 * WIN_W - 1
    qc = lax.broadcasted_iota(jnp.int32, (GRID_W, LANES), 0)
    lane = lax.broadcasted_iota(jnp.int32, (GRID_W, LANES), 1)
    kc = lane % GRID_W
    d_col = kc - qc + (WIN_W - 1)
    cs = jnp.clip(qc - WIN_W // 2, 0, GRID_W - WIN_W)
    col_ok = (kc >= cs) & (kc < cs + WIN_W)
    neg = jnp.full((GRID_W, LANES), NEG, F32)
    for dr in range(n_dr):
        t = neg
        for dc in range(n_dc):
            t = jnp.where(d_col == dc, rpb_ref[(h * n_dr + dr) * n_dc + dc] * LOG2E, t)
        t_ref[dr] = jnp.where(col_ok, t, neg)
    first = lane < GRID_W
    for kind in range(3):
        for qr in range(NA_ROWS):
            for kb in range(3):
                for pr in range(NA_ROWS // 2):
                    ia, ib = (int(plan[kind, qr, kb, 2 * pr + s]) for s in range(2))
                    a = t_ref[ia] if ia >= 0 else neg
                    b = t_ref[ib] if ib >= 0 else neg
                    col0 = kb * NA_TOK + pr * LANES
                    o_ref[kind, 0, qr * GRID_W:(qr + 1) * GRID_W, col0:col0 + LANES] = jnp.where(first, a, b)


def _na_bias_tables(rpb, rows):
    n_dr, n_dc = 2 * WIN_H - 1, 2 * WIN_W - 1
    return pl.pallas_call(
        functools.partial(_na_bias_kernel, plan=_na_bias_plan(rows)),
        grid=(NA_HEADS,),
        in_specs=[pl.BlockSpec(memory_space=pltpu.SMEM)],
        out_specs=pl.BlockSpec((3, 1, NA_TOK, 3 * NA_TOK), lambda h: (0, h, 0, 0)),
        out_shape=jax.ShapeDtypeStruct((3, NA_HEADS, NA_TOK, 3 * NA_TOK), F32),
        scratch_shapes=[pltpu.VMEM((n_dr, GRID_W, LANES), F32)],
        name="na_bias",
    )(rpb.astype(F32).reshape(NA_HEADS * n_dr * n_dc))


def _fold_lanes(blocks, op):
    tiles = [blk[:, j:j + LANES] for blk in blocks for j in range(0, blk.shape[1], LANES)]
    acc = tiles[0]
    for t in tiles[1:]:
        acc = op(acc, t)
    return acc


def _na_kernel(q_ref, kp_ref, kc_ref, kn_ref, vp_ref, vc_ref, vn_ref, kx_ref, vx_ref, g_ref, bias_ref, o_ref):
    head0 = _lane_is_head0((NA_TOK, LANES))
    for pi in range(PAIRS):
        cols = slice(pi * LANES, (pi + 1) * LANES)
        q = q_ref[0, :, cols]
        zero = jnp.zeros_like(q)
        keys = (kp_ref[0, :, cols], kc_ref[0, :, cols], kn_ref[0, :, cols], kx_ref[0, :, cols])
        vals = (vp_ref[0, :, cols], vc_ref[0, :, cols], vn_ref[0, :, cols], vx_ref[0, :, cols])
        probs, inv = [], []
        for hh in range(2):
            qh = jnp.where(head0, q, zero) if hh == 0 else jnp.where(head0, zero, q)
            s = [_dot_nt(qh, keys[j]) + bias_ref[0, 2 * pi + hh, :, j * NA_TOK:(j + 1) * NA_TOK] for j in range(3)]
            s.append(_dot_nt(qh, keys[3]))
            m = jnp.max(_fold_lanes(s, jnp.maximum), axis=-1, keepdims=True)
            p = [jnp.exp2(sj - m) for sj in s]
            inv.append(1.0 / jnp.sum(_fold_lanes(p, jnp.add), axis=-1, keepdims=True))
            probs += [pj.astype(BF16) for pj in p]
        v_cat = jnp.concatenate([jnp.where(head0, v, zero) for v in vals]
                                + [jnp.where(head0, zero, v) for v in vals], axis=0)
        o = _dot(jnp.concatenate(probs, axis=1), v_cat) * jnp.where(head0, inv[0], inv[1])
        o_ref[0, :, cols] = (o * _silu(g_ref[0, :, cols].astype(F32))).astype(o_ref.dtype)


def _neighbourhood(p, pc, bias):
    b, n, _ = p.shape
    nblk = n // NA_TOK
    grp = NA_W // LANES

    def blk(cb, shift):
        return pl.BlockSpec((1, NA_TOK, NA_W),
                            lambda bi, i: (bi, jnp.clip(i + shift, 0, nblk - 1), cb // grp))

    lc = pc.shape[1]

    def ctx(cb):
        return pl.BlockSpec((1, lc, NA_W), lambda bi, i: (bi, 0, cb // grp))

    def bias_index(bi, i):
        kind = jnp.where(i == 0, 0, jnp.where(i == nblk - 1, 2, 1))
        return (kind, 0, 0, 0)

    return pl.pallas_call(
        _na_kernel,
        grid=(b, nblk),
        in_specs=[blk(CB_NQ, 0), blk(CB_NK, -1), blk(CB_NK, 0), blk(CB_NK, 1),
                  blk(CB_NV, -1), blk(CB_NV, 0), blk(CB_NV, 1), ctx(CB_NK), ctx(CB_NV), blk(CB_NG, 0),
                  pl.BlockSpec((1, NA_HEADS, NA_TOK, 3 * NA_TOK), bias_index)],
        out_specs=pl.BlockSpec((1, NA_TOK, NA_W), lambda bi, i: (bi, i, 0)),
        out_shape=jax.ShapeDtypeStruct((b, n, NA_W), BF16),
        compiler_params=pltpu.CompilerParams(dimension_semantics=("parallel", "arbitrary")),
        name="neighbourhood",
    )(p, p, p, p, p, p, p, pc, pc, p, bias)


def _ctx_attn_kernel(q_ref, k_ref, v_ref, g_ref, o_ref):
    q, k, v = q_ref[0], k_ref[0], v_ref[0]
    head0 = _lane_is_head0(q.shape)
    zero = jnp.zeros_like(q)
    outs = []
    for hh in range(2):
        qh = jnp.where(head0, q, zero) if hh == 0 else jnp.where(head0, zero, q)
        s = _dot_nt(qh, k)
        p = jnp.exp2(s - jnp.max(s, axis=-1, keepdims=True))
        outs.append(_dot(p.astype(BF16), v) / jnp.sum(p, axis=-1, keepdims=True))
    o = jnp.where(head0, outs[0], outs[1])
    o_ref[0] = (o * _silu(g_ref[0].astype(F32))).astype(o_ref.dtype)


def _ctx_attention(pc):
    b, lc, _ = pc.shape

    def col(cb):
        return pl.BlockSpec((1, lc, LANES), lambda bi, pi: (bi, 0, cb + pi))

    return pl.pallas_call(
        _ctx_attn_kernel,
        grid=(b, PAIRS),
        in_specs=[col(CB_NQ), col(CB_NK), col(CB_NV), col(CB_NG)],
        out_specs=col(0),
        out_shape=jax.ShapeDtypeStruct((b, lc, NA_W), BF16),
        name="context_attention",
    )(pc, pc, pc, pc)


def _out_kernel(x_ref, gate_ref, ch_ref, cb_ref, cc_ref, cz_ref, hp_ref, cp_ref, hn_ref, cn_ref,
                yr_ref, yn_ref, cw_ref, cbias_ref, w_ref, lg_ref, lb_ref, o_ref):
    i = pl.program_id(1)
    last = pl.num_programs(1) - 1
    u = cc_ref[0].astype(F32) * ch_ref[0].astype(F32)
    tm = u.shape[0]
    halo = hp_ref.shape[1]
    u_before = cp_ref[0, halo - 1:halo, :].astype(F32) * hp_ref[0, halo - 1:halo, :].astype(F32)
    u_after = cn_ref[0, 0:1, :].astype(F32) * hn_ref[0, 0:1, :].astype(F32)
    u_before = jnp.where(i == 0, 0.0, u_before)
    u_after = jnp.where(i == last, 0.0, u_after)
    row = lax.broadcasted_iota(jnp.int32, u.shape, 0)
    u_prev = jnp.where(row == 0, u_before, pltpu.roll(u, 1, axis=0))
    u_next = jnp.where(row == tm - 1, u_after, pltpu.roll(u, tm - 1, axis=0))
    cw = cw_ref[...]
    conv = u_prev * cw[0:1] + u * cw[1:2] + u_next * cw[2:3] + cbias_ref[...]
    y_conv = cb_ref[0].astype(F32) * conv * _silu(cz_ref[0].astype(F32))
    y = jnp.concatenate([y_conv.astype(BF16), yr_ref[0], yn_ref[0]], axis=-1)
    z = DEEPNORM_ALPHA * x_ref[0] + gate_ref[0] * _dot(y, w_ref[...])
    mu = jnp.mean(z, axis=-1, keepdims=True)
    dlt = z - mu
    var = jnp.mean(dlt * dlt, axis=-1, keepdims=True)
    o_ref[0] = dlt * lax.rsqrt(var + LN_EPS) * lg_ref[...] + lb_ref[...]


def _output(x, gate, p, y_ret, y_na, conv_w, conv_b, w_out_bf, layer, ln_g, ln_b):
    b, n, _ = x.shape
    tm = min(ROW_TILE, n)
    halo = 16
    per = tm // halo
    nh = n // halo

    def conv(cb):
        return pl.BlockSpec((1, tm, CONV_W), lambda bi, i: (bi, i, cb))

    def before(cb):
        return pl.BlockSpec((1, halo, CONV_W), lambda bi, i: (bi, jnp.maximum(i * per - 1, 0), cb))

    def after(cb):
        return pl.BlockSpec((1, halo, CONV_W), lambda bi, i: (bi, jnp.minimum((i + 1) * per, nh - 1), cb))

    def const(shape):
        return pl.BlockSpec(shape, lambda bi, i: (0,) * len(shape))

    return pl.pallas_call(
        _out_kernel,
        grid=(b, n // tm),
        in_specs=[pl.BlockSpec((1, tm, D_MODEL), lambda bi, i: (bi, i, 0)),
                  pl.BlockSpec((1, 1, D_MODEL), lambda bi, i: (bi, 0, 0)),
                  conv(CB_CH), conv(CB_CB), conv(CB_CC), conv(CB_CZ),
                  before(CB_CH), before(CB_CC), after(CB_CH), after(CB_CC),
                  pl.BlockSpec((1, tm, RET_W), lambda bi, i: (bi, i, 0)),
                  pl.BlockSpec((1, tm, NA_W), lambda bi, i: (bi, i, 0)),
                  const((3, CONV_W)), const((1, CONV_W)),
                  pl.BlockSpec((None, MIX_W, D_MODEL), lambda bi, i: (layer, 0, 0)),
                  const((1, D_MODEL)), const((1, D_MODEL))],
        out_specs=pl.BlockSpec((1, tm, D_MODEL), lambda bi, i: (bi, i, 0)),
        out_shape=jax.ShapeDtypeStruct((b, n, D_MODEL), F32),
        compiler_params=pltpu.CompilerParams(dimension_semantics=("parallel", "parallel")),
        name="output",
    )(x, gate, p, p, p, p, p, p, p, p, y_ret, y_na, conv_w, conv_b.reshape(1, CONV_W), w_out_bf,
      ln_g.reshape(1, D_MODEL), ln_b.reshape(1, D_MODEL))


def kernel(x, c, ctx, c_ctx, w_mod, b_mod, w_in, conv_w, conv_b, ret_decay, na_rpb, w_out, ln_g, ln_b):
    b, n, d = x.shape
    act_t = jnp.pad(jnp.concatenate([c, c_ctx[None]], axis=0).T, ((0, 0), (0, 5)))
    mod = _modulation(act_t, w_mod, b_mod)
    tables = _rope_tables(n)
    w_in_bf = w_in.astype(BF16)
    w_out_bf = w_out.astype(BF16)
    xc = ctx
    for l in range(DEPTH):
        need_ctx = l < DEPTH - 1
        shift, scale, gate = (mod[l, :b, None, j * d:(j + 1) * d] for j in range(3))
        shift_c, scale_c, gate_c = (jnp.broadcast_to(mod[l, b, j * d:(j + 1) * d], (b, 1, d)) for j in range(3))
        p = _projection(x, shift, scale, w_in_bf, l, tables, PROJ_W)
        pc = _projection(xc, shift_c, scale_c, w_in_bf, l, None, PROJ_W if need_ctx else KV_W)
        y_ret = _retention(p, pc, ret_decay[l], init_state=True)
        y_na = _neighbourhood(p, pc, _na_bias_tables(na_rpb[l], n // GRID_W))
        x_new = _output(x, gate, p, y_ret, y_na, conv_w[l], conv_b[l], w_out_bf, l, ln_g[l], ln_b[l])
        if need_ctx:
            yc_ret = _retention(pc, None, ret_decay[l], init_state=False)
            yc_na = _ctx_attention(pc)
            xc = _output(xc, gate_c, pc, yc_ret, yc_na, conv_w[l], conv_b[l], w_out_bf, l, ln_g[l], ln_b[l])
        x = x_new
    return x
```

```python
import functools

import numpy as np
import jax
import jax.numpy as jnp
from jax import lax
from jax.experimental import pallas as pl
from jax.experimental.pallas import tpu as pltpu

D_MODEL = 1024
DEPTH = 2
GRID_W = 64
HEAD_DIM = 64
CONV_W = 256
RET_HEADS = 6
RET_W = RET_HEADS * HEAD_DIM
NA_HEADS = 6
NA_W = NA_HEADS * HEAD_DIM
MIX_W = CONV_W + RET_W + NA_W
RET_CHUNK = 128
WIN_H = 8
WIN_W = 16
ROPE_BASE = 10000.0
LN_EPS = 1e-5
DEEPNORM_ALPHA = (2 * DEPTH) ** 0.25
PROJ_SPLITS = (RET_W, RET_W, NA_W, NA_W, RET_W, RET_W, NA_W, NA_W, CONV_W, CONV_W, CONV_W, CONV_W)
PROJ_W = sum(PROJ_SPLITS)
KV_W = 2 * RET_W + 2 * NA_W

LANES = 128
PAIRS = RET_HEADS // 2
CB_RK, CB_RV, CB_NK, CB_NV, CB_RQ, CB_RG, CB_NQ, CB_NG = 0, 3, 6, 9, 12, 15, 18, 21
CB_CH, CB_CB, CB_CC, CB_CZ = 12, 13, 14, 15
QK_SCALE = HEAD_DIM ** -0.5
NEG = -1e30
LOG2E = 1.4426950408889634
NA_ROWS = 4
NA_TOK = NA_ROWS * GRID_W
RET_GROUP = 8
ROW_TILE = 1024
PROJ_CHUNK = 512

F32 = jnp.float32
BF16 = jnp.bfloat16


def _silu(v):
    return v * jax.nn.sigmoid(v)


def _dot(a, b):
    return jnp.dot(a, b, preferred_element_type=F32)


def _dot_nt(a, b):
    return lax.dot_general(a, b, (((1,), (1,)), ((), ())), preferred_element_type=F32)


def _dot_tn(a, b):
    return lax.dot_general(a, b, (((0,), (0,)), ((), ())), preferred_element_type=F32)


def _lane_is_head0(shape):
    return lax.broadcasted_iota(jnp.int32, shape, len(shape) - 1) < HEAD_DIM


def _mod_kernel(act_ref, w_ref, b_ref, o_ref):
    a = _silu(act_ref[...])
    w = w_ref[0]
    bias = b_ref[0]
    for r in range(3):
        o_ref[0, r:r + 1, :] = jnp.sum(a[:, r:r + 1] * w, axis=0, keepdims=True) + bias
    o_ref[0, 3:8, :] = jnp.zeros((5, w.shape[1]), F32)


def _modulation(act_t, w_mod, b_mod):
    tn = 1536
    n = w_mod.shape[-1]
    return pl.pallas_call(
        _mod_kernel,
        grid=(DEPTH, n // tn),
        in_specs=[pl.BlockSpec((D_MODEL, 8), lambda l, j: (0, 0)),
                  pl.BlockSpec((1, D_MODEL, tn), lambda l, j: (l, 0, j)),
                  pl.BlockSpec((1, 1, tn), lambda l, j: (l, 0, j))],
        out_specs=pl.BlockSpec((1, 8, tn), lambda l, j: (l, 0, j)),
        out_shape=jax.ShapeDtypeStruct((DEPTH, 8, n), F32),
        name="modulation",
    )(act_t, w_mod, b_mod.reshape(DEPTH, 1, n))


def _rope_tables(n):
    rows = n // GRID_W
    nf = HEAD_DIM // 4
    inv = ROPE_BASE ** (-jnp.arange(nf, dtype=F32) / nf)
    lane = np.arange(LANES)
    by_row = ((lane % HEAD_DIM) < HEAD_DIM // 2)[None]
    first = ((lane % (2 * nf)) < nf)[None]
    ang_r = jnp.tile(jnp.arange(rows).astype(F32)[:, None] * inv, (1, LANES // nf))
    ang_c = jnp.tile(jnp.arange(GRID_W).astype(F32)[:, None] * inv, (1, LANES // nf))
    parts = []
    for ang, own in ((ang_r, by_row), (ang_c, ~by_row)):
        cos, sin = jnp.cos(ang), jnp.sin(ang)
        parts.append((jnp.where(own, cos, 0.0), jnp.where(own & first, -sin, 0.0), jnp.where(own & ~first, sin, 0.0)))
    return parts[0] + parts[1]


def _proj_kernel(x_ref, shift_ref, scale_ref, w_ref, *rest, rope, n_cols):
    if rope:
        *table_refs, o_ref = rest
        cos, up, dn = (jnp.concatenate([r_ref[g:g + 1, :] + c_ref[...] for g in range(r_ref.shape[0])], axis=0)
                       for r_ref, c_ref in zip(table_refs[:3], table_refs[3:]))
    else:
        (o_ref,) = rest
    h = (x_ref[0] * (1.0 + scale_ref[0]) + shift_ref[0]).astype(BF16)
    rotated = tuple(range(CB_RK, CB_RK + PAIRS)) + tuple(range(CB_RQ, CB_RQ + PAIRS))
    col_scale = {cb: QK_SCALE for cb in range(CB_RK, CB_RK + PAIRS)}
    col_scale.update({cb: QK_SCALE * LOG2E for cb in range(CB_NQ, CB_NQ + PAIRS)})
    for off in range(0, n_cols, PROJ_CHUNK):
        acc = _dot(h, w_ref[:, off:off + PROJ_CHUNK])
        tiles = []
        for j in range(PROJ_CHUNK // LANES):
            cb = off // LANES + j
            v = acc[:, j * LANES:(j + 1) * LANES]
            if rope and cb in rotated:
                v = v * cos + pltpu.roll(v, LANES - 16, axis=1) * up + pltpu.roll(v, 16, axis=1) * dn
            if cb in col_scale:
                v = v * col_scale[cb]
            tiles.append(v.astype(BF16))
        o_ref[0, :, off:off + PROJ_CHUNK] = jnp.concatenate(tiles, axis=1)


def _projection(x, shift, scale, w_bf, layer, tables, n_cols):
    b, n, _ = x.shape
    tm = min(ROW_TILE, n)
    rope = tables is not None
    in_specs = [pl.BlockSpec((1, tm, D_MODEL), lambda bi, i: (bi, i, 0)),
                pl.BlockSpec((1, 1, D_MODEL), lambda bi, i: (bi, 0, 0)),
                pl.BlockSpec((1, 1, D_MODEL), lambda bi, i: (bi, 0, 0)),
                pl.BlockSpec((None, D_MODEL, n_cols), lambda bi, i: (layer, 0, 0))]
    args = [x, shift, scale, w_bf]
    if rope:
        in_specs += ([pl.BlockSpec((tm // GRID_W, LANES), lambda bi, i: (i, 0))] * 3
                     + [pl.BlockSpec((GRID_W, LANES), lambda bi, i: (0, 0))] * 3)
        args += list(tables)
    return pl.pallas_call(
        functools.partial(_proj_kernel, rope=rope, n_cols=n_cols),
        grid=(b, n // tm),
        in_specs=in_specs,
        out_specs=pl.BlockSpec((1, tm, n_cols), lambda bi, i: (bi, i, 0)),
        out_shape=jax.ShapeDtypeStruct((b, n, n_cols), BF16),
        compiler_params=pltpu.CompilerParams(dimension_semantics=("parallel", "parallel")),
        name="projection_rope" if rope else "projection_ctx",
    )(*args)


def _log_sigmoid(v):
    return jnp.minimum(v, 0.0) - jnp.log1p(jnp.exp(-jnp.abs(v)))


def _block_diag(m):
    r = lax.broadcasted_iota(jnp.int32, m.shape, 0) < HEAD_DIM
    c = lax.broadcasted_iota(jnp.int32, m.shape, 1) < HEAD_DIM
    return jnp.where(r == c, m, 0.0)


def _head_mean(x, avg2):
    hi = x.astype(BF16)
    lo = (x - hi.astype(F32)).astype(BF16)
    return _dot(jnp.concatenate([hi, lo], axis=1), avg2)


def _ret_kernel(*refs, init_state, group):
    refs = list(refs)
    dec_ref, q_ref, k_ref, v_ref, g_ref = refs[:5]
    refs = refs[5:]
    if init_state:
        kc_ref, vc_ref = refs[:2]
        refs = refs[2:]
    o_ref, s_ref, u_ref, wq_ref, wkt_ref, d_ref = refs
    c = RET_CHUNK
    n_chunks = q_ref.shape[1] // c
    n_groups = n_chunks // group
    head0 = _lane_is_head0((c, LANES))
    lg_f = _log_sigmoid(dec_ref[0, 0])
    lg_b = _log_sigmoid(dec_ref[1, 0])
    i = lax.broadcasted_iota(jnp.int32, (c, LANES), 0).astype(F32)
    wq_ref[:, :LANES] = jnp.exp((i + 1.0) * lg_f)
    wq_ref[:, LANES:] = jnp.exp((c - i) * lg_b)
    wkt_ref[:LANES, :] = jnp.exp((c - 1.0 - i) * lg_f).T
    wkt_ref[LANES:, :] = jnp.exp(i * lg_b).T
    gc_f = jnp.exp(float(c) * lg_f)
    gc_b = jnp.exp(float(c) * lg_b)
    diff = (lax.broadcasted_iota(jnp.int32, (c, c), 0) - lax.broadcasted_iota(jnp.int32, (c, c), 1)).astype(F32)
    lower = diff >= 0
    for hh in range(2):
        lf = lg_f[:, hh * HEAD_DIM:hh * HEAD_DIM + 1]
        lb = lg_b[:, hh * HEAD_DIM:hh * HEAD_DIM + 1]
        d_ref[:, hh * c:(hh + 1) * c] = jnp.where(lower, jnp.exp(jnp.where(lower, diff, 0.0) * lf),
                                                  jnp.exp(jnp.where(lower, 0.0, -diff) * lb))
    if init_state:
        lc = kc_ref.shape[1]
        m = lax.broadcasted_iota(jnp.int32, (lc, LANES), 0).astype(F32)
        kcf = kc_ref[0].astype(F32)
        s_f0 = _block_diag(_dot_tn((kcf * jnp.exp((lc - 1.0 - m) * lg_f)).astype(BF16), vc_ref[0]))
        s_b0 = _block_diag(_dot_tn((kcf * jnp.exp(m * lg_b)).astype(BF16), vc_ref[0]))
    else:
        s_f0 = s_b0 = jnp.zeros((LANES, LANES), F32)
    r2 = lax.broadcasted_iota(jnp.int32, (2 * LANES, LANES), 0) % LANES < HEAD_DIM
    c2 = lax.broadcasted_iota(jnp.int32, (2 * LANES, LANES), 1) < HEAD_DIM
    diag2 = r2 == c2
    avg2 = jnp.where(diag2, 1.0 / HEAD_DIM, 0.0).astype(BF16)

    def chunk(n):
        return pl.ds(pl.multiple_of(n * c, c), c)

    def increments(gi, carry):
        for j in range(group):
            n = gi * group + j
            kt = k_ref[0, chunk(n), :].astype(F32).T
            lhs = jnp.concatenate([kt * wkt_ref[:LANES, :], kt * wkt_ref[LANES:, :]], axis=0).astype(BF16)
            u_ref[n] = jnp.where(diag2, _dot(lhs, v_ref[0, chunk(n), :]), 0.0)
        return carry

    lax.fori_loop(0, n_groups, increments, 0)

    def fwd_scan(gi, s):
        for j in range(group):
            n = gi * group + j
            s_ref[n, :LANES, :] = s.astype(BF16)
            s = s * gc_f + u_ref[n, :LANES, :]
        return s

    def bwd_scan(gi, s):
        for j in range(group):
            n = n_chunks - 1 - (gi * group + j)
            s_ref[n, LANES:, :] = s.astype(BF16)
            s = s * gc_b + u_ref[n, LANES:, :]
        return s

    lax.fori_loop(0, n_groups, fwd_scan, s_f0)
    lax.fori_loop(0, n_groups, bwd_scan, s_b0)

    def body(gi, carry):
        outs = []
        for j in range(group):
            n = gi * group + j
            rows = chunk(n)
            q, k, v = q_ref[0, rows, :], k_ref[0, rows, :], v_ref[0, rows, :]
            zero = jnp.zeros_like(k)
            kcat = jnp.concatenate([jnp.where(head0, k, zero), jnp.where(head0, zero, k)], axis=0)
            vcat = jnp.concatenate([jnp.where(head0, v, zero), jnp.where(head0, zero, v)], axis=0)
            scores = _dot_nt(q, kcat) * d_ref[...]
            qf = q.astype(F32)
            qw = jnp.concatenate([qf * wq_ref[:, :LANES], qf * wq_ref[:, LANES:]], axis=1).astype(BF16)
            outs.append(_dot(scores.astype(BF16), vcat) + _dot(qw, s_ref[n]))
        o = jnp.concatenate(outs, axis=0)
        rows = pl.ds(pl.multiple_of(gi * (group * c), group * c), group * c)
        dlt = o - _head_mean(o, avg2)
        var = _head_mean(dlt * dlt, avg2)
        y = dlt * lax.rsqrt(var + LN_EPS) * _silu(g_ref[0, rows, :].astype(F32))
        o_ref[0, rows, :] = y.astype(o_ref.dtype)
        return carry

    lax.fori_loop(0, n_groups, body, 0)


def _retention(p, pc, ret_decay, init_state):
    b, n, _ = p.shape
    dec = jnp.repeat(ret_decay.astype(F32), HEAD_DIM, axis=-1).reshape(2, PAIRS, 1, LANES)

    def col(cb):
        return pl.BlockSpec((1, n, LANES), lambda bi, pi: (bi, 0, cb + pi))

    in_specs = [pl.BlockSpec((2, 1, 1, LANES), lambda bi, pi: (0, pi, 0, 0)),
                col(CB_RQ), col(CB_RK), col(CB_RV), col(CB_RG)]
    args = [dec, p, p, p, p]
    if init_state:
        lc = pc.shape[1]
        in_specs += [pl.BlockSpec((1, lc, LANES), lambda bi, pi: (bi, 0, CB_RK + pi)),
                     pl.BlockSpec((1, lc, LANES), lambda bi, pi: (bi, 0, CB_RV + pi))]
        args += [pc, pc]
    c = RET_CHUNK
    return pl.pallas_call(
        functools.partial(_ret_kernel, init_state=init_state, group=min(RET_GROUP, n // c)),
        grid=(b, PAIRS),
        in_specs=in_specs,
        out_specs=col(0),
        out_shape=jax.ShapeDtypeStruct((b, n, RET_W), BF16),
        scratch_shapes=[pltpu.VMEM((n // c, 2 * LANES, LANES), BF16),
                        pltpu.VMEM((n // c, 2 * LANES, LANES), F32),
                        pltpu.VMEM((c, 2 * LANES), F32),
                        pltpu.VMEM((2 * LANES, c), F32),
                        pltpu.VMEM((c, 2 * c), F32)],
        compiler_params=pltpu.CompilerParams(dimension_semantics=("parallel", "parallel")),
        name="retention",
    )(*args)


def _na_bias_plan(rows):
    nblk = rows // NA_ROWS
    win_h = min(WIN_H, rows)
    plan = np.full((3, NA_ROWS, 3, NA_ROWS), -1, np.int64)
    for kind, blk in enumerate((0, 1, nblk - 1)):
        for qr in range(NA_ROWS):
            r = blk * NA_ROWS + qr
            rs = min(max(r - win_h // 2, 0), rows - win_h)
            for kb in range(3):
                if not 0 <= blk + kb - 1 <= nblk - 1:
                    continue
                for krl in range(NA_ROWS):
                    kr = (blk + kb - 1) * NA_ROWS + krl
                    if rs <= kr < rs + win_h:
                        plan[kind, qr, kb, krl] = kr - r + WIN_H - 1
    return plan


def _na_bias_kernel(rpb_ref, o_ref, t_ref, *, plan):
    h = pl.program_id(0)
    n_dr, n_dc = 2 * WIN_H - 1, 2 * WIN_W - 1
    qc = lax.broadcasted_iota(jnp.int32, (GRID_W, LANES), 0)
    lane = lax.broadcasted_iota(jnp.int32, (GRID_W, LANES), 1)
    kc = lane % GRID_W
    d_col = kc - qc + (WIN_W - 1)
    cs = jnp.clip(qc - WIN_W // 2, 0, GRID_W - WIN_W)
    col_ok = (kc >= cs) & (kc < cs + WIN_W)
    neg = jnp.full((GRID_W, LANES), NEG, F32)
    for dr in range(n_dr):
        t = neg
        for dc in range(n_dc):
            t = jnp.where(d_col == dc, rpb_ref[(h * n_dr + dr) * n_dc + dc] * LOG2E, t)
        t_ref[dr] = jnp.where(col_ok, t, neg)
    first = lane < GRID_W
    for kind in range(3):
        for qr in range(NA_ROWS):
            for kb in range(3):
                for pr in range(NA_ROWS // 2):
                    ia, ib = (int(plan[kind, qr, kb, 2 * pr + s]) for s in range(2))
                    a = t_ref[ia] if ia >= 0 else neg
                    b = t_ref[ib] if ib >= 0 else neg
                    col0 = kb * NA_TOK + pr * LANES
                    o_ref[kind, 0, qr * GRID_W:(qr + 1) * GRID_W, col0:col0 + LANES] = jnp.where(first, a, b)


def _na_bias_tables(rpb, rows):
    n_dr, n_dc = 2 * WIN_H - 1, 2 * WIN_W - 1
    return pl.pallas_call(
        functools.partial(_na_bias_kernel, plan=_na_bias_plan(rows)),
        grid=(NA_HEADS,),
        in_specs=[pl.BlockSpec(memory_space=pltpu.SMEM)],
        out_specs=pl.BlockSpec((3, 1, NA_TOK, 3 * NA_TOK), lambda h: (0, h, 0, 0)),
        out_shape=jax.ShapeDtypeStruct((3, NA_HEADS, NA_TOK, 3 * NA_TOK), F32),
        scratch_shapes=[pltpu.VMEM((n_dr, GRID_W, LANES), F32)],
        name="na_bias",
    )(rpb.astype(F32).reshape(NA_HEADS * n_dr * n_dc))


def _fold_lanes(blocks, op):
    tiles = [blk[:, j:j + LANES] for blk in blocks for j in range(0, blk.shape[1], LANES)]
    acc = tiles[0]
    for t in tiles[1:]:
        acc = op(acc, t)
    return acc


def _na_kernel(q_ref, kp_ref, kc_ref, kn_ref, vp_ref, vc_ref, vn_ref, kx_ref, vx_ref, g_ref, bias_ref, o_ref):
    head0 = _lane_is_head0((NA_TOK, LANES))
    for pi in range(PAIRS):
        cols = slice(pi * LANES, (pi + 1) * LANES)
        q = q_ref[0, :, cols]
        zero = jnp.zeros_like(q)
        keys = (kp_ref[0, :, cols], kc_ref[0, :, cols], kn_ref[0, :, cols], kx_ref[0, :, cols])
        vals = (vp_ref[0, :, cols], vc_ref[0, :, cols], vn_ref[0, :, cols], vx_ref[0, :, cols])
        probs, inv = [], []
        for hh in range(2):
            qh = jnp.where(head0, q, zero) if hh == 0 else jnp.where(head0, zero, q)
            s = [_dot_nt(qh, keys[j]) + bias_ref[0, 2 * pi + hh, :, j * NA_TOK:(j + 1) * NA_TOK] for j in range(3)]
            s.append(_dot_nt(qh, keys[3]))
            m = jnp.max(_fold_lanes(s, jnp.maximum), axis=-1, keepdims=True)
            p = [jnp.exp2(sj - m) for sj in s]
            inv.append(1.0 / jnp.sum(_fold_lanes(p, jnp.add), axis=-1, keepdims=True))
            probs += [pj.astype(BF16) for pj in p]
        v_cat = jnp.concatenate([jnp.where(head0, v, zero) for v in vals]
                                + [jnp.where(head0, zero, v) for v in vals], axis=0)
        o = _dot(jnp.concatenate(probs, axis=1), v_cat) * jnp.where(head0, inv[0], inv[1])
        o_ref[0, :, cols] = (o * _silu(g_ref[0, :, cols].astype(F32))).astype(o_ref.dtype)


def _neighbourhood(p, pc, bias):
    b, n, _ = p.shape
    nblk = n // NA_TOK
    grp = NA_W // LANES

    def blk(cb, shift):
        return pl.BlockSpec((1, NA_TOK, NA_W),
                            lambda bi, i: (bi, jnp.clip(i + shift, 0, nblk - 1), cb // grp))

    lc = pc.shape[1]

    def ctx(cb):
        return pl.BlockSpec((1, lc, NA_W), lambda bi, i: (bi, 0, cb // grp))

    def bias_index(bi, i):
        kind = jnp.where(i == 0, 0, jnp.where(i == nblk - 1, 2, 1))
        return (kind, 0, 0, 0)

    return pl.pallas_call(
        _na_kernel,
        grid=(b, nblk),
        in_specs=[blk(CB_NQ, 0), blk(CB_NK, -1), blk(CB_NK, 0), blk(CB_NK, 1),
                  blk(CB_NV, -1), blk(CB_NV, 0), blk(CB_NV, 1), ctx(CB_NK), ctx(CB_NV), blk(CB_NG, 0),
                  pl.BlockSpec((1, NA_HEADS, NA_TOK, 3 * NA_TOK), bias_index)],
        out_specs=pl.BlockSpec((1, NA_TOK, NA_W), lambda bi, i: (bi, i, 0)),
        out_shape=jax.ShapeDtypeStruct((b, n, NA_W), BF16),
        compiler_params=pltpu.CompilerParams(dimension_semantics=("parallel", "arbitrary")),
        name="neighbourhood",
    )(p, p, p, p, p, p, p, pc, pc, p, bias)


def _ctx_attn_kernel(q_ref, k_ref, v_ref, g_ref, o_ref):
    q, k, v = q_ref[0], k_ref[0], v_ref[0]
    head0 = _lane_is_head0(q.shape)
    zero = jnp.zeros_like(q)
    outs = []
    for hh in range(2):
        qh = jnp.where(head0, q, zero) if hh == 0 else jnp.where(head0, zero, q)
        s = _dot_nt(qh, k)
        p = jnp.exp2(s - jnp.max(s, axis=-1, keepdims=True))
        outs.append(_dot(p.astype(BF16), v) / jnp.sum(p, axis=-1, keepdims=True))
    o = jnp.where(head0, outs[0], outs[1])
    o_ref[0] = (o * _silu(g_ref[0].astype(F32))).astype(o_ref.dtype)


def _ctx_attention(pc):
    b, lc, _ = pc.shape

    def col(cb):
        return pl.BlockSpec((1, lc, LANES), lambda bi, pi: (bi, 0, cb + pi))

    return pl.pallas_call(
        _ctx_attn_kernel,
        grid=(b, PAIRS),
        in_specs=[col(CB_NQ), col(CB_NK), col(CB_NV), col(CB_NG)],
        out_specs=col(0),
        out_shape=jax.ShapeDtypeStruct((b, lc, NA_W), BF16),
        name="context_attention",
    )(pc, pc, pc, pc)


def _out_kernel(x_ref, gate_ref, ch_ref, cb_ref, cc_ref, cz_ref, hp_ref, cp_ref, hn_ref, cn_ref,
                yr_ref, yn_ref, cw_ref, cbias_ref, w_ref, lg_ref, lb_ref, o_ref):
    i = pl.program_id(1)
    last = pl.num_programs(1) - 1
    u = cc_ref[0].astype(F32) * ch_ref[0].astype(F32)
    tm = u.shape[0]
    halo = hp_ref.shape[1]
    u_before = cp_ref[0, halo - 1:halo, :].astype(F32) * hp_ref[0, halo - 1:halo, :].astype(F32)
    u_after = cn_ref[0, 0:1, :].astype(F32) * hn_ref[0, 0:1, :].astype(F32)
    u_before = jnp.where(i == 0, 0.0, u_before)
    u_after = jnp.where(i == last, 0.0, u_after)
    row = lax.broadcasted_iota(jnp.int32, u.shape, 0)
    u_prev = jnp.where(row == 0, u_before, pltpu.roll(u, 1, axis=0))
    u_next = jnp.where(row == tm - 1, u_after, pltpu.roll(u, tm - 1, axis=0))
    cw = cw_ref[...]
    conv = u_prev * cw[0:1] + u * cw[1:2] + u_next * cw[2:3] + cbias_ref[...]
    y_conv = cb_ref[0].astype(F32) * conv * _silu(cz_ref[0].astype(F32))
    y = jnp.concatenate([y_conv.astype(BF16), yr_ref[0], yn_ref[0]], axis=-1)
    z = DEEPNORM_ALPHA * x_ref[0] + gate_ref[0] * _dot(y, w_ref[...])
    mu = jnp.mean(z, axis=-1, keepdims=True)
    dlt = z - mu
    var = jnp.mean(dlt * dlt, axis=-1, keepdims=True)
    o_ref[0] = dlt * lax.rsqrt(var + LN_EPS) * lg_ref[...] + lb_ref[...]


def _output(x, gate, p, y_ret, y_na, conv_w, conv_b, w_out_bf, layer, ln_g, ln_b):
    b, n, _ = x.shape
    tm = min(ROW_TILE, n)
    halo = 16
    per = tm // halo
    nh = n // halo

    def conv(cb):
        return pl.BlockSpec((1, tm, CONV_W), lambda bi, i: (bi, i, cb))

    def before(cb):
        return pl.BlockSpec((1, halo, CONV_W), lambda bi, i: (bi, jnp.maximum(i * per - 1, 0), cb))

    def after(cb):
        return pl.BlockSpec((1, halo, CONV_W), lambda bi, i: (bi, jnp.minimum((i + 1) * per, nh - 1), cb))

    def const(shape):
        return pl.BlockSpec(shape, lambda bi, i: (0,) * len(shape))

    return pl.pallas_call(
        _out_kernel,
        grid=(b, n // tm),
        in_specs=[pl.BlockSpec((1, tm, D_MODEL), lambda bi, i: (bi, i, 0)),
                  pl.BlockSpec((1, 1, D_MODEL), lambda bi, i: (bi, 0, 0)),
                  conv(CB_CH), conv(CB_CB), conv(CB_CC), conv(CB_CZ),
                  before(CB_CH), before(CB_CC), after(CB_CH), after(CB_CC),
                  pl.BlockSpec((1, tm, RET_W), lambda bi, i: (bi, i, 0)),
                  pl.BlockSpec((1, tm, NA_W), lambda bi, i: (bi, i, 0)),
                  const((3, CONV_W)), const((1, CONV_W)),
                  pl.BlockSpec((None, MIX_W, D_MODEL), lambda bi, i: (layer, 0, 0)),
                  const((1, D_MODEL)), const((1, D_MODEL))],
        out_specs=pl.BlockSpec((1, tm, D_MODEL), lambda bi, i: (bi, i, 0)),
        out_shape=jax.ShapeDtypeStruct((b, n, D_MODEL), F32),
        compiler_params=pltpu.CompilerParams(dimension_semantics=("parallel", "parallel")),
        name="output",
    )(x, gate, p, p, p, p, p, p, p, p, y_ret, y_na, conv_w, conv_b.reshape(1, CONV_W), w_out_bf,
      ln_g.reshape(1, D_MODEL), ln_b.reshape(1, D_MODEL))


def kernel(x, c, ctx, c_ctx, w_mod, b_mod, w_in, conv_w, conv_b, ret_decay, na_rpb, w_out, ln_g, ln_b):
    b, n, d = x.shape
    act_t = jnp.pad(jnp.concatenate([c, c_ctx[None]], axis=0).T, ((0, 0), (0, 5)))
    mod = _modulation(act_t, w_mod, b_mod)
    tables = _rope_tables(n)
    w_in_bf = w_in.astype(BF16)
    w_out_bf = w_out.astype(BF16)
    xc = ctx
    for l in range(DEPTH):
        need_ctx = l < DEPTH - 1
        shift, scale, gate = (mod[l, :b, None, j * d:(j + 1) * d] for j in range(3))
        shift_c, scale_c, gate_c = (jnp.broadcast_to(mod[l, b, j * d:(j + 1) * d], (b, 1, d)) for j in range(3))
        p = _projection(x, shift, scale, w_in_bf, l, tables, PROJ_W)
        pc = _projection(xc, shift_c, scale_c, w_in_bf, l, None, PROJ_W if need_ctx else KV_W)
        y_ret = _retention(p, pc, ret_decay[l], init_state=True)
        y_na = _neighbourhood(p, pc, _na_bias_tables(na_rpb[l], n // GRID_W))
        x_new = _output(x, gate, p, y_ret, y_na, conv_w[l], conv_b[l], w_out_bf, l, ln_g[l], ln_b[l])
        if need_ctx:
            yc_ret = _retention(pc, None, ret_decay[l], init_state=False)
            yc_na = _ctx_attention(pc)
            xc = _output(xc, gate_c, pc, yc_ret, yc_na, conv_w[l], conv_b[l], w_out_bf, l, ln_g[l], ln_b[l])
        x = x_new
    return x
```

```python
import functools

import numpy as np
import jax
import jax.numpy as jnp
from jax import lax
from jax.experimental import pallas as pl
from jax.experimental.pallas import tpu as pltpu

D_MODEL = 1024
DEPTH = 2
GRID_W = 64
HEAD_DIM = 64
CONV_W = 256
RET_HEADS = 6
RET_W = RET_HEADS * HEAD_DIM
NA_HEADS = 6
NA_W = NA_HEADS * HEAD_DIM
MIX_W = CONV_W + RET_W + NA_W
RET_CHUNK = 128
WIN_H = 8
WIN_W = 16
ROPE_BASE = 10000.0
LN_EPS = 1e-5
DEEPNORM_ALPHA = (2 * DEPTH) ** 0.25
PROJ_SPLITS = (RET_W, RET_W, NA_W, NA_W, RET_W, RET_W, NA_W, NA_W, CONV_W, CONV_W, CONV_W, CONV_W)
PROJ_W = sum(PROJ_SPLITS)
KV_W = 2 * RET_W + 2 * NA_W

LANES = 128
PAIRS = RET_HEADS // 2
CB_RK, CB_RV, CB_NK, CB_NV, CB_RQ, CB_RG, CB_NQ, CB_NG = 0, 3, 6, 9, 12, 15, 18, 21
CB_CH, CB_CB, CB_CC, CB_CZ = 12, 13, 14, 15
QK_SCALE = HEAD_DIM ** -0.5
NEG = -1e30
LOG2E = 1.4426950408889634
NA_ROWS = 4
NA_TOK = NA_ROWS * GRID_W
RET_GROUP = 8
ROW_TILE = 1024
PROJ_CHUNK = 512

F32 = jnp.float32
BF16 = jnp.bfloat16


def _silu(v):
    return v * jax.nn.sigmoid(v)


def _dot(a, b):
    return jnp.dot(a, b, preferred_element_type=F32)


def _dot_nt(a, b):
    return lax.dot_general(a, b, (((1,), (1,)), ((), ())), preferred_element_type=F32)


def _dot_tn(a, b):
    return lax.dot_general(a, b, (((0,), (0,)), ((), ())), preferred_element_type=F32)


def _lane_is_head0(shape):
    return lax.broadcasted_iota(jnp.int32, shape, len(shape) - 1) < HEAD_DIM


def _mod_kernel(act_ref, w_ref, b_ref, o_ref):
    a = _silu(act_ref[...])
    w = w_ref[0]
    bias = b_ref[0]
    for r in range(3):
        o_ref[0, r:r + 1, :] = jnp.sum(a[:, r:r + 1] * w, axis=0, keepdims=True) + bias
    o_ref[0, 3:8, :] = jnp.zeros((5, w.shape[1]), F32)


def _modulation(act_t, w_mod, b_mod):
    tn = 1536
    n = w_mod.shape[-1]
    return pl.pallas_call(
        _mod_kernel,
        grid=(DEPTH, n // tn),
        in_specs=[pl.BlockSpec((D_MODEL, 8), lambda l, j: (0, 0)),
                  pl.BlockSpec((1, D_MODEL, tn), lambda l, j: (l, 0, j)),
                  pl.BlockSpec((1, 1, tn), lambda l, j: (l, 0, j))],
        out_specs=pl.BlockSpec((1, 8, tn), lambda l, j: (l, 0, j)),
        out_shape=jax.ShapeDtypeStruct((DEPTH, 8, n), F32),
        name="modulation",
    )(act_t, w_mod, b_mod.reshape(DEPTH, 1, n))


def _rope_tables(n):
    rows = n // GRID_W
    nf = HEAD_DIM // 4
    inv = ROPE_BASE ** (-jnp.arange(nf, dtype=F32) / nf)
    lane = np.arange(LANES)
    by_row = ((lane % HEAD_DIM) < HEAD_DIM // 2)[None]
    first = ((lane % (2 * nf)) < nf)[None]
    ang_r = jnp.tile(jnp.arange(rows).astype(F32)[:, None] * inv, (1, LANES // nf))
    ang_c = jnp.tile(jnp.arange(GRID_W).astype(F32)[:, None] * inv, (1, LANES // nf))
    parts = []
    for ang, own in ((ang_r, by_row), (ang_c, ~by_row)):
        cos, sin = jnp.cos(ang), jnp.sin(ang)
        parts.append((jnp.where(own, cos, 0.0), jnp.where(own & first, -sin, 0.0), jnp.where(own & ~first, sin, 0.0)))
    return parts[0] + parts[1]


def _proj_kernel(x_ref, shift_ref, scale_ref, w_ref, *rest, rope, n_cols):
    if rope:
        *table_refs, o_ref = rest
        cos, up, dn = (jnp.concatenate([r_ref[g:g + 1, :] + c_ref[...] for g in range(r_ref.shape[0])], axis=0)
                       for r_ref, c_ref in zip(table_refs[:3], table_refs[3:]))
    else:
        (o_ref,) = rest
    h = (x_ref[0] * (1.0 + scale_ref[0]) + shift_ref[0]).astype(BF16)
    rotated = tuple(range(CB_RK, CB_RK + PAIRS)) + tuple(range(CB_RQ, CB_RQ + PAIRS))
    col_scale = {cb: QK_SCALE for cb in range(CB_RK, CB_RK + PAIRS)}
    col_scale.update({cb: QK_SCALE * LOG2E for cb in range(CB_NQ, CB_NQ + PAIRS)})
    for off in range(0, n_cols, PROJ_CHUNK):
        acc = _dot(h, w_ref[:, off:off + PROJ_CHUNK])
        tiles = []
        for j in range(PROJ_CHUNK // LANES):
            cb = off // LANES + j
            v = acc[:, j * LANES:(j + 1) * LANES]
            if rope and cb in rotated:
                v = v * cos + pltpu.roll(v, LANES - 16, axis=1) * up + pltpu.roll(v, 16, axis=1) * dn
            if cb in col_scale:
                v = v * col_scale[cb]
            tiles.append(v.astype(BF16))
        o_ref[0, :, off:off + PROJ_CHUNK] = jnp.concatenate(tiles, axis=1)


def _projection(x, shift, scale, w_bf, layer, tables, n_cols):
    b, n, _ = x.shape
    tm = min(ROW_TILE, n)
    rope = tables is not None
    in_specs = [pl.BlockSpec((1, tm, D_MODEL), lambda bi, i: (bi, i, 0)),
                pl.BlockSpec((1, 1, D_MODEL), lambda bi, i: (bi, 0, 0)),
                pl.BlockSpec((1, 1, D_MODEL), lambda bi, i: (bi, 0, 0)),
                pl.BlockSpec((None, D_MODEL, n_cols), lambda bi, i: (layer, 0, 0))]
    args = [x, shift, scale, w_bf]
    if rope:
        in_specs += ([pl.BlockSpec((tm // GRID_W, LANES), lambda bi, i: (i, 0))] * 3
                     + [pl.BlockSpec((GRID_W, LANES), lambda bi, i: (0, 0))] * 3)
        args += list(tables)
    return pl.pallas_call(
        functools.partial(_proj_kernel, rope=rope, n_cols=n_cols),
        grid=(b, n // tm),
        in_specs=in_specs,
        out_specs=pl.BlockSpec((1, tm, n_cols), lambda bi, i: (bi, i, 0)),
        out_shape=jax.ShapeDtypeStruct((b, n, n_cols), BF16),
        compiler_params=pltpu.CompilerParams(dimension_semantics=("parallel", "parallel")),
        name="projection_rope" if rope else "projection_ctx",
    )(*args)


def _log_sigmoid(v):
    return jnp.minimum(v, 0.0) - jnp.log1p(jnp.exp(-jnp.abs(v)))


def _block_diag(m):
    r = lax.broadcasted_iota(jnp.int32, m.shape, 0) < HEAD_DIM
    c = lax.broadcasted_iota(jnp.int32, m.shape, 1) < HEAD_DIM
    return jnp.where(r == c, m, 0.0)


def _head_mean(x):
    head0 = _lane_is_head0(x.shape)
    s0 = jnp.sum(jnp.where(head0, x, 0.0), axis=-1, keepdims=True)
    s1 = jnp.sum(jnp.where(head0, 0.0, x), axis=-1, keepdims=True)
    return jnp.where(head0, s0, s1) * (1.0 / HEAD_DIM)


def _ret_kernel(*refs, init_state, group):
    refs = list(refs)
    dec_ref, q_ref, k_ref, v_ref, g_ref = refs[:5]
    refs = refs[5:]
    if init_state:
        kc_ref, vc_ref = refs[:2]
        refs = refs[2:]
    o_ref, s_ref, u_ref, kt_ref, wq_ref, wkt_ref, d_ref = refs
    c = RET_CHUNK
    n_chunks = q_ref.shape[1] // c
    n_groups = n_chunks // group
    head0 = _lane_is_head0((c, LANES))
    lg_f = _log_sigmoid(dec_ref[0, 0])
    lg_b = _log_sigmoid(dec_ref[1, 0])
    i = lax.broadcasted_iota(jnp.int32, (c, LANES), 0).astype(F32)
    wq_ref[:, :LANES] = jnp.exp((i + 1.0) * lg_f)
    wq_ref[:, LANES:] = jnp.exp((c - i) * lg_b)
    wkt_ref[:LANES, :] = jnp.exp((c - 1.0 - i) * lg_f).T
    wkt_ref[LANES:, :] = jnp.exp(i * lg_b).T
    gc_f = jnp.exp(float(c) * lg_f)
    gc_b = jnp.exp(float(c) * lg_b)
    diff = (lax.broadcasted_iota(jnp.int32, (c, c), 0) - lax.broadcasted_iota(jnp.int32, (c, c), 1)).astype(F32)
    lower = diff >= 0
    for hh in range(2):
        lf = lg_f[:, hh * HEAD_DIM:hh * HEAD_DIM + 1]
        lb = lg_b[:, hh * HEAD_DIM:hh * HEAD_DIM + 1]
        d_ref[:, hh * c:(hh + 1) * c] = jnp.where(lower, jnp.exp(jnp.where(lower, diff, 0.0) * lf),
                                                  jnp.exp(jnp.where(lower, 0.0, -diff) * lb))
    if init_state:
        lc = kc_ref.shape[1]
        m = lax.broadcasted_iota(jnp.int32, (lc, LANES), 0).astype(F32)
        kcf = kc_ref[0].astype(F32)
        s_f0 = _block_diag(_dot_tn((kcf * jnp.exp((lc - 1.0 - m) * lg_f)).astype(BF16), vc_ref[0]))
        s_b0 = _block_diag(_dot_tn((kcf * jnp.exp(m * lg_b)).astype(BF16), vc_ref[0]))
    else:
        s_f0 = s_b0 = jnp.zeros((LANES, LANES), F32)
    r2 = lax.broadcasted_iota(jnp.int32, (2 * LANES, LANES), 0) % LANES < HEAD_DIM
    c2 = lax.broadcasted_iota(jnp.int32, (2 * LANES, LANES), 1) < HEAD_DIM
    diag2 = r2 == c2

    chan0 = lax.broadcasted_iota(jnp.int32, (LANES, c), 0) < HEAD_DIM

    def chunk(n):
        return pl.ds(pl.multiple_of(n * c, c), c)

    def increments(gi, carry):
        for j in range(group):
            n = gi * group + j
            kt = k_ref[0, chunk(n), :].astype(F32).T
            ktb = kt.astype(BF16)
            zero = jnp.zeros_like(ktb)
            kt_ref[n] = jnp.concatenate([jnp.where(chan0, ktb, zero), jnp.where(chan0, zero, ktb)], axis=1)
            lhs = jnp.concatenate([kt * wkt_ref[:LANES, :], kt * wkt_ref[LANES:, :]], axis=0).astype(BF16)
            u_ref[n] = jnp.where(diag2, _dot(lhs, v_ref[0, chunk(n), :]), 0.0)
        return carry

    lax.fori_loop(0, n_groups, increments, 0)

    def fwd_scan(gi, s):
        for j in range(group):
            n = gi * group + j
            s_ref[n, :LANES, :] = s.astype(BF16)
            s = s * gc_f + u_ref[n, :LANES, :]
        return s

    def bwd_scan(gi, s):
        for j in range(group):
            n = n_chunks - 1 - (gi * group + j)
            s_ref[n, LANES:, :] = s.astype(BF16)
            s = s * gc_b + u_ref[n, LANES:, :]
        return s

    lax.fori_loop(0, n_groups, fwd_scan, s_f0)
    lax.fori_loop(0, n_groups, bwd_scan, s_b0)

    def body(gi, carry):
        outs = []
        for j in range(group):
            n = gi * group + j
            rows = chunk(n)
            q, v = q_ref[0, rows, :], v_ref[0, rows, :]
            zero = jnp.zeros_like(v)
            vcat = jnp.concatenate([jnp.where(head0, v, zero), jnp.where(head0, zero, v)], axis=0)
            scores = _dot(q, kt_ref[n]) * d_ref[...]
            qf = q.astype(F32)
            qw = jnp.concatenate([qf * wq_ref[:, :LANES], qf * wq_ref[:, LANES:]], axis=1).astype(BF16)
            outs.append(_dot(scores.astype(BF16), vcat) + _dot(qw, s_ref[n]))
        o = jnp.concatenate(outs, axis=0)
        rows = pl.ds(pl.multiple_of(gi * (group * c), group * c), group * c)
        dlt = o - _head_mean(o)
        var = _head_mean(dlt * dlt)
        y = dlt * lax.rsqrt(var + LN_EPS) * _silu(g_ref[0, rows, :].astype(F32))
        o_ref[0, rows, :] = y.astype(o_ref.dtype)
        return carry

    lax.fori_loop(0, n_groups, body, 0)


def _retention(p, pc, ret_decay, init_state):
    b, n, _ = p.shape
    dec = jnp.repeat(ret_decay.astype(F32), HEAD_DIM, axis=-1).reshape(2, PAIRS, 1, LANES)

    def col(cb):
        return pl.BlockSpec((1, n, LANES), lambda bi, pi: (bi, 0, cb + pi))

    in_specs = [pl.BlockSpec((2, 1, 1, LANES), lambda bi, pi: (0, pi, 0, 0)),
                col(CB_RQ), col(CB_RK), col(CB_RV), col(CB_RG)]
    args = [dec, p, p, p, p]
    if init_state:
        lc = pc.shape[1]
        in_specs += [pl.BlockSpec((1, lc, LANES), lambda bi, pi: (bi, 0, CB_RK + pi)),
                     pl.BlockSpec((1, lc, LANES), lambda bi, pi: (bi, 0, CB_RV + pi))]
        args += [pc, pc]
    c = RET_CHUNK
    return pl.pallas_call(
        functools.partial(_ret_kernel, init_state=init_state, group=min(RET_GROUP, n // c)),
        grid=(b, PAIRS),
        in_specs=in_specs,
        out_specs=col(0),
        out_shape=jax.ShapeDtypeStruct((b, n, RET_W), BF16),
        scratch_shapes=[pltpu.VMEM((n // c, 2 * LANES, LANES), BF16),
                        pltpu.VMEM((n // c, 2 * LANES, LANES), F32),
                        pltpu.VMEM((n // c, LANES, 2 * c), BF16),
                        pltpu.VMEM((c, 2 * LANES), F32),
                        pltpu.VMEM((2 * LANES, c), F32),
                        pltpu.VMEM((c, 2 * c), F32)],
        compiler_params=pltpu.CompilerParams(dimension_semantics=("parallel", "parallel")),
        name="retention",
    )(*args)


def _na_bias_plan(rows):
    nblk = rows // NA_ROWS
    win_h = min(WIN_H, rows)
    plan = np.full((3, NA_ROWS, 3, NA_ROWS), -1, np.int64)
    for kind, blk in enumerate((0, 1, nblk - 1)):
        for qr in range(NA_ROWS):
            r = blk * NA_ROWS + qr
            rs = min(max(r - win_h // 2, 0), rows - win_h)
            for kb in range(3):
                if not 0 <= blk + kb - 1 <= nblk - 1:
                    continue
                for krl in range(NA_ROWS):
                    kr = (blk + kb - 1) * NA_ROWS + krl
                    if rs <= kr < rs + win_h:
                        plan[kind, qr, kb, krl] = kr - r + WIN_H - 1
    return plan


def _na_bias_kernel(rpb_ref, o_ref, t_ref, *, plan):
    n_dr = 2 * WIN_H - 1
    qc = lax.broadcasted_iota(jnp.int32, (GRID_W, LANES), 0)
    lane = lax.broadcasted_iota(jnp.int32, (GRID_W, LANES), 1)
    kc = lane % GRID_W
    cs = jnp.clip(qc - WIN_W // 2, 0, GRID_W - WIN_W)
    col_ok = (kc >= cs) & (kc < cs + WIN_W)
    first = lane < GRID_W
    neg = jnp.full((GRID_W, LANES), NEG, F32)
    for dr in range(n_dr):
        row = jnp.broadcast_to(rpb_ref[0, 0, dr:dr + 1, :] * LOG2E, (GRID_W, LANES))
        lo = pltpu.roll(row, LANES - (WIN_W - 1), 1, stride=1, stride_axis=0)
        hi = pltpu.roll(row, GRID_W - (WIN_W - 1), 1, stride=1, stride_axis=0)
        t_ref[dr] = jnp.where(col_ok, jnp.where(first, lo, hi), neg)
    for kind in range(3):
        for qr in range(NA_ROWS):
            for kb in range(3):
                for pr in range(NA_ROWS // 2):
                    ia, ib = (int(plan[kind, qr, kb, 2 * pr + s]) for s in range(2))
                    a = t_ref[ia] if ia >= 0 else neg
                    b = t_ref[ib] if ib >= 0 else neg
                    col0 = kb * NA_TOK + pr * LANES
                    o_ref[0, kind, 0, qr * GRID_W:(qr + 1) * GRID_W, col0:col0 + LANES] = jnp.where(first, a, b)


def _na_bias_tables(rpb, rows):
    depth = rpb.shape[0]
    n_dr, n_dc = 2 * WIN_H - 1, 2 * WIN_W - 1
    padded = jnp.pad(rpb.astype(F32), ((0, 0), (0, 0), (0, 16 - n_dr), (0, LANES - n_dc)))
    return pl.pallas_call(
        functools.partial(_na_bias_kernel, plan=_na_bias_plan(rows)),
        grid=(depth, NA_HEADS),
        in_specs=[pl.BlockSpec((1, 1, 16, LANES), lambda l, h: (l, h, 0, 0))],
        out_specs=pl.BlockSpec((1, 3, 1, NA_TOK, 3 * NA_TOK), lambda l, h: (l, 0, h, 0, 0)),
        out_shape=jax.ShapeDtypeStruct((depth, 3, NA_HEADS, NA_TOK, 3 * NA_TOK), F32),
        scratch_shapes=[pltpu.VMEM((n_dr, GRID_W, LANES), F32)],
        name="na_bias",
    )(padded)


def _fold_lanes(blocks, op):
    tiles = [blk[:, j:j + LANES] for blk in blocks for j in range(0, blk.shape[1], LANES)]
    acc = tiles[0]
    for t in tiles[1:]:
        acc = op(acc, t)
    return acc


def _na_kernel(q_ref, kp_ref, kc_ref, kn_ref, vp_ref, vc_ref, vn_ref, kx_ref, vx_ref, g_ref, bias_ref, o_ref):
    head0 = _lane_is_head0((NA_TOK, LANES))
    for pi in range(PAIRS):
        cols = slice(pi * LANES, (pi + 1) * LANES)
        q = q_ref[0, :, cols]
        zero = jnp.zeros_like(q)
        keys = (kp_ref[0, :, cols], kc_ref[0, :, cols], kn_ref[0, :, cols], kx_ref[0, :, cols])
        vals = (vp_ref[0, :, cols], vc_ref[0, :, cols], vn_ref[0, :, cols], vx_ref[0, :, cols])
        probs, inv = [], []
        for hh in range(2):
            qh = jnp.where(head0, q, zero) if hh == 0 else jnp.where(head0, zero, q)
            s = [_dot_nt(qh, keys[j]) + bias_ref[0, 2 * pi + hh, :, j * NA_TOK:(j + 1) * NA_TOK] for j in range(3)]
            s.append(_dot_nt(qh, keys[3]))
            m = jnp.max(_fold_lanes(s, jnp.maximum), axis=-1, keepdims=True)
            p = [jnp.exp2(sj - m) for sj in s]
            inv.append(1.0 / jnp.sum(_fold_lanes(p, jnp.add), axis=-1, keepdims=True))
            probs += [pj.astype(BF16) for pj in p]
        v_cat = jnp.concatenate([jnp.where(head0, v, zero) for v in vals]
                                + [jnp.where(head0, zero, v) for v in vals], axis=0)
        o = _dot(jnp.concatenate(probs, axis=1), v_cat) * jnp.where(head0, inv[0], inv[1])
        o_ref[0, :, cols] = (o * _silu(g_ref[0, :, cols].astype(F32))).astype(o_ref.dtype)


def _neighbourhood(p, pc, bias, layer):
    b, n, _ = p.shape
    nblk = n // NA_TOK
    grp = NA_W // LANES

    def blk(cb, shift):
        return pl.BlockSpec((1, NA_TOK, NA_W),
                            lambda bi, i: (bi, jnp.clip(i + shift, 0, nblk - 1), cb // grp))

    lc = pc.shape[1]

    def ctx(cb):
        return pl.BlockSpec((1, lc, NA_W), lambda bi, i: (bi, 0, cb // grp))

    def bias_index(bi, i):
        kind = jnp.where(i == 0, 0, jnp.where(i == nblk - 1, 2, 1))
        return (layer, kind, 0, 0, 0)

    return pl.pallas_call(
        _na_kernel,
        grid=(b, nblk),
        in_specs=[blk(CB_NQ, 0), blk(CB_NK, -1), blk(CB_NK, 0), blk(CB_NK, 1),
                  blk(CB_NV, -1), blk(CB_NV, 0), blk(CB_NV, 1), ctx(CB_NK), ctx(CB_NV), blk(CB_NG, 0),
                  pl.BlockSpec((None, 1, NA_HEADS, NA_TOK, 3 * NA_TOK), bias_index)],
        out_specs=pl.BlockSpec((1, NA_TOK, NA_W), lambda bi, i: (bi, i, 0)),
        out_shape=jax.ShapeDtypeStruct((b, n, NA_W), BF16),
        compiler_params=pltpu.CompilerParams(dimension_semantics=("parallel", "arbitrary")),
        name="neighbourhood",
    )(p, p, p, p, p, p, p, pc, pc, p, bias)


def _ctx_attn_kernel(q_ref, k_ref, v_ref, g_ref, o_ref):
    q, k, v = q_ref[0], k_ref[0], v_ref[0]
    head0 = _lane_is_head0(q.shape)
    zero = jnp.zeros_like(q)
    outs = []
    for hh in range(2):
        qh = jnp.where(head0, q, zero) if hh == 0 else jnp.where(head0, zero, q)
        s = _dot_nt(qh, k)
        p = jnp.exp2(s - jnp.max(s, axis=-1, keepdims=True))
        outs.append(_dot(p.astype(BF16), v) / jnp.sum(p, axis=-1, keepdims=True))
    o = jnp.where(head0, outs[0], outs[1])
    o_ref[0] = (o * _silu(g_ref[0].astype(F32))).astype(o_ref.dtype)


def _ctx_attention(pc):
    b, lc, _ = pc.shape

    def col(cb):
        return pl.BlockSpec((1, lc, LANES), lambda bi, pi: (bi, 0, cb + pi))

    return pl.pallas_call(
        _ctx_attn_kernel,
        grid=(b, PAIRS),
        in_specs=[col(CB_NQ), col(CB_NK), col(CB_NV), col(CB_NG)],
        out_specs=col(0),
        out_shape=jax.ShapeDtypeStruct((b, lc, NA_W), BF16),
        name="context_attention",
    )(pc, pc, pc, pc)


def _out_kernel(x_ref, gate_ref, ch_ref, cb_ref, cc_ref, cz_ref, hp_ref, cp_ref, hn_ref, cn_ref,
                yr_ref, yn_ref, cw_ref, cbias_ref, w_ref, lg_ref, lb_ref, o_ref):
    i = pl.program_id(1)
    last = pl.num_programs(1) - 1
    u = cc_ref[0].astype(F32) * ch_ref[0].astype(F32)
    tm = u.shape[0]
    halo = hp_ref.shape[1]
    u_before = cp_ref[0, halo - 1:halo, :].astype(F32) * hp_ref[0, halo - 1:halo, :].astype(F32)
    u_after = cn_ref[0, 0:1, :].astype(F32) * hn_ref[0, 0:1, :].astype(F32)
    u_before = jnp.where(i == 0, 0.0, u_before)
    u_after = jnp.where(i == last, 0.0, u_after)
    row = lax.broadcasted_iota(jnp.int32, u.shape, 0)
    u_prev = jnp.where(row == 0, u_before, pltpu.roll(u, 1, axis=0))
    u_next = jnp.where(row == tm - 1, u_after, pltpu.roll(u, tm - 1, axis=0))
    cw = cw_ref[...]
    conv = u_prev * cw[0:1] + u * cw[1:2] + u_next * cw[2:3] + cbias_ref[...]
    y_conv = cb_ref[0].astype(F32) * conv * _silu(cz_ref[0].astype(F32))
    y = jnp.concatenate([y_conv.astype(BF16), yr_ref[0], yn_ref[0]], axis=-1)
    z = DEEPNORM_ALPHA * x_ref[0] + gate_ref[0] * _dot(y, w_ref[...])
    mu = jnp.mean(z, axis=-1, keepdims=True)
    dlt = z - mu
    var = jnp.mean(dlt * dlt, axis=-1, keepdims=True)
    o_ref[0] = dlt * lax.rsqrt(var + LN_EPS) * lg_ref[...] + lb_ref[...]


def _output(x, gate, p, y_ret, y_na, conv_w, conv_b, w_out_bf, layer, ln_g, ln_b):
    b, n, _ = x.shape
    tm = min(ROW_TILE, n)
    halo = 16
    per = tm // halo
    nh = n // halo

    def conv(cb):
        return pl.BlockSpec((1, tm, CONV_W), lambda bi, i: (bi, i, cb))

    def before(cb):
        return pl.BlockSpec((1, halo, CONV_W), lambda bi, i: (bi, jnp.maximum(i * per - 1, 0), cb))

    def after(cb):
        return pl.BlockSpec((1, halo, CONV_W), lambda bi, i: (bi, jnp.minimum((i + 1) * per, nh - 1), cb))

    def const(shape):
        return pl.BlockSpec(shape, lambda bi, i: (0,) * len(shape))

    return pl.pallas_call(
        _out_kernel,
        grid=(b, n // tm),
        in_specs=[pl.BlockSpec((1, tm, D_MODEL), lambda bi, i: (bi, i, 0)),
                  pl.BlockSpec((1, 1, D_MODEL), lambda bi, i: (bi, 0, 0)),
                  conv(CB_CH), conv(CB_CB), conv(CB_CC), conv(CB_CZ),
                  before(CB_CH), before(CB_CC), after(CB_CH), after(CB_CC),
                  pl.BlockSpec((1, tm, RET_W), lambda bi, i: (bi, i, 0)),
                  pl.BlockSpec((1, tm, NA_W), lambda bi, i: (bi, i, 0)),
                  const((3, CONV_W)), const((1, CONV_W)),
                  pl.BlockSpec((None, MIX_W, D_MODEL), lambda bi, i: (layer, 0, 0)),
                  const((1, D_MODEL)), const((1, D_MODEL))],
        out_specs=pl.BlockSpec((1, tm, D_MODEL), lambda bi, i: (bi, i, 0)),
        out_shape=jax.ShapeDtypeStruct((b, n, D_MODEL), F32),
        compiler_params=pltpu.CompilerParams(dimension_semantics=("parallel", "parallel")),
        name="output",
    )(x, gate, p, p, p, p, p, p, p, p, y_ret, y_na, conv_w, conv_b.reshape(1, CONV_W), w_out_bf,
      ln_g.reshape(1, D_MODEL), ln_b.reshape(1, D_MODEL))


def kernel(x, c, ctx, c_ctx, w_mod, b_mod, w_in, conv_w, conv_b, ret_decay, na_rpb, w_out, ln_g, ln_b):
    b, n, d = x.shape
    act_t = jnp.pad(jnp.concatenate([c, c_ctx[None]], axis=0).T, ((0, 0), (0, 5)))
    mod = _modulation(act_t, w_mod, b_mod)
    tables = _rope_tables(n)
    w_in_bf = w_in.astype(BF16)
    w_out_bf = w_out.astype(BF16)
    na_bias = _na_bias_tables(na_rpb, n // GRID_W)
    xc = ctx
    for l in range(DEPTH):
        need_ctx = l < DEPTH - 1
        shift, scale, gate = (mod[l, :b, None, j * d:(j + 1) * d] for j in range(3))
        shift_c, scale_c, gate_c = (jnp.broadcast_to(mod[l, b, j * d:(j + 1) * d], (b, 1, d)) for j in range(3))
        p = _projection(x, shift, scale, w_in_bf, l, tables, PROJ_W)
        pc = _projection(xc, shift_c, scale_c, w_in_bf, l, None, PROJ_W if need_ctx else KV_W)
        y_ret = _retention(p, pc, ret_decay[l], init_state=True)
        y_na = _neighbourhood(p, pc, na_bias, l)
        x_new = _output(x, gate, p, y_ret, y_na, conv_w[l], conv_b[l], w_out_bf, l, ln_g[l], ln_b[l])
        if need_ctx:
            yc_ret = _retention(pc, None, ret_decay[l], init_state=False)
            yc_na = _ctx_attention(pc)
            xc = _output(xc, gate_c, pc, yc_ret, yc_na, conv_w[l], conv_b[l], w_out_bf, l, ln_g[l], ln_b[l])
        x = x_new
    return x
```

```python
import functools

import numpy as np
import jax
import jax.numpy as jnp
from jax import lax
from jax.experimental import pallas as pl
from jax.experimental.pallas import tpu as pltpu

D_MODEL = 1024
DEPTH = 2
GRID_W = 64
HEAD_DIM = 64
CONV_W = 256
RET_HEADS = 6
RET_W = RET_HEADS * HEAD_DIM
NA_HEADS = 6
NA_W = NA_HEADS * HEAD_DIM
MIX_W = CONV_W + RET_W + NA_W
RET_CHUNK = 128
WIN_H = 8
WIN_W = 16
ROPE_BASE = 10000.0
LN_EPS = 1e-5
DEEPNORM_ALPHA = (2 * DEPTH) ** 0.25
PROJ_SPLITS = (RET_W, RET_W, NA_W, NA_W, RET_W, RET_W, NA_W, NA_W, CONV_W, CONV_W, CONV_W, CONV_W)
PROJ_W = sum(PROJ_SPLITS)
KV_W = 2 * RET_W + 2 * NA_W

LANES = 128
PAIRS = RET_HEADS // 2
CB_RK, CB_RV, CB_NK, CB_NV, CB_RQ, CB_RG, CB_NQ, CB_NG = 0, 3, 6, 9, 12, 15, 18, 21
CB_CH, CB_CB, CB_CC, CB_CZ = 12, 13, 14, 15
QK_SCALE = HEAD_DIM ** -0.5
NEG = -1e30
LOG2E = 1.4426950408889634
NA_ROWS = 4
NA_TOK = NA_ROWS * GRID_W
NA_STEP = 2
RET_GROUP = 8
ROW_TILE = 1024
PROJ_CHUNK = 512

F32 = jnp.float32
BF16 = jnp.bfloat16


def _silu(v):
    return v * jax.nn.sigmoid(v)


def _dot(a, b):
    return jnp.dot(a, b, preferred_element_type=F32)


def _dot_nt(a, b):
    return lax.dot_general(a, b, (((1,), (1,)), ((), ())), preferred_element_type=F32)


def _dot_tn(a, b):
    return lax.dot_general(a, b, (((0,), (0,)), ((), ())), preferred_element_type=F32)


def _lane_is_head0(shape):
    return lax.broadcasted_iota(jnp.int32, shape, len(shape) - 1) < HEAD_DIM


def _mod_kernel(act_ref, w_ref, b_ref, o_ref):
    a = _silu(act_ref[...])
    w = w_ref[0]
    bias = b_ref[0]
    for r in range(3):
        o_ref[0, r:r + 1, :] = jnp.sum(a[:, r:r + 1] * w, axis=0, keepdims=True) + bias
    o_ref[0, 3:8, :] = jnp.zeros((5, w.shape[1]), F32)


def _modulation(act_t, w_mod, b_mod):
    tn = 1536
    n = w_mod.shape[-1]
    return pl.pallas_call(
        _mod_kernel,
        grid=(DEPTH, n // tn),
        in_specs=[pl.BlockSpec((D_MODEL, 8), lambda l, j: (0, 0)),
                  pl.BlockSpec((1, D_MODEL, tn), lambda l, j: (l, 0, j)),
                  pl.BlockSpec((1, 1, tn), lambda l, j: (l, 0, j))],
        out_specs=pl.BlockSpec((1, 8, tn), lambda l, j: (l, 0, j)),
        out_shape=jax.ShapeDtypeStruct((DEPTH, 8, n), F32),
        name="modulation",
    )(act_t, w_mod, b_mod.reshape(DEPTH, 1, n))


def _rope_tables(n):
    rows = n // GRID_W
    nf = HEAD_DIM // 4
    inv = ROPE_BASE ** (-jnp.arange(nf, dtype=F32) / nf)
    lane = np.arange(LANES)
    by_row = ((lane % HEAD_DIM) < HEAD_DIM // 2)[None]
    first = ((lane % (2 * nf)) < nf)[None]
    ang_r = jnp.tile(jnp.arange(rows).astype(F32)[:, None] * inv, (1, LANES // nf))
    ang_c = jnp.tile(jnp.arange(GRID_W).astype(F32)[:, None] * inv, (1, LANES // nf))
    parts = []
    for ang, own in ((ang_r, by_row), (ang_c, ~by_row)):
        cos, sin = jnp.cos(ang), jnp.sin(ang)
        parts.append((jnp.where(own, cos, 0.0), jnp.where(own & first, -sin, 0.0), jnp.where(own & ~first, sin, 0.0)))
    return parts[0] + parts[1]


def _proj_kernel(x_ref, shift_ref, scale_ref, w_ref, *rest, rope, n_cols):
    if rope:
        *table_refs, o_ref = rest
        cos, up, dn = (jnp.concatenate([r_ref[g:g + 1, :] + c_ref[...] for g in range(r_ref.shape[0])], axis=0)
                       for r_ref, c_ref in zip(table_refs[:3], table_refs[3:]))
    else:
        (o_ref,) = rest
    h = (x_ref[0] * (1.0 + scale_ref[0]) + shift_ref[0]).astype(BF16)
    rotated = tuple(range(CB_RK, CB_RK + PAIRS)) + tuple(range(CB_RQ, CB_RQ + PAIRS))
    col_scale = {cb: QK_SCALE for cb in range(CB_RK, CB_RK + PAIRS)}
    col_scale.update({cb: QK_SCALE * LOG2E for cb in range(CB_NQ, CB_NQ + PAIRS)})
    for off in range(0, n_cols, PROJ_CHUNK):
        acc = _dot(h, w_ref[:, off:off + PROJ_CHUNK].astype(BF16))
        tiles = []
        for j in range(PROJ_CHUNK // LANES):
            cb = off // LANES + j
            v = acc[:, j * LANES:(j + 1) * LANES]
            if rope and cb in rotated:
                v = v * cos + pltpu.roll(v, LANES - 16, axis=1) * up + pltpu.roll(v, 16, axis=1) * dn
            if cb in col_scale:
                v = v * col_scale[cb]
            tiles.append(v.astype(BF16))
        o_ref[0, :, off:off + PROJ_CHUNK] = jnp.concatenate(tiles, axis=1)


def _projection(x, shift, scale, w_in, layer, tables, n_cols):
    b, n, _ = x.shape
    tm = min(ROW_TILE, n)
    rope = tables is not None
    in_specs = [pl.BlockSpec((1, tm, D_MODEL), lambda bi, i: (bi, i, 0)),
                pl.BlockSpec((1, 1, D_MODEL), lambda bi, i: (bi, 0, 0)),
                pl.BlockSpec((1, 1, D_MODEL), lambda bi, i: (bi, 0, 0)),
                pl.BlockSpec((None, D_MODEL, n_cols), lambda bi, i: (layer, 0, 0), pipeline_mode=pl.Buffered(1))]
    args = [x, shift, scale, w_in]
    if rope:
        in_specs += ([pl.BlockSpec((tm // GRID_W, LANES), lambda bi, i: (i, 0))] * 3
                     + [pl.BlockSpec((GRID_W, LANES), lambda bi, i: (0, 0))] * 3)
        args += list(tables)
    return pl.pallas_call(
        functools.partial(_proj_kernel, rope=rope, n_cols=n_cols),
        grid=(b, n // tm),
        in_specs=in_specs,
        out_specs=pl.BlockSpec((1, tm, n_cols), lambda bi, i: (bi, i, 0)),
        out_shape=jax.ShapeDtypeStruct((b, n, n_cols), BF16),
        compiler_params=pltpu.CompilerParams(dimension_semantics=("parallel", "parallel")),
        name="projection_rope" if rope else "projection_ctx",
    )(*args)


def _log_sigmoid(v):
    return jnp.minimum(v, 0.0) - jnp.log1p(jnp.exp(-jnp.abs(v)))


def _block_diag(m):
    r = lax.broadcasted_iota(jnp.int32, m.shape, 0) < HEAD_DIM
    c = lax.broadcasted_iota(jnp.int32, m.shape, 1) < HEAD_DIM
    return jnp.where(r == c, m, 0.0)


def _head_mean(x):
    head0 = _lane_is_head0(x.shape)
    s0 = jnp.sum(jnp.where(head0, x, 0.0), axis=-1, keepdims=True)
    s1 = jnp.sum(jnp.where(head0, 0.0, x), axis=-1, keepdims=True)
    return jnp.where(head0, s0, s1) * (1.0 / HEAD_DIM)


def _ret_kernel(*refs, init_state, group):
    refs = list(refs)
    dec_ref, q_ref, k_ref, v_ref, g_ref = refs[:5]
    refs = refs[5:]
    if init_state:
        kc_ref, vc_ref = refs[:2]
        refs = refs[2:]
    o_ref, s_ref, u_ref, kt_ref, wq_ref, wkt_ref, d_ref = refs
    c = RET_CHUNK
    n_chunks = q_ref.shape[1] // c
    n_groups = n_chunks // group
    head0 = _lane_is_head0((c, LANES))
    lg_f = _log_sigmoid(dec_ref[0, 0])
    lg_b = _log_sigmoid(dec_ref[1, 0])
    i = lax.broadcasted_iota(jnp.int32, (c, LANES), 0).astype(F32)
    wq_ref[:, :LANES] = jnp.exp((i + 1.0) * lg_f)
    wq_ref[:, LANES:] = jnp.exp((c - i) * lg_b)
    wkt_ref[:LANES, :] = jnp.exp((c - 1.0 - i) * lg_f).T
    wkt_ref[LANES:, :] = jnp.exp(i * lg_b).T
    gc_f = jnp.exp(float(c) * lg_f)
    gc_b = jnp.exp(float(c) * lg_b)
    diff = (lax.broadcasted_iota(jnp.int32, (c, c), 0) - lax.broadcasted_iota(jnp.int32, (c, c), 1)).astype(F32)
    lower = diff >= 0
    for hh in range(2):
        lf = lg_f[:, hh * HEAD_DIM:hh * HEAD_DIM + 1]
        lb = lg_b[:, hh * HEAD_DIM:hh * HEAD_DIM + 1]
        d_ref[:, hh * c:(hh + 1) * c] = jnp.where(lower, jnp.exp(jnp.where(lower, diff, 0.0) * lf),
                                                  jnp.exp(jnp.where(lower, 0.0, -diff) * lb))
    if init_state:
        lc = kc_ref.shape[1]
        m = lax.broadcasted_iota(jnp.int32, (lc, LANES), 0).astype(F32)
        kcf = kc_ref[0].astype(F32)
        s_f0 = _block_diag(_dot_tn((kcf * jnp.exp((lc - 1.0 - m) * lg_f)).astype(BF16), vc_ref[0]))
        s_b0 = _block_diag(_dot_tn((kcf * jnp.exp(m * lg_b)).astype(BF16), vc_ref[0]))
    else:
        s_f0 = s_b0 = jnp.zeros((LANES, LANES), F32)
    r2 = lax.broadcasted_iota(jnp.int32, (2 * LANES, LANES), 0) % LANES < HEAD_DIM
    c2 = lax.broadcasted_iota(jnp.int32, (2 * LANES, LANES), 1) < HEAD_DIM
    diag2 = r2 == c2

    chan0 = lax.broadcasted_iota(jnp.int32, (LANES, c), 0) < HEAD_DIM

    def chunk(n):
        return pl.ds(pl.multiple_of(n * c, c), c)

    def increments(gi, carry):
        for j in range(group):
            n = gi * group + j
            kt = k_ref[0, chunk(n), :].astype(F32).T
            ktb = kt.astype(BF16)
            zero = jnp.zeros_like(ktb)
            kt_ref[n] = jnp.concatenate([jnp.where(chan0, ktb, zero), jnp.where(chan0, zero, ktb)], axis=1)
            lhs = jnp.concatenate([kt * wkt_ref[:LANES, :], kt * wkt_ref[LANES:, :]], axis=0).astype(BF16)
            u_ref[n] = jnp.where(diag2, _dot(lhs, v_ref[0, chunk(n), :]), 0.0)
        return carry

    lax.fori_loop(0, n_groups, increments, 0)

    def fwd_scan(gi, s):
        for j in range(group):
            n = gi * group + j
            s_ref[n, :LANES, :] = s.astype(BF16)
            s = s * gc_f + u_ref[n, :LANES, :]
        return s

    def bwd_scan(gi, s):
        for j in range(group):
            n = n_chunks - 1 - (gi * group + j)
            s_ref[n, LANES:, :] = s.astype(BF16)
            s = s * gc_b + u_ref[n, LANES:, :]
        return s

    lax.fori_loop(0, n_groups, fwd_scan, s_f0)
    lax.fori_loop(0, n_groups, bwd_scan, s_b0)

    def body(gi, carry):
        outs = []
        for j in range(group):
            n = gi * group + j
            rows = chunk(n)
            q, v = q_ref[0, rows, :], v_ref[0, rows, :]
            zero = jnp.zeros_like(v)
            vcat = jnp.concatenate([jnp.where(head0, v, zero), jnp.where(head0, zero, v)], axis=0)
            scores = _dot(q, kt_ref[n]) * d_ref[...]
            qf = q.astype(F32)
            qw = jnp.concatenate([qf * wq_ref[:, :LANES], qf * wq_ref[:, LANES:]], axis=1).astype(BF16)
            outs.append(_dot(scores.astype(BF16), vcat) + _dot(qw, s_ref[n]))
        o = jnp.concatenate(outs, axis=0)
        rows = pl.ds(pl.multiple_of(gi * (group * c), group * c), group * c)
        dlt = o - _head_mean(o)
        var = _head_mean(dlt * dlt)
        y = dlt * lax.rsqrt(var + LN_EPS) * _silu(g_ref[0, rows, :].astype(F32))
        o_ref[0, rows, :] = y.astype(o_ref.dtype)
        return carry

    lax.fori_loop(0, n_groups, body, 0)


def _retention(p, pc, ret_decay, init_state):
    b, n, _ = p.shape
    dec = jnp.repeat(ret_decay.astype(F32), HEAD_DIM, axis=-1).reshape(2, PAIRS, 1, LANES)

    def col(cb):
        return pl.BlockSpec((1, n, LANES), lambda bi, pi: (bi, 0, cb + pi))

    in_specs = [pl.BlockSpec((2, 1, 1, LANES), lambda bi, pi: (0, pi, 0, 0)),
                col(CB_RQ), col(CB_RK), col(CB_RV), col(CB_RG)]
    args = [dec, p, p, p, p]
    if init_state:
        lc = pc.shape[1]
        in_specs += [pl.BlockSpec((1, lc, LANES), lambda bi, pi: (bi, 0, CB_RK + pi)),
                     pl.BlockSpec((1, lc, LANES), lambda bi, pi: (bi, 0, CB_RV + pi))]
        args += [pc, pc]
    c = RET_CHUNK
    return pl.pallas_call(
        functools.partial(_ret_kernel, init_state=init_state, group=min(RET_GROUP, n // c)),
        grid=(b, PAIRS),
        in_specs=in_specs,
        out_specs=col(0),
        out_shape=jax.ShapeDtypeStruct((b, n, RET_W), BF16),
        scratch_shapes=[pltpu.VMEM((n // c, 2 * LANES, LANES), BF16),
                        pltpu.VMEM((n // c, 2 * LANES, LANES), F32),
                        pltpu.VMEM((n // c, LANES, 2 * c), BF16),
                        pltpu.VMEM((c, 2 * LANES), F32),
                        pltpu.VMEM((2 * LANES, c), F32),
                        pltpu.VMEM((c, 2 * c), F32)],
        compiler_params=pltpu.CompilerParams(dimension_semantics=("parallel", "parallel")),
        name="retention",
    )(*args)


def _na_bias_plan(rows):
    nblk = rows // NA_ROWS
    win_h = min(WIN_H, rows)
    plan = np.full((3, NA_ROWS, 3, NA_ROWS), -1, np.int64)
    for kind, blk in enumerate((0, 1, nblk - 1)):
        for qr in range(NA_ROWS):
            r = blk * NA_ROWS + qr
            rs = min(max(r - win_h // 2, 0), rows - win_h)
            for kb in range(3):
                if not 0 <= blk + kb - 1 <= nblk - 1:
                    continue
                for krl in range(NA_ROWS):
                    kr = (blk + kb - 1) * NA_ROWS + krl
                    if rs <= kr < rs + win_h:
                        plan[kind, qr, kb, krl] = kr - r + WIN_H - 1
    return plan


def _na_bias_kernel(rpb_ref, o_ref, t_ref, *, plan):
    n_dr = 2 * WIN_H - 1
    qc = lax.broadcasted_iota(jnp.int32, (GRID_W, LANES), 0)
    lane = lax.broadcasted_iota(jnp.int32, (GRID_W, LANES), 1)
    kc = lane % GRID_W
    cs = jnp.clip(qc - WIN_W // 2, 0, GRID_W - WIN_W)
    col_ok = (kc >= cs) & (kc < cs + WIN_W)
    first = lane < GRID_W
    neg = jnp.full((GRID_W, LANES), NEG, F32)
    for dr in range(n_dr):
        row = jnp.broadcast_to(rpb_ref[0, 0, dr:dr + 1, :] * LOG2E, (GRID_W, LANES))
        lo = pltpu.roll(row, LANES - (WIN_W - 1), 1, stride=1, stride_axis=0)
        hi = pltpu.roll(row, GRID_W - (WIN_W - 1), 1, stride=1, stride_axis=0)
        t_ref[dr] = jnp.where(col_ok, jnp.where(first, lo, hi), neg)
    for kind in range(3):
        for qr in range(NA_ROWS):
            for kb in range(3):
                for pr in range(NA_ROWS // 2):
                    ia, ib = (int(plan[kind, qr, kb, 2 * pr + s]) for s in range(2))
                    a = t_ref[ia] if ia >= 0 else neg
                    b = t_ref[ib] if ib >= 0 else neg
                    col0 = kb * NA_TOK + pr * LANES
                    o_ref[0, kind, 0, qr * GRID_W:(qr + 1) * GRID_W, col0:col0 + LANES] = jnp.where(first, a, b)


def _na_bias_tables(rpb, rows):
    depth = rpb.shape[0]
    n_dr, n_dc = 2 * WIN_H - 1, 2 * WIN_W - 1
    padded = jnp.pad(rpb.astype(F32), ((0, 0), (0, 0), (0, 16 - n_dr), (0, LANES - n_dc)))
    return pl.pallas_call(
        functools.partial(_na_bias_kernel, plan=_na_bias_plan(rows)),
        grid=(depth, NA_HEADS),
        in_specs=[pl.BlockSpec((1, 1, 16, LANES), lambda l, h: (l, h, 0, 0))],
        out_specs=pl.BlockSpec((1, 3, 1, NA_TOK, 3 * NA_TOK), lambda l, h: (l, 0, h, 0, 0)),
        out_shape=jax.ShapeDtypeStruct((depth, 3, NA_HEADS, NA_TOK, 3 * NA_TOK), F32),
        scratch_shapes=[pltpu.VMEM((n_dr, GRID_W, LANES), F32)],
        name="na_bias",
    )(padded)


def _fold_lanes(blocks, op):
    tiles = [blk[:, j:j + LANES] for blk in blocks for j in range(0, blk.shape[1], LANES)]
    acc = tiles[0]
    for t in tiles[1:]:
        acc = op(acc, t)
    return acc


def _na_kernel(*refs):
    nkb = NA_STEP + 2
    q_ref, k_refs, v_refs = refs[0], refs[1:1 + nkb], refs[1 + nkb:1 + 2 * nkb]
    kx_ref, vx_ref, g_ref = refs[1 + 2 * nkb:4 + 2 * nkb]
    bias_refs, o_ref = refs[4 + 2 * nkb:-1], refs[-1]
    head0 = _lane_is_head0((NA_TOK, LANES))
    for j, bias_ref in enumerate(bias_refs):
        rows = slice(j * NA_TOK, (j + 1) * NA_TOK)
        for pi in range(PAIRS):
            cols = slice(pi * LANES, (pi + 1) * LANES)
            q = q_ref[0, rows, cols]
            zero = jnp.zeros_like(q)
            keys = [r[0, :, cols] for r in k_refs[j:j + 3]] + [kx_ref[0, :, cols]]
            vals = [r[0, :, cols] for r in v_refs[j:j + 3]] + [vx_ref[0, :, cols]]
            probs, inv = [], []
            for hh in range(2):
                qh = jnp.where(head0, q, zero) if hh == 0 else jnp.where(head0, zero, q)
                s = [_dot_nt(qh, keys[t]) + bias_ref[0, 2 * pi + hh, :, t * NA_TOK:(t + 1) * NA_TOK] for t in range(3)]
                s.append(_dot_nt(qh, keys[3]))
                m = jnp.max(_fold_lanes(s, jnp.maximum), axis=-1, keepdims=True)
                p = [jnp.exp2(st - m) for st in s]
                inv.append(1.0 / jnp.sum(_fold_lanes(p, jnp.add), axis=-1, keepdims=True))
                probs += [pt.astype(BF16) for pt in p]
            v_cat = jnp.concatenate([jnp.where(head0, v, zero) for v in vals]
                                    + [jnp.where(head0, zero, v) for v in vals], axis=0)
            o = _dot(jnp.concatenate(probs, axis=1), v_cat) * jnp.where(head0, inv[0], inv[1])
            o_ref[0, rows, cols] = (o * _silu(g_ref[0, rows, cols].astype(F32))).astype(o_ref.dtype)


def _neighbourhood(p, pc, bias, layer):
    b, n, _ = p.shape
    nblk = n // NA_TOK
    nstep = nblk // NA_STEP
    grp = NA_W // LANES

    def key_blk(cb, shift):
        return pl.BlockSpec((1, NA_TOK, NA_W),
                            lambda bi, i: (bi, jnp.clip(NA_STEP * i + shift, 0, nblk - 1), cb // grp))

    def step_blk(cb):
        return pl.BlockSpec((1, NA_STEP * NA_TOK, NA_W), lambda bi, i: (bi, i, cb // grp))

    lc = pc.shape[1]

    def ctx(cb):
        return pl.BlockSpec((1, lc, NA_W), lambda bi, i: (bi, 0, cb // grp))

    def bias_blk(j):
        def index(bi, i):
            blk = NA_STEP * i + j
            return (layer, jnp.where(blk == 0, 0, jnp.where(blk == nblk - 1, 2, 1)), 0, 0, 0)
        return pl.BlockSpec((None, 1, NA_HEADS, NA_TOK, 3 * NA_TOK), index)

    shifts = range(-1, NA_STEP + 1)
    return pl.pallas_call(
        _na_kernel,
        grid=(b, nstep),
        in_specs=([step_blk(CB_NQ)] + [key_blk(CB_NK, s) for s in shifts] + [key_blk(CB_NV, s) for s in shifts]
                  + [ctx(CB_NK), ctx(CB_NV), step_blk(CB_NG)] + [bias_blk(j) for j in range(NA_STEP)]),
        out_specs=pl.BlockSpec((1, NA_STEP * NA_TOK, NA_W), lambda bi, i: (bi, i, 0)),
        out_shape=jax.ShapeDtypeStruct((b, n, NA_W), BF16),
        compiler_params=pltpu.CompilerParams(dimension_semantics=("parallel", "arbitrary")),
        name="neighbourhood",
    )(p, *([p] * (2 * len(shifts))), pc, pc, p, *([bias] * NA_STEP))


def _ctx_attn_kernel(q_ref, k_ref, v_ref, g_ref, o_ref):
    q, k, v = q_ref[0], k_ref[0], v_ref[0]
    head0 = _lane_is_head0(q.shape)
    zero = jnp.zeros_like(q)
    outs = []
    for hh in range(2):
        qh = jnp.where(head0, q, zero) if hh == 0 else jnp.where(head0, zero, q)
        s = _dot_nt(qh, k)
        p = jnp.exp2(s - jnp.max(s, axis=-1, keepdims=True))
        outs.append(_dot(p.astype(BF16), v) / jnp.sum(p, axis=-1, keepdims=True))
    o = jnp.where(head0, outs[0], outs[1])
    o_ref[0] = (o * _silu(g_ref[0].astype(F32))).astype(o_ref.dtype)


def _ctx_attention(pc):
    b, lc, _ = pc.shape

    def col(cb):
        return pl.BlockSpec((1, lc, LANES), lambda bi, pi: (bi, 0, cb + pi))

    return pl.pallas_call(
        _ctx_attn_kernel,
        grid=(b, PAIRS),
        in_specs=[col(CB_NQ), col(CB_NK), col(CB_NV), col(CB_NG)],
        out_specs=col(0),
        out_shape=jax.ShapeDtypeStruct((b, lc, NA_W), BF16),
        name="context_attention",
    )(pc, pc, pc, pc)


def _out_kernel(x_ref, gate_ref, ch_ref, cb_ref, cc_ref, cz_ref, hp_ref, cp_ref, hn_ref, cn_ref,
                yr_ref, yn_ref, cw_ref, cbias_ref, w_ref, lg_ref, lb_ref, o_ref):
    i = pl.program_id(1)
    last = pl.num_programs(1) - 1
    u = cc_ref[0].astype(F32) * ch_ref[0].astype(F32)
    tm = u.shape[0]
    halo = hp_ref.shape[1]
    u_before = cp_ref[0, halo - 1:halo, :].astype(F32) * hp_ref[0, halo - 1:halo, :].astype(F32)
    u_after = cn_ref[0, 0:1, :].astype(F32) * hn_ref[0, 0:1, :].astype(F32)
    u_before = jnp.where(i == 0, 0.0, u_before)
    u_after = jnp.where(i == last, 0.0, u_after)
    row = lax.broadcasted_iota(jnp.int32, u.shape, 0)
    u_prev = jnp.where(row == 0, u_before, pltpu.roll(u, 1, axis=0))
    u_next = jnp.where(row == tm - 1, u_after, pltpu.roll(u, tm - 1, axis=0))
    cw = cw_ref[...]
    conv = u_prev * cw[0:1] + u * cw[1:2] + u_next * cw[2:3] + cbias_ref[...]
    y_conv = cb_ref[0].astype(F32) * conv * _silu(cz_ref[0].astype(F32))
    y = jnp.concatenate([y_conv.astype(BF16), yr_ref[0], yn_ref[0]], axis=-1)
    z = x_ref[0] + (gate_ref[0] * (1.0 / DEEPNORM_ALPHA)) * _dot(y, w_ref[...].astype(BF16))
    mu = jnp.mean(z, axis=-1, keepdims=True)
    dlt = z - mu
    var = jnp.mean(dlt * dlt, axis=-1, keepdims=True)
    o_ref[0] = dlt * lax.rsqrt(var + LN_EPS / DEEPNORM_ALPHA ** 2) * lg_ref[...] + lb_ref[...]


def _output(x, gate, p, y_ret, y_na, conv_w, conv_b, w_out, layer, ln_g, ln_b):
    b, n, _ = x.shape
    tm = min(ROW_TILE, n)
    halo = 16
    per = tm // halo
    nh = n // halo

    def conv(cb):
        return pl.BlockSpec((1, tm, CONV_W), lambda bi, i: (bi, i, cb))

    def before(cb):
        return pl.BlockSpec((1, halo, CONV_W), lambda bi, i: (bi, jnp.maximum(i * per - 1, 0), cb))

    def after(cb):
        return pl.BlockSpec((1, halo, CONV_W), lambda bi, i: (bi, jnp.minimum((i + 1) * per, nh - 1), cb))

    def const(shape):
        return pl.BlockSpec(shape, lambda bi, i: (0,) * len(shape))

    return pl.pallas_call(
        _out_kernel,
        grid=(b, n // tm),
        in_specs=[pl.BlockSpec((1, tm, D_MODEL), lambda bi, i: (bi, i, 0)),
                  pl.BlockSpec((1, 1, D_MODEL), lambda bi, i: (bi, 0, 0)),
                  conv(CB_CH), conv(CB_CB), conv(CB_CC), conv(CB_CZ),
                  before(CB_CH), before(CB_CC), after(CB_CH), after(CB_CC),
                  pl.BlockSpec((1, tm, RET_W), lambda bi, i: (bi, i, 0)),
                  pl.BlockSpec((1, tm, NA_W), lambda bi, i: (bi, i, 0)),
                  const((3, CONV_W)), const((1, CONV_W)),
                  pl.BlockSpec((None, MIX_W, D_MODEL), lambda bi, i: (layer, 0, 0)),
                  const((1, D_MODEL)), const((1, D_MODEL))],
        out_specs=pl.BlockSpec((1, tm, D_MODEL), lambda bi, i: (bi, i, 0)),
        out_shape=jax.ShapeDtypeStruct((b, n, D_MODEL), F32),
        compiler_params=pltpu.CompilerParams(dimension_semantics=("parallel", "parallel")),
        name="output",
    )(x, gate, p, p, p, p, p, p, p, p, y_ret, y_na, conv_w, conv_b.reshape(1, CONV_W), w_out,
      ln_g.reshape(1, D_MODEL), ln_b.reshape(1, D_MODEL))


def kernel(x, c, ctx, c_ctx, w_mod, b_mod, w_in, conv_w, conv_b, ret_decay, na_rpb, w_out, ln_g, ln_b):
    b, n, d = x.shape
    act_t = jnp.pad(jnp.concatenate([c, c_ctx[None]], axis=0).T, ((0, 0), (0, 5)))
    mod = _modulation(act_t, w_mod, b_mod)
    tables = _rope_tables(n)
    na_bias = _na_bias_tables(na_rpb, n // GRID_W)
    xc = ctx
    for l in range(DEPTH):
        need_ctx = l < DEPTH - 1
        shift, scale, gate = (mod[l, :b, None, j * d:(j + 1) * d] for j in range(3))
        shift_c, scale_c, gate_c = (jnp.broadcast_to(mod[l, b, j * d:(j + 1) * d], (b, 1, d)) for j in range(3))
        p = _projection(x, shift, scale, w_in, l, tables, PROJ_W)
        pc = _projection(xc, shift_c, scale_c, w_in, l, None, PROJ_W if need_ctx else KV_W)
        y_ret = _retention(p, pc, ret_decay[l], init_state=True)
        y_na = _neighbourhood(p, pc, na_bias, l)
        x_new = _output(x, gate, p, y_ret, y_na, conv_w[l], conv_b[l], w_out, l, ln_g[l], ln_b[l])
        if need_ctx:
            yc_ret = _retention(pc, None, ret_decay[l], init_state=False)
            yc_na = _ctx_attention(pc)
            xc = _output(xc, gate_c, pc, yc_ret, yc_na, conv_w[l], conv_b[l], w_out, l, ln_g[l], ln_b[l])
        x = x_new
    return x
```

```python
import functools

import numpy as np
import jax
import jax.numpy as jnp
from jax import lax
from jax.experimental import pallas as pl
from jax.experimental.pallas import tpu as pltpu

D_MODEL = 1024
DEPTH = 2
GRID_W = 64
HEAD_DIM = 64
CONV_W = 256
RET_HEADS = 6
RET_W = RET_HEADS * HEAD_DIM
NA_HEADS = 6
NA_W = NA_HEADS * HEAD_DIM
MIX_W = CONV_W + RET_W + NA_W
RET_CHUNK = 128
WIN_H = 8
WIN_W = 16
ROPE_BASE = 10000.0
LN_EPS = 1e-5
DEEPNORM_ALPHA = (2 * DEPTH) ** 0.25
PROJ_SPLITS = (RET_W, RET_W, NA_W, NA_W, RET_W, RET_W, NA_W, NA_W, CONV_W, CONV_W, CONV_W, CONV_W)
PROJ_W = sum(PROJ_SPLITS)
KV_W = 2 * RET_W + 2 * NA_W

LANES = 128
PAIRS = RET_HEADS // 2
CB_RK, CB_RV, CB_NK, CB_NV, CB_RQ, CB_RG, CB_NQ, CB_NG = 0, 3, 6, 9, 12, 15, 18, 21
CB_CH, CB_CB, CB_CC, CB_CZ = 12, 13, 14, 15
QK_SCALE = HEAD_DIM ** -0.5
NEG = -1e30
LOG2E = 1.4426950408889634
NA_ROWS = 4
NA_TOK = NA_ROWS * GRID_W
NA_STEP = 2
RET_GROUP = 8
ROW_TILE = 1024
OUT_SPLIT = 4
PROJ_CHUNK = 512

F32 = jnp.float32
BF16 = jnp.bfloat16


def _silu(v):
    return v * jax.nn.sigmoid(v)


def _dot(a, b):
    return jnp.dot(a, b, preferred_element_type=F32)


def _dot_nt(a, b):
    return lax.dot_general(a, b, (((1,), (1,)), ((), ())), preferred_element_type=F32)


def _dot_tn(a, b):
    return lax.dot_general(a, b, (((0,), (0,)), ((), ())), preferred_element_type=F32)


def _lane_is_head0(shape):
    return lax.broadcasted_iota(jnp.int32, shape, len(shape) - 1) < HEAD_DIM


def _mod_kernel(act_ref, w_ref, b_ref, o_ref):
    a = _silu(act_ref[...])
    w = w_ref[0]
    bias = b_ref[0]
    for r in range(3):
        o_ref[0, r:r + 1, :] = jnp.sum(a[:, r:r + 1] * w, axis=0, keepdims=True) + bias
    o_ref[0, 3:8, :] = jnp.zeros((5, w.shape[1]), F32)


def _modulation(act_t, w_mod, b_mod):
    tn = 1536
    n = w_mod.shape[-1]
    return pl.pallas_call(
        _mod_kernel,
        grid=(DEPTH, n // tn),
        in_specs=[pl.BlockSpec((D_MODEL, 8), lambda l, j: (0, 0)),
                  pl.BlockSpec((1, D_MODEL, tn), lambda l, j: (l, 0, j)),
                  pl.BlockSpec((1, 1, tn), lambda l, j: (l, 0, j))],
        out_specs=pl.BlockSpec((1, 8, tn), lambda l, j: (l, 0, j)),
        out_shape=jax.ShapeDtypeStruct((DEPTH, 8, n), F32),
        name="modulation",
    )(act_t, w_mod, b_mod.reshape(DEPTH, 1, n))


def _rope_tables(n):
    rows = n // GRID_W
    nf = HEAD_DIM // 4
    inv = ROPE_BASE ** (-jnp.arange(nf, dtype=F32) / nf)
    lane = np.arange(LANES)
    by_row = ((lane % HEAD_DIM) < HEAD_DIM // 2)[None]
    first = ((lane % (2 * nf)) < nf)[None]
    ang_r = jnp.tile(jnp.arange(rows).astype(F32)[:, None] * inv, (1, LANES // nf))
    ang_c = jnp.tile(jnp.arange(GRID_W).astype(F32)[:, None] * inv, (1, LANES // nf))
    parts = []
    for ang, own in ((ang_r, by_row), (ang_c, ~by_row)):
        cos, sin = jnp.cos(ang), jnp.sin(ang)
        parts.append((jnp.where(own, cos, 0.0), jnp.where(own & first, -sin, 0.0), jnp.where(own & ~first, sin, 0.0)))
    return parts[0] + parts[1]


def _project(x, mod_row, w_ref, tables, o_ref, n_cols):
    shift, scale = mod_row[:, :D_MODEL], mod_row[:, D_MODEL:2 * D_MODEL]
    h = (x * (1.0 + scale) + shift).astype(BF16)
    rotated = tuple(range(CB_RK, CB_RK + PAIRS)) + tuple(range(CB_RQ, CB_RQ + PAIRS))
    col_scale = {cb: QK_SCALE for cb in range(CB_RK, CB_RK + PAIRS)}
    col_scale.update({cb: QK_SCALE * LOG2E for cb in range(CB_NQ, CB_NQ + PAIRS)})
    for off in range(0, n_cols, PROJ_CHUNK):
        acc = _dot(h, w_ref[:, off:off + PROJ_CHUNK].astype(BF16))
        tiles = []
        for j in range(PROJ_CHUNK // LANES):
            cb = off // LANES + j
            v = acc[:, j * LANES:(j + 1) * LANES]
            if tables is not None and cb in rotated:
                cos, up, dn = tables
                v = v * cos + pltpu.roll(v, LANES - 16, axis=1) * up + pltpu.roll(v, 16, axis=1) * dn
            if cb in col_scale:
                v = v * col_scale[cb]
            tiles.append(v.astype(BF16))
        o_ref[0, :, off:off + PROJ_CHUNK] = jnp.concatenate(tiles, axis=1)


def _proj_kernel(x_ref, xc_ref, mod_ref, w_ref, *rest, ctx_row, n_cols_ctx):
    *table_refs, o_ref, oc_ref = rest
    tables = tuple(jnp.concatenate([r_ref[g:g + 1, :] + c_ref[...] for g in range(r_ref.shape[0])], axis=0)
                   for r_ref, c_ref in zip(table_refs[:3], table_refs[3:]))
    _project(x_ref[0], mod_ref[pl.ds(pl.program_id(0), 1), :], w_ref, tables, o_ref, PROJ_W)

    @pl.when(pl.program_id(1) == 0)
    def _():
        _project(xc_ref[0], mod_ref[ctx_row:ctx_row + 1, :], w_ref, None, oc_ref, n_cols_ctx)


def _projection(x, xc, mod, w_in, layer, tables, n_cols_ctx):
    b, n, _ = x.shape
    lc = xc.shape[1]
    tm = min(ROW_TILE, n)
    in_specs = ([pl.BlockSpec((1, tm, D_MODEL), lambda bi, i: (bi, i, 0)),
                 pl.BlockSpec((1, lc, D_MODEL), lambda bi, i: (bi, 0, 0)),
                 pl.BlockSpec((None, 8, 3 * D_MODEL), lambda bi, i: (layer, 0, 0)),
                 pl.BlockSpec((None, D_MODEL, PROJ_W), lambda bi, i: (layer, 0, 0), pipeline_mode=pl.Buffered(1))]
                + [pl.BlockSpec((tm // GRID_W, LANES), lambda bi, i: (i, 0))] * 3
                + [pl.BlockSpec((GRID_W, LANES), lambda bi, i: (0, 0))] * 3)
    return pl.pallas_call(
        functools.partial(_proj_kernel, ctx_row=b, n_cols_ctx=n_cols_ctx),
        grid=(b, n // tm),
        in_specs=in_specs,
        out_specs=[pl.BlockSpec((1, tm, PROJ_W), lambda bi, i: (bi, i, 0)),
                   pl.BlockSpec((1, lc, n_cols_ctx), lambda bi, i: (bi, 0, 0))],
        out_shape=[jax.ShapeDtypeStruct((b, n, PROJ_W), BF16), jax.ShapeDtypeStruct((b, lc, n_cols_ctx), BF16)],
        compiler_params=pltpu.CompilerParams(dimension_semantics=("parallel", "arbitrary")),
        name="projection",
    )(x, xc, mod, w_in, *tables)


def _log_sigmoid(v):
    return jnp.minimum(v, 0.0) - jnp.log1p(jnp.exp(-jnp.abs(v)))


def _block_diag(m):
    r = lax.broadcasted_iota(jnp.int32, m.shape, 0) < HEAD_DIM
    c = lax.broadcasted_iota(jnp.int32, m.shape, 1) < HEAD_DIM
    return jnp.where(r == c, m, 0.0)


def _head_mean(x):
    head0 = _lane_is_head0(x.shape)
    s0 = jnp.sum(jnp.where(head0, x, 0.0), axis=-1, keepdims=True)
    s1 = jnp.sum(jnp.where(head0, 0.0, x), axis=-1, keepdims=True)
    return jnp.where(head0, s0, s1) * (1.0 / HEAD_DIM)


def _ret_kernel(*refs, init_state, group):
    refs = list(refs)
    dec_ref, q_ref, k_ref, v_ref, g_ref = refs[:5]
    refs = refs[5:]
    if init_state:
        kc_ref, vc_ref = refs[:2]
        refs = refs[2:]
    o_ref, s_ref, u_ref, kt_ref, wq_ref, wkt_ref, d_ref = refs
    c = RET_CHUNK
    n_chunks = q_ref.shape[1] // c
    n_groups = n_chunks // group
    head0 = _lane_is_head0((c, LANES))
    lg_f = _log_sigmoid(dec_ref[0, 0])
    lg_b = _log_sigmoid(dec_ref[1, 0])
    i = lax.broadcasted_iota(jnp.int32, (c, LANES), 0).astype(F32)
    wq_ref[:, :LANES] = jnp.exp((i + 1.0) * lg_f)
    wq_ref[:, LANES:] = jnp.exp((c - i) * lg_b)
    wkt_ref[:LANES, :] = jnp.exp((c - 1.0 - i) * lg_f).T
    wkt_ref[LANES:, :] = jnp.exp(i * lg_b).T
    gc_f = jnp.exp(float(c) * lg_f)
    gc_b = jnp.exp(float(c) * lg_b)
    diff = (lax.broadcasted_iota(jnp.int32, (c, c), 0) - lax.broadcasted_iota(jnp.int32, (c, c), 1)).astype(F32)
    lower = diff >= 0
    for hh in range(2):
        lf = lg_f[:, hh * HEAD_DIM:hh * HEAD_DIM + 1]
        lb = lg_b[:, hh * HEAD_DIM:hh * HEAD_DIM + 1]
        d_ref[:, hh * c:(hh + 1) * c] = jnp.where(lower, jnp.exp(jnp.where(lower, diff, 0.0) * lf),
                                                  jnp.exp(jnp.where(lower, 0.0, -diff) * lb))
    if init_state:
        lc = kc_ref.shape[1]
        m = lax.broadcasted_iota(jnp.int32, (lc, LANES), 0).astype(F32)
        kcf = kc_ref[0].astype(F32)
        s_f0 = _block_diag(_dot_tn((kcf * jnp.exp((lc - 1.0 - m) * lg_f)).astype(BF16), vc_ref[0]))
        s_b0 = _block_diag(_dot_tn((kcf * jnp.exp(m * lg_b)).astype(BF16), vc_ref[0]))
    else:
        s_f0 = s_b0 = jnp.zeros((LANES, LANES), F32)
    r2 = lax.broadcasted_iota(jnp.int32, (2 * LANES, LANES), 0) % LANES < HEAD_DIM
    c2 = lax.broadcasted_iota(jnp.int32, (2 * LANES, LANES), 1) < HEAD_DIM
    diag2 = r2 == c2

    chan0 = lax.broadcasted_iota(jnp.int32, (LANES, c), 0) < HEAD_DIM

    def chunk(n):
        return pl.ds(pl.multiple_of(n * c, c), c)

    def increments(gi, carry):
        for j in range(group):
            n = gi * group + j
            kt = k_ref[0, chunk(n), :].astype(F32).T
            ktb = kt.astype(BF16)
            zero = jnp.zeros_like(ktb)
            kt_ref[n] = jnp.concatenate([jnp.where(chan0, ktb, zero), jnp.where(chan0, zero, ktb)], axis=1)
            lhs = jnp.concatenate([kt * wkt_ref[:LANES, :], kt * wkt_ref[LANES:, :]], axis=0).astype(BF16)
            u_ref[n] = jnp.where(diag2, _dot(lhs, v_ref[0, chunk(n), :]), 0.0)
        return carry

    lax.fori_loop(0, n_groups, increments, 0)

    def fwd_scan(gi, s):
        for j in range(group):
            n = gi * group + j
            s_ref[n, :LANES, :] = s.astype(BF16)
            s = s * gc_f + u_ref[n, :LANES, :]
        return s

    def bwd_scan(gi, s):
        for j in range(group):
            n = n_chunks - 1 - (gi * group + j)
            s_ref[n, LANES:, :] = s.astype(BF16)
            s = s * gc_b + u_ref[n, LANES:, :]
        return s

    lax.fori_loop(0, n_groups, fwd_scan, s_f0)
    lax.fori_loop(0, n_groups, bwd_scan, s_b0)

    def body(gi, carry):
        outs = []
        for j in range(group):
            n = gi * group + j
            rows = chunk(n)
            q, v = q_ref[0, rows, :], v_ref[0, rows, :]
            zero = jnp.zeros_like(v)
            vcat = jnp.concatenate([jnp.where(head0, v, zero), jnp.where(head0, zero, v)], axis=0)
            scores = _dot(q, kt_ref[n]) * d_ref[...]
            qf = q.astype(F32)
            qw = jnp.concatenate([qf * wq_ref[:, :LANES], qf * wq_ref[:, LANES:]], axis=1).astype(BF16)
            outs.append(_dot(scores.astype(BF16), vcat) + _dot(qw, s_ref[n]))
        o = jnp.concatenate(outs, axis=0)
        rows = pl.ds(pl.multiple_of(gi * (group * c), group * c), group * c)
        dlt = o - _head_mean(o)
        var = _head_mean(dlt * dlt)
        y = dlt * lax.rsqrt(var + LN_EPS) * _silu(g_ref[0, rows, :].astype(F32))
        o_ref[0, rows, :] = y.astype(o_ref.dtype)
        return carry

    lax.fori_loop(0, n_groups, body, 0)


def _retention(p, pc, ret_decay, init_state):
    b, n, _ = p.shape
    dec = jnp.repeat(ret_decay.astype(F32), HEAD_DIM, axis=-1).reshape(2, PAIRS, 1, LANES)

    def col(cb):
        return pl.BlockSpec((1, n, LANES), lambda bi, pi: (bi, 0, cb + pi))

    in_specs = [pl.BlockSpec((2, 1, 1, LANES), lambda bi, pi: (0, pi, 0, 0)),
                col(CB_RQ), col(CB_RK), col(CB_RV), col(CB_RG)]
    args = [dec, p, p, p, p]
    if init_state:
        lc = pc.shape[1]
        in_specs += [pl.BlockSpec((1, lc, LANES), lambda bi, pi: (bi, 0, CB_RK + pi)),
                     pl.BlockSpec((1, lc, LANES), lambda bi, pi: (bi, 0, CB_RV + pi))]
        args += [pc, pc]
    c = RET_CHUNK
    return pl.pallas_call(
        functools.partial(_ret_kernel, init_state=init_state, group=min(RET_GROUP, n // c)),
        grid=(b, PAIRS),
        in_specs=in_specs,
        out_specs=col(0),
        out_shape=jax.ShapeDtypeStruct((b, n, RET_W), BF16),
        scratch_shapes=[pltpu.VMEM((n // c, 2 * LANES, LANES), BF16),
                        pltpu.VMEM((n // c, 2 * LANES, LANES), F32),
                        pltpu.VMEM((n // c, LANES, 2 * c), BF16),
                        pltpu.VMEM((c, 2 * LANES), F32),
                        pltpu.VMEM((2 * LANES, c), F32),
                        pltpu.VMEM((c, 2 * c), F32)],
        compiler_params=pltpu.CompilerParams(dimension_semantics=("parallel", "parallel")),
        name="retention",
    )(*args)


def _na_bias_plan(rows):
    nblk = rows // NA_ROWS
    win_h = min(WIN_H, rows)
    plan = np.full((3, NA_ROWS, 3, NA_ROWS), -1, np.int64)
    for kind, blk in enumerate((0, 1, nblk - 1)):
        for qr in range(NA_ROWS):
            r = blk * NA_ROWS + qr
            rs = min(max(r - win_h // 2, 0), rows - win_h)
            for kb in range(3):
                if not 0 <= blk + kb - 1 <= nblk - 1:
                    continue
                for krl in range(NA_ROWS):
                    kr = (blk + kb - 1) * NA_ROWS + krl
                    if rs <= kr < rs + win_h:
                        plan[kind, qr, kb, krl] = kr - r + WIN_H - 1
    return plan


def _na_bias_kernel(rpb_ref, o_ref, t_ref, *, plan):
    n_dr = 2 * WIN_H - 1
    qc = lax.broadcasted_iota(jnp.int32, (GRID_W, LANES), 0)
    lane = lax.broadcasted_iota(jnp.int32, (GRID_W, LANES), 1)
    kc = lane % GRID_W
    cs = jnp.clip(qc - WIN_W // 2, 0, GRID_W - WIN_W)
    col_ok = (kc >= cs) & (kc < cs + WIN_W)
    first = lane < GRID_W
    neg = jnp.full((GRID_W, LANES), NEG, F32)
    for dr in range(n_dr):
        row = jnp.broadcast_to(rpb_ref[0, 0, dr:dr + 1, :] * LOG2E, (GRID_W, LANES))
        lo = pltpu.roll(row, LANES - (WIN_W - 1), 1, stride=1, stride_axis=0)
        hi = pltpu.roll(row, GRID_W - (WIN_W - 1), 1, stride=1, stride_axis=0)
        t_ref[dr] = jnp.where(col_ok, jnp.where(first, lo, hi), neg)
    for kind in range(3):
        for qr in range(NA_ROWS):
            for kb in range(3):
                for pr in range(NA_ROWS // 2):
                    ia, ib = (int(plan[kind, qr, kb, 2 * pr + s]) for s in range(2))
                    a = t_ref[ia] if ia >= 0 else neg
                    b = t_ref[ib] if ib >= 0 else neg
                    col0 = kb * NA_TOK + pr * LANES
                    o_ref[0, kind, 0, qr * GRID_W:(qr + 1) * GRID_W, col0:col0 + LANES] = jnp.where(first, a, b)


def _na_bias_tables(rpb, rows):
    depth = rpb.shape[0]
    n_dr, n_dc = 2 * WIN_H - 1, 2 * WIN_W - 1
    padded = jnp.pad(rpb.astype(F32), ((0, 0), (0, 0), (0, 16 - n_dr), (0, LANES - n_dc)))
    return pl.pallas_call(
        functools.partial(_na_bias_kernel, plan=_na_bias_plan(rows)),
        grid=(depth, NA_HEADS),
        in_specs=[pl.BlockSpec((1, 1, 16, LANES), lambda l, h: (l, h, 0, 0))],
        out_specs=pl.BlockSpec((1, 3, 1, NA_TOK, 3 * NA_TOK), lambda l, h: (l, 0, h, 0, 0)),
        out_shape=jax.ShapeDtypeStruct((depth, 3, NA_HEADS, NA_TOK, 3 * NA_TOK), F32),
        scratch_shapes=[pltpu.VMEM((n_dr, GRID_W, LANES), F32)],
        name="na_bias",
    )(padded)


def _fold_lanes(blocks, op):
    tiles = [blk[:, j:j + LANES] for blk in blocks for j in range(0, blk.shape[1], LANES)]
    acc = tiles[0]
    for t in tiles[1:]:
        acc = op(acc, t)
    return acc


def _na_kernel(*refs):
    nkb = NA_STEP + 2
    q_ref, k_refs, v_refs = refs[0], refs[1:1 + nkb], refs[1 + nkb:1 + 2 * nkb]
    kx_ref, vx_ref, g_ref = refs[1 + 2 * nkb:4 + 2 * nkb]
    bias_refs, o_ref = refs[4 + 2 * nkb:-1], refs[-1]
    head0 = _lane_is_head0((NA_TOK, LANES))
    for j, bias_ref in enumerate(bias_refs):
        rows = slice(j * NA_TOK, (j + 1) * NA_TOK)
        for pi in range(PAIRS):
            cols = slice(pi * LANES, (pi + 1) * LANES)
            q = q_ref[0, rows, cols]
            zero = jnp.zeros_like(q)
            keys = [r[0, :, cols] for r in k_refs[j:j + 3]] + [kx_ref[0, :, cols]]
            vals = [r[0, :, cols] for r in v_refs[j:j + 3]] + [vx_ref[0, :, cols]]
            probs, inv = [], []
            for hh in range(2):
                qh = jnp.where(head0, q, zero) if hh == 0 else jnp.where(head0, zero, q)
                s = [_dot_nt(qh, keys[t]) + bias_ref[0, 2 * pi + hh, :, t * NA_TOK:(t + 1) * NA_TOK] for t in range(3)]
                s.append(_dot_nt(qh, keys[3]))
                m = jnp.max(_fold_lanes(s, jnp.maximum), axis=-1, keepdims=True)
                p = [jnp.exp2(st - m) for st in s]
                inv.append(1.0 / jnp.sum(_fold_lanes(p, jnp.add), axis=-1, keepdims=True))
                probs += [pt.astype(BF16) for pt in p]
            v_cat = jnp.concatenate([jnp.where(head0, v, zero) for v in vals]
                                    + [jnp.where(head0, zero, v) for v in vals], axis=0)
            o = _dot(jnp.concatenate(probs, axis=1), v_cat) * jnp.where(head0, inv[0], inv[1])
            o_ref[0, rows, cols] = (o * _silu(g_ref[0, rows, cols].astype(F32))).astype(o_ref.dtype)


def _neighbourhood(p, pc, bias, layer):
    b, n, _ = p.shape
    nblk = n // NA_TOK
    nstep = nblk // NA_STEP
    grp = NA_W // LANES

    def key_blk(cb, shift):
        return pl.BlockSpec((1, NA_TOK, NA_W),
                            lambda bi, i: (bi, jnp.clip(NA_STEP * i + shift, 0, nblk - 1), cb // grp))

    def step_blk(cb):
        return pl.BlockSpec((1, NA_STEP * NA_TOK, NA_W), lambda bi, i: (bi, i, cb // grp))

    lc = pc.shape[1]

    def ctx(cb):
        return pl.BlockSpec((1, lc, NA_W), lambda bi, i: (bi, 0, cb // grp))

    def bias_blk(j):
        def index(bi, i):
            blk = NA_STEP * i + j
            return (layer, jnp.where(blk == 0, 0, jnp.where(blk == nblk - 1, 2, 1)), 0, 0, 0)
        return pl.BlockSpec((None, 1, NA_HEADS, NA_TOK, 3 * NA_TOK), index)

    shifts = range(-1, NA_STEP + 1)
    return pl.pallas_call(
        _na_kernel,
        grid=(b, nstep),
        in_specs=([step_blk(CB_NQ)] + [key_blk(CB_NK, s) for s in shifts] + [key_blk(CB_NV, s) for s in shifts]
                  + [ctx(CB_NK), ctx(CB_NV), step_blk(CB_NG)] + [bias_blk(j) for j in range(NA_STEP)]),
        out_specs=pl.BlockSpec((1, NA_STEP * NA_TOK, NA_W), lambda bi, i: (bi, i, 0)),
        out_shape=jax.ShapeDtypeStruct((b, n, NA_W), BF16),
        compiler_params=pltpu.CompilerParams(dimension_semantics=("parallel", "arbitrary")),
        name="neighbourhood",
    )(p, *([p] * (2 * len(shifts))), pc, pc, p, *([bias] * NA_STEP))


def _ctx_attn_kernel(q_ref, k_ref, v_ref, g_ref, o_ref):
    q, k, v = q_ref[0], k_ref[0], v_ref[0]
    head0 = _lane_is_head0(q.shape)
    zero = jnp.zeros_like(q)
    outs = []
    for hh in range(2):
        qh = jnp.where(head0, q, zero) if hh == 0 else jnp.where(head0, zero, q)
        s = _dot_nt(qh, k)
        p = jnp.exp2(s - jnp.max(s, axis=-1, keepdims=True))
        outs.append(_dot(p.astype(BF16), v) / jnp.sum(p, axis=-1, keepdims=True))
    o = jnp.where(head0, outs[0], outs[1])
    o_ref[0] = (o * _silu(g_ref[0].astype(F32))).astype(o_ref.dtype)


def _ctx_attention(pc):
    b, lc, _ = pc.shape

    def col(cb):
        return pl.BlockSpec((1, lc, LANES), lambda bi, pi: (bi, 0, cb + pi))

    return pl.pallas_call(
        _ctx_attn_kernel,
        grid=(b, PAIRS),
        in_specs=[col(CB_NQ), col(CB_NK), col(CB_NV), col(CB_NG)],
        out_specs=col(0),
        out_shape=jax.ShapeDtypeStruct((b, lc, NA_W), BF16),
        name="context_attention",
    )(pc, pc, pc, pc)


def _out_kernel(x_ref, mod_ref, ch_ref, cb_ref, cc_ref, cz_ref, hp_ref, cp_ref, hn_ref, cn_ref,
                yr_ref, yn_ref, cw_ref, cbias_ref, w_ref, lg_ref, lb_ref, o_ref, *, mod_row):
    i = pl.program_id(1)
    last = pl.num_programs(1) - 1
    u = cc_ref[0].astype(F32) * ch_ref[0].astype(F32)
    tm = u.shape[0]
    halo = hp_ref.shape[1]
    u_before = cp_ref[0, halo - 1:halo, :].astype(F32) * hp_ref[0, halo - 1:halo, :].astype(F32)
    u_after = cn_ref[0, 0:1, :].astype(F32) * hn_ref[0, 0:1, :].astype(F32)
    u_before = jnp.where(i == 0, 0.0, u_before)
    u_after = jnp.where(i == last, 0.0, u_after)
    row = lax.broadcasted_iota(jnp.int32, u.shape, 0)
    u_prev = jnp.where(row == 0, u_before, pltpu.roll(u, 1, axis=0))
    u_next = jnp.where(row == tm - 1, u_after, pltpu.roll(u, tm - 1, axis=0))
    cw = cw_ref[...]
    row = pl.program_id(0) if mod_row is None else mod_row
    g = mod_ref[pl.ds(row, 1), 2 * D_MODEL:] * (1.0 / DEEPNORM_ALPHA)
    w = w_ref[...].astype(BF16)
    sub = tm // OUT_SPLIT
    for r in range(OUT_SPLIT):
        rs = slice(r * sub, (r + 1) * sub)
        conv = u_prev[rs] * cw[0:1] + u[rs] * cw[1:2] + u_next[rs] * cw[2:3] + cbias_ref[...]
        y_conv = cb_ref[0, rs, :].astype(F32) * conv * _silu(cz_ref[0, rs, :].astype(F32))
        y = jnp.concatenate([y_conv.astype(BF16), yr_ref[0, rs, :], yn_ref[0, rs, :]], axis=-1)
        z = x_ref[0, rs, :] + g * _dot(y, w)
        mu = jnp.mean(z, axis=-1, keepdims=True)
        dlt = z - mu
        var = jnp.mean(dlt * dlt, axis=-1, keepdims=True)
        o_ref[0, rs, :] = dlt * lax.rsqrt(var + LN_EPS / DEEPNORM_ALPHA ** 2) * lg_ref[...] + lb_ref[...]


def _output(x, mod, mod_row, p, y_ret, y_na, conv_w, conv_b, w_out, layer, ln_g, ln_b):
    b, n, _ = x.shape
    tm = min(ROW_TILE, n)
    halo = 16
    per = tm // halo
    nh = n // halo

    def conv(cb):
        return pl.BlockSpec((1, tm, CONV_W), lambda bi, i: (bi, i, cb))

    def before(cb):
        return pl.BlockSpec((1, halo, CONV_W), lambda bi, i: (bi, jnp.maximum(i * per - 1, 0), cb))

    def after(cb):
        return pl.BlockSpec((1, halo, CONV_W), lambda bi, i: (bi, jnp.minimum((i + 1) * per, nh - 1), cb))

    def const(shape):
        return pl.BlockSpec(shape, lambda bi, i: (0,) * len(shape))

    return pl.pallas_call(
        functools.partial(_out_kernel, mod_row=mod_row),
        grid=(b, n // tm),
        in_specs=[pl.BlockSpec((1, tm, D_MODEL), lambda bi, i: (bi, i, 0)),
                  pl.BlockSpec((None, 8, 3 * D_MODEL), lambda bi, i: (layer, 0, 0)),
                  conv(CB_CH), conv(CB_CB), conv(CB_CC), conv(CB_CZ),
                  before(CB_CH), before(CB_CC), after(CB_CH), after(CB_CC),
                  pl.BlockSpec((1, tm, RET_W), lambda bi, i: (bi, i, 0)),
                  pl.BlockSpec((1, tm, NA_W), lambda bi, i: (bi, i, 0)),
                  const((3, CONV_W)), const((1, CONV_W)),
                  pl.BlockSpec((None, MIX_W, D_MODEL), lambda bi, i: (layer, 0, 0)),
                  const((1, D_MODEL)), const((1, D_MODEL))],
        out_specs=pl.BlockSpec((1, tm, D_MODEL), lambda bi, i: (bi, i, 0)),
        out_shape=jax.ShapeDtypeStruct((b, n, D_MODEL), F32),
        compiler_params=pltpu.CompilerParams(dimension_semantics=("parallel", "parallel")),
        name="output",
    )(x, mod, p, p, p, p, p, p, p, p, y_ret, y_na, conv_w, conv_b.reshape(1, CONV_W), w_out,
      ln_g.reshape(1, D_MODEL), ln_b.reshape(1, D_MODEL))


def kernel(x, c, ctx, c_ctx, w_mod, b_mod, w_in, conv_w, conv_b, ret_decay, na_rpb, w_out, ln_g, ln_b):
    b, n, d = x.shape
    act_t = jnp.pad(jnp.concatenate([c, c_ctx[None]], axis=0).T, ((0, 0), (0, 5)))
    mod = _modulation(act_t, w_mod, b_mod)
    tables = _rope_tables(n)
    na_bias = _na_bias_tables(na_rpb, n // GRID_W)
    xc = ctx
    for l in range(DEPTH):
        need_ctx = l < DEPTH - 1
        p, pc = _projection(x, xc, mod, w_in, l, tables, PROJ_W if need_ctx else KV_W)
        y_ret = _retention(p, pc, ret_decay[l], init_state=True)
        y_na = _neighbourhood(p, pc, na_bias, l)
        x_new = _output(x, mod, None, p, y_ret, y_na, conv_w[l], conv_b[l], w_out, l, ln_g[l], ln_b[l])
        if need_ctx:
            yc_ret = _retention(pc, None, ret_decay[l], init_state=False)
            yc_na = _ctx_attention(pc)
            xc = _output(xc, mod, b, pc, yc_ret, yc_na, conv_w[l], conv_b[l], w_out, l, ln_g[l], ln_b[l])
        x = x_new
    return x
```

```python
import functools

import numpy as np
import jax
import jax.numpy as jnp
from jax import lax
from jax.experimental import pallas as pl
from jax.experimental.pallas import tpu as pltpu

D_MODEL = 1024
DEPTH = 2
GRID_W = 64
HEAD_DIM = 64
CONV_W = 256
RET_HEADS = 6
RET_W = RET_HEADS * HEAD_DIM
NA_HEADS = 6
NA_W = NA_HEADS * HEAD_DIM
MIX_W = CONV_W + RET_W + NA_W
RET_CHUNK = 128
WIN_H = 8
WIN_W = 16
ROPE_BASE = 10000.0
LN_EPS = 1e-5
DEEPNORM_ALPHA = (2 * DEPTH) ** 0.25
PROJ_SPLITS = (RET_W, RET_W, NA_W, NA_W, RET_W, RET_W, NA_W, NA_W, CONV_W, CONV_W, CONV_W, CONV_W)
PROJ_W = sum(PROJ_SPLITS)
KV_W = 2 * RET_W + 2 * NA_W

LANES = 128
PAIRS = RET_HEADS // 2
CB_RK, CB_RV, CB_NK, CB_NV, CB_RQ, CB_RG, CB_NQ, CB_NG = 0, 3, 6, 9, 12, 15, 18, 21
CB_CH, CB_CB, CB_CC, CB_CZ = 12, 13, 14, 15
QK_SCALE = HEAD_DIM ** -0.5
NEG = -1e30
LOG2E = 1.4426950408889634
NA_ROWS = 4
NA_TOK = NA_ROWS * GRID_W
NA_STEP = 2
RET_GROUP = 8
ROW_TILE = 1024
OUT_SPLIT = 4
PROJ_CHUNK = 512

F32 = jnp.float32
BF16 = jnp.bfloat16


def _silu(v):
    return v * jax.nn.sigmoid(v)


def _dot(a, b):
    return jnp.dot(a, b, preferred_element_type=F32)


def _dot_nt(a, b):
    return lax.dot_general(a, b, (((1,), (1,)), ((), ())), preferred_element_type=F32)


def _dot_tn(a, b):
    return lax.dot_general(a, b, (((0,), (0,)), ((), ())), preferred_element_type=F32)


def _lane_is_head0(shape):
    return lax.broadcasted_iota(jnp.int32, shape, len(shape) - 1) < HEAD_DIM


def _mod_kernel(act_ref, w_ref, b_ref, o_ref):
    a = _silu(act_ref[...])
    w = w_ref[0]
    bias = b_ref[0]
    for r in range(3):
        o_ref[0, r:r + 1, :] = jnp.sum(a[:, r:r + 1] * w, axis=0, keepdims=True) + bias
    o_ref[0, 3:8, :] = jnp.zeros((5, w.shape[1]), F32)


def _modulation(act_t, w_mod, b_mod):
    tn = 1536
    n = w_mod.shape[-1]
    return pl.pallas_call(
        _mod_kernel,
        grid=(DEPTH, n // tn),
        in_specs=[pl.BlockSpec((D_MODEL, 8), lambda l, j: (0, 0)),
                  pl.BlockSpec((1, D_MODEL, tn), lambda l, j: (l, 0, j)),
                  pl.BlockSpec((1, 1, tn), lambda l, j: (l, 0, j))],
        out_specs=pl.BlockSpec((1, 8, tn), lambda l, j: (l, 0, j)),
        out_shape=jax.ShapeDtypeStruct((DEPTH, 8, n), F32),
        name="modulation",
    )(act_t, w_mod, b_mod.reshape(DEPTH, 1, n))


def _rope_tables(n):
    rows = n // GRID_W
    nf = HEAD_DIM // 4
    inv = ROPE_BASE ** (-jnp.arange(nf, dtype=F32) / nf)
    lane = np.arange(LANES)
    by_row = ((lane % HEAD_DIM) < HEAD_DIM // 2)[None]
    first = ((lane % (2 * nf)) < nf)[None]
    ang_r = jnp.tile(jnp.arange(rows).astype(F32)[:, None] * inv, (1, LANES // nf))
    ang_c = jnp.tile(jnp.arange(GRID_W).astype(F32)[:, None] * inv, (1, LANES // nf))
    parts = []
    for ang, own in ((ang_r, by_row), (ang_c, ~by_row)):
        cos, sin = jnp.cos(ang), jnp.sin(ang)
        parts.append((jnp.where(own, cos, 0.0), jnp.where(own & first, -sin, 0.0), jnp.where(own & ~first, sin, 0.0)))
    return parts[0] + parts[1]


def _project(x, mod_row, w_ref, tables, o_ref, n_cols):
    shift, scale = mod_row[:, :D_MODEL], mod_row[:, D_MODEL:2 * D_MODEL]
    h = (x * (1.0 + scale) + shift).astype(BF16)
    rotated = tuple(range(CB_RK, CB_RK + PAIRS)) + tuple(range(CB_RQ, CB_RQ + PAIRS))
    col_scale = {cb: QK_SCALE for cb in range(CB_RK, CB_RK + PAIRS)}
    col_scale.update({cb: QK_SCALE * LOG2E for cb in range(CB_NQ, CB_NQ + PAIRS)})
    for off in range(0, n_cols, PROJ_CHUNK):
        acc = _dot(h, w_ref[:, off:off + PROJ_CHUNK].astype(BF16))
        tiles = []
        for j in range(PROJ_CHUNK // LANES):
            cb = off // LANES + j
            v = acc[:, j * LANES:(j + 1) * LANES]
            if tables is not None and cb in rotated:
                cos, up, dn = tables
                v = v * cos + pltpu.roll(v, LANES - 16, axis=1) * up + pltpu.roll(v, 16, axis=1) * dn
            if cb in col_scale:
                v = v * col_scale[cb]
            tiles.append(v.astype(BF16))
        o_ref[0, :, off:off + PROJ_CHUNK] = jnp.concatenate(tiles, axis=1)


def _proj_kernel(x_ref, xc_ref, mod_ref, w_ref, *rest, ctx_row, n_cols_ctx):
    *table_refs, o_ref, oc_ref = rest
    tables = tuple(jnp.concatenate([r_ref[g:g + 1, :] + c_ref[...] for g in range(r_ref.shape[0])], axis=0)
                   for r_ref, c_ref in zip(table_refs[:3], table_refs[3:]))
    _project(x_ref[0], mod_ref[pl.ds(pl.program_id(0), 1), :], w_ref, tables, o_ref, PROJ_W)

    @pl.when(pl.program_id(1) == 0)
    def _():
        _project(xc_ref[0], mod_ref[ctx_row:ctx_row + 1, :], w_ref, None, oc_ref, n_cols_ctx)


def _projection(x, xc, mod, w_in, layer, tables, n_cols_ctx):
    b, n, _ = x.shape
    lc = xc.shape[1]
    tm = min(ROW_TILE, n)
    in_specs = ([pl.BlockSpec((1, tm, D_MODEL), lambda bi, i: (bi, i, 0)),
                 pl.BlockSpec((1, lc, D_MODEL), lambda bi, i: (bi, 0, 0)),
                 pl.BlockSpec((None, 8, 3 * D_MODEL), lambda bi, i: (layer, 0, 0)),
                 pl.BlockSpec((None, D_MODEL, PROJ_W), lambda bi, i: (layer, 0, 0), pipeline_mode=pl.Buffered(1))]
                + [pl.BlockSpec((tm // GRID_W, LANES), lambda bi, i: (i, 0))] * 3
                + [pl.BlockSpec((GRID_W, LANES), lambda bi, i: (0, 0))] * 3)
    return pl.pallas_call(
        functools.partial(_proj_kernel, ctx_row=b, n_cols_ctx=n_cols_ctx),
        grid=(b, n // tm),
        in_specs=in_specs,
        out_specs=[pl.BlockSpec((1, tm, PROJ_W), lambda bi, i: (bi, i, 0)),
                   pl.BlockSpec((1, lc, n_cols_ctx), lambda bi, i: (bi, 0, 0))],
        out_shape=[jax.ShapeDtypeStruct((b, n, PROJ_W), BF16), jax.ShapeDtypeStruct((b, lc, n_cols_ctx), BF16)],
        compiler_params=pltpu.CompilerParams(dimension_semantics=("parallel", "arbitrary")),
        name="projection",
    )(x, xc, mod, w_in, *tables)


def _log_sigmoid(v):
    return jnp.minimum(v, 0.0) - jnp.log1p(jnp.exp(-jnp.abs(v)))


def _block_diag(m):
    r = lax.broadcasted_iota(jnp.int32, m.shape, 0) < HEAD_DIM
    c = lax.broadcasted_iota(jnp.int32, m.shape, 1) < HEAD_DIM
    return jnp.where(r == c, m, 0.0)


def _head_mean(x):
    head0 = _lane_is_head0(x.shape)
    s0 = jnp.sum(jnp.where(head0, x, 0.0), axis=-1, keepdims=True)
    s1 = jnp.sum(jnp.where(head0, 0.0, x), axis=-1, keepdims=True)
    return jnp.where(head0, s0, s1) * (1.0 / HEAD_DIM)


def _ret_kernel(*refs, init_state, group):
    refs = list(refs)
    dec_ref, q_ref, k_ref, v_ref, g_ref = refs[:5]
    refs = refs[5:]
    if init_state:
        kc_ref, vc_ref = refs[:2]
        refs = refs[2:]
    o_ref, s_ref, u_ref, kt_ref, wq_ref, wkt_ref, d_ref = refs
    c = RET_CHUNK
    n_chunks = q_ref.shape[1] // c
    n_groups = n_chunks // group
    head0 = _lane_is_head0((c, LANES))
    lg_f = _log_sigmoid(dec_ref[0, 0])
    lg_b = _log_sigmoid(dec_ref[1, 0])
    i = lax.broadcasted_iota(jnp.int32, (c, LANES), 0).astype(F32)
    wq_ref[:, :LANES] = jnp.exp((i + 1.0) * lg_f)
    wq_ref[:, LANES:] = jnp.exp((c - i) * lg_b)
    wkt_ref[:LANES, :] = jnp.exp((c - 1.0 - i) * lg_f).T
    wkt_ref[LANES:, :] = jnp.exp(i * lg_b).T
    gc_f = jnp.exp(float(c) * lg_f)
    gc_b = jnp.exp(float(c) * lg_b)
    diff = (lax.broadcasted_iota(jnp.int32, (c, c), 0) - lax.broadcasted_iota(jnp.int32, (c, c), 1)).astype(F32)
    lower = diff >= 0
    for hh in range(2):
        lf = lg_f[:, hh * HEAD_DIM:hh * HEAD_DIM + 1]
        lb = lg_b[:, hh * HEAD_DIM:hh * HEAD_DIM + 1]
        d_ref[:, hh * c:(hh + 1) * c] = jnp.where(lower, jnp.exp(jnp.where(lower, diff, 0.0) * lf),
                                                  jnp.exp(jnp.where(lower, 0.0, -diff) * lb))
    if init_state:
        lc = kc_ref.shape[1]
        m = lax.broadcasted_iota(jnp.int32, (lc, LANES), 0).astype(F32)
        kcf = kc_ref[0].astype(F32)
        s_f0 = _block_diag(_dot_tn((kcf * jnp.exp((lc - 1.0 - m) * lg_f)).astype(BF16), vc_ref[0]))
        s_b0 = _block_diag(_dot_tn((kcf * jnp.exp(m * lg_b)).astype(BF16), vc_ref[0]))
    else:
        s_f0 = s_b0 = jnp.zeros((LANES, LANES), F32)
    r2 = lax.broadcasted_iota(jnp.int32, (2 * LANES, LANES), 0) % LANES < HEAD_DIM
    c2 = lax.broadcasted_iota(jnp.int32, (2 * LANES, LANES), 1) < HEAD_DIM
    diag2 = r2 == c2

    chan0 = lax.broadcasted_iota(jnp.int32, (LANES, c), 0) < HEAD_DIM

    def chunk(n):
        return pl.ds(pl.multiple_of(n * c, c), c)

    def increments(gi, carry):
        for j in range(group):
            n = gi * group + j
            kt = k_ref[0, chunk(n), :].astype(F32).T
            ktb = kt.astype(BF16)
            zero = jnp.zeros_like(ktb)
            kt_ref[n] = jnp.concatenate([jnp.where(chan0, ktb, zero), jnp.where(chan0, zero, ktb)], axis=1)
            lhs = jnp.concatenate([kt * wkt_ref[:LANES, :], kt * wkt_ref[LANES:, :]], axis=0).astype(BF16)
            u_ref[n] = jnp.where(diag2, _dot(lhs, v_ref[0, chunk(n), :]), 0.0)
        return carry

    lax.fori_loop(0, n_groups, increments, 0)

    def fwd_scan(gi, s):
        for j in range(group):
            n = gi * group + j
            s_ref[n, :LANES, :] = s.astype(BF16)
            s = s * gc_f + u_ref[n, :LANES, :]
        return s

    def bwd_scan(gi, s):
        for j in range(group):
            n = n_chunks - 1 - (gi * group + j)
            s_ref[n, LANES:, :] = s.astype(BF16)
            s = s * gc_b + u_ref[n, LANES:, :]
        return s

    lax.fori_loop(0, n_groups, fwd_scan, s_f0)
    lax.fori_loop(0, n_groups, bwd_scan, s_b0)

    def body(gi, carry):
        outs = []
        for j in range(group):
            n = gi * group + j
            rows = chunk(n)
            q, v = q_ref[0, rows, :], v_ref[0, rows, :]
            zero = jnp.zeros_like(v)
            vcat = jnp.concatenate([jnp.where(head0, v, zero), jnp.where(head0, zero, v)], axis=0)
            scores = _dot(q, kt_ref[n]) * d_ref[...]
            qf = q.astype(F32)
            qw = jnp.concatenate([qf * wq_ref[:, :LANES], qf * wq_ref[:, LANES:]], axis=1).astype(BF16)
            outs.append(_dot(scores.astype(BF16), vcat) + _dot(qw, s_ref[n]))
        o = jnp.concatenate(outs, axis=0)
        rows = pl.ds(pl.multiple_of(gi * (group * c), group * c), group * c)
        dlt = o - _head_mean(o)
        var = _head_mean(dlt * dlt)
        y = dlt * lax.rsqrt(var + LN_EPS) * _silu(g_ref[0, rows, :].astype(F32))
        o_ref[0, rows, :] = y.astype(o_ref.dtype)
        return carry

    lax.fori_loop(0, n_groups, body, 0)


def _retention(p, pc, ret_decay, init_state):
    b, n, _ = p.shape
    dec = jnp.repeat(ret_decay.astype(F32), HEAD_DIM, axis=-1).reshape(2, PAIRS, 1, LANES)

    def col(cb):
        return pl.BlockSpec((1, n, LANES), lambda bi, pi: (bi, 0, cb + pi))

    in_specs = [pl.BlockSpec((2, 1, 1, LANES), lambda bi, pi: (0, pi, 0, 0)),
                col(CB_RQ), col(CB_RK), col(CB_RV), col(CB_RG)]
    args = [dec, p, p, p, p]
    if init_state:
        lc = pc.shape[1]
        in_specs += [pl.BlockSpec((1, lc, LANES), lambda bi, pi: (bi, 0, CB_RK + pi)),
                     pl.BlockSpec((1, lc, LANES), lambda bi, pi: (bi, 0, CB_RV + pi))]
        args += [pc, pc]
    c = RET_CHUNK
    return pl.pallas_call(
        functools.partial(_ret_kernel, init_state=init_state, group=min(RET_GROUP, n // c)),
        grid=(b, PAIRS),
        in_specs=in_specs,
        out_specs=col(0),
        out_shape=jax.ShapeDtypeStruct((b, n, RET_W), BF16),
        scratch_shapes=[pltpu.VMEM((n // c, 2 * LANES, LANES), BF16),
                        pltpu.VMEM((n // c, 2 * LANES, LANES), F32),
                        pltpu.VMEM((n // c, LANES, 2 * c), BF16),
                        pltpu.VMEM((c, 2 * LANES), F32),
                        pltpu.VMEM((2 * LANES, c), F32),
                        pltpu.VMEM((c, 2 * c), F32)],
        compiler_params=pltpu.CompilerParams(dimension_semantics=("parallel", "parallel")),
        name="retention",
    )(*args)


def _na_bias_plan(rows):
    nblk = rows // NA_ROWS
    win_h = min(WIN_H, rows)
    plan = np.full((3, NA_ROWS, 3, NA_ROWS), -1, np.int64)
    for kind, blk in enumerate((0, 1, nblk - 1)):
        for qr in range(NA_ROWS):
            r = blk * NA_ROWS + qr
            rs = min(max(r - win_h // 2, 0), rows - win_h)
            for kb in range(3):
                if not 0 <= blk + kb - 1 <= nblk - 1:
                    continue
                for krl in range(NA_ROWS):
                    kr = (blk + kb - 1) * NA_ROWS + krl
                    if rs <= kr < rs + win_h:
                        plan[kind, qr, kb, krl] = kr - r + WIN_H - 1
    return plan


def _na_bias_kernel(rpb_ref, o_ref, t_ref, *, plan):
    n_dr = 2 * WIN_H - 1
    qc = lax.broadcasted_iota(jnp.int32, (GRID_W, LANES), 0)
    lane = lax.broadcasted_iota(jnp.int32, (GRID_W, LANES), 1)
    kc = lane % GRID_W
    cs = jnp.clip(qc - WIN_W // 2, 0, GRID_W - WIN_W)
    col_ok = (kc >= cs) & (kc < cs + WIN_W)
    first = lane < GRID_W
    neg = jnp.full((GRID_W, LANES), NEG, F32)
    for dr in range(n_dr):
        row = jnp.broadcast_to(rpb_ref[0, 0, dr:dr + 1, :] * LOG2E, (GRID_W, LANES))
        lo = pltpu.roll(row, LANES - (WIN_W - 1), 1, stride=1, stride_axis=0)
        hi = pltpu.roll(row, GRID_W - (WIN_W - 1), 1, stride=1, stride_axis=0)
        t_ref[dr] = jnp.where(col_ok, jnp.where(first, lo, hi), neg)
    for kind in range(3):
        for qr in range(NA_ROWS):
            for kb in range(3):
                for pr in range(NA_ROWS // 2):
                    ia, ib = (int(plan[kind, qr, kb, 2 * pr + s]) for s in range(2))
                    a = t_ref[ia] if ia >= 0 else neg
                    b = t_ref[ib] if ib >= 0 else neg
                    col0 = kb * NA_TOK + pr * LANES
                    o_ref[0, kind, 0, qr * GRID_W:(qr + 1) * GRID_W, col0:col0 + LANES] = jnp.where(first, a, b)


def _na_bias_tables(rpb, rows):
    depth = rpb.shape[0]
    n_dr, n_dc = 2 * WIN_H - 1, 2 * WIN_W - 1
    padded = jnp.pad(rpb.astype(F32), ((0, 0), (0, 0), (0, 16 - n_dr), (0, LANES - n_dc)))
    return pl.pallas_call(
        functools.partial(_na_bias_kernel, plan=_na_bias_plan(rows)),
        grid=(depth, NA_HEADS),
        in_specs=[pl.BlockSpec((1, 1, 16, LANES), lambda l, h: (l, h, 0, 0))],
        out_specs=pl.BlockSpec((1, 3, 1, NA_TOK, 3 * NA_TOK), lambda l, h: (l, 0, h, 0, 0)),
        out_shape=jax.ShapeDtypeStruct((depth, 3, NA_HEADS, NA_TOK, 3 * NA_TOK), F32),
        scratch_shapes=[pltpu.VMEM((n_dr, GRID_W, LANES), F32)],
        name="na_bias",
    )(padded)


def _fold_lanes(blocks, op):
    tiles = [blk[:, j:j + LANES] for blk in blocks for j in range(0, blk.shape[1], LANES)]
    acc = tiles[0]
    for t in tiles[1:]:
        acc = op(acc, t)
    return acc


def _na_block(q, keys, vals, bias_ref, gate, needed):
    head0 = _lane_is_head0((NA_TOK, LANES))
    zero = jnp.zeros_like(q)
    n_loc = 3 * NA_TOK // LANES
    zero_tile = jnp.zeros((GRID_W, LANES), BF16)
    probs, inv = [], []
    for hh in range(2):
        qh = jnp.where(head0, q, zero) if hh == 0 else jnp.where(head0, zero, q)
        s_loc = [_dot_nt(qh, keys[t]) for t in range(3)]
        s_ctx = _dot_nt(qh, keys[3])
        p_rows, inv_rows = [], []
        for qr in range(NA_ROWS):
            rows = slice(qr * GRID_W, (qr + 1) * GRID_W)
            tiles = {}
            for t in needed[qr]:
                blk, half = divmod(t, NA_TOK // LANES)
                lanes = slice(half * LANES, (half + 1) * LANES)
                tiles[t] = s_loc[blk][rows, lanes] + bias_ref[0, hh, rows, t * LANES:(t + 1) * LANES]
            for c in range(s_ctx.shape[1] // LANES):
                tiles[n_loc + c] = s_ctx[rows, c * LANES:(c + 1) * LANES]
            m = jnp.max(_fold_lanes(list(tiles.values()), jnp.maximum), axis=-1, keepdims=True)
            p = {t: jnp.exp2(v - m) for t, v in tiles.items()}
            inv_rows.append(1.0 / jnp.sum(_fold_lanes(list(p.values()), jnp.add), axis=-1, keepdims=True))
            p_rows.append([p[t].astype(BF16) if t in p else zero_tile for t in range(n_loc + s_ctx.shape[1] // LANES)])
        probs.append(jnp.concatenate([jnp.concatenate(r, axis=1) for r in p_rows], axis=0))
        inv.append(jnp.concatenate(inv_rows, axis=0))
    v_cat = jnp.concatenate([jnp.where(head0, v, zero) for v in vals]
                            + [jnp.where(head0, zero, v) for v in vals], axis=0)
    o = _dot(jnp.concatenate(probs, axis=1), v_cat) * jnp.where(head0, inv[0], inv[1])
    return (o * _silu(gate.astype(F32))).astype(BF16)


def _na_kernel(*refs, interior):
    nkb = NA_STEP + 2
    q_ref, k_refs, v_refs = refs[0], refs[1:1 + nkb], refs[1 + nkb:1 + 2 * nkb]
    kx_ref, vx_ref, g_ref = refs[1 + 2 * nkb:4 + 2 * nkb]
    bias_refs, o_ref = refs[4 + 2 * nkb:-1], refs[-1]
    every = (tuple(range(3 * NA_TOK // LANES)),) * NA_ROWS

    def run(needed):
        for j, bias_ref in enumerate(bias_refs):
            rows = slice(j * NA_TOK, (j + 1) * NA_TOK)
            for pi in range(PAIRS):
                cols = slice(pi * LANES, (pi + 1) * LANES)
                keys = [r[0, :, cols] for r in k_refs[j:j + 3]] + [kx_ref[0, :, cols]]
                vals = [r[0, :, cols] for r in v_refs[j:j + 3]] + [vx_ref[0, :, cols]]
                o_ref[0, rows, cols] = _na_block(q_ref[0, rows, cols], keys, vals, bias_ref.at[:, 2 * pi:2 * pi + 2],
                                                 g_ref[0, rows, cols], needed)

    i = pl.program_id(1)
    at_end = (i == 0) | (i == pl.num_programs(1) - 1)

    @pl.when(at_end)
    def _():
        run(every)

    @pl.when(jnp.logical_not(at_end))
    def _():
        run(interior)


def _neighbourhood(p, pc, bias, layer):
    b, n, _ = p.shape
    nblk = n // NA_TOK
    nstep = nblk // NA_STEP
    grp = NA_W // LANES

    def key_blk(cb, shift):
        return pl.BlockSpec((1, NA_TOK, NA_W),
                            lambda bi, i: (bi, jnp.clip(NA_STEP * i + shift, 0, nblk - 1), cb // grp))

    def step_blk(cb):
        return pl.BlockSpec((1, NA_STEP * NA_TOK, NA_W), lambda bi, i: (bi, i, cb // grp))

    lc = pc.shape[1]

    def ctx(cb):
        return pl.BlockSpec((1, lc, NA_W), lambda bi, i: (bi, 0, cb // grp))

    def bias_blk(j):
        def index(bi, i):
            blk = NA_STEP * i + j
            return (layer, jnp.where(blk == 0, 0, jnp.where(blk == nblk - 1, 2, 1)), 0, 0, 0)
        return pl.BlockSpec((None, 1, NA_HEADS, NA_TOK, 3 * NA_TOK), index)

    per_tile = LANES // GRID_W
    seen = _na_bias_plan(n // GRID_W)[1].reshape(NA_ROWS, 3 * NA_ROWS // per_tile, per_tile) >= 0
    interior = tuple(tuple(int(t) for t in np.flatnonzero(seen[qr].any(axis=-1))) for qr in range(NA_ROWS))
    assert nstep > 2 and NA_ROWS >= WIN_H // 2
    shifts = range(-1, NA_STEP + 1)
    return pl.pallas_call(
        functools.partial(_na_kernel, interior=interior),
        grid=(b, nstep),
        in_specs=([step_blk(CB_NQ)] + [key_blk(CB_NK, s) for s in shifts] + [key_blk(CB_NV, s) for s in shifts]
                  + [ctx(CB_NK), ctx(CB_NV), step_blk(CB_NG)] + [bias_blk(j) for j in range(NA_STEP)]),
        out_specs=pl.BlockSpec((1, NA_STEP * NA_TOK, NA_W), lambda bi, i: (bi, i, 0)),
        out_shape=jax.ShapeDtypeStruct((b, n, NA_W), BF16),
        compiler_params=pltpu.CompilerParams(dimension_semantics=("parallel", "arbitrary")),
        name="neighbourhood",
    )(p, *([p] * (2 * len(shifts))), pc, pc, p, *([bias] * NA_STEP))


def _ctx_attn_kernel(q_ref, k_ref, v_ref, g_ref, o_ref):
    q, k, v = q_ref[0], k_ref[0], v_ref[0]
    head0 = _lane_is_head0(q.shape)
    zero = jnp.zeros_like(q)
    outs = []
    for hh in range(2):
        qh = jnp.where(head0, q, zero) if hh == 0 else jnp.where(head0, zero, q)
        s = _dot_nt(qh, k)
        p = jnp.exp2(s - jnp.max(s, axis=-1, keepdims=True))
        outs.append(_dot(p.astype(BF16), v) / jnp.sum(p, axis=-1, keepdims=True))
    o = jnp.where(head0, outs[0], outs[1])
    o_ref[0] = (o * _silu(g_ref[0].astype(F32))).astype(o_ref.dtype)


def _ctx_attention(pc):
    b, lc, _ = pc.shape

    def col(cb):
        return pl.BlockSpec((1, lc, LANES), lambda bi, pi: (bi, 0, cb + pi))

    return pl.pallas_call(
        _ctx_attn_kernel,
        grid=(b, PAIRS),
        in_specs=[col(CB_NQ), col(CB_NK), col(CB_NV), col(CB_NG)],
        out_specs=col(0),
        out_shape=jax.ShapeDtypeStruct((b, lc, NA_W), BF16),
        name="context_attention",
    )(pc, pc, pc, pc)


def _out_kernel(x_ref, mod_ref, ch_ref, cb_ref, cc_ref, cz_ref, hp_ref, cp_ref, hn_ref, cn_ref,
                yr_ref, yn_ref, cw_ref, cbias_ref, w_ref, lg_ref, lb_ref, o_ref, *, mod_row):
    i = pl.program_id(1)
    last = pl.num_programs(1) - 1
    u = cc_ref[0].astype(F32) * ch_ref[0].astype(F32)
    tm = u.shape[0]
    halo = hp_ref.shape[1]
    u_before = cp_ref[0, halo - 1:halo, :].astype(F32) * hp_ref[0, halo - 1:halo, :].astype(F32)
    u_after = cn_ref[0, 0:1, :].astype(F32) * hn_ref[0, 0:1, :].astype(F32)
    u_before = jnp.where(i == 0, 0.0, u_before)
    u_after = jnp.where(i == last, 0.0, u_after)
    row = lax.broadcasted_iota(jnp.int32, u.shape, 0)
    u_prev = jnp.where(row == 0, u_before, pltpu.roll(u, 1, axis=0))
    u_next = jnp.where(row == tm - 1, u_after, pltpu.roll(u, tm - 1, axis=0))
    cw = cw_ref[...]
    row = pl.program_id(0) if mod_row is None else mod_row
    g = mod_ref[pl.ds(row, 1), 2 * D_MODEL:] * (1.0 / DEEPNORM_ALPHA)
    w = w_ref[...].astype(BF16)
    sub = tm // OUT_SPLIT
    for r in range(OUT_SPLIT):
        rs = slice(r * sub, (r + 1) * sub)
        conv = u_prev[rs] * cw[0:1] + u[rs] * cw[1:2] + u_next[rs] * cw[2:3] + cbias_ref[...]
        y_conv = cb_ref[0, rs, :].astype(F32) * conv * _silu(cz_ref[0, rs, :].astype(F32))
        y = jnp.concatenate([y_conv.astype(BF16), yr_ref[0, rs, :], yn_ref[0, rs, :]], axis=-1)
        z = x_ref[0, rs, :] + g * _dot(y, w)
        mu = jnp.mean(z, axis=-1, keepdims=True)
        dlt = z - mu
        var = jnp.mean(dlt * dlt, axis=-1, keepdims=True)
        o_ref[0, rs, :] = dlt * lax.rsqrt(var + LN_EPS / DEEPNORM_ALPHA ** 2) * lg_ref[...] + lb_ref[...]


def _output(x, mod, mod_row, p, y_ret, y_na, conv_w, conv_b, w_out, layer, ln_g, ln_b):
    b, n, _ = x.shape
    tm = min(ROW_TILE, n)
    halo = 16
    per = tm // halo
    nh = n // halo

    def conv(cb):
        return pl.BlockSpec((1, tm, CONV_W), lambda bi, i: (bi, i, cb))

    def before(cb):
        return pl.BlockSpec((1, halo, CONV_W), lambda bi, i: (bi, jnp.maximum(i * per - 1, 0), cb))

    def after(cb):
        return pl.BlockSpec((1, halo, CONV_W), lambda bi, i: (bi, jnp.minimum((i + 1) * per, nh - 1), cb))

    def const(shape):
        return pl.BlockSpec(shape, lambda bi, i: (0,) * len(shape))

    return pl.pallas_call(
        functools.partial(_out_kernel, mod_row=mod_row),
        grid=(b, n // tm),
        in_specs=[pl.BlockSpec((1, tm, D_MODEL), lambda bi, i: (bi, i, 0)),
                  pl.BlockSpec((None, 8, 3 * D_MODEL), lambda bi, i: (layer, 0, 0)),
                  conv(CB_CH), conv(CB_CB), conv(CB_CC), conv(CB_CZ),
                  before(CB_CH), before(CB_CC), after(CB_CH), after(CB_CC),
                  pl.BlockSpec((1, tm, RET_W), lambda bi, i: (bi, i, 0)),
                  pl.BlockSpec((1, tm, NA_W), lambda bi, i: (bi, i, 0)),
                  const((3, CONV_W)), const((1, CONV_W)),
                  pl.BlockSpec((None, MIX_W, D_MODEL), lambda bi, i: (layer, 0, 0)),
                  const((1, D_MODEL)), const((1, D_MODEL))],
        out_specs=pl.BlockSpec((1, tm, D_MODEL), lambda bi, i: (bi, i, 0)),
        out_shape=jax.ShapeDtypeStruct((b, n, D_MODEL), F32),
        compiler_params=pltpu.CompilerParams(dimension_semantics=("parallel", "parallel")),
        name="output",
    )(x, mod, p, p, p, p, p, p, p, p, y_ret, y_na, conv_w, conv_b.reshape(1, CONV_W), w_out,
      ln_g.reshape(1, D_MODEL), ln_b.reshape(1, D_MODEL))


def kernel(x, c, ctx, c_ctx, w_mod, b_mod, w_in, conv_w, conv_b, ret_decay, na_rpb, w_out, ln_g, ln_b):
    b, n, d = x.shape
    act_t = jnp.pad(jnp.concatenate([c, c_ctx[None]], axis=0).T, ((0, 0), (0, 5)))
    mod = _modulation(act_t, w_mod, b_mod)
    tables = _rope_tables(n)
    na_bias = _na_bias_tables(na_rpb, n // GRID_W)
    xc = ctx
    for l in range(DEPTH):
        need_ctx = l < DEPTH - 1
        p, pc = _projection(x, xc, mod, w_in, l, tables, PROJ_W if need_ctx else KV_W)
        y_ret = _retention(p, pc, ret_decay[l], init_state=True)
        y_na = _neighbourhood(p, pc, na_bias, l)
        x_new = _output(x, mod, None, p, y_ret, y_na, conv_w[l], conv_b[l], w_out, l, ln_g[l], ln_b[l])
        if need_ctx:
            yc_ret = _retention(pc, None, ret_decay[l], init_state=False)
            yc_na = _ctx_attention(pc)
            xc = _output(xc, mod, b, pc, yc_ret, yc_na, conv_w[l], conv_b[l], w_out, l, ln_g[l], ln_b[l])
        x = x_new
    return x
```

```python
import functools

import numpy as np
import jax
import jax.numpy as jnp
from jax import lax
from jax.experimental import pallas as pl
from jax.experimental.pallas import tpu as pltpu

D_MODEL = 1024
DEPTH = 2
GRID_W = 64
HEAD_DIM = 64
CONV_W = 256
RET_HEADS = 6
RET_W = RET_HEADS * HEAD_DIM
NA_HEADS = 6
NA_W = NA_HEADS * HEAD_DIM
MIX_W = CONV_W + RET_W + NA_W
RET_CHUNK = 128
WIN_H = 8
WIN_W = 16
ROPE_BASE = 10000.0
LN_EPS = 1e-5
DEEPNORM_ALPHA = (2 * DEPTH) ** 0.25
PROJ_SPLITS = (RET_W, RET_W, NA_W, NA_W, RET_W, RET_W, NA_W, NA_W, CONV_W, CONV_W, CONV_W, CONV_W)
PROJ_W = sum(PROJ_SPLITS)
KV_W = 2 * RET_W + 2 * NA_W

LANES = 128
PAIRS = RET_HEADS // 2
CB_RK, CB_RV, CB_NK, CB_NV, CB_RQ, CB_RG, CB_NQ, CB_NG = 0, 3, 6, 9, 12, 15, 18, 21
CB_CH, CB_CB, CB_CC, CB_CZ = 12, 13, 14, 15
QK_SCALE = HEAD_DIM ** -0.5
NEG = -1e30
LOG2E = 1.4426950408889634
NA_ROWS = 4
NA_TOK = NA_ROWS * GRID_W
NA_STEP = 2
RET_GROUP = 64
ROW_TILE = 1024
OUT_SPLIT = 4
PROJ_CHUNK = 512

F32 = jnp.float32
BF16 = jnp.bfloat16


def _silu(v):
    return v * jax.nn.sigmoid(v)


def _dot(a, b):
    return jnp.dot(a, b, preferred_element_type=F32)


def _dot_nt(a, b):
    return lax.dot_general(a, b, (((1,), (1,)), ((), ())), preferred_element_type=F32)


def _dot_tn(a, b):
    return lax.dot_general(a, b, (((0,), (0,)), ((), ())), preferred_element_type=F32)


def _lane_is_head0(shape):
    return lax.broadcasted_iota(jnp.int32, shape, len(shape) - 1) < HEAD_DIM


def _mod_kernel(act_ref, w_ref, b_ref, o_ref):
    a = _silu(act_ref[...])
    w = w_ref[0]
    bias = b_ref[0]
    for r in range(3):
        o_ref[0, r:r + 1, :] = jnp.sum(a[:, r:r + 1] * w, axis=0, keepdims=True) + bias
    o_ref[0, 3:8, :] = jnp.zeros((5, w.shape[1]), F32)


def _modulation(act_t, w_mod, b_mod):
    tn = 1536
    n = w_mod.shape[-1]
    return pl.pallas_call(
        _mod_kernel,
        grid=(DEPTH, n // tn),
        in_specs=[pl.BlockSpec((D_MODEL, 8), lambda l, j: (0, 0)),
                  pl.BlockSpec((1, D_MODEL, tn), lambda l, j: (l, 0, j)),
                  pl.BlockSpec((1, 1, tn), lambda l, j: (l, 0, j))],
        out_specs=pl.BlockSpec((1, 8, tn), lambda l, j: (l, 0, j)),
        out_shape=jax.ShapeDtypeStruct((DEPTH, 8, n), F32),
        name="modulation",
    )(act_t, w_mod, b_mod.reshape(DEPTH, 1, n))


def _rope_tables(n):
    rows = n // GRID_W
    nf = HEAD_DIM // 4
    inv = ROPE_BASE ** (-jnp.arange(nf, dtype=F32) / nf)
    lane = np.arange(LANES)
    by_row = ((lane % HEAD_DIM) < HEAD_DIM // 2)[None]
    first = ((lane % (2 * nf)) < nf)[None]
    ang_r = jnp.tile(jnp.arange(rows).astype(F32)[:, None] * inv, (1, LANES // nf))
    ang_c = jnp.tile(jnp.arange(GRID_W).astype(F32)[:, None] * inv, (1, LANES // nf))
    parts = []
    for ang, own in ((ang_r, by_row), (ang_c, ~by_row)):
        cos, sin = jnp.cos(ang), jnp.sin(ang)
        parts.append((jnp.where(own, cos, 0.0), jnp.where(own & first, -sin, 0.0), jnp.where(own & ~first, sin, 0.0)))
    return parts[0] + parts[1]


def _project(x, mod_row, w_ref, tables, o_ref, n_cols):
    shift, scale = mod_row[:, :D_MODEL], mod_row[:, D_MODEL:2 * D_MODEL]
    h = (x * (1.0 + scale) + shift).astype(BF16)
    rotated = tuple(range(CB_RK, CB_RK + PAIRS)) + tuple(range(CB_RQ, CB_RQ + PAIRS))
    col_scale = {cb: QK_SCALE for cb in range(CB_RK, CB_RK + PAIRS)}
    col_scale.update({cb: QK_SCALE * LOG2E for cb in range(CB_NQ, CB_NQ + PAIRS)})
    for off in range(0, n_cols, PROJ_CHUNK):
        acc = _dot(h, w_ref[:, off:off + PROJ_CHUNK].astype(BF16))
        tiles = []
        for j in range(PROJ_CHUNK // LANES):
            cb = off // LANES + j
            v = acc[:, j * LANES:(j + 1) * LANES]
            if tables is not None and cb in rotated:
                cos, up, dn = tables
                v = v * cos + pltpu.roll(v, LANES - 16, axis=1) * up + pltpu.roll(v, 16, axis=1) * dn
            if cb in col_scale:
                v = v * col_scale[cb]
            tiles.append(v.astype(BF16))
        o_ref[0, :, off:off + PROJ_CHUNK] = jnp.concatenate(tiles, axis=1)


def _proj_kernel(x_ref, xc_ref, mod_ref, w_ref, *rest, ctx_row, n_cols_ctx):
    *table_refs, o_ref, oc_ref = rest
    tables = tuple(jnp.concatenate([r_ref[g:g + 1, :] + c_ref[...] for g in range(r_ref.shape[0])], axis=0)
                   for r_ref, c_ref in zip(table_refs[:3], table_refs[3:]))
    _project(x_ref[0], mod_ref[pl.ds(pl.program_id(0), 1), :], w_ref, tables, o_ref, PROJ_W)

    @pl.when(pl.program_id(1) == 0)
    def _():
        _project(xc_ref[0], mod_ref[ctx_row:ctx_row + 1, :], w_ref, None, oc_ref, n_cols_ctx)


def _projection(x, xc, mod, w_in, layer, tables, n_cols_ctx):
    b, n, _ = x.shape
    lc = xc.shape[1]
    tm = min(ROW_TILE, n)
    in_specs = ([pl.BlockSpec((1, tm, D_MODEL), lambda bi, i: (bi, i, 0)),
                 pl.BlockSpec((1, lc, D_MODEL), lambda bi, i: (bi, 0, 0)),
                 pl.BlockSpec((None, 8, 3 * D_MODEL), lambda bi, i: (layer, 0, 0)),
                 pl.BlockSpec((None, D_MODEL, PROJ_W), lambda bi, i: (layer, 0, 0), pipeline_mode=pl.Buffered(1))]
                + [pl.BlockSpec((tm // GRID_W, LANES), lambda bi, i: (i, 0))] * 3
                + [pl.BlockSpec((GRID_W, LANES), lambda bi, i: (0, 0))] * 3)
    return pl.pallas_call(
        functools.partial(_proj_kernel, ctx_row=b, n_cols_ctx=n_cols_ctx),
        grid=(b, n // tm),
        in_specs=in_specs,
        out_specs=[pl.BlockSpec((1, tm, PROJ_W), lambda bi, i: (bi, i, 0)),
                   pl.BlockSpec((1, lc, n_cols_ctx), lambda bi, i: (bi, 0, 0))],
        out_shape=[jax.ShapeDtypeStruct((b, n, PROJ_W), BF16), jax.ShapeDtypeStruct((b, lc, n_cols_ctx), BF16)],
        compiler_params=pltpu.CompilerParams(dimension_semantics=("parallel", "arbitrary")),
        name="projection",
    )(x, xc, mod, w_in, *tables)


def _log_sigmoid(v):
    return jnp.minimum(v, 0.0) - jnp.log1p(jnp.exp(-jnp.abs(v)))


def _block_diag(m):
    r = lax.broadcasted_iota(jnp.int32, m.shape, 0) < HEAD_DIM
    c = lax.broadcasted_iota(jnp.int32, m.shape, 1) < HEAD_DIM
    return jnp.where(r == c, m, 0.0)


def _head_mean(x):
    head0 = _lane_is_head0(x.shape)
    s0 = jnp.sum(jnp.where(head0, x, 0.0), axis=-1, keepdims=True)
    s1 = jnp.sum(jnp.where(head0, 0.0, x), axis=-1, keepdims=True)
    return jnp.where(head0, s0, s1) * (1.0 / HEAD_DIM)


def _ret_kernel(*refs, init_state, group):
    refs = list(refs)
    dec_ref, q_ref, k_ref, v_ref, g_ref = refs[:5]
    refs = refs[5:]
    if init_state:
        kc_ref, vc_ref = refs[:2]
        refs = refs[2:]
    o_ref, s_ref, u_ref, kt_ref, wq_ref, wkt_ref, d_ref = refs
    c = RET_CHUNK
    n_chunks = q_ref.shape[1] // c
    n_groups = n_chunks // group
    head0 = _lane_is_head0((c, LANES))
    lg_f = _log_sigmoid(dec_ref[0, 0])
    lg_b = _log_sigmoid(dec_ref[1, 0])
    i = lax.broadcasted_iota(jnp.int32, (c, LANES), 0).astype(F32)
    wq_ref[:, :LANES] = jnp.exp((i + 1.0) * lg_f)
    wq_ref[:, LANES:] = jnp.exp((c - i) * lg_b)
    wkt_ref[:LANES, :] = jnp.exp((c - 1.0 - i) * lg_f).T
    wkt_ref[LANES:, :] = jnp.exp(i * lg_b).T
    gc_f = jnp.exp(float(c) * lg_f)
    gc_b = jnp.exp(float(c) * lg_b)
    diff = (lax.broadcasted_iota(jnp.int32, (c, c), 0) - lax.broadcasted_iota(jnp.int32, (c, c), 1)).astype(F32)
    lower = diff >= 0
    for hh in range(2):
        lf = lg_f[:, hh * HEAD_DIM:hh * HEAD_DIM + 1]
        lb = lg_b[:, hh * HEAD_DIM:hh * HEAD_DIM + 1]
        d_ref[:, hh * c:(hh + 1) * c] = jnp.where(lower, jnp.exp(jnp.where(lower, diff, 0.0) * lf),
                                                  jnp.exp(jnp.where(lower, 0.0, -diff) * lb))
    if init_state:
        lc = kc_ref.shape[1]
        m = lax.broadcasted_iota(jnp.int32, (lc, LANES), 0).astype(F32)
        kcf = kc_ref[0].astype(F32)
        s_f0 = _block_diag(_dot_tn((kcf * jnp.exp((lc - 1.0 - m) * lg_f)).astype(BF16), vc_ref[0]))
        s_b0 = _block_diag(_dot_tn((kcf * jnp.exp(m * lg_b)).astype(BF16), vc_ref[0]))
    else:
        s_f0 = s_b0 = jnp.zeros((LANES, LANES), F32)
    r2 = lax.broadcasted_iota(jnp.int32, (2 * LANES, LANES), 0) % LANES < HEAD_DIM
    c2 = lax.broadcasted_iota(jnp.int32, (2 * LANES, LANES), 1) < HEAD_DIM
    diag2 = r2 == c2

    chan0 = lax.broadcasted_iota(jnp.int32, (LANES, c), 0) < HEAD_DIM

    def chunk(n):
        return pl.ds(pl.multiple_of(n * c, c), c)

    def increments(gi, carry):
        for j in range(group):
            n = gi * group + j
            ktb = k_ref[0, chunk(n), :].T
            kt = ktb.astype(F32)
            zero = jnp.zeros_like(ktb)
            kt_ref[n] = jnp.concatenate([jnp.where(chan0, ktb, zero), jnp.where(chan0, zero, ktb)], axis=1)
            lhs = jnp.concatenate([kt * wkt_ref[:LANES, :], kt * wkt_ref[LANES:, :]], axis=0).astype(BF16)
            u_ref[n] = jnp.where(diag2, _dot(lhs, v_ref[0, chunk(n), :]), 0.0)
        return carry

    lax.fori_loop(0, n_groups, increments, 0)

    def fwd_scan(gi, s):
        for j in range(group):
            n = gi * group + j
            s_ref[n, :LANES, :] = s.astype(BF16)
            s = s * gc_f + u_ref[n, :LANES, :]
        return s

    def bwd_scan(gi, s):
        for j in range(group):
            n = n_chunks - 1 - (gi * group + j)
            s_ref[n, LANES:, :] = s.astype(BF16)
            s = s * gc_b + u_ref[n, LANES:, :]
        return s

    lax.fori_loop(0, n_groups, fwd_scan, s_f0)
    lax.fori_loop(0, n_groups, bwd_scan, s_b0)

    def body(gi, carry):
        outs = []
        for j in range(group):
            n = gi * group + j
            rows = chunk(n)
            q, v = q_ref[0, rows, :], v_ref[0, rows, :]
            zero = jnp.zeros_like(v)
            vcat = jnp.concatenate([jnp.where(head0, v, zero), jnp.where(head0, zero, v)], axis=0)
            scores = _dot(q, kt_ref[n]) * d_ref[...]
            qf = q.astype(F32)
            qw = jnp.concatenate([qf * wq_ref[:, :LANES], qf * wq_ref[:, LANES:]], axis=1).astype(BF16)
            outs.append(_dot(scores.astype(BF16), vcat) + _dot(qw, s_ref[n]))
        o = jnp.concatenate(outs, axis=0)
        rows = pl.ds(pl.multiple_of(gi * (group * c), group * c), group * c)
        dlt = o - _head_mean(o)
        var = _head_mean(dlt * dlt)
        y = dlt * lax.rsqrt(var + LN_EPS) * _silu(g_ref[0, rows, :].astype(F32))
        o_ref[0, rows, :] = y.astype(o_ref.dtype)
        return carry

    lax.fori_loop(0, n_groups, body, 0)


def _retention(p, pc, ret_decay, init_state):
    b, n, _ = p.shape
    dec = jnp.repeat(ret_decay.astype(F32), HEAD_DIM, axis=-1).reshape(2, PAIRS, 1, LANES)

    def col(cb):
        return pl.BlockSpec((1, n, LANES), lambda bi, pi: (bi, 0, cb + pi))

    in_specs = [pl.BlockSpec((2, 1, 1, LANES), lambda bi, pi: (0, pi, 0, 0)),
                col(CB_RQ), col(CB_RK), col(CB_RV), col(CB_RG)]
    args = [dec, p, p, p, p]
    if init_state:
        lc = pc.shape[1]
        in_specs += [pl.BlockSpec((1, lc, LANES), lambda bi, pi: (bi, 0, CB_RK + pi)),
                     pl.BlockSpec((1, lc, LANES), lambda bi, pi: (bi, 0, CB_RV + pi))]
        args += [pc, pc]
    c = RET_CHUNK
    return pl.pallas_call(
        functools.partial(_ret_kernel, init_state=init_state, group=min(RET_GROUP, n // c)),
        grid=(b, PAIRS),
        in_specs=in_specs,
        out_specs=col(0),
        out_shape=jax.ShapeDtypeStruct((b, n, RET_W), BF16),
        scratch_shapes=[pltpu.VMEM((n // c, 2 * LANES, LANES), BF16),
                        pltpu.VMEM((n // c, 2 * LANES, LANES), F32),
                        pltpu.VMEM((n // c, LANES, 2 * c), BF16),
                        pltpu.VMEM((c, 2 * LANES), F32),
                        pltpu.VMEM((2 * LANES, c), F32),
                        pltpu.VMEM((c, 2 * c), F32)],
        compiler_params=pltpu.CompilerParams(dimension_semantics=("parallel", "parallel")),
        name="retention",
    )(*args)


def _na_bias_plan(rows):
    nblk = rows // NA_ROWS
    win_h = min(WIN_H, rows)
    plan = np.full((3, NA_ROWS, 3, NA_ROWS), -1, np.int64)
    for kind, blk in enumerate((0, 1, nblk - 1)):
        for qr in range(NA_ROWS):
            r = blk * NA_ROWS + qr
            rs = min(max(r - win_h // 2, 0), rows - win_h)
            for kb in range(3):
                if not 0 <= blk + kb - 1 <= nblk - 1:
                    continue
                for krl in range(NA_ROWS):
                    kr = (blk + kb - 1) * NA_ROWS + krl
                    if rs <= kr < rs + win_h:
                        plan[kind, qr, kb, krl] = kr - r + WIN_H - 1
    return plan


def _na_bias_kernel(rpb_ref, o_ref, t_ref, *, plan):
    n_dr = 2 * WIN_H - 1
    qc = lax.broadcasted_iota(jnp.int32, (GRID_W, LANES), 0)
    lane = lax.broadcasted_iota(jnp.int32, (GRID_W, LANES), 1)
    kc = lane % GRID_W
    cs = jnp.clip(qc - WIN_W // 2, 0, GRID_W - WIN_W)
    col_ok = (kc >= cs) & (kc < cs + WIN_W)
    first = lane < GRID_W
    neg = jnp.full((GRID_W, LANES), NEG, F32)
    for dr in range(n_dr):
        row = jnp.broadcast_to(rpb_ref[0, 0, dr:dr + 1, :] * LOG2E, (GRID_W, LANES))
        lo = pltpu.roll(row, LANES - (WIN_W - 1), 1, stride=1, stride_axis=0)
        hi = pltpu.roll(row, GRID_W - (WIN_W - 1), 1, stride=1, stride_axis=0)
        t_ref[dr] = jnp.where(col_ok, jnp.where(first, lo, hi), neg)
    for kind in range(3):
        for qr in range(NA_ROWS):
            for kb in range(3):
                for pr in range(NA_ROWS // 2):
                    ia, ib = (int(plan[kind, qr, kb, 2 * pr + s]) for s in range(2))
                    a = t_ref[ia] if ia >= 0 else neg
                    b = t_ref[ib] if ib >= 0 else neg
                    col0 = kb * NA_TOK + pr * LANES
                    o_ref[0, kind, 0, qr * GRID_W:(qr + 1) * GRID_W, col0:col0 + LANES] = jnp.where(first, a, b)


def _na_bias_tables(rpb, rows):
    depth = rpb.shape[0]
    n_dr, n_dc = 2 * WIN_H - 1, 2 * WIN_W - 1
    padded = jnp.pad(rpb.astype(F32), ((0, 0), (0, 0), (0, 16 - n_dr), (0, LANES - n_dc)))
    return pl.pallas_call(
        functools.partial(_na_bias_kernel, plan=_na_bias_plan(rows)),
        grid=(depth, NA_HEADS),
        in_specs=[pl.BlockSpec((1, 1, 16, LANES), lambda l, h: (l, h, 0, 0))],
        out_specs=pl.BlockSpec((1, 3, 1, NA_TOK, 3 * NA_TOK), lambda l, h: (l, 0, h, 0, 0)),
        out_shape=jax.ShapeDtypeStruct((depth, 3, NA_HEADS, NA_TOK, 3 * NA_TOK), F32),
        scratch_shapes=[pltpu.VMEM((n_dr, GRID_W, LANES), F32)],
        name="na_bias",
    )(padded)


def _fold_lanes(blocks, op):
    tiles = [blk[:, j:j + LANES] for blk in blocks for j in range(0, blk.shape[1], LANES)]
    acc = tiles[0]
    for t in tiles[1:]:
        acc = op(acc, t)
    return acc


def _na_kernel(*refs):
    nkb = NA_STEP + 2
    q_ref, k_refs, v_refs = refs[0], refs[1:1 + nkb], refs[1 + nkb:1 + 2 * nkb]
    kx_ref, vx_ref, g_ref = refs[1 + 2 * nkb:4 + 2 * nkb]
    bias_refs, o_ref = refs[4 + 2 * nkb:-1], refs[-1]
    head0 = _lane_is_head0((NA_TOK, LANES))
    for j, bias_ref in enumerate(bias_refs):
        rows = slice(j * NA_TOK, (j + 1) * NA_TOK)
        for pi in range(PAIRS):
            cols = slice(pi * LANES, (pi + 1) * LANES)
            q = q_ref[0, rows, cols]
            zero = jnp.zeros_like(q)
            keys = [r[0, :, cols] for r in k_refs[j:j + 3]] + [kx_ref[0, :, cols]]
            vals = [r[0, :, cols] for r in v_refs[j:j + 3]] + [vx_ref[0, :, cols]]
            probs, inv = [], []
            for hh in range(2):
                qh = jnp.where(head0, q, zero) if hh == 0 else jnp.where(head0, zero, q)
                s = [_dot_nt(qh, keys[t]) + bias_ref[0, 2 * pi + hh, :, t * NA_TOK:(t + 1) * NA_TOK] for t in range(3)]
                s.append(_dot_nt(qh, keys[3]))
                m = jnp.max(_fold_lanes(s, jnp.maximum), axis=-1, keepdims=True)
                p = [jnp.exp2(st - m) for st in s]
                inv.append(1.0 / jnp.sum(_fold_lanes(p, jnp.add), axis=-1, keepdims=True))
                probs += [pt.astype(BF16) for pt in p]
            v_cat = jnp.concatenate([jnp.where(head0, v, zero) for v in vals]
                                    + [jnp.where(head0, zero, v) for v in vals], axis=0)
            o = _dot(jnp.concatenate(probs, axis=1), v_cat) * jnp.where(head0, inv[0], inv[1])
            o_ref[0, rows, cols] = (o * _silu(g_ref[0, rows, cols].astype(F32))).astype(o_ref.dtype)


def _neighbourhood(p, pc, bias, layer):
    b, n, _ = p.shape
    nblk = n // NA_TOK
    nstep = nblk // NA_STEP
    grp = NA_W // LANES

    def key_blk(cb, shift):
        return pl.BlockSpec((1, NA_TOK, NA_W),
                            lambda bi, i: (bi, jnp.clip(NA_STEP * i + shift, 0, nblk - 1), cb // grp))

    def step_blk(cb):
        return pl.BlockSpec((1, NA_STEP * NA_TOK, NA_W), lambda bi, i: (bi, i, cb // grp))

    lc = pc.shape[1]

    def ctx(cb):
        return pl.BlockSpec((1, lc, NA_W), lambda bi, i: (bi, 0, cb // grp))

    def bias_blk(j):
        def index(bi, i):
            blk = NA_STEP * i + j
            return (layer, jnp.where(blk == 0, 0, jnp.where(blk == nblk - 1, 2, 1)), 0, 0, 0)
        return pl.BlockSpec((None, 1, NA_HEADS, NA_TOK, 3 * NA_TOK), index)

    shifts = range(-1, NA_STEP + 1)
    return pl.pallas_call(
        _na_kernel,
        grid=(b, nstep),
        in_specs=([step_blk(CB_NQ)] + [key_blk(CB_NK, s) for s in shifts] + [key_blk(CB_NV, s) for s in shifts]
                  + [ctx(CB_NK), ctx(CB_NV), step_blk(CB_NG)] + [bias_blk(j) for j in range(NA_STEP)]),
        out_specs=pl.BlockSpec((1, NA_STEP * NA_TOK, NA_W), lambda bi, i: (bi, i, 0)),
        out_shape=jax.ShapeDtypeStruct((b, n, NA_W), BF16),
        compiler_params=pltpu.CompilerParams(dimension_semantics=("parallel", "arbitrary")),
        name="neighbourhood",
    )(p, *([p] * (2 * len(shifts))), pc, pc, p, *([bias] * NA_STEP))


def _ctx_attn_kernel(q_ref, k_ref, v_ref, g_ref, o_ref):
    q, k, v = q_ref[0], k_ref[0], v_ref[0]
    head0 = _lane_is_head0(q.shape)
    zero = jnp.zeros_like(q)
    outs = []
    for hh in range(2):
        qh = jnp.where(head0, q, zero) if hh == 0 else jnp.where(head0, zero, q)
        s = _dot_nt(qh, k)
        p = jnp.exp2(s - jnp.max(s, axis=-1, keepdims=True))
        outs.append(_dot(p.astype(BF16), v) / jnp.sum(p, axis=-1, keepdims=True))
    o = jnp.where(head0, outs[0], outs[1])
    o_ref[0] = (o * _silu(g_ref[0].astype(F32))).astype(o_ref.dtype)


def _ctx_attention(pc):
    b, lc, _ = pc.shape

    def col(cb):
        return pl.BlockSpec((1, lc, LANES), lambda bi, pi: (bi, 0, cb + pi))

    return pl.pallas_call(
        _ctx_attn_kernel,
        grid=(b, PAIRS),
        in_specs=[col(CB_NQ), col(CB_NK), col(CB_NV), col(CB_NG)],
        out_specs=col(0),
        out_shape=jax.ShapeDtypeStruct((b, lc, NA_W), BF16),
        name="context_attention",
    )(pc, pc, pc, pc)


def _out_kernel(x_ref, mod_ref, ch_ref, cb_ref, cc_ref, cz_ref, hp_ref, cp_ref, hn_ref, cn_ref,
                yr_ref, yn_ref, cw_ref, cbias_ref, w_ref, lg_ref, lb_ref, o_ref, *, mod_row):
    i = pl.program_id(1)
    last = pl.num_programs(1) - 1
    u = cc_ref[0].astype(F32) * ch_ref[0].astype(F32)
    tm = u.shape[0]
    halo = hp_ref.shape[1]
    u_before = cp_ref[0, halo - 1:halo, :].astype(F32) * hp_ref[0, halo - 1:halo, :].astype(F32)
    u_after = cn_ref[0, 0:1, :].astype(F32) * hn_ref[0, 0:1, :].astype(F32)
    u_before = jnp.where(i == 0, 0.0, u_before)
    u_after = jnp.where(i == last, 0.0, u_after)
    row = lax.broadcasted_iota(jnp.int32, u.shape, 0)
    u_prev = jnp.where(row == 0, u_before, pltpu.roll(u, 1, axis=0))
    u_next = jnp.where(row == tm - 1, u_after, pltpu.roll(u, tm - 1, axis=0))
    cw = cw_ref[...]
    row = pl.program_id(0) if mod_row is None else mod_row
    g = mod_ref[pl.ds(row, 1), 2 * D_MODEL:] * (1.0 / DEEPNORM_ALPHA)
    w = w_ref[...].astype(BF16)
    sub = tm // OUT_SPLIT
    for r in range(OUT_SPLIT):
        rs = slice(r * sub, (r + 1) * sub)
        conv = u_prev[rs] * cw[0:1] + u[rs] * cw[1:2] + u_next[rs] * cw[2:3] + cbias_ref[...]
        y_conv = cb_ref[0, rs, :].astype(F32) * conv * _silu(cz_ref[0, rs, :].astype(F32))
        y = jnp.concatenate([y_conv.astype(BF16), yr_ref[0, rs, :], yn_ref[0, rs, :]], axis=-1)
        z = x_ref[0, rs, :] + g * _dot(y, w)
        mu = jnp.mean(z, axis=-1, keepdims=True)
        dlt = z - mu
        var = jnp.mean(dlt * dlt, axis=-1, keepdims=True)
        o_ref[0, rs, :] = dlt * lax.rsqrt(var + LN_EPS / DEEPNORM_ALPHA ** 2) * lg_ref[...] + lb_ref[...]


def _output(x, mod, mod_row, p, y_ret, y_na, conv_w, conv_b, w_out, layer, ln_g, ln_b):
    b, n, _ = x.shape
    tm = min(ROW_TILE, n)
    halo = 16
    per = tm // halo
    nh = n // halo

    def conv(cb):
        return pl.BlockSpec((1, tm, CONV_W), lambda bi, i: (bi, i, cb))

    def before(cb):
        return pl.BlockSpec((1, halo, CONV_W), lambda bi, i: (bi, jnp.maximum(i * per - 1, 0), cb))

    def after(cb):
        return pl.BlockSpec((1, halo, CONV_W), lambda bi, i: (bi, jnp.minimum((i + 1) * per, nh - 1), cb))

    def const(shape):
        return pl.BlockSpec(shape, lambda bi, i: (0,) * len(shape))

    return pl.pallas_call(
        functools.partial(_out_kernel, mod_row=mod_row),
        grid=(b, n // tm),
        in_specs=[pl.BlockSpec((1, tm, D_MODEL), lambda bi, i: (bi, i, 0)),
                  pl.BlockSpec((None, 8, 3 * D_MODEL), lambda bi, i: (layer, 0, 0)),
                  conv(CB_CH), conv(CB_CB), conv(CB_CC), conv(CB_CZ),
                  before(CB_CH), before(CB_CC), after(CB_CH), after(CB_CC),
                  pl.BlockSpec((1, tm, RET_W), lambda bi, i: (bi, i, 0)),
                  pl.BlockSpec((1, tm, NA_W), lambda bi, i: (bi, i, 0)),
                  const((3, CONV_W)), const((1, CONV_W)),
                  pl.BlockSpec((None, MIX_W, D_MODEL), lambda bi, i: (layer, 0, 0)),
                  const((1, D_MODEL)), const((1, D_MODEL))],
        out_specs=pl.BlockSpec((1, tm, D_MODEL), lambda bi, i: (bi, i, 0)),
        out_shape=jax.ShapeDtypeStruct((b, n, D_MODEL), F32),
        compiler_params=pltpu.CompilerParams(dimension_semantics=("parallel", "parallel")),
        name="output",
    )(x, mod, p, p, p, p, p, p, p, p, y_ret, y_na, conv_w, conv_b.reshape(1, CONV_W), w_out,
      ln_g.reshape(1, D_MODEL), ln_b.reshape(1, D_MODEL))


def kernel(x, c, ctx, c_ctx, w_mod, b_mod, w_in, conv_w, conv_b, ret_decay, na_rpb, w_out, ln_g, ln_b):
    b, n, d = x.shape
    act_t = jnp.pad(jnp.concatenate([c, c_ctx[None]], axis=0).T, ((0, 0), (0, 5)))
    mod = _modulation(act_t, w_mod, b_mod)
    tables = _rope_tables(n)
    na_bias = _na_bias_tables(na_rpb, n // GRID_W)
    xc = ctx
    for l in range(DEPTH):
        need_ctx = l < DEPTH - 1
        p, pc = _projection(x, xc, mod, w_in, l, tables, PROJ_W if need_ctx else KV_W)
        y_ret = _retention(p, pc, ret_decay[l], init_state=True)
        y_na = _neighbourhood(p, pc, na_bias, l)
        x_new = _output(x, mod, None, p, y_ret, y_na, conv_w[l], conv_b[l], w_out, l, ln_g[l], ln_b[l])
        if need_ctx:
            yc_ret = _retention(pc, None, ret_decay[l], init_state=False)
            yc_na = _ctx_attention(pc)
            xc = _output(xc, mod, b, pc, yc_ret, yc_na, conv_w[l], conv_b[l], w_out, l, ln_g[l], ln_b[l])
        x = x_new
    return x
```

```python
import functools

import numpy as np
import jax
import jax.numpy as jnp
from jax import lax
from jax.experimental import pallas as pl
from jax.experimental.pallas import tpu as pltpu

D_MODEL = 1024
DEPTH = 2
GRID_W = 64
HEAD_DIM = 64
CONV_W = 256
RET_HEADS = 6
RET_W = RET_HEADS * HEAD_DIM
NA_HEADS = 6
NA_W = NA_HEADS * HEAD_DIM
MIX_W = CONV_W + RET_W + NA_W
RET_CHUNK = 128
WIN_H = 8
WIN_W = 16
ROPE_BASE = 10000.0
LN_EPS = 1e-5
DEEPNORM_ALPHA = (2 * DEPTH) ** 0.25
PROJ_SPLITS = (RET_W, RET_W, NA_W, NA_W, RET_W, RET_W, NA_W, NA_W, CONV_W, CONV_W, CONV_W, CONV_W)
PROJ_W = sum(PROJ_SPLITS)
KV_W = 2 * RET_W + 2 * NA_W

LANES = 128
PAIRS = RET_HEADS // 2
CB_RK, CB_RV, CB_NK, CB_NV, CB_RQ, CB_RG, CB_NQ, CB_NG = 0, 3, 6, 9, 12, 15, 18, 21
CONV_OFF = PROJ_W - 4 * CONV_W
P_W = CONV_OFF + 2 * CONV_W
CB_U, CB_GZ = CONV_OFF // CONV_W, CONV_OFF // CONV_W + 1
QK_SCALE = HEAD_DIM ** -0.5
NEG = -1e30
LOG2E = 1.4426950408889634
NA_ROWS = 4
NA_TOK = NA_ROWS * GRID_W
NA_STEP = 2
RET_GROUP = 64
ROW_TILE = 1024
OUT_SPLIT = 4
PROJ_CHUNK = 512

F32 = jnp.float32
BF16 = jnp.bfloat16


def _silu(v):
    return v * jax.nn.sigmoid(v)


def _dot(a, b):
    return jnp.dot(a, b, preferred_element_type=F32)


def _dot_nt(a, b):
    return lax.dot_general(a, b, (((1,), (1,)), ((), ())), preferred_element_type=F32)


def _dot_tn(a, b):
    return lax.dot_general(a, b, (((0,), (0,)), ((), ())), preferred_element_type=F32)


def _lane_is_head0(shape):
    return lax.broadcasted_iota(jnp.int32, shape, len(shape) - 1) < HEAD_DIM


def _mod_kernel(act_ref, w_ref, b_ref, o_ref):
    a = _silu(act_ref[...])
    w = w_ref[0]
    bias = b_ref[0]
    for r in range(3):
        o_ref[0, r:r + 1, :] = jnp.sum(a[:, r:r + 1] * w, axis=0, keepdims=True) + bias
    o_ref[0, 3:8, :] = jnp.zeros((5, w.shape[1]), F32)


def _modulation(act_t, w_mod, b_mod):
    tn = 1536
    n = w_mod.shape[-1]
    return pl.pallas_call(
        _mod_kernel,
        grid=(DEPTH, n // tn),
        in_specs=[pl.BlockSpec((D_MODEL, 8), lambda l, j: (0, 0)),
                  pl.BlockSpec((1, D_MODEL, tn), lambda l, j: (l, 0, j)),
                  pl.BlockSpec((1, 1, tn), lambda l, j: (l, 0, j))],
        out_specs=pl.BlockSpec((1, 8, tn), lambda l, j: (l, 0, j)),
        out_shape=jax.ShapeDtypeStruct((DEPTH, 8, n), F32),
        name="modulation",
    )(act_t, w_mod, b_mod.reshape(DEPTH, 1, n))


def _rope_tables(n):
    rows = n // GRID_W
    nf = HEAD_DIM // 4
    inv = ROPE_BASE ** (-jnp.arange(nf, dtype=F32) / nf)
    lane = np.arange(LANES)
    by_row = ((lane % HEAD_DIM) < HEAD_DIM // 2)[None]
    first = ((lane % (2 * nf)) < nf)[None]
    ang_r = jnp.tile(jnp.arange(rows).astype(F32)[:, None] * inv, (1, LANES // nf))
    ang_c = jnp.tile(jnp.arange(GRID_W).astype(F32)[:, None] * inv, (1, LANES // nf))
    parts = []
    for ang, own in ((ang_r, by_row), (ang_c, ~by_row)):
        cos, sin = jnp.cos(ang), jnp.sin(ang)
        parts.append((jnp.where(own, cos, 0.0), jnp.where(own & first, -sin, 0.0), jnp.where(own & ~first, sin, 0.0)))
    return parts[0] + parts[1]


def _project(x, mod_row, w_ref, tables, o_ref, n_cols):
    shift, scale = mod_row[:, :D_MODEL], mod_row[:, D_MODEL:2 * D_MODEL]
    h = (x * (1.0 + scale) + shift).astype(BF16)
    rotated = tuple(range(CB_RK, CB_RK + PAIRS)) + tuple(range(CB_RQ, CB_RQ + PAIRS))
    col_scale = {cb: QK_SCALE for cb in range(CB_RK, CB_RK + PAIRS)}
    col_scale.update({cb: QK_SCALE * LOG2E for cb in range(CB_NQ, CB_NQ + PAIRS)})
    for off in range(0, min(n_cols, CONV_OFF), PROJ_CHUNK):
        acc = _dot(h, w_ref[:, off:off + PROJ_CHUNK].astype(BF16))
        tiles = []
        for j in range(PROJ_CHUNK // LANES):
            cb = off // LANES + j
            v = acc[:, j * LANES:(j + 1) * LANES]
            if tables is not None and cb in rotated:
                cos, up, dn = tables
                v = v * cos + pltpu.roll(v, LANES - 16, axis=1) * up + pltpu.roll(v, 16, axis=1) * dn
            if cb in col_scale:
                v = v * col_scale[cb]
            tiles.append(v.astype(BF16))
        o_ref[0, :, off:off + PROJ_CHUNK] = jnp.concatenate(tiles, axis=1)
    if n_cols > CONV_OFF:
        ch, cb, cc, cz = (_dot(h, w_ref[:, CONV_OFF + j * CONV_W:CONV_OFF + (j + 1) * CONV_W].astype(BF16))
                          for j in range(4))
        o_ref[0, :, CONV_OFF:CONV_OFF + CONV_W] = (cc * ch).astype(BF16)
        o_ref[0, :, CONV_OFF + CONV_W:CONV_OFF + 2 * CONV_W] = (cb * _silu(cz)).astype(BF16)


def _proj_kernel(x_ref, xc_ref, mod_ref, w_ref, *rest, ctx_row, n_cols_ctx):
    *table_refs, o_ref, oc_ref = rest
    tables = tuple(jnp.concatenate([r_ref[g:g + 1, :] + c_ref[...] for g in range(r_ref.shape[0])], axis=0)
                   for r_ref, c_ref in zip(table_refs[:3], table_refs[3:]))
    _project(x_ref[0], mod_ref[pl.ds(pl.program_id(0), 1), :], w_ref, tables, o_ref, PROJ_W)

    @pl.when(pl.program_id(1) == 0)
    def _():
        _project(xc_ref[0], mod_ref[ctx_row:ctx_row + 1, :], w_ref, None, oc_ref, n_cols_ctx)


def _projection(x, xc, mod, w_in, layer, tables, n_cols_ctx):
    b, n, _ = x.shape
    lc = xc.shape[1]
    tm = min(ROW_TILE, n)
    pc_w = P_W if n_cols_ctx > CONV_OFF else n_cols_ctx
    in_specs = ([pl.BlockSpec((1, tm, D_MODEL), lambda bi, i: (bi, i, 0)),
                 pl.BlockSpec((1, lc, D_MODEL), lambda bi, i: (bi, 0, 0)),
                 pl.BlockSpec((None, 8, 3 * D_MODEL), lambda bi, i: (layer, 0, 0)),
                 pl.BlockSpec((None, D_MODEL, PROJ_W), lambda bi, i: (layer, 0, 0), pipeline_mode=pl.Buffered(1))]
                + [pl.BlockSpec((tm // GRID_W, LANES), lambda bi, i: (i, 0))] * 3
                + [pl.BlockSpec((GRID_W, LANES), lambda bi, i: (0, 0))] * 3)
    return pl.pallas_call(
        functools.partial(_proj_kernel, ctx_row=b, n_cols_ctx=n_cols_ctx),
        grid=(b, n // tm),
        in_specs=in_specs,
        out_specs=[pl.BlockSpec((1, tm, P_W), lambda bi, i: (bi, i, 0)),
                   pl.BlockSpec((1, lc, pc_w), lambda bi, i: (bi, 0, 0))],
        out_shape=[jax.ShapeDtypeStruct((b, n, P_W), BF16), jax.ShapeDtypeStruct((b, lc, pc_w), BF16)],
        compiler_params=pltpu.CompilerParams(dimension_semantics=("parallel", "arbitrary")),
        name="projection",
    )(x, xc, mod, w_in, *tables)


def _log_sigmoid(v):
    return jnp.minimum(v, 0.0) - jnp.log1p(jnp.exp(-jnp.abs(v)))


def _block_diag(m):
    r = lax.broadcasted_iota(jnp.int32, m.shape, 0) < HEAD_DIM
    c = lax.broadcasted_iota(jnp.int32, m.shape, 1) < HEAD_DIM
    return jnp.where(r == c, m, 0.0)


def _head_mean(x):
    head0 = _lane_is_head0(x.shape)
    s0 = jnp.sum(jnp.where(head0, x, 0.0), axis=-1, keepdims=True)
    s1 = jnp.sum(jnp.where(head0, 0.0, x), axis=-1, keepdims=True)
    return jnp.where(head0, s0, s1) * (1.0 / HEAD_DIM)


def _ret_kernel(*refs, init_state, group):
    refs = list(refs)
    dec_ref, q_ref, k_ref, v_ref, g_ref = refs[:5]
    refs = refs[5:]
    if init_state:
        kc_ref, vc_ref = refs[:2]
        refs = refs[2:]
    o_ref, s_ref, u_ref, kt_ref, wq_ref, wkt_ref, d_ref = refs
    c = RET_CHUNK
    n_chunks = q_ref.shape[1] // c
    n_groups = n_chunks // group
    head0 = _lane_is_head0((c, LANES))
    lg_f = _log_sigmoid(dec_ref[0, 0])
    lg_b = _log_sigmoid(dec_ref[1, 0])
    i = lax.broadcasted_iota(jnp.int32, (c, LANES), 0).astype(F32)
    wq_ref[:, :LANES] = jnp.exp((i + 1.0) * lg_f)
    wq_ref[:, LANES:] = jnp.exp((c - i) * lg_b)
    wkt_ref[:LANES, :] = jnp.exp((c - 1.0 - i) * lg_f).T
    wkt_ref[LANES:, :] = jnp.exp(i * lg_b).T
    gc_f = jnp.exp(float(c) * lg_f)
    gc_b = jnp.exp(float(c) * lg_b)
    diff = (lax.broadcasted_iota(jnp.int32, (c, c), 0) - lax.broadcasted_iota(jnp.int32, (c, c), 1)).astype(F32)
    lower = diff >= 0
    for hh in range(2):
        lf = lg_f[:, hh * HEAD_DIM:hh * HEAD_DIM + 1]
        lb = lg_b[:, hh * HEAD_DIM:hh * HEAD_DIM + 1]
        d_ref[:, hh * c:(hh + 1) * c] = jnp.where(lower, jnp.exp(jnp.where(lower, diff, 0.0) * lf),
                                                  jnp.exp(jnp.where(lower, 0.0, -diff) * lb))
    if init_state:
        lc = kc_ref.shape[1]
        m = lax.broadcasted_iota(jnp.int32, (lc, LANES), 0).astype(F32)
        kcf = kc_ref[0].astype(F32)
        s_f0 = _block_diag(_dot_tn((kcf * jnp.exp((lc - 1.0 - m) * lg_f)).astype(BF16), vc_ref[0]))
        s_b0 = _block_diag(_dot_tn((kcf * jnp.exp(m * lg_b)).astype(BF16), vc_ref[0]))
    else:
        s_f0 = s_b0 = jnp.zeros((LANES, LANES), F32)
    r2 = lax.broadcasted_iota(jnp.int32, (2 * LANES, LANES), 0) % LANES < HEAD_DIM
    c2 = lax.broadcasted_iota(jnp.int32, (2 * LANES, LANES), 1) < HEAD_DIM
    diag2 = r2 == c2

    chan0 = lax.broadcasted_iota(jnp.int32, (LANES, c), 0) < HEAD_DIM

    def chunk(n):
        return pl.ds(pl.multiple_of(n * c, c), c)

    def increments(gi, carry):
        for j in range(group):
            n = gi * group + j
            ktb = k_ref[0, chunk(n), :].T
            kt = ktb.astype(F32)
            zero = jnp.zeros_like(ktb)
            kt_ref[n] = jnp.concatenate([jnp.where(chan0, ktb, zero), jnp.where(chan0, zero, ktb)], axis=1)
            lhs = jnp.concatenate([kt * wkt_ref[:LANES, :], kt * wkt_ref[LANES:, :]], axis=0).astype(BF16)
            u_ref[n] = jnp.where(diag2, _dot(lhs, v_ref[0, chunk(n), :]), 0.0)
        return carry

    lax.fori_loop(0, n_groups, increments, 0)

    def fwd_scan(gi, s):
        for j in range(group):
            n = gi * group + j
            s_ref[n, :LANES, :] = s.astype(BF16)
            s = s * gc_f + u_ref[n, :LANES, :]
        return s

    def bwd_scan(gi, s):
        for j in range(group):
            n = n_chunks - 1 - (gi * group + j)
            s_ref[n, LANES:, :] = s.astype(BF16)
            s = s * gc_b + u_ref[n, LANES:, :]
        return s

    lax.fori_loop(0, n_groups, fwd_scan, s_f0)
    lax.fori_loop(0, n_groups, bwd_scan, s_b0)

    def body(gi, carry):
        outs = []
        for j in range(group):
            n = gi * group + j
            rows = chunk(n)
            q, v = q_ref[0, rows, :], v_ref[0, rows, :]
            zero = jnp.zeros_like(v)
            vcat = jnp.concatenate([jnp.where(head0, v, zero), jnp.where(head0, zero, v)], axis=0)
            scores = _dot(q, kt_ref[n]) * d_ref[...]
            qf = q.astype(F32)
            qw = jnp.concatenate([qf * wq_ref[:, :LANES], qf * wq_ref[:, LANES:]], axis=1).astype(BF16)
            outs.append(_dot(scores.astype(BF16), vcat) + _dot(qw, s_ref[n]))
        o = jnp.concatenate(outs, axis=0)
        rows = pl.ds(pl.multiple_of(gi * (group * c), group * c), group * c)
        dlt = o - _head_mean(o)
        var = _head_mean(dlt * dlt)
        y = dlt * lax.rsqrt(var + LN_EPS) * _silu(g_ref[0, rows, :].astype(F32))
        o_ref[0, rows, :] = y.astype(o_ref.dtype)
        return carry

    lax.fori_loop(0, n_groups, body, 0)


def _retention(p, pc, ret_decay, init_state):
    b, n, _ = p.shape
    dec = jnp.repeat(ret_decay.astype(F32), HEAD_DIM, axis=-1).reshape(2, PAIRS, 1, LANES)

    def col(cb):
        return pl.BlockSpec((1, n, LANES), lambda bi, pi: (bi, 0, cb + pi))

    in_specs = [pl.BlockSpec((2, 1, 1, LANES), lambda bi, pi: (0, pi, 0, 0)),
                col(CB_RQ), col(CB_RK), col(CB_RV), col(CB_RG)]
    args = [dec, p, p, p, p]
    if init_state:
        lc = pc.shape[1]
        in_specs += [pl.BlockSpec((1, lc, LANES), lambda bi, pi: (bi, 0, CB_RK + pi)),
                     pl.BlockSpec((1, lc, LANES), lambda bi, pi: (bi, 0, CB_RV + pi))]
        args += [pc, pc]
    c = RET_CHUNK
    return pl.pallas_call(
        functools.partial(_ret_kernel, init_state=init_state, group=min(RET_GROUP, n // c)),
        grid=(b, PAIRS),
        in_specs=in_specs,
        out_specs=col(0),
        out_shape=jax.ShapeDtypeStruct((b, n, RET_W), BF16),
        scratch_shapes=[pltpu.VMEM((n // c, 2 * LANES, LANES), BF16),
                        pltpu.VMEM((n // c, 2 * LANES, LANES), F32),
                        pltpu.VMEM((n // c, LANES, 2 * c), BF16),
                        pltpu.VMEM((c, 2 * LANES), F32),
                        pltpu.VMEM((2 * LANES, c), F32),
                        pltpu.VMEM((c, 2 * c), F32)],
        compiler_params=pltpu.CompilerParams(dimension_semantics=("parallel", "parallel")),
        name="retention",
    )(*args)


def _na_bias_plan(rows):
    nblk = rows // NA_ROWS
    win_h = min(WIN_H, rows)
    plan = np.full((3, NA_ROWS, 3, NA_ROWS), -1, np.int64)
    for kind, blk in enumerate((0, 1, nblk - 1)):
        for qr in range(NA_ROWS):
            r = blk * NA_ROWS + qr
            rs = min(max(r - win_h // 2, 0), rows - win_h)
            for kb in range(3):
                if not 0 <= blk + kb - 1 <= nblk - 1:
                    continue
                for krl in range(NA_ROWS):
                    kr = (blk + kb - 1) * NA_ROWS + krl
                    if rs <= kr < rs + win_h:
                        plan[kind, qr, kb, krl] = kr - r + WIN_H - 1
    return plan


def _na_bias_kernel(rpb_ref, o_ref, t_ref, *, plan):
    n_dr = 2 * WIN_H - 1
    qc = lax.broadcasted_iota(jnp.int32, (GRID_W, LANES), 0)
    lane = lax.broadcasted_iota(jnp.int32, (GRID_W, LANES), 1)
    kc = lane % GRID_W
    cs = jnp.clip(qc - WIN_W // 2, 0, GRID_W - WIN_W)
    col_ok = (kc >= cs) & (kc < cs + WIN_W)
    first = lane < GRID_W
    neg = jnp.full((GRID_W, LANES), NEG, F32)
    for dr in range(n_dr):
        row = jnp.broadcast_to(rpb_ref[0, 0, dr:dr + 1, :] * LOG2E, (GRID_W, LANES))
        lo = pltpu.roll(row, LANES - (WIN_W - 1), 1, stride=1, stride_axis=0)
        hi = pltpu.roll(row, GRID_W - (WIN_W - 1), 1, stride=1, stride_axis=0)
        t_ref[dr] = jnp.where(col_ok, jnp.where(first, lo, hi), neg)
    for kind in range(3):
        for qr in range(NA_ROWS):
            for kb in range(3):
                for pr in range(NA_ROWS // 2):
                    ia, ib = (int(plan[kind, qr, kb, 2 * pr + s]) for s in range(2))
                    a = t_ref[ia] if ia >= 0 else neg
                    b = t_ref[ib] if ib >= 0 else neg
                    col0 = kb * NA_TOK + pr * LANES
                    o_ref[0, kind, 0, qr * GRID_W:(qr + 1) * GRID_W, col0:col0 + LANES] = jnp.where(first, a, b)


def _na_bias_tables(rpb, rows):
    depth = rpb.shape[0]
    n_dr, n_dc = 2 * WIN_H - 1, 2 * WIN_W - 1
    padded = jnp.pad(rpb.astype(F32), ((0, 0), (0, 0), (0, 16 - n_dr), (0, LANES - n_dc)))
    return pl.pallas_call(
        functools.partial(_na_bias_kernel, plan=_na_bias_plan(rows)),
        grid=(depth, NA_HEADS),
        in_specs=[pl.BlockSpec((1, 1, 16, LANES), lambda l, h: (l, h, 0, 0))],
        out_specs=pl.BlockSpec((1, 3, 1, NA_TOK, 3 * NA_TOK), lambda l, h: (l, 0, h, 0, 0)),
        out_shape=jax.ShapeDtypeStruct((depth, 3, NA_HEADS, NA_TOK, 3 * NA_TOK), F32),
        scratch_shapes=[pltpu.VMEM((n_dr, GRID_W, LANES), F32)],
        name="na_bias",
    )(padded)


def _fold_lanes(blocks, op):
    tiles = [blk[:, j:j + LANES] for blk in blocks for j in range(0, blk.shape[1], LANES)]
    acc = tiles[0]
    for t in tiles[1:]:
        acc = op(acc, t)
    return acc


def _na_kernel(*refs):
    nkb = NA_STEP + 2
    q_ref, k_refs, v_refs = refs[0], refs[1:1 + nkb], refs[1 + nkb:1 + 2 * nkb]
    kx_ref, vx_ref, g_ref = refs[1 + 2 * nkb:4 + 2 * nkb]
    bias_refs, o_ref = refs[4 + 2 * nkb:-1], refs[-1]
    head0 = _lane_is_head0((NA_TOK, LANES))
    for j, bias_ref in enumerate(bias_refs):
        rows = slice(j * NA_TOK, (j + 1) * NA_TOK)
        for pi in range(PAIRS):
            cols = slice(pi * LANES, (pi + 1) * LANES)
            q = q_ref[0, rows, cols]
            zero = jnp.zeros_like(q)
            keys = [r[0, :, cols] for r in k_refs[j:j + 3]] + [kx_ref[0, :, cols]]
            vals = [r[0, :, cols] for r in v_refs[j:j + 3]] + [vx_ref[0, :, cols]]
            probs, inv = [], []
            for hh in range(2):
                qh = jnp.where(head0, q, zero) if hh == 0 else jnp.where(head0, zero, q)
                s = [_dot_nt(qh, keys[t]) + bias_ref[0, 2 * pi + hh, :, t * NA_TOK:(t + 1) * NA_TOK] for t in range(3)]
                s.append(_dot_nt(qh, keys[3]))
                m = jnp.max(_fold_lanes(s, jnp.maximum), axis=-1, keepdims=True)
                p = [jnp.exp2(st - m) for st in s]
                inv.append(1.0 / jnp.sum(_fold_lanes(p, jnp.add), axis=-1, keepdims=True))
                probs += [pt.astype(BF16) for pt in p]
            v_cat = jnp.concatenate([jnp.where(head0, v, zero) for v in vals]
                                    + [jnp.where(head0, zero, v) for v in vals], axis=0)
            o = _dot(jnp.concatenate(probs, axis=1), v_cat) * jnp.where(head0, inv[0], inv[1])
            o_ref[0, rows, cols] = (o * _silu(g_ref[0, rows, cols].astype(F32))).astype(o_ref.dtype)


def _neighbourhood(p, pc, bias, layer):
    b, n, _ = p.shape
    nblk = n // NA_TOK
    nstep = nblk // NA_STEP
    grp = NA_W // LANES

    def key_blk(cb, shift):
        return pl.BlockSpec((1, NA_TOK, NA_W),
                            lambda bi, i: (bi, jnp.clip(NA_STEP * i + shift, 0, nblk - 1), cb // grp))

    def step_blk(cb):
        return pl.BlockSpec((1, NA_STEP * NA_TOK, NA_W), lambda bi, i: (bi, i, cb // grp))

    lc = pc.shape[1]

    def ctx(cb):
        return pl.BlockSpec((1, lc, NA_W), lambda bi, i: (bi, 0, cb // grp))

    def bias_blk(j):
        def index(bi, i):
            blk = NA_STEP * i + j
            return (layer, jnp.where(blk == 0, 0, jnp.where(blk == nblk - 1, 2, 1)), 0, 0, 0)
        return pl.BlockSpec((None, 1, NA_HEADS, NA_TOK, 3 * NA_TOK), index)

    shifts = range(-1, NA_STEP + 1)
    return pl.pallas_call(
        _na_kernel,
        grid=(b, nstep),
        in_specs=([step_blk(CB_NQ)] + [key_blk(CB_NK, s) for s in shifts] + [key_blk(CB_NV, s) for s in shifts]
                  + [ctx(CB_NK), ctx(CB_NV), step_blk(CB_NG)] + [bias_blk(j) for j in range(NA_STEP)]),
        out_specs=pl.BlockSpec((1, NA_STEP * NA_TOK, NA_W), lambda bi, i: (bi, i, 0)),
        out_shape=jax.ShapeDtypeStruct((b, n, NA_W), BF16),
        compiler_params=pltpu.CompilerParams(dimension_semantics=("parallel", "arbitrary")),
        name="neighbourhood",
    )(p, *([p] * (2 * len(shifts))), pc, pc, p, *([bias] * NA_STEP))


def _ctx_attn_kernel(q_ref, k_ref, v_ref, g_ref, o_ref):
    q, k, v = q_ref[0], k_ref[0], v_ref[0]
    head0 = _lane_is_head0(q.shape)
    zero = jnp.zeros_like(q)
    outs = []
    for hh in range(2):
        qh = jnp.where(head0, q, zero) if hh == 0 else jnp.where(head0, zero, q)
        s = _dot_nt(qh, k)
        p = jnp.exp2(s - jnp.max(s, axis=-1, keepdims=True))
        outs.append(_dot(p.astype(BF16), v) / jnp.sum(p, axis=-1, keepdims=True))
    o = jnp.where(head0, outs[0], outs[1])
    o_ref[0] = (o * _silu(g_ref[0].astype(F32))).astype(o_ref.dtype)


def _ctx_attention(pc):
    b, lc, _ = pc.shape

    def col(cb):
        return pl.BlockSpec((1, lc, LANES), lambda bi, pi: (bi, 0, cb + pi))

    return pl.pallas_call(
        _ctx_attn_kernel,
        grid=(b, PAIRS),
        in_specs=[col(CB_NQ), col(CB_NK), col(CB_NV), col(CB_NG)],
        out_specs=col(0),
        out_shape=jax.ShapeDtypeStruct((b, lc, NA_W), BF16),
        name="context_attention",
    )(pc, pc, pc, pc)


def _out_kernel(x_ref, mod_ref, u_ref, gz_ref, ub_ref, ua_ref,
                yr_ref, yn_ref, cw_ref, cbias_ref, w_ref, lg_ref, lb_ref, o_ref, *, mod_row):
    i = pl.program_id(1)
    last = pl.num_programs(1) - 1
    u = u_ref[0].astype(F32)
    tm = u.shape[0]
    halo = ub_ref.shape[1]
    u_before = jnp.where(i == 0, 0.0, ub_ref[0, halo - 1:halo, :].astype(F32))
    u_after = jnp.where(i == last, 0.0, ua_ref[0, 0:1, :].astype(F32))
    row = lax.broadcasted_iota(jnp.int32, u.shape, 0)
    u_prev = jnp.where(row == 0, u_before, pltpu.roll(u, 1, axis=0))
    u_next = jnp.where(row == tm - 1, u_after, pltpu.roll(u, tm - 1, axis=0))
    cw = cw_ref[...]
    row = pl.program_id(0) if mod_row is None else mod_row
    g = mod_ref[pl.ds(row, 1), 2 * D_MODEL:] * (1.0 / DEEPNORM_ALPHA)
    w = w_ref[...].astype(BF16)
    sub = tm // OUT_SPLIT
    for r in range(OUT_SPLIT):
        rs = slice(r * sub, (r + 1) * sub)
        conv = u_prev[rs] * cw[0:1] + u[rs] * cw[1:2] + u_next[rs] * cw[2:3] + cbias_ref[...]
        y_conv = gz_ref[0, rs, :].astype(F32) * conv
        y = jnp.concatenate([y_conv.astype(BF16), yr_ref[0, rs, :], yn_ref[0, rs, :]], axis=-1)
        z = x_ref[0, rs, :] + g * _dot(y, w)
        mu = jnp.mean(z, axis=-1, keepdims=True)
        dlt = z - mu
        var = jnp.mean(dlt * dlt, axis=-1, keepdims=True)
        o_ref[0, rs, :] = dlt * lax.rsqrt(var + LN_EPS / DEEPNORM_ALPHA ** 2) * lg_ref[...] + lb_ref[...]


def _output(x, mod, mod_row, p, y_ret, y_na, conv_w, conv_b, w_out, layer, ln_g, ln_b):
    b, n, _ = x.shape
    tm = min(ROW_TILE, n)
    halo = 16
    per = tm // halo
    nh = n // halo

    def conv(cb):
        return pl.BlockSpec((1, tm, CONV_W), lambda bi, i: (bi, i, cb))

    def before(cb):
        return pl.BlockSpec((1, halo, CONV_W), lambda bi, i: (bi, jnp.maximum(i * per - 1, 0), cb))

    def after(cb):
        return pl.BlockSpec((1, halo, CONV_W), lambda bi, i: (bi, jnp.minimum((i + 1) * per, nh - 1), cb))

    def const(shape):
        return pl.BlockSpec(shape, lambda bi, i: (0,) * len(shape))

    return pl.pallas_call(
        functools.partial(_out_kernel, mod_row=mod_row),
        grid=(b, n // tm),
        in_specs=[pl.BlockSpec((1, tm, D_MODEL), lambda bi, i: (bi, i, 0)),
                  pl.BlockSpec((None, 8, 3 * D_MODEL), lambda bi, i: (layer, 0, 0)),
                  conv(CB_U), conv(CB_GZ), before(CB_U), after(CB_U),
                  pl.BlockSpec((1, tm, RET_W), lambda bi, i: (bi, i, 0)),
                  pl.BlockSpec((1, tm, NA_W), lambda bi, i: (bi, i, 0)),
                  const((3, CONV_W)), const((1, CONV_W)),
                  pl.BlockSpec((None, MIX_W, D_MODEL), lambda bi, i: (layer, 0, 0)),
                  const((1, D_MODEL)), const((1, D_MODEL))],
        out_specs=pl.BlockSpec((1, tm, D_MODEL), lambda bi, i: (bi, i, 0)),
        out_shape=jax.ShapeDtypeStruct((b, n, D_MODEL), F32),
        compiler_params=pltpu.CompilerParams(dimension_semantics=("parallel", "parallel")),
        name="output",
    )(x, mod, p, p, p, p, y_ret, y_na, conv_w, conv_b.reshape(1, CONV_W), w_out,
      ln_g.reshape(1, D_MODEL), ln_b.reshape(1, D_MODEL))


def kernel(x, c, ctx, c_ctx, w_mod, b_mod, w_in, conv_w, conv_b, ret_decay, na_rpb, w_out, ln_g, ln_b):
    b, n, d = x.shape
    act_t = jnp.pad(jnp.concatenate([c, c_ctx[None]], axis=0).T, ((0, 0), (0, 5)))
    mod = _modulation(act_t, w_mod, b_mod)
    tables = _rope_tables(n)
    na_bias = _na_bias_tables(na_rpb, n // GRID_W)
    xc = ctx
    for l in range(DEPTH):
        need_ctx = l < DEPTH - 1
        p, pc = _projection(x, xc, mod, w_in, l, tables, PROJ_W if need_ctx else KV_W)
        y_ret = _retention(p, pc, ret_decay[l], init_state=True)
        y_na = _neighbourhood(p, pc, na_bias, l)
        x_new = _output(x, mod, None, p, y_ret, y_na, conv_w[l], conv_b[l], w_out, l, ln_g[l], ln_b[l])
        if need_ctx:
            yc_ret = _retention(pc, None, ret_decay[l], init_state=False)
            yc_na = _ctx_attention(pc)
            xc = _output(xc, mod, b, pc, yc_ret, yc_na, conv_w[l], conv_b[l], w_out, l, ln_g[l], ln_b[l])
        x = x_new
    return x
```

```python
import functools

import numpy as np
import jax
import jax.numpy as jnp
from jax import lax
from jax.experimental import pallas as pl
from jax.experimental.pallas import tpu as pltpu

D_MODEL = 1024
DEPTH = 2
GRID_W = 64
HEAD_DIM = 64
CONV_W = 256
RET_HEADS = 6
RET_W = RET_HEADS * HEAD_DIM
NA_HEADS = 6
NA_W = NA_HEADS * HEAD_DIM
MIX_W = CONV_W + RET_W + NA_W
RET_CHUNK = 128
WIN_H = 8
WIN_W = 16
ROPE_BASE = 10000.0
LN_EPS = 1e-5
DEEPNORM_ALPHA = (2 * DEPTH) ** 0.25
PROJ_SPLITS = (RET_W, RET_W, NA_W, NA_W, RET_W, RET_W, NA_W, NA_W, CONV_W, CONV_W, CONV_W, CONV_W)
PROJ_W = sum(PROJ_SPLITS)
KV_W = 2 * RET_W + 2 * NA_W

LANES = 128
PAIRS = RET_HEADS // 2
CB_RK, CB_RV, CB_NK, CB_NV, CB_RQ, CB_RG, CB_NQ, CB_NG = 0, 3, 6, 9, 12, 15, 18, 21
CONV_OFF = PROJ_W - 4 * CONV_W
P_W = CONV_OFF + 2 * CONV_W
CB_U, CB_GZ = CONV_OFF // CONV_W, CONV_OFF // CONV_W + 1
QK_SCALE = HEAD_DIM ** -0.5
NEG = -1e30
LOG2E = 1.4426950408889634
NA_ROWS = 4
NA_TOK = NA_ROWS * GRID_W
NA_STEP = 2
RET_GROUP = 64
ROW_TILE = 1024
OUT_TILE = 2048
OUT_SUB = 256
PROJ_CHUNK = 512

F32 = jnp.float32
BF16 = jnp.bfloat16


def _silu(v):
    return v * jax.nn.sigmoid(v)


def _dot(a, b):
    return jnp.dot(a, b, preferred_element_type=F32)


def _dot_nt(a, b):
    return lax.dot_general(a, b, (((1,), (1,)), ((), ())), preferred_element_type=F32)


def _dot_tn(a, b):
    return lax.dot_general(a, b, (((0,), (0,)), ((), ())), preferred_element_type=F32)


def _lane_is_head0(shape):
    return lax.broadcasted_iota(jnp.int32, shape, len(shape) - 1) < HEAD_DIM


def _mod_kernel(act_ref, w_ref, b_ref, o_ref):
    a = _silu(act_ref[...])
    w = w_ref[0]
    bias = b_ref[0]
    for r in range(3):
        o_ref[0, r:r + 1, :] = jnp.sum(a[:, r:r + 1] * w, axis=0, keepdims=True) + bias
    o_ref[0, 3:8, :] = jnp.zeros((5, w.shape[1]), F32)


def _modulation(act_t, w_mod, b_mod):
    tn = 1536
    n = w_mod.shape[-1]
    return pl.pallas_call(
        _mod_kernel,
        grid=(DEPTH, n // tn),
        in_specs=[pl.BlockSpec((D_MODEL, 8), lambda l, j: (0, 0)),
                  pl.BlockSpec((1, D_MODEL, tn), lambda l, j: (l, 0, j)),
                  pl.BlockSpec((1, 1, tn), lambda l, j: (l, 0, j))],
        out_specs=pl.BlockSpec((1, 8, tn), lambda l, j: (l, 0, j)),
        out_shape=jax.ShapeDtypeStruct((DEPTH, 8, n), F32),
        name="modulation",
    )(act_t, w_mod, b_mod.reshape(DEPTH, 1, n))


def _rope_tables(n):
    rows = n // GRID_W
    nf = HEAD_DIM // 4
    inv = ROPE_BASE ** (-jnp.arange(nf, dtype=F32) / nf)
    lane = np.arange(LANES)
    by_row = ((lane % HEAD_DIM) < HEAD_DIM // 2)[None]
    first = ((lane % (2 * nf)) < nf)[None]
    ang_r = jnp.tile(jnp.arange(rows).astype(F32)[:, None] * inv, (1, LANES // nf))
    ang_c = jnp.tile(jnp.arange(GRID_W).astype(F32)[:, None] * inv, (1, LANES // nf))
    parts = []
    for ang, own in ((ang_r, by_row), (ang_c, ~by_row)):
        cos, sin = jnp.cos(ang), jnp.sin(ang)
        parts.append((jnp.where(own, cos, 0.0), jnp.where(own & first, -sin, 0.0), jnp.where(own & ~first, sin, 0.0)))
    return parts[0] + parts[1]


def _project(x, mod_row, w_ref, tables, o_ref, n_cols):
    shift, scale = mod_row[:, :D_MODEL], mod_row[:, D_MODEL:2 * D_MODEL]
    h = (x * (1.0 + scale) + shift).astype(BF16)
    rotated = tuple(range(CB_RK, CB_RK + PAIRS)) + tuple(range(CB_RQ, CB_RQ + PAIRS))
    col_scale = {cb: QK_SCALE for cb in range(CB_RK, CB_RK + PAIRS)}
    col_scale.update({cb: QK_SCALE * LOG2E for cb in range(CB_NQ, CB_NQ + PAIRS)})
    for off in range(0, min(n_cols, CONV_OFF), PROJ_CHUNK):
        acc = _dot(h, w_ref[:, off:off + PROJ_CHUNK].astype(BF16))
        tiles = []
        for j in range(PROJ_CHUNK // LANES):
            cb = off // LANES + j
            v = acc[:, j * LANES:(j + 1) * LANES]
            if tables is not None and cb in rotated:
                cos, up, dn = tables
                v = v * cos + pltpu.roll(v, LANES - 16, axis=1) * up + pltpu.roll(v, 16, axis=1) * dn
            if cb in col_scale:
                v = v * col_scale[cb]
            tiles.append(v.astype(BF16))
        o_ref[0, :, off:off + PROJ_CHUNK] = jnp.concatenate(tiles, axis=1)
    if n_cols > CONV_OFF:
        ch, cb, cc, cz = (_dot(h, w_ref[:, CONV_OFF + j * CONV_W:CONV_OFF + (j + 1) * CONV_W].astype(BF16))
                          for j in range(4))
        o_ref[0, :, CONV_OFF:CONV_OFF + CONV_W] = (cc * ch).astype(BF16)
        o_ref[0, :, CONV_OFF + CONV_W:CONV_OFF + 2 * CONV_W] = (cb * _silu(cz)).astype(BF16)


def _proj_kernel(x_ref, xc_ref, mod_ref, w_ref, *rest, ctx_row, n_cols_ctx):
    *table_refs, o_ref, oc_ref = rest
    tables = tuple(jnp.concatenate([r_ref[g:g + 1, :] + c_ref[...] for g in range(r_ref.shape[0])], axis=0)
                   for r_ref, c_ref in zip(table_refs[:3], table_refs[3:]))
    _project(x_ref[0], mod_ref[pl.ds(pl.program_id(0), 1), :], w_ref, tables, o_ref, PROJ_W)

    @pl.when(pl.program_id(1) == 0)
    def _():
        _project(xc_ref[0], mod_ref[ctx_row:ctx_row + 1, :], w_ref, None, oc_ref, n_cols_ctx)


def _projection(x, xc, mod, w_in, layer, tables, n_cols_ctx):
    b, n, _ = x.shape
    lc = xc.shape[1]
    tm = min(ROW_TILE, n)
    pc_w = P_W if n_cols_ctx > CONV_OFF else n_cols_ctx
    in_specs = ([pl.BlockSpec((1, tm, D_MODEL), lambda bi, i: (bi, i, 0)),
                 pl.BlockSpec((1, lc, D_MODEL), lambda bi, i: (bi, 0, 0)),
                 pl.BlockSpec((None, 8, 3 * D_MODEL), lambda bi, i: (layer, 0, 0)),
                 pl.BlockSpec((None, D_MODEL, PROJ_W), lambda bi, i: (layer, 0, 0), pipeline_mode=pl.Buffered(1))]
                + [pl.BlockSpec((tm // GRID_W, LANES), lambda bi, i: (i, 0))] * 3
                + [pl.BlockSpec((GRID_W, LANES), lambda bi, i: (0, 0))] * 3)
    return pl.pallas_call(
        functools.partial(_proj_kernel, ctx_row=b, n_cols_ctx=n_cols_ctx),
        grid=(b, n // tm),
        in_specs=in_specs,
        out_specs=[pl.BlockSpec((1, tm, P_W), lambda bi, i: (bi, i, 0)),
                   pl.BlockSpec((1, lc, pc_w), lambda bi, i: (bi, 0, 0))],
        out_shape=[jax.ShapeDtypeStruct((b, n, P_W), BF16), jax.ShapeDtypeStruct((b, lc, pc_w), BF16)],
        compiler_params=pltpu.CompilerParams(dimension_semantics=("parallel", "arbitrary")),
        name="projection",
    )(x, xc, mod, w_in, *tables)


def _log_sigmoid(v):
    return jnp.minimum(v, 0.0) - jnp.log1p(jnp.exp(-jnp.abs(v)))


def _block_diag(m):
    r = lax.broadcasted_iota(jnp.int32, m.shape, 0) < HEAD_DIM
    c = lax.broadcasted_iota(jnp.int32, m.shape, 1) < HEAD_DIM
    return jnp.where(r == c, m, 0.0)


def _head_mean(x):
    head0 = _lane_is_head0(x.shape)
    s0 = jnp.sum(jnp.where(head0, x, 0.0), axis=-1, keepdims=True)
    s1 = jnp.sum(jnp.where(head0, 0.0, x), axis=-1, keepdims=True)
    return jnp.where(head0, s0, s1) * (1.0 / HEAD_DIM)


def _attend(q_ref, k_ref, v_ref, g_ref, o_ref):
    q, k, v = q_ref[0], k_ref[0], v_ref[0]
    head0 = _lane_is_head0(q.shape)
    zero = jnp.zeros_like(q)
    outs = []
    for hh in range(2):
        qh = jnp.where(head0, q, zero) if hh == 0 else jnp.where(head0, zero, q)
        s = _dot_nt(qh, k)
        p = jnp.exp2(s - jnp.max(s, axis=-1, keepdims=True))
        outs.append(_dot(p.astype(BF16), v) / jnp.sum(p, axis=-1, keepdims=True))
    o = jnp.where(head0, outs[0], outs[1])
    o_ref[0] = (o * _silu(g_ref[0].astype(F32))).astype(o_ref.dtype)


def _ret_kernel(*refs, init_state, attend, group):
    refs = list(refs)
    dec_ref, q_ref, k_ref, v_ref, g_ref = refs[:5]
    refs = refs[5:]
    if init_state:
        kc_ref, vc_ref = refs[:2]
        refs = refs[2:]
    if attend:
        attn_refs, refs = refs[:4], refs[4:]
        o_ref, ao_ref, s_ref, u_ref, kt_ref, wq_ref, wkt_ref, d_ref = refs
        _attend(*attn_refs, ao_ref)
    else:
        o_ref, s_ref, u_ref, kt_ref, wq_ref, wkt_ref, d_ref = refs
    c = RET_CHUNK
    n_chunks = q_ref.shape[1] // c
    n_groups = n_chunks // group
    head0 = _lane_is_head0((c, LANES))
    lg_f = _log_sigmoid(dec_ref[0, 0])
    lg_b = _log_sigmoid(dec_ref[1, 0])
    i = lax.broadcasted_iota(jnp.int32, (c, LANES), 0).astype(F32)
    wq_ref[:, :LANES] = jnp.exp((i + 1.0) * lg_f)
    wq_ref[:, LANES:] = jnp.exp((c - i) * lg_b)
    wkt_ref[:LANES, :] = jnp.exp((c - 1.0 - i) * lg_f).T
    wkt_ref[LANES:, :] = jnp.exp(i * lg_b).T
    gc_f = jnp.exp(float(c) * lg_f)
    gc_b = jnp.exp(float(c) * lg_b)
    diff = (lax.broadcasted_iota(jnp.int32, (c, c), 0) - lax.broadcasted_iota(jnp.int32, (c, c), 1)).astype(F32)
    lower = diff >= 0
    for hh in range(2):
        lf = lg_f[:, hh * HEAD_DIM:hh * HEAD_DIM + 1]
        lb = lg_b[:, hh * HEAD_DIM:hh * HEAD_DIM + 1]
        d_ref[:, hh * c:(hh + 1) * c] = jnp.where(lower, jnp.exp(jnp.where(lower, diff, 0.0) * lf),
                                                  jnp.exp(jnp.where(lower, 0.0, -diff) * lb))
    if init_state:
        lc = kc_ref.shape[1]
        m = lax.broadcasted_iota(jnp.int32, (lc, LANES), 0).astype(F32)
        kcf = kc_ref[0].astype(F32)
        s_f0 = _block_diag(_dot_tn((kcf * jnp.exp((lc - 1.0 - m) * lg_f)).astype(BF16), vc_ref[0]))
        s_b0 = _block_diag(_dot_tn((kcf * jnp.exp(m * lg_b)).astype(BF16), vc_ref[0]))
    else:
        s_f0 = s_b0 = jnp.zeros((LANES, LANES), F32)
    r2 = lax.broadcasted_iota(jnp.int32, (2 * LANES, LANES), 0) % LANES < HEAD_DIM
    c2 = lax.broadcasted_iota(jnp.int32, (2 * LANES, LANES), 1) < HEAD_DIM
    diag2 = r2 == c2

    chan0 = lax.broadcasted_iota(jnp.int32, (LANES, c), 0) < HEAD_DIM

    def chunk(n):
        return pl.ds(pl.multiple_of(n * c, c), c)

    def increments(gi, carry):
        for j in range(group):
            n = gi * group + j
            ktb = k_ref[0, chunk(n), :].T
            kt = ktb.astype(F32)
            zero = jnp.zeros_like(ktb)
            kt_ref[n] = jnp.concatenate([jnp.where(chan0, ktb, zero), jnp.where(chan0, zero, ktb)], axis=1)
            lhs = jnp.concatenate([kt * wkt_ref[:LANES, :], kt * wkt_ref[LANES:, :]], axis=0).astype(BF16)
            u_ref[n] = jnp.where(diag2, _dot(lhs, v_ref[0, chunk(n), :]), 0.0)
        return carry

    lax.fori_loop(0, n_groups, increments, 0)

    def fwd_scan(gi, s):
        for j in range(group):
            n = gi * group + j
            s_ref[n, :LANES, :] = s.astype(BF16)
            s = s * gc_f + u_ref[n, :LANES, :]
        return s

    def bwd_scan(gi, s):
        for j in range(group):
            n = n_chunks - 1 - (gi * group + j)
            s_ref[n, LANES:, :] = s.astype(BF16)
            s = s * gc_b + u_ref[n, LANES:, :]
        return s

    lax.fori_loop(0, n_groups, fwd_scan, s_f0)
    lax.fori_loop(0, n_groups, bwd_scan, s_b0)

    def body(gi, carry):
        outs = []
        for j in range(group):
            n = gi * group + j
            rows = chunk(n)
            q, v = q_ref[0, rows, :], v_ref[0, rows, :]
            zero = jnp.zeros_like(v)
            vcat = jnp.concatenate([jnp.where(head0, v, zero), jnp.where(head0, zero, v)], axis=0)
            scores = _dot(q, kt_ref[n]) * d_ref[...]
            qf = q.astype(F32)
            qw = jnp.concatenate([qf * wq_ref[:, :LANES], qf * wq_ref[:, LANES:]], axis=1).astype(BF16)
            outs.append(_dot(scores.astype(BF16), vcat) + _dot(qw, s_ref[n]))
        o = jnp.concatenate(outs, axis=0)
        rows = pl.ds(pl.multiple_of(gi * (group * c), group * c), group * c)
        dlt = o - _head_mean(o)
        var = _head_mean(dlt * dlt)
        y = dlt * lax.rsqrt(var + LN_EPS) * _silu(g_ref[0, rows, :].astype(F32))
        o_ref[0, rows, :] = y.astype(o_ref.dtype)
        return carry

    lax.fori_loop(0, n_groups, body, 0)


def _retention(p, pc, ret_decay, init_state, attend=False):
    b, n, _ = p.shape
    dec = jnp.repeat(ret_decay.astype(F32), HEAD_DIM, axis=-1).reshape(2, PAIRS, 1, LANES)

    def col(cb):
        return pl.BlockSpec((1, n, LANES), lambda bi, pi: (bi, 0, cb + pi))

    in_specs = [pl.BlockSpec((2, 1, 1, LANES), lambda bi, pi: (0, pi, 0, 0)),
                col(CB_RQ), col(CB_RK), col(CB_RV), col(CB_RG)]
    args = [dec, p, p, p, p]
    if init_state:
        lc = pc.shape[1]
        in_specs += [pl.BlockSpec((1, lc, LANES), lambda bi, pi: (bi, 0, CB_RK + pi)),
                     pl.BlockSpec((1, lc, LANES), lambda bi, pi: (bi, 0, CB_RV + pi))]
        args += [pc, pc]
    out_specs, out_shape = col(0), jax.ShapeDtypeStruct((b, n, RET_W), BF16)
    if attend:
        in_specs += [col(CB_NQ), col(CB_NK), col(CB_NV), col(CB_NG)]
        args += [p, p, p, p]
        out_specs, out_shape = [out_specs, col(0)], [out_shape, jax.ShapeDtypeStruct((b, n, NA_W), BF16)]
    c = RET_CHUNK
    return pl.pallas_call(
        functools.partial(_ret_kernel, init_state=init_state, attend=attend, group=min(RET_GROUP, n // c)),
        grid=(b, PAIRS),
        in_specs=in_specs,
        out_specs=out_specs,
        out_shape=out_shape,
        scratch_shapes=[pltpu.VMEM((n // c, 2 * LANES, LANES), BF16),
                        pltpu.VMEM((n // c, 2 * LANES, LANES), F32),
                        pltpu.VMEM((n // c, LANES, 2 * c), BF16),
                        pltpu.VMEM((c, 2 * LANES), F32),
                        pltpu.VMEM((2 * LANES, c), F32),
                        pltpu.VMEM((c, 2 * c), F32)],
        compiler_params=pltpu.CompilerParams(dimension_semantics=("parallel", "parallel")),
        name="retention",
    )(*args)


def _na_bias_plan(rows):
    nblk = rows // NA_ROWS
    win_h = min(WIN_H, rows)
    plan = np.full((3, NA_ROWS, 3, NA_ROWS), -1, np.int64)
    for kind, blk in enumerate((0, 1, nblk - 1)):
        for qr in range(NA_ROWS):
            r = blk * NA_ROWS + qr
            rs = min(max(r - win_h // 2, 0), rows - win_h)
            for kb in range(3):
                if not 0 <= blk + kb - 1 <= nblk - 1:
                    continue
                for krl in range(NA_ROWS):
                    kr = (blk + kb - 1) * NA_ROWS + krl
                    if rs <= kr < rs + win_h:
                        plan[kind, qr, kb, krl] = kr - r + WIN_H - 1
    return plan


def _na_bias_kernel(rpb_ref, o_ref, t_ref, *, plan):
    n_dr = 2 * WIN_H - 1
    qc = lax.broadcasted_iota(jnp.int32, (GRID_W, LANES), 0)
    lane = lax.broadcasted_iota(jnp.int32, (GRID_W, LANES), 1)
    kc = lane % GRID_W
    cs = jnp.clip(qc - WIN_W // 2, 0, GRID_W - WIN_W)
    col_ok = (kc >= cs) & (kc < cs + WIN_W)
    first = lane < GRID_W
    neg = jnp.full((GRID_W, LANES), NEG, F32)
    for dr in range(n_dr):
        row = jnp.broadcast_to(rpb_ref[0, 0, dr:dr + 1, :] * LOG2E, (GRID_W, LANES))
        lo = pltpu.roll(row, LANES - (WIN_W - 1), 1, stride=1, stride_axis=0)
        hi = pltpu.roll(row, GRID_W - (WIN_W - 1), 1, stride=1, stride_axis=0)
        t_ref[dr] = jnp.where(col_ok, jnp.where(first, lo, hi), neg)
    for kind in range(3):
        for qr in range(NA_ROWS):
            for kb in range(3):
                for pr in range(NA_ROWS // 2):
                    ia, ib = (int(plan[kind, qr, kb, 2 * pr + s]) for s in range(2))
                    a = t_ref[ia] if ia >= 0 else neg
                    b = t_ref[ib] if ib >= 0 else neg
                    col0 = kb * NA_TOK + pr * LANES
                    o_ref[0, kind, 0, qr * GRID_W:(qr + 1) * GRID_W, col0:col0 + LANES] = jnp.where(first, a, b)


def _na_bias_tables(rpb, rows):
    depth = rpb.shape[0]
    n_dr, n_dc = 2 * WIN_H - 1, 2 * WIN_W - 1
    padded = jnp.pad(rpb.astype(F32), ((0, 0), (0, 0), (0, 16 - n_dr), (0, LANES - n_dc)))
    return pl.pallas_call(
        functools.partial(_na_bias_kernel, plan=_na_bias_plan(rows)),
        grid=(depth, NA_HEADS),
        in_specs=[pl.BlockSpec((1, 1, 16, LANES), lambda l, h: (l, h, 0, 0))],
        out_specs=pl.BlockSpec((1, 3, 1, NA_TOK, 3 * NA_TOK), lambda l, h: (l, 0, h, 0, 0)),
        out_shape=jax.ShapeDtypeStruct((depth, 3, NA_HEADS, NA_TOK, 3 * NA_TOK), F32),
        scratch_shapes=[pltpu.VMEM((n_dr, GRID_W, LANES), F32)],
        name="na_bias",
    )(padded)


def _fold_lanes(blocks, op):
    tiles = [blk[:, j:j + LANES] for blk in blocks for j in range(0, blk.shape[1], LANES)]
    acc = tiles[0]
    for t in tiles[1:]:
        acc = op(acc, t)
    return acc


def _na_kernel(*refs):
    nkb = NA_STEP + 2
    q_ref, k_refs, v_refs = refs[0], refs[1:1 + nkb], refs[1 + nkb:1 + 2 * nkb]
    kx_ref, vx_ref, g_ref = refs[1 + 2 * nkb:4 + 2 * nkb]
    bias_refs, o_ref = refs[4 + 2 * nkb:-1], refs[-1]
    head0 = _lane_is_head0((NA_TOK, LANES))
    for j, bias_ref in enumerate(bias_refs):
        rows = slice(j * NA_TOK, (j + 1) * NA_TOK)
        for pi in range(PAIRS):
            cols = slice(pi * LANES, (pi + 1) * LANES)
            q = q_ref[0, rows, cols]
            zero = jnp.zeros_like(q)
            keys = [r[0, :, cols] for r in k_refs[j:j + 3]] + [kx_ref[0, :, cols]]
            vals = [r[0, :, cols] for r in v_refs[j:j + 3]] + [vx_ref[0, :, cols]]
            probs, inv = [], []
            for hh in range(2):
                qh = jnp.where(head0, q, zero) if hh == 0 else jnp.where(head0, zero, q)
                s = [_dot_nt(qh, keys[t]) + bias_ref[0, 2 * pi + hh, :, t * NA_TOK:(t + 1) * NA_TOK] for t in range(3)]
                s.append(_dot_nt(qh, keys[3]))
                m = jnp.max(_fold_lanes(s, jnp.maximum), axis=-1, keepdims=True)
                p = [jnp.exp2(st - m) for st in s]
                inv.append(1.0 / jnp.sum(_fold_lanes(p, jnp.add), axis=-1, keepdims=True))
                probs += [pt.astype(BF16) for pt in p]
            v_cat = jnp.concatenate([jnp.where(head0, v, zero) for v in vals]
                                    + [jnp.where(head0, zero, v) for v in vals], axis=0)
            o = _dot(jnp.concatenate(probs, axis=1), v_cat) * jnp.where(head0, inv[0], inv[1])
            o_ref[0, rows, cols] = (o * _silu(g_ref[0, rows, cols].astype(F32))).astype(o_ref.dtype)


def _neighbourhood(p, pc, bias, layer):
    b, n, _ = p.shape
    nblk = n // NA_TOK
    nstep = nblk // NA_STEP
    grp = NA_W // LANES

    def key_blk(cb, shift):
        return pl.BlockSpec((1, NA_TOK, NA_W),
                            lambda bi, i: (bi, jnp.clip(NA_STEP * i + shift, 0, nblk - 1), cb // grp))

    def step_blk(cb):
        return pl.BlockSpec((1, NA_STEP * NA_TOK, NA_W), lambda bi, i: (bi, i, cb // grp))

    lc = pc.shape[1]

    def ctx(cb):
        return pl.BlockSpec((1, lc, NA_W), lambda bi, i: (bi, 0, cb // grp))

    def bias_blk(j):
        def index(bi, i):
            blk = NA_STEP * i + j
            return (layer, jnp.where(blk == 0, 0, jnp.where(blk == nblk - 1, 2, 1)), 0, 0, 0)
        return pl.BlockSpec((None, 1, NA_HEADS, NA_TOK, 3 * NA_TOK), index)

    shifts = range(-1, NA_STEP + 1)
    return pl.pallas_call(
        _na_kernel,
        grid=(b, nstep),
        in_specs=([step_blk(CB_NQ)] + [key_blk(CB_NK, s) for s in shifts] + [key_blk(CB_NV, s) for s in shifts]
                  + [ctx(CB_NK), ctx(CB_NV), step_blk(CB_NG)] + [bias_blk(j) for j in range(NA_STEP)]),
        out_specs=pl.BlockSpec((1, NA_STEP * NA_TOK, NA_W), lambda bi, i: (bi, i, 0)),
        out_shape=jax.ShapeDtypeStruct((b, n, NA_W), BF16),
        compiler_params=pltpu.CompilerParams(dimension_semantics=("parallel", "arbitrary")),
        name="neighbourhood",
    )(p, *([p] * (2 * len(shifts))), pc, pc, p, *([bias] * NA_STEP))


def _out_kernel(x_ref, mod_ref, u_ref, gz_ref, ub_ref, ua_ref,
                yr_ref, yn_ref, cw_ref, cbias_ref, w_ref, lg_ref, lb_ref, o_ref, *, mod_row):
    i = pl.program_id(1)
    last = pl.num_programs(1) - 1
    u = u_ref[0].astype(F32)
    tm = u.shape[0]
    halo = ub_ref.shape[1]
    u_before = jnp.where(i == 0, 0.0, ub_ref[0, halo - 1:halo, :].astype(F32))
    u_after = jnp.where(i == last, 0.0, ua_ref[0, 0:1, :].astype(F32))
    row = lax.broadcasted_iota(jnp.int32, u.shape, 0)
    u_prev = jnp.where(row == 0, u_before, pltpu.roll(u, 1, axis=0))
    u_next = jnp.where(row == tm - 1, u_after, pltpu.roll(u, tm - 1, axis=0))
    cw = cw_ref[...]
    row = pl.program_id(0) if mod_row is None else mod_row
    g = mod_ref[pl.ds(row, 1), 2 * D_MODEL:] * (1.0 / DEEPNORM_ALPHA)
    w = w_ref[...].astype(BF16)
    sub = min(OUT_SUB, tm)
    for r in range(tm // sub):
        rs = slice(r * sub, (r + 1) * sub)
        conv = u_prev[rs] * cw[0:1] + u[rs] * cw[1:2] + u_next[rs] * cw[2:3] + cbias_ref[...]
        y_conv = gz_ref[0, rs, :].astype(F32) * conv
        y = jnp.concatenate([y_conv.astype(BF16), yr_ref[0, rs, :], yn_ref[0, rs, :]], axis=-1)
        z = x_ref[0, rs, :] + g * _dot(y, w)
        mu = jnp.mean(z, axis=-1, keepdims=True)
        dlt = z - mu
        var = jnp.mean(dlt * dlt, axis=-1, keepdims=True)
        o_ref[0, rs, :] = dlt * lax.rsqrt(var + LN_EPS / DEEPNORM_ALPHA ** 2) * lg_ref[...] + lb_ref[...]


def _output(x, mod, mod_row, p, y_ret, y_na, conv_w, conv_b, w_out, layer, ln_g, ln_b):
    b, n, _ = x.shape
    tm = min(OUT_TILE, n)
    halo = 16
    per = tm // halo
    nh = n // halo

    def conv(cb):
        return pl.BlockSpec((1, tm, CONV_W), lambda bi, i: (bi, i, cb))

    def before(cb):
        return pl.BlockSpec((1, halo, CONV_W), lambda bi, i: (bi, jnp.maximum(i * per - 1, 0), cb))

    def after(cb):
        return pl.BlockSpec((1, halo, CONV_W), lambda bi, i: (bi, jnp.minimum((i + 1) * per, nh - 1), cb))

    def const(shape):
        return pl.BlockSpec(shape, lambda bi, i: (0,) * len(shape))

    return pl.pallas_call(
        functools.partial(_out_kernel, mod_row=mod_row),
        grid=(b, n // tm),
        in_specs=[pl.BlockSpec((1, tm, D_MODEL), lambda bi, i: (bi, i, 0)),
                  pl.BlockSpec((None, 8, 3 * D_MODEL), lambda bi, i: (layer, 0, 0)),
                  conv(CB_U), conv(CB_GZ), before(CB_U), after(CB_U),
                  pl.BlockSpec((1, tm, RET_W), lambda bi, i: (bi, i, 0)),
                  pl.BlockSpec((1, tm, NA_W), lambda bi, i: (bi, i, 0)),
                  const((3, CONV_W)), const((1, CONV_W)),
                  pl.BlockSpec((None, MIX_W, D_MODEL), lambda bi, i: (layer, 0, 0), pipeline_mode=pl.Buffered(1)),
                  const((1, D_MODEL)), const((1, D_MODEL))],
        out_specs=pl.BlockSpec((1, tm, D_MODEL), lambda bi, i: (bi, i, 0)),
        out_shape=jax.ShapeDtypeStruct((b, n, D_MODEL), F32),
        compiler_params=pltpu.CompilerParams(dimension_semantics=("parallel", "parallel")),
        name="output",
    )(x, mod, p, p, p, p, y_ret, y_na, conv_w, conv_b.reshape(1, CONV_W), w_out,
      ln_g.reshape(1, D_MODEL), ln_b.reshape(1, D_MODEL))


def kernel(x, c, ctx, c_ctx, w_mod, b_mod, w_in, conv_w, conv_b, ret_decay, na_rpb, w_out, ln_g, ln_b):
    b, n, d = x.shape
    act_t = jnp.pad(jnp.concatenate([c, c_ctx[None]], axis=0).T, ((0, 0), (0, 5)))
    mod = _modulation(act_t, w_mod, b_mod)
    tables = _rope_tables(n)
    na_bias = _na_bias_tables(na_rpb, n // GRID_W)
    xc = ctx
    for l in range(DEPTH):
        need_ctx = l < DEPTH - 1
        p, pc = _projection(x, xc, mod, w_in, l, tables, PROJ_W if need_ctx else KV_W)
        y_ret = _retention(p, pc, ret_decay[l], init_state=True)
        y_na = _neighbourhood(p, pc, na_bias, l)
        x_new = _output(x, mod, None, p, y_ret, y_na, conv_w[l], conv_b[l], w_out, l, ln_g[l], ln_b[l])
        if need_ctx:
            yc_ret, yc_na = _retention(pc, None, ret_decay[l], init_state=False, attend=True)
            xc = _output(xc, mod, b, pc, yc_ret, yc_na, conv_w[l], conv_b[l], w_out, l, ln_g[l], ln_b[l])
        x = x_new
    return x
```

```python
import functools

import numpy as np
import jax
import jax.numpy as jnp
from jax import lax
from jax.experimental import pallas as pl
from jax.experimental.pallas import tpu as pltpu

D_MODEL = 1024
DEPTH = 2
GRID_W = 64
HEAD_DIM = 64
CONV_W = 256
RET_HEADS = 6
RET_W = RET_HEADS * HEAD_DIM
NA_HEADS = 6
NA_W = NA_HEADS * HEAD_DIM
MIX_W = CONV_W + RET_W + NA_W
RET_CHUNK = 128
WIN_H = 8
WIN_W = 16
ROPE_BASE = 10000.0
LN_EPS = 1e-5
DEEPNORM_ALPHA = (2 * DEPTH) ** 0.25
PROJ_SPLITS = (RET_W, RET_W, NA_W, NA_W, RET_W, RET_W, NA_W, NA_W, CONV_W, CONV_W, CONV_W, CONV_W)
PROJ_W = sum(PROJ_SPLITS)
KV_W = 2 * RET_W + 2 * NA_W

LANES = 128
PAIRS = RET_HEADS // 2
CB_RK, CB_RV, CB_NK, CB_NV, CB_RQ, CB_RG, CB_NQ, CB_NG = 0, 3, 6, 9, 12, 15, 18, 21
CONV_OFF = PROJ_W - 4 * CONV_W
P_W = CONV_OFF + 2 * CONV_W
CB_U, CB_GZ = CONV_OFF // CONV_W, CONV_OFF // CONV_W + 1
QK_SCALE = HEAD_DIM ** -0.5
NEG = -1e30
LOG2E = 1.4426950408889634
NA_ROWS = 4
NA_TOK = NA_ROWS * GRID_W
NA_STEP = 2
RET_GROUP = 64
ROW_TILE = 1024
OUT_TILE = 2048
OUT_SUB = 256
PROJ_CHUNK = 512

F32 = jnp.float32
BF16 = jnp.bfloat16


def _silu(v):
    return v * jax.nn.sigmoid(v)


def _dot(a, b):
    return jnp.dot(a, b, preferred_element_type=F32)


def _dot_nt(a, b):
    return lax.dot_general(a, b, (((1,), (1,)), ((), ())), preferred_element_type=F32)


def _dot_tn(a, b):
    return lax.dot_general(a, b, (((0,), (0,)), ((), ())), preferred_element_type=F32)


def _lane_is_head0(shape):
    return lax.broadcasted_iota(jnp.int32, shape, len(shape) - 1) < HEAD_DIM


def _mod_kernel(act_ref, w_ref, b_ref, o_ref):
    a = _silu(act_ref[...])
    w = w_ref[0]
    bias = b_ref[0]
    for r in range(3):
        o_ref[0, r:r + 1, :] = jnp.sum(a[:, r:r + 1] * w, axis=0, keepdims=True) + bias
    o_ref[0, 3:8, :] = jnp.zeros((5, w.shape[1]), F32)


def _modulation(act_t, w_mod, b_mod):
    tn = 1536
    n = w_mod.shape[-1]
    return pl.pallas_call(
        _mod_kernel,
        grid=(DEPTH, n // tn),
        in_specs=[pl.BlockSpec((D_MODEL, 8), lambda l, j: (0, 0)),
                  pl.BlockSpec((1, D_MODEL, tn), lambda l, j: (l, 0, j)),
                  pl.BlockSpec((1, 1, tn), lambda l, j: (l, 0, j))],
        out_specs=pl.BlockSpec((1, 8, tn), lambda l, j: (l, 0, j)),
        out_shape=jax.ShapeDtypeStruct((DEPTH, 8, n), F32),
        name="modulation",
    )(act_t, w_mod, b_mod.reshape(DEPTH, 1, n))


def _rope_tables(n):
    rows = n // GRID_W
    nf = HEAD_DIM // 4
    inv = ROPE_BASE ** (-jnp.arange(nf, dtype=F32) / nf)
    lane = np.arange(LANES)
    by_row = ((lane % HEAD_DIM) < HEAD_DIM // 2)[None]
    first = ((lane % (2 * nf)) < nf)[None]
    ang_r = jnp.tile(jnp.arange(rows).astype(F32)[:, None] * inv, (1, LANES // nf))
    ang_c = jnp.tile(jnp.arange(GRID_W).astype(F32)[:, None] * inv, (1, LANES // nf))
    parts = []
    for ang, own in ((ang_r, by_row), (ang_c, ~by_row)):
        cos, sin = jnp.cos(ang), jnp.sin(ang)
        parts.append((jnp.where(own, cos, 0.0), jnp.where(own & first, -sin, 0.0), jnp.where(own & ~first, sin, 0.0)))
    return parts[0] + parts[1]


def _project(x, mod_row, w_ref, tables, o_ref, n_cols):
    shift, scale = mod_row[:, :D_MODEL], mod_row[:, D_MODEL:2 * D_MODEL]
    h = (x * (1.0 + scale) + shift).astype(BF16)
    rotated = tuple(range(CB_RK, CB_RK + PAIRS)) + tuple(range(CB_RQ, CB_RQ + PAIRS))
    col_scale = {cb: QK_SCALE for cb in range(CB_RK, CB_RK + PAIRS)}
    col_scale.update({cb: QK_SCALE * LOG2E for cb in range(CB_NQ, CB_NQ + PAIRS)})
    for off in range(0, min(n_cols, CONV_OFF), PROJ_CHUNK):
        acc = _dot(h, w_ref[:, off:off + PROJ_CHUNK].astype(BF16))
        tiles = []
        for j in range(PROJ_CHUNK // LANES):
            cb = off // LANES + j
            v = acc[:, j * LANES:(j + 1) * LANES]
            if tables is not None and cb in rotated:
                cos, up, dn = tables
                v = v * cos + pltpu.roll(v, LANES - 16, axis=1) * up + pltpu.roll(v, 16, axis=1) * dn
            if cb in col_scale:
                v = v * col_scale[cb]
            tiles.append(v.astype(BF16))
        o_ref[0, :, off:off + PROJ_CHUNK] = jnp.concatenate(tiles, axis=1)
    if n_cols > CONV_OFF:
        ch, cb, cc, cz = (_dot(h, w_ref[:, CONV_OFF + j * CONV_W:CONV_OFF + (j + 1) * CONV_W].astype(BF16))
                          for j in range(4))
        o_ref[0, :, CONV_OFF:CONV_OFF + CONV_W] = (cc * ch).astype(BF16)
        o_ref[0, :, CONV_OFF + CONV_W:CONV_OFF + 2 * CONV_W] = (cb * _silu(cz)).astype(BF16)


def _proj_kernel(x_ref, xc_ref, mod_ref, w_ref, *rest, ctx_row, n_cols_ctx):
    *table_refs, o_ref, oc_ref = rest
    tables = tuple(jnp.concatenate([r_ref[g:g + 1, :] + c_ref[...] for g in range(r_ref.shape[0])], axis=0)
                   for r_ref, c_ref in zip(table_refs[:3], table_refs[3:]))
    _project(x_ref[0], mod_ref[pl.ds(pl.program_id(0), 1), :], w_ref, tables, o_ref, PROJ_W)

    @pl.when(pl.program_id(1) == 0)
    def _():
        _project(xc_ref[0], mod_ref[ctx_row:ctx_row + 1, :], w_ref, None, oc_ref, n_cols_ctx)


def _projection(x, xc, mod, w_in, layer, tables, n_cols_ctx):
    b, n, _ = x.shape
    lc = xc.shape[1]
    tm = min(ROW_TILE, n)
    pc_w = P_W if n_cols_ctx > CONV_OFF else n_cols_ctx
    in_specs = ([pl.BlockSpec((1, tm, D_MODEL), lambda bi, i: (bi, i, 0)),
                 pl.BlockSpec((1, lc, D_MODEL), lambda bi, i: (bi, 0, 0)),
                 pl.BlockSpec((None, 8, 3 * D_MODEL), lambda bi, i: (layer, 0, 0)),
                 pl.BlockSpec((None, D_MODEL, PROJ_W), lambda bi, i: (layer, 0, 0), pipeline_mode=pl.Buffered(1))]
                + [pl.BlockSpec((tm // GRID_W, LANES), lambda bi, i: (i, 0))] * 3
                + [pl.BlockSpec((GRID_W, LANES), lambda bi, i: (0, 0))] * 3)
    return pl.pallas_call(
        functools.partial(_proj_kernel, ctx_row=b, n_cols_ctx=n_cols_ctx),
        grid=(b, n // tm),
        in_specs=in_specs,
        out_specs=[pl.BlockSpec((1, tm, P_W), lambda bi, i: (bi, i, 0)),
                   pl.BlockSpec((1, lc, pc_w), lambda bi, i: (bi, 0, 0))],
        out_shape=[jax.ShapeDtypeStruct((b, n, P_W), BF16), jax.ShapeDtypeStruct((b, lc, pc_w), BF16)],
        compiler_params=pltpu.CompilerParams(dimension_semantics=("parallel", "arbitrary")),
        name="projection",
    )(x, xc, mod, w_in, *tables)


def _log_sigmoid(v):
    return jnp.minimum(v, 0.0) - jnp.log1p(jnp.exp(-jnp.abs(v)))


def _block_diag(m):
    r = lax.broadcasted_iota(jnp.int32, m.shape, 0) < HEAD_DIM
    c = lax.broadcasted_iota(jnp.int32, m.shape, 1) < HEAD_DIM
    return jnp.where(r == c, m, 0.0)


def _head_mean(x):
    head0 = _lane_is_head0(x.shape)
    s0 = jnp.sum(jnp.where(head0, x, 0.0), axis=-1, keepdims=True)
    s1 = jnp.sum(jnp.where(head0, 0.0, x), axis=-1, keepdims=True)
    return jnp.where(head0, s0, s1) * (1.0 / HEAD_DIM)


def _attend(q_ref, k_ref, v_ref, g_ref, o_ref):
    q, k, v = q_ref[0], k_ref[0], v_ref[0]
    head0 = _lane_is_head0(q.shape)
    zero = jnp.zeros_like(q)
    outs = []
    for hh in range(2):
        qh = jnp.where(head0, q, zero) if hh == 0 else jnp.where(head0, zero, q)
        s = _dot_nt(qh, k)
        p = jnp.exp2(s - jnp.max(s, axis=-1, keepdims=True))
        outs.append(_dot(p.astype(BF16), v) / jnp.sum(p, axis=-1, keepdims=True))
    o = jnp.where(head0, outs[0], outs[1])
    o_ref[0] = (o * _silu(g_ref[0].astype(F32))).astype(o_ref.dtype)


def _ret_kernel(*refs, init_state, attend, group):
    refs = list(refs)
    dec_ref, q_ref, k_ref, v_ref, g_ref = refs[:5]
    refs = refs[5:]
    if init_state:
        kc_ref, vc_ref = refs[:2]
        refs = refs[2:]
    if attend:
        attn_refs, refs = refs[:4], refs[4:]
        o_ref, ao_ref, s_ref, u_ref, kt_ref, wq_ref, wkt_ref, d_ref = refs
        _attend(*attn_refs, ao_ref)
    else:
        o_ref, s_ref, u_ref, kt_ref, wq_ref, wkt_ref, d_ref = refs
    c = RET_CHUNK
    n_chunks = q_ref.shape[1] // c
    n_groups = n_chunks // group
    head0 = _lane_is_head0((c, LANES))
    lg_f = _log_sigmoid(dec_ref[0, 0])
    lg_b = _log_sigmoid(dec_ref[1, 0])
    i = lax.broadcasted_iota(jnp.int32, (c, LANES), 0).astype(F32)
    wq_ref[:, :LANES] = jnp.exp((i + 1.0) * lg_f)
    wq_ref[:, LANES:] = jnp.exp((c - i) * lg_b)
    wkt_ref[:LANES, :] = jnp.exp((c - 1.0 - i) * lg_f).T
    wkt_ref[LANES:, :] = jnp.exp(i * lg_b).T
    gc_f = jnp.exp(float(c) * lg_f)
    gc_b = jnp.exp(float(c) * lg_b)
    diff = (lax.broadcasted_iota(jnp.int32, (c, c), 0) - lax.broadcasted_iota(jnp.int32, (c, c), 1)).astype(F32)
    lower = diff >= 0
    for hh in range(2):
        lf = lg_f[:, hh * HEAD_DIM:hh * HEAD_DIM + 1]
        lb = lg_b[:, hh * HEAD_DIM:hh * HEAD_DIM + 1]
        d_ref[:, hh * c:(hh + 1) * c] = jnp.where(lower, jnp.exp(jnp.where(lower, diff, 0.0) * lf),
                                                  jnp.exp(jnp.where(lower, 0.0, -diff) * lb))
    if init_state:
        lc = kc_ref.shape[1]
        m = lax.broadcasted_iota(jnp.int32, (lc, LANES), 0).astype(F32)
        kcf = kc_ref[0].astype(F32)
        s_f0 = _block_diag(_dot_tn((kcf * jnp.exp((lc - 1.0 - m) * lg_f)).astype(BF16), vc_ref[0]))
        s_b0 = _block_diag(_dot_tn((kcf * jnp.exp(m * lg_b)).astype(BF16), vc_ref[0]))
    else:
        s_f0 = s_b0 = jnp.zeros((LANES, LANES), F32)
    r2 = lax.broadcasted_iota(jnp.int32, (2 * LANES, LANES), 0) % LANES < HEAD_DIM
    c2 = lax.broadcasted_iota(jnp.int32, (2 * LANES, LANES), 1) < HEAD_DIM
    diag2 = r2 == c2

    chan0 = lax.broadcasted_iota(jnp.int32, (LANES, c), 0) < HEAD_DIM

    def chunk(n):
        return pl.ds(pl.multiple_of(n * c, c), c)

    def increments(gi, carry):
        for j in range(group):
            n = gi * group + j
            ktb = k_ref[0, chunk(n), :].T
            kt = ktb.astype(F32)
            zero = jnp.zeros_like(ktb)
            kt_ref[n] = jnp.concatenate([jnp.where(chan0, ktb, zero), jnp.where(chan0, zero, ktb)], axis=1)
            lhs = jnp.concatenate([kt * wkt_ref[:LANES, :], kt * wkt_ref[LANES:, :]], axis=0).astype(BF16)
            u_ref[n] = jnp.where(diag2, _dot(lhs, v_ref[0, chunk(n), :]), 0.0)
        return carry

    lax.fori_loop(0, n_groups, increments, 0)

    def fwd_scan(gi, s):
        for j in range(group):
            n = gi * group + j
            s_ref[n, :LANES, :] = s.astype(BF16)
            s = s * gc_f + u_ref[n, :LANES, :]
        return s

    def bwd_scan(gi, s):
        for j in range(group):
            n = n_chunks - 1 - (gi * group + j)
            s_ref[n, LANES:, :] = s.astype(BF16)
            s = s * gc_b + u_ref[n, LANES:, :]
        return s

    lax.fori_loop(0, n_groups, fwd_scan, s_f0)
    lax.fori_loop(0, n_groups, bwd_scan, s_b0)

    def body(gi, carry):
        outs = []
        for j in range(group):
            n = gi * group + j
            rows = chunk(n)
            q, v = q_ref[0, rows, :], v_ref[0, rows, :]
            zero = jnp.zeros_like(v)
            vcat = jnp.concatenate([jnp.where(head0, v, zero), jnp.where(head0, zero, v)], axis=0)
            scores = _dot(q, kt_ref[n]) * d_ref[...]
            qf = q.astype(F32)
            qw = jnp.concatenate([qf * wq_ref[:, :LANES], qf * wq_ref[:, LANES:]], axis=1).astype(BF16)
            outs.append(_dot(scores.astype(BF16), vcat) + _dot(qw, s_ref[n]))
        o = jnp.concatenate(outs, axis=0)
        rows = pl.ds(pl.multiple_of(gi * (group * c), group * c), group * c)
        dlt = o - _head_mean(o)
        var = _head_mean(dlt * dlt)
        y = dlt * lax.rsqrt(var + LN_EPS) * _silu(g_ref[0, rows, :].astype(F32))
        o_ref[0, rows, :] = y.astype(o_ref.dtype)
        return carry

    lax.fori_loop(0, n_groups, body, 0)


def _retention(p, pc, ret_decay, init_state, attend=False):
    b, n, _ = p.shape
    dec = jnp.repeat(ret_decay.astype(F32), HEAD_DIM, axis=-1).reshape(2, PAIRS, 1, LANES)

    def col(cb):
        return pl.BlockSpec((1, n, LANES), lambda bi, pi: (bi, 0, cb + pi))

    in_specs = [pl.BlockSpec((2, 1, 1, LANES), lambda bi, pi: (0, pi, 0, 0)),
                col(CB_RQ), col(CB_RK), col(CB_RV), col(CB_RG)]
    args = [dec, p, p, p, p]
    if init_state:
        lc = pc.shape[1]
        in_specs += [pl.BlockSpec((1, lc, LANES), lambda bi, pi: (bi, 0, CB_RK + pi)),
                     pl.BlockSpec((1, lc, LANES), lambda bi, pi: (bi, 0, CB_RV + pi))]
        args += [pc, pc]
    out_specs, out_shape = col(0), jax.ShapeDtypeStruct((b, n, RET_W), BF16)
    if attend:
        in_specs += [col(CB_NQ), col(CB_NK), col(CB_NV), col(CB_NG)]
        args += [p, p, p, p]
        out_specs, out_shape = [out_specs, col(0)], [out_shape, jax.ShapeDtypeStruct((b, n, NA_W), BF16)]
    c = RET_CHUNK
    return pl.pallas_call(
        functools.partial(_ret_kernel, init_state=init_state, attend=attend, group=min(RET_GROUP, n // c)),
        grid=(b, PAIRS),
        in_specs=in_specs,
        out_specs=out_specs,
        out_shape=out_shape,
        scratch_shapes=[pltpu.VMEM((n // c, 2 * LANES, LANES), BF16),
                        pltpu.VMEM((n // c, 2 * LANES, LANES), F32),
                        pltpu.VMEM((n // c, LANES, 2 * c), BF16),
                        pltpu.VMEM((c, 2 * LANES), F32),
                        pltpu.VMEM((2 * LANES, c), F32),
                        pltpu.VMEM((c, 2 * c), F32)],
        compiler_params=pltpu.CompilerParams(dimension_semantics=("parallel", "parallel")),
        name="retention",
    )(*args)


def _na_bias_plan(rows):
    nblk = rows // NA_ROWS
    win_h = min(WIN_H, rows)
    plan = np.full((3, NA_ROWS, 3, NA_ROWS), -1, np.int64)
    for kind, blk in enumerate((0, 1, nblk - 1)):
        for qr in range(NA_ROWS):
            r = blk * NA_ROWS + qr
            rs = min(max(r - win_h // 2, 0), rows - win_h)
            for kb in range(3):
                if not 0 <= blk + kb - 1 <= nblk - 1:
                    continue
                for krl in range(NA_ROWS):
                    kr = (blk + kb - 1) * NA_ROWS + krl
                    if rs <= kr < rs + win_h:
                        plan[kind, qr, kb, krl] = kr - r + WIN_H - 1
    return plan


def _fill_na_bias(rpb_ref, bias_ref, t_ref, plan):
    n_dr = 2 * WIN_H - 1
    qc = lax.broadcasted_iota(jnp.int32, (GRID_W, LANES), 0)
    lane = lax.broadcasted_iota(jnp.int32, (GRID_W, LANES), 1)
    kc = lane % GRID_W
    cs = jnp.clip(qc - WIN_W // 2, 0, GRID_W - WIN_W)
    col_ok = (kc >= cs) & (kc < cs + WIN_W)
    first = lane < GRID_W
    neg = jnp.full((GRID_W, LANES), NEG, F32)
    for h in range(NA_HEADS):
        for dr in range(n_dr):
            row = jnp.broadcast_to(rpb_ref[h, dr:dr + 1, :] * LOG2E, (GRID_W, LANES))
            lo = pltpu.roll(row, LANES - (WIN_W - 1), 1, stride=1, stride_axis=0)
            hi = pltpu.roll(row, GRID_W - (WIN_W - 1), 1, stride=1, stride_axis=0)
            t_ref[dr] = jnp.where(col_ok, jnp.where(first, lo, hi), neg)
        for kind in range(3):
            for qr in range(NA_ROWS):
                for kb in range(3):
                    for pr in range(NA_ROWS // 2):
                        ia, ib = (int(plan[kind, qr, kb, 2 * pr + s]) for s in range(2))
                        a = t_ref[ia] if ia >= 0 else neg
                        b = t_ref[ib] if ib >= 0 else neg
                        col0 = kb * NA_TOK + pr * LANES
                        bias_ref[kind, h, qr * GRID_W:(qr + 1) * GRID_W, col0:col0 + LANES] = jnp.where(first, a, b)


def _fold_lanes(blocks, op):
    tiles = [blk[:, j:j + LANES] for blk in blocks for j in range(0, blk.shape[1], LANES)]
    acc = tiles[0]
    for t in tiles[1:]:
        acc = op(acc, t)
    return acc


def _na_kernel(*refs, plan):
    nkb = NA_STEP + 2
    q_ref, k_refs, v_refs = refs[0], refs[1:1 + nkb], refs[1 + nkb:1 + 2 * nkb]
    kx_ref, vx_ref, g_ref, rpb_ref, o_ref, bias_ref, t_ref = refs[1 + 2 * nkb:]
    i, last = pl.program_id(1), pl.num_programs(1) - 1

    @pl.when((pl.program_id(0) == 0) & (i == 0))
    def _():
        _fill_na_bias(rpb_ref, bias_ref, t_ref, plan)

    head0 = _lane_is_head0((NA_TOK, LANES))
    for j in range(NA_STEP):
        edge = 0 if j == 0 else 2 if j == NA_STEP - 1 else None
        kind = 1 if edge is None else jnp.where(i == (0 if edge == 0 else last), edge, 1)
        rows = slice(j * NA_TOK, (j + 1) * NA_TOK)
        for pi in range(PAIRS):
            cols = slice(pi * LANES, (pi + 1) * LANES)
            q = q_ref[0, rows, cols]
            zero = jnp.zeros_like(q)
            keys = [r[0, :, cols] for r in k_refs[j:j + 3]] + [kx_ref[0, :, cols]]
            vals = [r[0, :, cols] for r in v_refs[j:j + 3]] + [vx_ref[0, :, cols]]
            probs, inv = [], []
            for hh in range(2):
                qh = jnp.where(head0, q, zero) if hh == 0 else jnp.where(head0, zero, q)
                s = [_dot_nt(qh, keys[t]) + bias_ref[kind, 2 * pi + hh, :, t * NA_TOK:(t + 1) * NA_TOK] for t in range(3)]
                s.append(_dot_nt(qh, keys[3]))
                m = jnp.max(_fold_lanes(s, jnp.maximum), axis=-1, keepdims=True)
                p = [jnp.exp2(st - m) for st in s]
                inv.append(1.0 / jnp.sum(_fold_lanes(p, jnp.add), axis=-1, keepdims=True))
                probs += [pt.astype(BF16) for pt in p]
            v_cat = jnp.concatenate([jnp.where(head0, v, zero) for v in vals]
                                    + [jnp.where(head0, zero, v) for v in vals], axis=0)
            o = _dot(jnp.concatenate(probs, axis=1), v_cat) * jnp.where(head0, inv[0], inv[1])
            o_ref[0, rows, cols] = (o * _silu(g_ref[0, rows, cols].astype(F32))).astype(o_ref.dtype)


def _neighbourhood(p, pc, rpb):
    b, n, _ = p.shape
    nblk = n // NA_TOK
    nstep = nblk // NA_STEP
    assert NA_STEP >= 2 and nstep >= 2
    grp = NA_W // LANES

    def key_blk(cb, shift):
        return pl.BlockSpec((1, NA_TOK, NA_W),
                            lambda bi, i: (bi, jnp.clip(NA_STEP * i + shift, 0, nblk - 1), cb // grp))

    def step_blk(cb):
        return pl.BlockSpec((1, NA_STEP * NA_TOK, NA_W), lambda bi, i: (bi, i, cb // grp))

    lc = pc.shape[1]

    def ctx(cb):
        return pl.BlockSpec((1, lc, NA_W), lambda bi, i: (bi, 0, cb // grp))

    n_dr, n_dc = 2 * WIN_H - 1, 2 * WIN_W - 1
    rpb_rows = jnp.pad(rpb.astype(F32), ((0, 0), (0, 16 - n_dr), (0, LANES - n_dc)))
    shifts = range(-1, NA_STEP + 1)
    return pl.pallas_call(
        functools.partial(_na_kernel, plan=_na_bias_plan(n // GRID_W)),
        grid=(b, nstep),
        in_specs=([step_blk(CB_NQ)] + [key_blk(CB_NK, s) for s in shifts] + [key_blk(CB_NV, s) for s in shifts]
                  + [ctx(CB_NK), ctx(CB_NV), step_blk(CB_NG),
                     pl.BlockSpec((NA_HEADS, 16, LANES), lambda bi, i: (0, 0, 0))]),
        out_specs=pl.BlockSpec((1, NA_STEP * NA_TOK, NA_W), lambda bi, i: (bi, i, 0)),
        out_shape=jax.ShapeDtypeStruct((b, n, NA_W), BF16),
        scratch_shapes=[pltpu.VMEM((3, NA_HEADS, NA_TOK, 3 * NA_TOK), F32),
                        pltpu.VMEM((n_dr, GRID_W, LANES), F32)],
        compiler_params=pltpu.CompilerParams(dimension_semantics=("arbitrary", "arbitrary")),
        name="neighbourhood",
    )(p, *([p] * (2 * len(shifts))), pc, pc, p, rpb_rows)


def _out_kernel(x_ref, mod_ref, u_ref, gz_ref, ub_ref, ua_ref,
                yr_ref, yn_ref, cw_ref, cbias_ref, w_ref, lg_ref, lb_ref, o_ref, *, mod_row):
    i = pl.program_id(1)
    last = pl.num_programs(1) - 1
    u = u_ref[0].astype(F32)
    tm = u.shape[0]
    halo = ub_ref.shape[1]
    u_before = jnp.where(i == 0, 0.0, ub_ref[0, halo - 1:halo, :].astype(F32))
    u_after = jnp.where(i == last, 0.0, ua_ref[0, 0:1, :].astype(F32))
    row = lax.broadcasted_iota(jnp.int32, u.shape, 0)
    u_prev = jnp.where(row == 0, u_before, pltpu.roll(u, 1, axis=0))
    u_next = jnp.where(row == tm - 1, u_after, pltpu.roll(u, tm - 1, axis=0))
    cw = cw_ref[...]
    row = pl.program_id(0) if mod_row is None else mod_row
    g = mod_ref[pl.ds(row, 1), 2 * D_MODEL:] * (1.0 / DEEPNORM_ALPHA)
    w = w_ref[...].astype(BF16)
    sub = min(OUT_SUB, tm)
    for r in range(tm // sub):
        rs = slice(r * sub, (r + 1) * sub)
        conv = u_prev[rs] * cw[0:1] + u[rs] * cw[1:2] + u_next[rs] * cw[2:3] + cbias_ref[...]
        y_conv = gz_ref[0, rs, :].astype(F32) * conv
        y = jnp.concatenate([y_conv.astype(BF16), yr_ref[0, rs, :], yn_ref[0, rs, :]], axis=-1)
        z = x_ref[0, rs, :] + g * _dot(y, w)
        mu = jnp.mean(z, axis=-1, keepdims=True)
        dlt = z - mu
        var = jnp.mean(dlt * dlt, axis=-1, keepdims=True)
        o_ref[0, rs, :] = dlt * lax.rsqrt(var + LN_EPS / DEEPNORM_ALPHA ** 2) * lg_ref[...] + lb_ref[...]


def _output(x, mod, mod_row, p, y_ret, y_na, conv_w, conv_b, w_out, layer, ln_g, ln_b):
    b, n, _ = x.shape
    tm = min(OUT_TILE, n)
    halo = 16
    per = tm // halo
    nh = n // halo

    def conv(cb):
        return pl.BlockSpec((1, tm, CONV_W), lambda bi, i: (bi, i, cb))

    def before(cb):
        return pl.BlockSpec((1, halo, CONV_W), lambda bi, i: (bi, jnp.maximum(i * per - 1, 0), cb))

    def after(cb):
        return pl.BlockSpec((1, halo, CONV_W), lambda bi, i: (bi, jnp.minimum((i + 1) * per, nh - 1), cb))

    def const(shape):
        return pl.BlockSpec(shape, lambda bi, i: (0,) * len(shape))

    return pl.pallas_call(
        functools.partial(_out_kernel, mod_row=mod_row),
        grid=(b, n // tm),
        in_specs=[pl.BlockSpec((1, tm, D_MODEL), lambda bi, i: (bi, i, 0)),
                  pl.BlockSpec((None, 8, 3 * D_MODEL), lambda bi, i: (layer, 0, 0)),
                  conv(CB_U), conv(CB_GZ), before(CB_U), after(CB_U),
                  pl.BlockSpec((1, tm, RET_W), lambda bi, i: (bi, i, 0)),
                  pl.BlockSpec((1, tm, NA_W), lambda bi, i: (bi, i, 0)),
                  const((3, CONV_W)), const((1, CONV_W)),
                  pl.BlockSpec((None, MIX_W, D_MODEL), lambda bi, i: (layer, 0, 0), pipeline_mode=pl.Buffered(1)),
                  const((1, D_MODEL)), const((1, D_MODEL))],
        out_specs=pl.BlockSpec((1, tm, D_MODEL), lambda bi, i: (bi, i, 0)),
        out_shape=jax.ShapeDtypeStruct((b, n, D_MODEL), F32),
        compiler_params=pltpu.CompilerParams(dimension_semantics=("parallel", "parallel")),
        name="output",
    )(x, mod, p, p, p, p, y_ret, y_na, conv_w, conv_b.reshape(1, CONV_W), w_out,
      ln_g.reshape(1, D_MODEL), ln_b.reshape(1, D_MODEL))


def kernel(x, c, ctx, c_ctx, w_mod, b_mod, w_in, conv_w, conv_b, ret_decay, na_rpb, w_out, ln_g, ln_b):
    b, n, d = x.shape
    act_t = jnp.pad(jnp.concatenate([c, c_ctx[None]], axis=0).T, ((0, 0), (0, 5)))
    mod = _modulation(act_t, w_mod, b_mod)
    tables = _rope_tables(n)
    xc = ctx
    for l in range(DEPTH):
        need_ctx = l < DEPTH - 1
        p, pc = _projection(x, xc, mod, w_in, l, tables, PROJ_W if need_ctx else KV_W)
        y_ret = _retention(p, pc, ret_decay[l], init_state=True)
        y_na = _neighbourhood(p, pc, na_rpb[l])
        x_new = _output(x, mod, None, p, y_ret, y_na, conv_w[l], conv_b[l], w_out, l, ln_g[l], ln_b[l])
        if need_ctx:
            yc_ret, yc_na = _retention(pc, None, ret_decay[l], init_state=False, attend=True)
            xc = _output(xc, mod, b, pc, yc_ret, yc_na, conv_w[l], conv_b[l], w_out, l, ln_g[l], ln_b[l])
        x = x_new
    return x
```

```python
import functools

import numpy as np
import jax
import jax.numpy as jnp
from jax import lax
from jax.experimental import pallas as pl
from jax.experimental.pallas import tpu as pltpu

D_MODEL = 1024
DEPTH = 2
GRID_W = 64
HEAD_DIM = 64
CONV_W = 256
RET_HEADS = 6
RET_W = RET_HEADS * HEAD_DIM
NA_HEADS = 6
NA_W = NA_HEADS * HEAD_DIM
MIX_W = CONV_W + RET_W + NA_W
RET_CHUNK = 128
WIN_H = 8
WIN_W = 16
ROPE_BASE = 10000.0
LN_EPS = 1e-5
DEEPNORM_ALPHA = (2 * DEPTH) ** 0.25
PROJ_SPLITS = (RET_W, RET_W, NA_W, NA_W, RET_W, RET_W, NA_W, NA_W, CONV_W, CONV_W, CONV_W, CONV_W)
PROJ_W = sum(PROJ_SPLITS)
KV_W = 2 * RET_W + 2 * NA_W

LANES = 128
PAIRS = RET_HEADS // 2
CB_RK, CB_RV, CB_NK, CB_NV, CB_RQ, CB_RG, CB_NQ, CB_NG = 0, 3, 6, 9, 12, 15, 18, 21
CONV_OFF = PROJ_W - 4 * CONV_W
P_W = CONV_OFF + 2 * CONV_W
CB_U, CB_GZ = CONV_OFF // CONV_W, CONV_OFF // CONV_W + 1
QK_SCALE = HEAD_DIM ** -0.5
NEG = -1e30
LOG2E = 1.4426950408889634
NA_ROWS = 4
NA_TOK = NA_ROWS * GRID_W
NA_STEP = 4
RET_GROUP = 64
ROW_TILE = 1024
OUT_TILE = 2048
OUT_SUB = 256
PROJ_CHUNK = 512

F32 = jnp.float32
BF16 = jnp.bfloat16


def _silu(v):
    return v * jax.nn.sigmoid(v)


def _dot(a, b):
    return jnp.dot(a, b, preferred_element_type=F32)


def _dot_nt(a, b):
    return lax.dot_general(a, b, (((1,), (1,)), ((), ())), preferred_element_type=F32)


def _dot_tn(a, b):
    return lax.dot_general(a, b, (((0,), (0,)), ((), ())), preferred_element_type=F32)


def _lane_is_head0(shape):
    return lax.broadcasted_iota(jnp.int32, shape, len(shape) - 1) < HEAD_DIM


def _mod_kernel(act_ref, w_ref, b_ref, o_ref):
    a = _silu(act_ref[...])
    w = w_ref[0]
    bias = b_ref[0]
    for r in range(3):
        o_ref[0, r:r + 1, :] = jnp.sum(a[:, r:r + 1] * w, axis=0, keepdims=True) + bias
    o_ref[0, 3:8, :] = jnp.zeros((5, w.shape[1]), F32)


def _modulation(act_t, w_mod, b_mod):
    tn = 1536
    n = w_mod.shape[-1]
    return pl.pallas_call(
        _mod_kernel,
        grid=(DEPTH, n // tn),
        in_specs=[pl.BlockSpec((D_MODEL, 8), lambda l, j: (0, 0)),
                  pl.BlockSpec((1, D_MODEL, tn), lambda l, j: (l, 0, j)),
                  pl.BlockSpec((1, 1, tn), lambda l, j: (l, 0, j))],
        out_specs=pl.BlockSpec((1, 8, tn), lambda l, j: (l, 0, j)),
        out_shape=jax.ShapeDtypeStruct((DEPTH, 8, n), F32),
        name="modulation",
    )(act_t, w_mod, b_mod.reshape(DEPTH, 1, n))


def _rope_tables(n):
    rows = n // GRID_W
    nf = HEAD_DIM // 4
    inv = ROPE_BASE ** (-jnp.arange(nf, dtype=F32) / nf)
    lane = np.arange(LANES)
    by_row = ((lane % HEAD_DIM) < HEAD_DIM // 2)[None]
    first = ((lane % (2 * nf)) < nf)[None]
    ang_r = jnp.tile(jnp.arange(rows).astype(F32)[:, None] * inv, (1, LANES // nf))
    ang_c = jnp.tile(jnp.arange(GRID_W).astype(F32)[:, None] * inv, (1, LANES // nf))
    parts = []
    for ang, own in ((ang_r, by_row), (ang_c, ~by_row)):
        cos, sin = jnp.cos(ang), jnp.sin(ang)
        parts.append((jnp.where(own, cos, 0.0), jnp.where(own & first, -sin, 0.0), jnp.where(own & ~first, sin, 0.0)))
    return parts[0] + parts[1]


def _project(x, mod_row, w_ref, tables, o_ref, n_cols):
    shift, scale = mod_row[:, :D_MODEL], mod_row[:, D_MODEL:2 * D_MODEL]
    h = (x * (1.0 + scale) + shift).astype(BF16)
    rotated = tuple(range(CB_RK, CB_RK + PAIRS)) + tuple(range(CB_RQ, CB_RQ + PAIRS))
    col_scale = {cb: QK_SCALE for cb in range(CB_RK, CB_RK + PAIRS)}
    col_scale.update({cb: QK_SCALE * LOG2E for cb in range(CB_NQ, CB_NQ + PAIRS)})
    for off in range(0, min(n_cols, CONV_OFF), PROJ_CHUNK):
        acc = _dot(h, w_ref[:, off:off + PROJ_CHUNK].astype(BF16))
        tiles = []
        for j in range(PROJ_CHUNK // LANES):
            cb = off // LANES + j
            v = acc[:, j * LANES:(j + 1) * LANES]
            if tables is not None and cb in rotated:
                cos, up, dn = tables
                v = v * cos + pltpu.roll(v, LANES - 16, axis=1) * up + pltpu.roll(v, 16, axis=1) * dn
            if cb in col_scale:
                v = v * col_scale[cb]
            tiles.append(v.astype(BF16))
        o_ref[0, :, off:off + PROJ_CHUNK] = jnp.concatenate(tiles, axis=1)
    if n_cols > CONV_OFF:
        ch, cb, cc, cz = (_dot(h, w_ref[:, CONV_OFF + j * CONV_W:CONV_OFF + (j + 1) * CONV_W].astype(BF16))
                          for j in range(4))
        o_ref[0, :, CONV_OFF:CONV_OFF + CONV_W] = (cc * ch).astype(BF16)
        o_ref[0, :, CONV_OFF + CONV_W:CONV_OFF + 2 * CONV_W] = (cb * _silu(cz)).astype(BF16)


def _proj_kernel(x_ref, xc_ref, mod_ref, w_ref, *rest, ctx_row, n_cols_ctx):
    *table_refs, o_ref, oc_ref = rest
    tables = tuple(jnp.concatenate([r_ref[g:g + 1, :] + c_ref[...] for g in range(r_ref.shape[0])], axis=0)
                   for r_ref, c_ref in zip(table_refs[:3], table_refs[3:]))
    _project(x_ref[0], mod_ref[pl.ds(pl.program_id(0), 1), :], w_ref, tables, o_ref, PROJ_W)

    @pl.when(pl.program_id(1) == 0)
    def _():
        _project(xc_ref[0], mod_ref[ctx_row:ctx_row + 1, :], w_ref, None, oc_ref, n_cols_ctx)


def _projection(x, xc, mod, w_in, layer, tables, n_cols_ctx):
    b, n, _ = x.shape
    lc = xc.shape[1]
    tm = min(ROW_TILE, n)
    pc_w = P_W if n_cols_ctx > CONV_OFF else n_cols_ctx
    in_specs = ([pl.BlockSpec((1, tm, D_MODEL), lambda bi, i: (bi, i, 0)),
                 pl.BlockSpec((1, lc, D_MODEL), lambda bi, i: (bi, 0, 0)),
                 pl.BlockSpec((None, 8, 3 * D_MODEL), lambda bi, i: (layer, 0, 0)),
                 pl.BlockSpec((None, D_MODEL, PROJ_W), lambda bi, i: (layer, 0, 0), pipeline_mode=pl.Buffered(1))]
                + [pl.BlockSpec((tm // GRID_W, LANES), lambda bi, i: (i, 0))] * 3
                + [pl.BlockSpec((GRID_W, LANES), lambda bi, i: (0, 0))] * 3)
    return pl.pallas_call(
        functools.partial(_proj_kernel, ctx_row=b, n_cols_ctx=n_cols_ctx),
        grid=(b, n // tm),
        in_specs=in_specs,
        out_specs=[pl.BlockSpec((1, tm, P_W), lambda bi, i: (bi, i, 0)),
                   pl.BlockSpec((1, lc, pc_w), lambda bi, i: (bi, 0, 0))],
        out_shape=[jax.ShapeDtypeStruct((b, n, P_W), BF16), jax.ShapeDtypeStruct((b, lc, pc_w), BF16)],
        compiler_params=pltpu.CompilerParams(dimension_semantics=("parallel", "arbitrary")),
        name="projection",
    )(x, xc, mod, w_in, *tables)


def _log_sigmoid(v):
    return jnp.minimum(v, 0.0) - jnp.log1p(jnp.exp(-jnp.abs(v)))


def _block_diag(m):
    r = lax.broadcasted_iota(jnp.int32, m.shape, 0) < HEAD_DIM
    c = lax.broadcasted_iota(jnp.int32, m.shape, 1) < HEAD_DIM
    return jnp.where(r == c, m, 0.0)


def _head_mean(x):
    head0 = _lane_is_head0(x.shape)
    s0 = jnp.sum(jnp.where(head0, x, 0.0), axis=-1, keepdims=True)
    s1 = jnp.sum(jnp.where(head0, 0.0, x), axis=-1, keepdims=True)
    return jnp.where(head0, s0, s1) * (1.0 / HEAD_DIM)


def _attend(q_ref, k_ref, v_ref, g_ref, o_ref):
    q, k, v = q_ref[0], k_ref[0], v_ref[0]
    head0 = _lane_is_head0(q.shape)
    zero = jnp.zeros_like(q)
    outs = []
    for hh in range(2):
        qh = jnp.where(head0, q, zero) if hh == 0 else jnp.where(head0, zero, q)
        s = _dot_nt(qh, k)
        p = jnp.exp2(s - jnp.max(s, axis=-1, keepdims=True))
        outs.append(_dot(p.astype(BF16), v) / jnp.sum(p, axis=-1, keepdims=True))
    o = jnp.where(head0, outs[0], outs[1])
    o_ref[0] = (o * _silu(g_ref[0].astype(F32))).astype(o_ref.dtype)


def _ret_kernel(*refs, init_state, attend, group):
    refs = list(refs)
    dec_ref, q_ref, k_ref, v_ref, g_ref = refs[:5]
    refs = refs[5:]
    if init_state:
        kc_ref, vc_ref = refs[:2]
        refs = refs[2:]
    if attend:
        attn_refs, refs = refs[:4], refs[4:]
        o_ref, ao_ref, s_ref, u_ref, kt_ref, wq_ref, wkt_ref, d_ref = refs
        _attend(*attn_refs, ao_ref)
    else:
        o_ref, s_ref, u_ref, kt_ref, wq_ref, wkt_ref, d_ref = refs
    c = RET_CHUNK
    n_chunks = q_ref.shape[1] // c
    n_groups = n_chunks // group
    head0 = _lane_is_head0((c, LANES))
    lg_f = _log_sigmoid(dec_ref[0, 0])
    lg_b = _log_sigmoid(dec_ref[1, 0])
    i = lax.broadcasted_iota(jnp.int32, (c, LANES), 0).astype(F32)
    wq_ref[:, :LANES] = jnp.exp((i + 1.0) * lg_f)
    wq_ref[:, LANES:] = jnp.exp((c - i) * lg_b)
    wkt_ref[:LANES, :] = jnp.exp((c - 1.0 - i) * lg_f).T
    wkt_ref[LANES:, :] = jnp.exp(i * lg_b).T
    gc_f = jnp.exp(float(c) * lg_f)
    gc_b = jnp.exp(float(c) * lg_b)
    diff = (lax.broadcasted_iota(jnp.int32, (c, c), 0) - lax.broadcasted_iota(jnp.int32, (c, c), 1)).astype(F32)
    lower = diff >= 0
    for hh in range(2):
        lf = lg_f[:, hh * HEAD_DIM:hh * HEAD_DIM + 1]
        lb = lg_b[:, hh * HEAD_DIM:hh * HEAD_DIM + 1]
        d_ref[:, hh * c:(hh + 1) * c] = jnp.where(lower, jnp.exp(jnp.where(lower, diff, 0.0) * lf),
                                                  jnp.exp(jnp.where(lower, 0.0, -diff) * lb))
    if init_state:
        lc = kc_ref.shape[1]
        m = lax.broadcasted_iota(jnp.int32, (lc, LANES), 0).astype(F32)
        kcf = kc_ref[0].astype(F32)
        s_f0 = _block_diag(_dot_tn((kcf * jnp.exp((lc - 1.0 - m) * lg_f)).astype(BF16), vc_ref[0]))
        s_b0 = _block_diag(_dot_tn((kcf * jnp.exp(m * lg_b)).astype(BF16), vc_ref[0]))
    else:
        s_f0 = s_b0 = jnp.zeros((LANES, LANES), F32)
    r2 = lax.broadcasted_iota(jnp.int32, (2 * LANES, LANES), 0) % LANES < HEAD_DIM
    c2 = lax.broadcasted_iota(jnp.int32, (2 * LANES, LANES), 1) < HEAD_DIM
    diag2 = r2 == c2

    chan0 = lax.broadcasted_iota(jnp.int32, (LANES, c), 0) < HEAD_DIM

    def chunk(n):
        return pl.ds(pl.multiple_of(n * c, c), c)

    def increments(gi, carry):
        for j in range(group):
            n = gi * group + j
            ktb = k_ref[0, chunk(n), :].T
            kt = ktb.astype(F32)
            zero = jnp.zeros_like(ktb)
            kt_ref[n] = jnp.concatenate([jnp.where(chan0, ktb, zero), jnp.where(chan0, zero, ktb)], axis=1)
            lhs = jnp.concatenate([kt * wkt_ref[:LANES, :], kt * wkt_ref[LANES:, :]], axis=0).astype(BF16)
            u_ref[n] = jnp.where(diag2, _dot(lhs, v_ref[0, chunk(n), :]), 0.0)
        return carry

    lax.fori_loop(0, n_groups, increments, 0)

    def fwd_scan(gi, s):
        for j in range(group):
            n = gi * group + j
            s_ref[n, :LANES, :] = s.astype(BF16)
            s = s * gc_f + u_ref[n, :LANES, :]
        return s

    def bwd_scan(gi, s):
        for j in range(group):
            n = n_chunks - 1 - (gi * group + j)
            s_ref[n, LANES:, :] = s.astype(BF16)
            s = s * gc_b + u_ref[n, LANES:, :]
        return s

    lax.fori_loop(0, n_groups, fwd_scan, s_f0)
    lax.fori_loop(0, n_groups, bwd_scan, s_b0)

    def body(gi, carry):
        outs = []
        for j in range(group):
            n = gi * group + j
            rows = chunk(n)
            q, v = q_ref[0, rows, :], v_ref[0, rows, :]
            zero = jnp.zeros_like(v)
            vcat = jnp.concatenate([jnp.where(head0, v, zero), jnp.where(head0, zero, v)], axis=0)
            scores = _dot(q, kt_ref[n]) * d_ref[...]
            qf = q.astype(F32)
            qw = jnp.concatenate([qf * wq_ref[:, :LANES], qf * wq_ref[:, LANES:]], axis=1).astype(BF16)
            outs.append(_dot(scores.astype(BF16), vcat) + _dot(qw, s_ref[n]))
        o = jnp.concatenate(outs, axis=0)
        rows = pl.ds(pl.multiple_of(gi * (group * c), group * c), group * c)
        dlt = o - _head_mean(o)
        var = _head_mean(dlt * dlt)
        y = dlt * lax.rsqrt(var + LN_EPS) * _silu(g_ref[0, rows, :].astype(F32))
        o_ref[0, rows, :] = y.astype(o_ref.dtype)
        return carry

    lax.fori_loop(0, n_groups, body, 0)


def _retention(p, pc, ret_decay, init_state, attend=False):
    b, n, _ = p.shape
    dec = jnp.repeat(ret_decay.astype(F32), HEAD_DIM, axis=-1).reshape(2, PAIRS, 1, LANES)

    def col(cb):
        return pl.BlockSpec((1, n, LANES), lambda bi, pi: (bi, 0, cb + pi))

    in_specs = [pl.BlockSpec((2, 1, 1, LANES), lambda bi, pi: (0, pi, 0, 0)),
                col(CB_RQ), col(CB_RK), col(CB_RV), col(CB_RG)]
    args = [dec, p, p, p, p]
    if init_state:
        lc = pc.shape[1]
        in_specs += [pl.BlockSpec((1, lc, LANES), lambda bi, pi: (bi, 0, CB_RK + pi)),
                     pl.BlockSpec((1, lc, LANES), lambda bi, pi: (bi, 0, CB_RV + pi))]
        args += [pc, pc]
    out_specs, out_shape = col(0), jax.ShapeDtypeStruct((b, n, RET_W), BF16)
    if attend:
        in_specs += [col(CB_NQ), col(CB_NK), col(CB_NV), col(CB_NG)]
        args += [p, p, p, p]
        out_specs, out_shape = [out_specs, col(0)], [out_shape, jax.ShapeDtypeStruct((b, n, NA_W), BF16)]
    c = RET_CHUNK
    return pl.pallas_call(
        functools.partial(_ret_kernel, init_state=init_state, attend=attend, group=min(RET_GROUP, n // c)),
        grid=(b, PAIRS),
        in_specs=in_specs,
        out_specs=out_specs,
        out_shape=out_shape,
        scratch_shapes=[pltpu.VMEM((n // c, 2 * LANES, LANES), BF16),
                        pltpu.VMEM((n // c, 2 * LANES, LANES), F32),
                        pltpu.VMEM((n // c, LANES, 2 * c), BF16),
                        pltpu.VMEM((c, 2 * LANES), F32),
                        pltpu.VMEM((2 * LANES, c), F32),
                        pltpu.VMEM((c, 2 * c), F32)],
        compiler_params=pltpu.CompilerParams(dimension_semantics=("parallel", "parallel")),
        name="retention",
    )(*args)


def _na_bias_plan(rows):
    nblk = rows // NA_ROWS
    win_h = min(WIN_H, rows)
    plan = np.full((3, NA_ROWS, 3, NA_ROWS), -1, np.int64)
    for kind, blk in enumerate((0, 1, nblk - 1)):
        for qr in range(NA_ROWS):
            r = blk * NA_ROWS + qr
            rs = min(max(r - win_h // 2, 0), rows - win_h)
            for kb in range(3):
                if not 0 <= blk + kb - 1 <= nblk - 1:
                    continue
                for krl in range(NA_ROWS):
                    kr = (blk + kb - 1) * NA_ROWS + krl
                    if rs <= kr < rs + win_h:
                        plan[kind, qr, kb, krl] = kr - r + WIN_H - 1
    return plan


def _fill_na_bias(rpb_ref, bias_ref, t_ref, plan):
    n_dr = 2 * WIN_H - 1
    qc = lax.broadcasted_iota(jnp.int32, (GRID_W, LANES), 0)
    lane = lax.broadcasted_iota(jnp.int32, (GRID_W, LANES), 1)
    kc = lane % GRID_W
    cs = jnp.clip(qc - WIN_W // 2, 0, GRID_W - WIN_W)
    col_ok = (kc >= cs) & (kc < cs + WIN_W)
    first = lane < GRID_W
    neg = jnp.full((GRID_W, LANES), NEG, F32)
    for h in range(NA_HEADS):
        for dr in range(n_dr):
            row = jnp.broadcast_to(rpb_ref[h, dr:dr + 1, :] * LOG2E, (GRID_W, LANES))
            lo = pltpu.roll(row, LANES - (WIN_W - 1), 1, stride=1, stride_axis=0)
            hi = pltpu.roll(row, GRID_W - (WIN_W - 1), 1, stride=1, stride_axis=0)
            t_ref[dr] = jnp.where(col_ok, jnp.where(first, lo, hi), neg)
        for kind in range(3):
            for qr in range(NA_ROWS):
                for kb in range(3):
                    for pr in range(NA_ROWS // 2):
                        ia, ib = (int(plan[kind, qr, kb, 2 * pr + s]) for s in range(2))
                        a = t_ref[ia] if ia >= 0 else neg
                        b = t_ref[ib] if ib >= 0 else neg
                        col0 = kb * NA_TOK + pr * LANES
                        bias_ref[kind, h, qr * GRID_W:(qr + 1) * GRID_W, col0:col0 + LANES] = jnp.where(first, a, b)


def _fold_lanes(blocks, op):
    tiles = [blk[:, j:j + LANES] for blk in blocks for j in range(0, blk.shape[1], LANES)]
    acc = tiles[0]
    for t in tiles[1:]:
        acc = op(acc, t)
    return acc


def _na_kernel(*refs, plan):
    nkb = NA_STEP + 2
    q_ref, k_refs, v_refs = refs[0], refs[1:1 + nkb], refs[1 + nkb:1 + 2 * nkb]
    kx_ref, vx_ref, g_ref, rpb_ref, o_ref, bias_ref, t_ref = refs[1 + 2 * nkb:]
    i, last = pl.program_id(1), pl.num_programs(1) - 1

    @pl.when((pl.program_id(0) == 0) & (i == 0))
    def _():
        _fill_na_bias(rpb_ref, bias_ref, t_ref, plan)

    head0 = _lane_is_head0((NA_TOK, LANES))
    for j in range(NA_STEP):
        edge = 0 if j == 0 else 2 if j == NA_STEP - 1 else None
        kind = 1 if edge is None else jnp.where(i == (0 if edge == 0 else last), edge, 1)
        rows = slice(j * NA_TOK, (j + 1) * NA_TOK)
        for pi in range(PAIRS):
            cols = slice(pi * LANES, (pi + 1) * LANES)
            q = q_ref[0, rows, cols]
            zero = jnp.zeros_like(q)
            keys = [r[0, :, cols] for r in k_refs[j:j + 3]] + [kx_ref[0, :, cols]]
            vals = [r[0, :, cols] for r in v_refs[j:j + 3]] + [vx_ref[0, :, cols]]
            probs, inv = [], []
            for hh in range(2):
                qh = jnp.where(head0, q, zero) if hh == 0 else jnp.where(head0, zero, q)
                s = [_dot_nt(qh, keys[t]) + bias_ref[kind, 2 * pi + hh, :, t * NA_TOK:(t + 1) * NA_TOK] for t in range(3)]
                s.append(_dot_nt(qh, keys[3]))
                m = jnp.max(_fold_lanes(s, jnp.maximum), axis=-1, keepdims=True)
                p = [jnp.exp2(st - m) for st in s]
                inv.append(1.0 / jnp.sum(_fold_lanes(p, jnp.add), axis=-1, keepdims=True))
                probs += [pt.astype(BF16) for pt in p]
            v_cat = jnp.concatenate([jnp.where(head0, v, zero) for v in vals]
                                    + [jnp.where(head0, zero, v) for v in vals], axis=0)
            o = _dot(jnp.concatenate(probs, axis=1), v_cat) * jnp.where(head0, inv[0], inv[1])
            o_ref[0, rows, cols] = (o * _silu(g_ref[0, rows, cols].astype(F32))).astype(o_ref.dtype)


def _neighbourhood(p, pc, rpb):
    b, n, _ = p.shape
    nblk = n // NA_TOK
    nstep = nblk // NA_STEP
    assert NA_STEP >= 2 and nstep >= 2
    grp = NA_W // LANES

    def key_blk(cb, shift):
        return pl.BlockSpec((1, NA_TOK, NA_W),
                            lambda bi, i: (bi, jnp.clip(NA_STEP * i + shift, 0, nblk - 1), cb // grp))

    def step_blk(cb):
        return pl.BlockSpec((1, NA_STEP * NA_TOK, NA_W), lambda bi, i: (bi, i, cb // grp))

    lc = pc.shape[1]

    def ctx(cb):
        return pl.BlockSpec((1, lc, NA_W), lambda bi, i: (bi, 0, cb // grp))

    n_dr, n_dc = 2 * WIN_H - 1, 2 * WIN_W - 1
    rpb_rows = jnp.pad(rpb.astype(F32), ((0, 0), (0, 16 - n_dr), (0, LANES - n_dc)))
    shifts = range(-1, NA_STEP + 1)
    return pl.pallas_call(
        functools.partial(_na_kernel, plan=_na_bias_plan(n // GRID_W)),
        grid=(b, nstep),
        in_specs=([step_blk(CB_NQ)] + [key_blk(CB_NK, s) for s in shifts] + [key_blk(CB_NV, s) for s in shifts]
                  + [ctx(CB_NK), ctx(CB_NV), step_blk(CB_NG),
                     pl.BlockSpec((NA_HEADS, 16, LANES), lambda bi, i: (0, 0, 0))]),
        out_specs=pl.BlockSpec((1, NA_STEP * NA_TOK, NA_W), lambda bi, i: (bi, i, 0)),
        out_shape=jax.ShapeDtypeStruct((b, n, NA_W), BF16),
        scratch_shapes=[pltpu.VMEM((3, NA_HEADS, NA_TOK, 3 * NA_TOK), F32),
                        pltpu.VMEM((n_dr, GRID_W, LANES), F32)],
        compiler_params=pltpu.CompilerParams(dimension_semantics=("arbitrary", "arbitrary")),
        name="neighbourhood",
    )(p, *([p] * (2 * len(shifts))), pc, pc, p, rpb_rows)


def _out_kernel(x_ref, mod_ref, u_ref, gz_ref, ub_ref, ua_ref,
                yr_ref, yn_ref, cw_ref, cbias_ref, w_ref, lg_ref, lb_ref, o_ref, *, mod_row):
    i = pl.program_id(1)
    last = pl.num_programs(1) - 1
    u = u_ref[0].astype(F32)
    tm = u.shape[0]
    halo = ub_ref.shape[1]
    u_before = jnp.where(i == 0, 0.0, ub_ref[0, halo - 1:halo, :].astype(F32))
    u_after = jnp.where(i == last, 0.0, ua_ref[0, 0:1, :].astype(F32))
    row = lax.broadcasted_iota(jnp.int32, u.shape, 0)
    u_prev = jnp.where(row == 0, u_before, pltpu.roll(u, 1, axis=0))
    u_next = jnp.where(row == tm - 1, u_after, pltpu.roll(u, tm - 1, axis=0))
    cw = cw_ref[...]
    row = pl.program_id(0) if mod_row is None else mod_row
    g = mod_ref[pl.ds(row, 1), 2 * D_MODEL:] * (1.0 / DEEPNORM_ALPHA)
    w = w_ref[...].astype(BF16)
    sub = min(OUT_SUB, tm)
    for r in range(tm // sub):
        rs = slice(r * sub, (r + 1) * sub)
        conv = u_prev[rs] * cw[0:1] + u[rs] * cw[1:2] + u_next[rs] * cw[2:3] + cbias_ref[...]
        y_conv = gz_ref[0, rs, :].astype(F32) * conv
        y = jnp.concatenate([y_conv.astype(BF16), yr_ref[0, rs, :], yn_ref[0, rs, :]], axis=-1)
        z = x_ref[0, rs, :] + g * _dot(y, w)
        mu = jnp.mean(z, axis=-1, keepdims=True)
        dlt = z - mu
        var = jnp.mean(dlt * dlt, axis=-1, keepdims=True)
        o_ref[0, rs, :] = dlt * lax.rsqrt(var + LN_EPS / DEEPNORM_ALPHA ** 2) * lg_ref[...] + lb_ref[...]


def _output(x, mod, mod_row, p, y_ret, y_na, conv_w, conv_b, w_out, layer, ln_g, ln_b):
    b, n, _ = x.shape
    tm = min(OUT_TILE, n)
    halo = 16
    per = tm // halo
    nh = n // halo

    def conv(cb):
        return pl.BlockSpec((1, tm, CONV_W), lambda bi, i: (bi, i, cb))

    def before(cb):
        return pl.BlockSpec((1, halo, CONV_W), lambda bi, i: (bi, jnp.maximum(i * per - 1, 0), cb))

    def after(cb):
        return pl.BlockSpec((1, halo, CONV_W), lambda bi, i: (bi, jnp.minimum((i + 1) * per, nh - 1), cb))

    def const(shape):
        return pl.BlockSpec(shape, lambda bi, i: (0,) * len(shape))

    return pl.pallas_call(
        functools.partial(_out_kernel, mod_row=mod_row),
        grid=(b, n // tm),
        in_specs=[pl.BlockSpec((1, tm, D_MODEL), lambda bi, i: (bi, i, 0)),
                  pl.BlockSpec((None, 8, 3 * D_MODEL), lambda bi, i: (layer, 0, 0)),
                  conv(CB_U), conv(CB_GZ), before(CB_U), after(CB_U),
                  pl.BlockSpec((1, tm, RET_W), lambda bi, i: (bi, i, 0)),
                  pl.BlockSpec((1, tm, NA_W), lambda bi, i: (bi, i, 0)),
                  const((3, CONV_W)), const((1, CONV_W)),
                  pl.BlockSpec((None, MIX_W, D_MODEL), lambda bi, i: (layer, 0, 0), pipeline_mode=pl.Buffered(1)),
                  const((1, D_MODEL)), const((1, D_MODEL))],
        out_specs=pl.BlockSpec((1, tm, D_MODEL), lambda bi, i: (bi, i, 0)),
        out_shape=jax.ShapeDtypeStruct((b, n, D_MODEL), F32),
        compiler_params=pltpu.CompilerParams(dimension_semantics=("parallel", "parallel")),
        name="output",
    )(x, mod, p, p, p, p, y_ret, y_na, conv_w, conv_b.reshape(1, CONV_W), w_out,
      ln_g.reshape(1, D_MODEL), ln_b.reshape(1, D_MODEL))


def kernel(x, c, ctx, c_ctx, w_mod, b_mod, w_in, conv_w, conv_b, ret_decay, na_rpb, w_out, ln_g, ln_b):
    b, n, d = x.shape
    act_t = jnp.pad(jnp.concatenate([c, c_ctx[None]], axis=0).T, ((0, 0), (0, 5)))
    mod = _modulation(act_t, w_mod, b_mod)
    tables = _rope_tables(n)
    xc = ctx
    for l in range(DEPTH):
        need_ctx = l < DEPTH - 1
        p, pc = _projection(x, xc, mod, w_in, l, tables, PROJ_W if need_ctx else KV_W)
        y_ret = _retention(p, pc, ret_decay[l], init_state=True)
        y_na = _neighbourhood(p, pc, na_rpb[l])
        x_new = _output(x, mod, None, p, y_ret, y_na, conv_w[l], conv_b[l], w_out, l, ln_g[l], ln_b[l])
        if need_ctx:
            yc_ret, yc_na = _retention(pc, None, ret_decay[l], init_state=False, attend=True)
            xc = _output(xc, mod, b, pc, yc_ret, yc_na, conv_w[l], conv_b[l], w_out, l, ln_g[l], ln_b[l])
        x = x_new
    return x
```

```python
import functools

import numpy as np
import jax
import jax.numpy as jnp
from jax import lax
from jax.experimental import pallas as pl
from jax.experimental.pallas import tpu as pltpu

D_MODEL = 1024
DEPTH = 2
GRID_W = 64
HEAD_DIM = 64
CONV_W = 256
RET_HEADS = 6
RET_W = RET_HEADS * HEAD_DIM
NA_HEADS = 6
NA_W = NA_HEADS * HEAD_DIM
MIX_W = CONV_W + RET_W + NA_W
RET_CHUNK = 128
WIN_H = 8
WIN_W = 16
ROPE_BASE = 10000.0
LN_EPS = 1e-5
DEEPNORM_ALPHA = (2 * DEPTH) ** 0.25
PROJ_SPLITS = (RET_W, RET_W, NA_W, NA_W, RET_W, RET_W, NA_W, NA_W, CONV_W, CONV_W, CONV_W, CONV_W)
PROJ_W = sum(PROJ_SPLITS)
KV_W = 2 * RET_W + 2 * NA_W

LANES = 128
PAIRS = RET_HEADS // 2
CB_RK, CB_RV, CB_NK, CB_NV, CB_RQ, CB_RG, CB_NQ, CB_NG = 0, 3, 6, 9, 12, 15, 18, 21
CONV_OFF = PROJ_W - 4 * CONV_W
P_W = CONV_OFF + 2 * CONV_W
CB_U, CB_GZ = CONV_OFF // CONV_W, CONV_OFF // CONV_W + 1
QK_SCALE = HEAD_DIM ** -0.5
NEG = -1e30
LOG2E = 1.4426950408889634
NA_ROWS = 4
NA_TOK = NA_ROWS * GRID_W
NA_STEP = 4
RET_GROUP = 64
ROW_TILE = 1024
OUT_TILE = 2048
OUT_SUB = 256
PROJ_CHUNK = 512

F32 = jnp.float32
BF16 = jnp.bfloat16


def _silu(v):
    return v * jax.nn.sigmoid(v)


def _dot(a, b):
    return jnp.dot(a, b, preferred_element_type=F32)


def _dot_nt(a, b):
    return lax.dot_general(a, b, (((1,), (1,)), ((), ())), preferred_element_type=F32)


def _dot_tn(a, b):
    return lax.dot_general(a, b, (((0,), (0,)), ((), ())), preferred_element_type=F32)


def _lane_is_head0(shape):
    return lax.broadcasted_iota(jnp.int32, shape, len(shape) - 1) < HEAD_DIM


def _mod_kernel(act_ref, w_ref, b_ref, o_ref):
    a = _silu(act_ref[...])
    w = w_ref[0]
    bias = b_ref[0]
    for r in range(3):
        o_ref[0, r:r + 1, :] = jnp.sum(a[:, r:r + 1] * w, axis=0, keepdims=True) + bias
    o_ref[0, 3:8, :] = jnp.zeros((5, w.shape[1]), F32)


def _modulation(act_t, w_mod, b_mod):
    tn = 1536
    n = w_mod.shape[-1]
    return pl.pallas_call(
        _mod_kernel,
        grid=(DEPTH, n // tn),
        in_specs=[pl.BlockSpec((D_MODEL, 8), lambda l, j: (0, 0)),
                  pl.BlockSpec((1, D_MODEL, tn), lambda l, j: (l, 0, j)),
                  pl.BlockSpec((1, 1, tn), lambda l, j: (l, 0, j))],
        out_specs=pl.BlockSpec((1, 8, tn), lambda l, j: (l, 0, j)),
        out_shape=jax.ShapeDtypeStruct((DEPTH, 8, n), F32),
        name="modulation",
    )(act_t, w_mod, b_mod.reshape(DEPTH, 1, n))


def _rope_tables(n):
    rows = n // GRID_W
    nf = HEAD_DIM // 4
    inv = ROPE_BASE ** (-np.arange(nf, dtype=np.float64) / nf)
    lane = np.arange(LANES)
    by_row = ((lane % HEAD_DIM) < HEAD_DIM // 2)[None]
    first = ((lane % (2 * nf)) < nf)[None]
    ang_r = np.tile(np.arange(rows, dtype=np.float64)[:, None] * inv, (1, LANES // nf))
    ang_c = np.tile(np.arange(GRID_W, dtype=np.float64)[:, None] * inv, (1, LANES // nf))
    parts = []
    for ang, own in ((ang_r, by_row), (ang_c, ~by_row)):
        cos, sin = np.cos(ang), np.sin(ang)
        parts += [np.where(own, cos, 0.0), np.where(own & first, -sin, 0.0), np.where(own & ~first, sin, 0.0)]
    return tuple(jnp.asarray(t, F32) for t in parts)


def _project(x, mod_row, w_ref, tables, o_ref, n_cols):
    shift, scale = mod_row[:, :D_MODEL], mod_row[:, D_MODEL:2 * D_MODEL]
    h = (x * (1.0 + scale) + shift).astype(BF16)
    rotated = tuple(range(CB_RK, CB_RK + PAIRS)) + tuple(range(CB_RQ, CB_RQ + PAIRS))
    col_scale = {cb: QK_SCALE for cb in range(CB_RK, CB_RK + PAIRS)}
    col_scale.update({cb: QK_SCALE * LOG2E for cb in range(CB_NQ, CB_NQ + PAIRS)})
    for off in range(0, min(n_cols, CONV_OFF), PROJ_CHUNK):
        acc = _dot(h, w_ref[:, off:off + PROJ_CHUNK].astype(BF16))
        tiles = []
        for j in range(PROJ_CHUNK // LANES):
            cb = off // LANES + j
            v = acc[:, j * LANES:(j + 1) * LANES]
            if tables is not None and cb in rotated:
                cos, up, dn = tables
                v = v * cos + pltpu.roll(v, LANES - 16, axis=1) * up + pltpu.roll(v, 16, axis=1) * dn
            if cb in col_scale:
                v = v * col_scale[cb]
            tiles.append(v.astype(BF16))
        o_ref[0, :, off:off + PROJ_CHUNK] = jnp.concatenate(tiles, axis=1)
    if n_cols > CONV_OFF:
        ch, cb, cc, cz = (_dot(h, w_ref[:, CONV_OFF + j * CONV_W:CONV_OFF + (j + 1) * CONV_W].astype(BF16))
                          for j in range(4))
        o_ref[0, :, CONV_OFF:CONV_OFF + CONV_W] = (cc * ch).astype(BF16)
        o_ref[0, :, CONV_OFF + CONV_W:CONV_OFF + 2 * CONV_W] = (cb * _silu(cz)).astype(BF16)


def _proj_kernel(x_ref, xc_ref, mod_ref, w_ref, *rest, ctx_row, n_cols_ctx):
    *table_refs, o_ref, oc_ref = rest
    tables = tuple(jnp.concatenate([r_ref[g:g + 1, :] + c_ref[...] for g in range(r_ref.shape[0])], axis=0)
                   for r_ref, c_ref in zip(table_refs[:3], table_refs[3:]))
    _project(x_ref[0], mod_ref[pl.ds(pl.program_id(0), 1), :], w_ref, tables, o_ref, PROJ_W)

    @pl.when(pl.program_id(1) == 0)
    def _():
        _project(xc_ref[0], mod_ref[ctx_row:ctx_row + 1, :], w_ref, None, oc_ref, n_cols_ctx)


def _projection(x, xc, mod, w_in, layer, tables, n_cols_ctx):
    b, n, _ = x.shape
    lc = xc.shape[1]
    tm = min(ROW_TILE, n)
    pc_w = P_W if n_cols_ctx > CONV_OFF else n_cols_ctx
    in_specs = ([pl.BlockSpec((1, tm, D_MODEL), lambda bi, i: (bi, i, 0)),
                 pl.BlockSpec((1, lc, D_MODEL), lambda bi, i: (bi, 0, 0)),
                 pl.BlockSpec((None, 8, 3 * D_MODEL), lambda bi, i: (layer, 0, 0)),
                 pl.BlockSpec((None, D_MODEL, PROJ_W), lambda bi, i: (layer, 0, 0), pipeline_mode=pl.Buffered(1))]
                + [pl.BlockSpec((tm // GRID_W, LANES), lambda bi, i: (i, 0))] * 3
                + [pl.BlockSpec((GRID_W, LANES), lambda bi, i: (0, 0))] * 3)
    return pl.pallas_call(
        functools.partial(_proj_kernel, ctx_row=b, n_cols_ctx=n_cols_ctx),
        grid=(b, n // tm),
        in_specs=in_specs,
        out_specs=[pl.BlockSpec((1, tm, P_W), lambda bi, i: (bi, i, 0)),
                   pl.BlockSpec((1, lc, pc_w), lambda bi, i: (bi, 0, 0))],
        out_shape=[jax.ShapeDtypeStruct((b, n, P_W), BF16), jax.ShapeDtypeStruct((b, lc, pc_w), BF16)],
        compiler_params=pltpu.CompilerParams(dimension_semantics=("parallel", "arbitrary")),
        name="projection",
    )(x, xc, mod, w_in, *tables)


def _log_sigmoid(v):
    return jnp.minimum(v, 0.0) - jnp.log1p(jnp.exp(-jnp.abs(v)))


def _block_diag(m):
    r = lax.broadcasted_iota(jnp.int32, m.shape, 0) < HEAD_DIM
    c = lax.broadcasted_iota(jnp.int32, m.shape, 1) < HEAD_DIM
    return jnp.where(r == c, m, 0.0)


def _head_mean(x):
    head0 = _lane_is_head0(x.shape)
    s0 = jnp.sum(jnp.where(head0, x, 0.0), axis=-1, keepdims=True)
    s1 = jnp.sum(jnp.where(head0, 0.0, x), axis=-1, keepdims=True)
    return jnp.where(head0, s0, s1) * (1.0 / HEAD_DIM)


def _attend(q_ref, k_ref, v_ref, g_ref, o_ref):
    q, k, v = q_ref[0], k_ref[0], v_ref[0]
    head0 = _lane_is_head0(q.shape)
    zero = jnp.zeros_like(q)
    outs = []
    for hh in range(2):
        qh = jnp.where(head0, q, zero) if hh == 0 else jnp.where(head0, zero, q)
        s = _dot_nt(qh, k)
        p = jnp.exp2(s - jnp.max(s, axis=-1, keepdims=True))
        outs.append(_dot(p.astype(BF16), v) / jnp.sum(p, axis=-1, keepdims=True))
    o = jnp.where(head0, outs[0], outs[1])
    o_ref[0] = (o * _silu(g_ref[0].astype(F32))).astype(o_ref.dtype)


def _ret_kernel(*refs, init_state, attend, group):
    refs = list(refs)
    dec_ref, q_ref, k_ref, v_ref, g_ref = refs[:5]
    refs = refs[5:]
    if init_state:
        kc_ref, vc_ref = refs[:2]
        refs = refs[2:]
    if attend:
        attn_refs, refs = refs[:4], refs[4:]
        o_ref, ao_ref, s_ref, u_ref, kt_ref, wq_ref, wkt_ref, d_ref = refs
        _attend(*attn_refs, ao_ref)
    else:
        o_ref, s_ref, u_ref, kt_ref, wq_ref, wkt_ref, d_ref = refs
    c = RET_CHUNK
    n_chunks = q_ref.shape[1] // c
    n_groups = n_chunks // group
    head0 = _lane_is_head0((c, LANES))
    lg_f = _log_sigmoid(dec_ref[0, 0])
    lg_b = _log_sigmoid(dec_ref[1, 0])
    i = lax.broadcasted_iota(jnp.int32, (c, LANES), 0).astype(F32)
    wq_ref[:, :LANES] = jnp.exp((i + 1.0) * lg_f)
    wq_ref[:, LANES:] = jnp.exp((c - i) * lg_b)
    wkt_ref[:LANES, :] = jnp.exp((c - 1.0 - i) * lg_f).T
    wkt_ref[LANES:, :] = jnp.exp(i * lg_b).T
    gc_f = jnp.exp(float(c) * lg_f)
    gc_b = jnp.exp(float(c) * lg_b)
    diff = (lax.broadcasted_iota(jnp.int32, (c, c), 0) - lax.broadcasted_iota(jnp.int32, (c, c), 1)).astype(F32)
    lower = diff >= 0
    for hh in range(2):
        lf = lg_f[:, hh * HEAD_DIM:hh * HEAD_DIM + 1]
        lb = lg_b[:, hh * HEAD_DIM:hh * HEAD_DIM + 1]
        d_ref[:, hh * c:(hh + 1) * c] = jnp.where(lower, jnp.exp(jnp.where(lower, diff, 0.0) * lf),
                                                  jnp.exp(jnp.where(lower, 0.0, -diff) * lb))
    if init_state:
        lc = kc_ref.shape[1]
        m = lax.broadcasted_iota(jnp.int32, (lc, LANES), 0).astype(F32)
        kcf = kc_ref[0].astype(F32)
        s_f0 = _block_diag(_dot_tn((kcf * jnp.exp((lc - 1.0 - m) * lg_f)).astype(BF16), vc_ref[0]))
        s_b0 = _block_diag(_dot_tn((kcf * jnp.exp(m * lg_b)).astype(BF16), vc_ref[0]))
    else:
        s_f0 = s_b0 = jnp.zeros((LANES, LANES), F32)
    r2 = lax.broadcasted_iota(jnp.int32, (2 * LANES, LANES), 0) % LANES < HEAD_DIM
    c2 = lax.broadcasted_iota(jnp.int32, (2 * LANES, LANES), 1) < HEAD_DIM
    diag2 = r2 == c2

    chan0 = lax.broadcasted_iota(jnp.int32, (LANES, c), 0) < HEAD_DIM

    def chunk(n):
        return pl.ds(pl.multiple_of(n * c, c), c)

    def increments(gi, carry):
        for j in range(group):
            n = gi * group + j
            ktb = k_ref[0, chunk(n), :].T
            kt = ktb.astype(F32)
            zero = jnp.zeros_like(ktb)
            kt_ref[n] = jnp.concatenate([jnp.where(chan0, ktb, zero), jnp.where(chan0, zero, ktb)], axis=1)
            lhs = jnp.concatenate([kt * wkt_ref[:LANES, :], kt * wkt_ref[LANES:, :]], axis=0).astype(BF16)
            u_ref[n] = jnp.where(diag2, _dot(lhs, v_ref[0, chunk(n), :]), 0.0)
        return carry

    lax.fori_loop(0, n_groups, increments, 0)

    def fwd_scan(gi, s):
        for j in range(group):
            n = gi * group + j
            s_ref[n, :LANES, :] = s.astype(BF16)
            s = s * gc_f + u_ref[n, :LANES, :]
        return s

    def bwd_scan(gi, s):
        for j in range(group):
            n = n_chunks - 1 - (gi * group + j)
            s_ref[n, LANES:, :] = s.astype(BF16)
            s = s * gc_b + u_ref[n, LANES:, :]
        return s

    lax.fori_loop(0, n_groups, fwd_scan, s_f0)
    lax.fori_loop(0, n_groups, bwd_scan, s_b0)

    def body(gi, carry):
        outs = []
        for j in range(group):
            n = gi * group + j
            rows = chunk(n)
            q, v = q_ref[0, rows, :], v_ref[0, rows, :]
            zero = jnp.zeros_like(v)
            vcat = jnp.concatenate([jnp.where(head0, v, zero), jnp.where(head0, zero, v)], axis=0)
            scores = _dot(q, kt_ref[n]) * d_ref[...]
            qf = q.astype(F32)
            qw = jnp.concatenate([qf * wq_ref[:, :LANES], qf * wq_ref[:, LANES:]], axis=1).astype(BF16)
            outs.append(_dot(scores.astype(BF16), vcat) + _dot(qw, s_ref[n]))
        o = jnp.concatenate(outs, axis=0)
        rows = pl.ds(pl.multiple_of(gi * (group * c), group * c), group * c)
        dlt = o - _head_mean(o)
        var = _head_mean(dlt * dlt)
        y = dlt * lax.rsqrt(var + LN_EPS) * _silu(g_ref[0, rows, :].astype(F32))
        o_ref[0, rows, :] = y.astype(o_ref.dtype)
        return carry

    lax.fori_loop(0, n_groups, body, 0)


def _retention(p, pc, ret_decay, init_state, attend=False):
    b, n, _ = p.shape
    dec = jnp.repeat(ret_decay.astype(F32), HEAD_DIM, axis=-1).reshape(2, PAIRS, 1, LANES)

    def col(cb):
        return pl.BlockSpec((1, n, LANES), lambda bi, pi: (bi, 0, cb + pi))

    in_specs = [pl.BlockSpec((2, 1, 1, LANES), lambda bi, pi: (0, pi, 0, 0)),
                col(CB_RQ), col(CB_RK), col(CB_RV), col(CB_RG)]
    args = [dec, p, p, p, p]
    if init_state:
        lc = pc.shape[1]
        in_specs += [pl.BlockSpec((1, lc, LANES), lambda bi, pi: (bi, 0, CB_RK + pi)),
                     pl.BlockSpec((1, lc, LANES), lambda bi, pi: (bi, 0, CB_RV + pi))]
        args += [pc, pc]
    out_specs, out_shape = col(0), jax.ShapeDtypeStruct((b, n, RET_W), BF16)
    if attend:
        in_specs += [col(CB_NQ), col(CB_NK), col(CB_NV), col(CB_NG)]
        args += [p, p, p, p]
        out_specs, out_shape = [out_specs, col(0)], [out_shape, jax.ShapeDtypeStruct((b, n, NA_W), BF16)]
    c = RET_CHUNK
    return pl.pallas_call(
        functools.partial(_ret_kernel, init_state=init_state, attend=attend, group=min(RET_GROUP, n // c)),
        grid=(b, PAIRS),
        in_specs=in_specs,
        out_specs=out_specs,
        out_shape=out_shape,
        scratch_shapes=[pltpu.VMEM((n // c, 2 * LANES, LANES), BF16),
                        pltpu.VMEM((n // c, 2 * LANES, LANES), F32),
                        pltpu.VMEM((n // c, LANES, 2 * c), BF16),
                        pltpu.VMEM((c, 2 * LANES), F32),
                        pltpu.VMEM((2 * LANES, c), F32),
                        pltpu.VMEM((c, 2 * c), F32)],
        compiler_params=pltpu.CompilerParams(dimension_semantics=("parallel", "parallel")),
        name="retention",
    )(*args)


def _na_bias_plan(rows):
    nblk = rows // NA_ROWS
    win_h = min(WIN_H, rows)
    plan = np.full((3, NA_ROWS, 3, NA_ROWS), -1, np.int64)
    for kind, blk in enumerate((0, 1, nblk - 1)):
        for qr in range(NA_ROWS):
            r = blk * NA_ROWS + qr
            rs = min(max(r - win_h // 2, 0), rows - win_h)
            for kb in range(3):
                if not 0 <= blk + kb - 1 <= nblk - 1:
                    continue
                for krl in range(NA_ROWS):
                    kr = (blk + kb - 1) * NA_ROWS + krl
                    if rs <= kr < rs + win_h:
                        plan[kind, qr, kb, krl] = kr - r + WIN_H - 1
    return plan


def _fill_na_bias(rpb_ref, bias_ref, t_ref, plan):
    n_dr = 2 * WIN_H - 1
    qc = lax.broadcasted_iota(jnp.int32, (GRID_W, LANES), 0)
    lane = lax.broadcasted_iota(jnp.int32, (GRID_W, LANES), 1)
    kc = lane % GRID_W
    cs = jnp.clip(qc - WIN_W // 2, 0, GRID_W - WIN_W)
    col_ok = (kc >= cs) & (kc < cs + WIN_W)
    first = lane < GRID_W
    neg = jnp.full((GRID_W, LANES), NEG, F32)
    for h in range(NA_HEADS):
        for dr in range(n_dr):
            row = jnp.broadcast_to(rpb_ref[h, dr:dr + 1, :] * LOG2E, (GRID_W, LANES))
            lo = pltpu.roll(row, LANES - (WIN_W - 1), 1, stride=1, stride_axis=0)
            hi = pltpu.roll(row, GRID_W - (WIN_W - 1), 1, stride=1, stride_axis=0)
            t_ref[dr] = jnp.where(col_ok, jnp.where(first, lo, hi), neg)
        for kind in range(3):
            for qr in range(NA_ROWS):
                for kb in range(3):
                    for pr in range(NA_ROWS // 2):
                        ia, ib = (int(plan[kind, qr, kb, 2 * pr + s]) for s in range(2))
                        a = t_ref[ia] if ia >= 0 else neg
                        b = t_ref[ib] if ib >= 0 else neg
                        col0 = kb * NA_TOK + pr * LANES
                        bias_ref[kind, h, qr * GRID_W:(qr + 1) * GRID_W, col0:col0 + LANES] = jnp.where(first, a, b)


def _fold_lanes(blocks, op):
    tiles = [blk[:, j:j + LANES] for blk in blocks for j in range(0, blk.shape[1], LANES)]
    acc = tiles[0]
    for t in tiles[1:]:
        acc = op(acc, t)
    return acc


def _na_kernel(*refs, plan):
    nkb = NA_STEP + 2
    q_ref, k_refs, v_refs = refs[0], refs[1:1 + nkb], refs[1 + nkb:1 + 2 * nkb]
    kx_ref, vx_ref, g_ref, rpb_ref, o_ref, bias_ref, t_ref = refs[1 + 2 * nkb:]
    i, last = pl.program_id(1), pl.num_programs(1) - 1

    @pl.when((pl.program_id(0) == 0) & (i == 0))
    def _():
        _fill_na_bias(rpb_ref, bias_ref, t_ref, plan)

    head0 = _lane_is_head0((NA_TOK, LANES))
    for j in range(NA_STEP):
        edge = 0 if j == 0 else 2 if j == NA_STEP - 1 else None
        kind = 1 if edge is None else jnp.where(i == (0 if edge == 0 else last), edge, 1)
        rows = slice(j * NA_TOK, (j + 1) * NA_TOK)
        for pi in range(PAIRS):
            cols = slice(pi * LANES, (pi + 1) * LANES)
            q = q_ref[0, rows, cols]
            zero = jnp.zeros_like(q)
            keys = [r[0, :, cols] for r in k_refs[j:j + 3]] + [kx_ref[0, :, cols]]
            vals = [r[0, :, cols] for r in v_refs[j:j + 3]] + [vx_ref[0, :, cols]]
            probs, inv = [], []
            for hh in range(2):
                qh = jnp.where(head0, q, zero) if hh == 0 else jnp.where(head0, zero, q)
                s = [_dot_nt(qh, keys[t]) + bias_ref[kind, 2 * pi + hh, :, t * NA_TOK:(t + 1) * NA_TOK] for t in range(3)]
                s.append(_dot_nt(qh, keys[3]))
                m = jnp.max(_fold_lanes(s, jnp.maximum), axis=-1, keepdims=True)
                p = [jnp.exp2(st - m) for st in s]
                inv.append(1.0 / jnp.sum(_fold_lanes(p, jnp.add), axis=-1, keepdims=True))
                probs += [pt.astype(BF16) for pt in p]
            v_cat = jnp.concatenate([jnp.where(head0, v, zero) for v in vals]
                                    + [jnp.where(head0, zero, v) for v in vals], axis=0)
            o = _dot(jnp.concatenate(probs, axis=1), v_cat) * jnp.where(head0, inv[0], inv[1])
            o_ref[0, rows, cols] = (o * _silu(g_ref[0, rows, cols].astype(F32))).astype(o_ref.dtype)


def _neighbourhood(p, pc, rpb):
    b, n, _ = p.shape
    nblk = n // NA_TOK
    nstep = nblk // NA_STEP
    assert NA_STEP >= 2 and nstep >= 2
    grp = NA_W // LANES

    def key_blk(cb, shift):
        return pl.BlockSpec((1, NA_TOK, NA_W),
                            lambda bi, i: (bi, jnp.clip(NA_STEP * i + shift, 0, nblk - 1), cb // grp))

    def step_blk(cb):
        return pl.BlockSpec((1, NA_STEP * NA_TOK, NA_W), lambda bi, i: (bi, i, cb // grp))

    lc = pc.shape[1]

    def ctx(cb):
        return pl.BlockSpec((1, lc, NA_W), lambda bi, i: (bi, 0, cb // grp))

    n_dr, n_dc = 2 * WIN_H - 1, 2 * WIN_W - 1
    rpb_rows = jnp.pad(rpb.astype(F32), ((0, 0), (0, 16 - n_dr), (0, LANES - n_dc)))
    shifts = range(-1, NA_STEP + 1)
    return pl.pallas_call(
        functools.partial(_na_kernel, plan=_na_bias_plan(n // GRID_W)),
        grid=(b, nstep),
        in_specs=([step_blk(CB_NQ)] + [key_blk(CB_NK, s) for s in shifts] + [key_blk(CB_NV, s) for s in shifts]
                  + [ctx(CB_NK), ctx(CB_NV), step_blk(CB_NG),
                     pl.BlockSpec((NA_HEADS, 16, LANES), lambda bi, i: (0, 0, 0))]),
        out_specs=pl.BlockSpec((1, NA_STEP * NA_TOK, NA_W), lambda bi, i: (bi, i, 0)),
        out_shape=jax.ShapeDtypeStruct((b, n, NA_W), BF16),
        scratch_shapes=[pltpu.VMEM((3, NA_HEADS, NA_TOK, 3 * NA_TOK), F32),
                        pltpu.VMEM((n_dr, GRID_W, LANES), F32)],
        compiler_params=pltpu.CompilerParams(dimension_semantics=("arbitrary", "arbitrary")),
        name="neighbourhood",
    )(p, *([p] * (2 * len(shifts))), pc, pc, p, rpb_rows)


def _out_kernel(x_ref, mod_ref, u_ref, gz_ref, ub_ref, ua_ref,
                yr_ref, yn_ref, cw_ref, cbias_ref, w_ref, lg_ref, lb_ref, o_ref, *, mod_row):
    i = pl.program_id(1)
    last = pl.num_programs(1) - 1
    u = u_ref[0].astype(F32)
    tm = u.shape[0]
    halo = ub_ref.shape[1]
    u_before = jnp.where(i == 0, 0.0, ub_ref[0, halo - 1:halo, :].astype(F32))
    u_after = jnp.where(i == last, 0.0, ua_ref[0, 0:1, :].astype(F32))
    row = lax.broadcasted_iota(jnp.int32, u.shape, 0)
    u_prev = jnp.where(row == 0, u_before, pltpu.roll(u, 1, axis=0))
    u_next = jnp.where(row == tm - 1, u_after, pltpu.roll(u, tm - 1, axis=0))
    cw = cw_ref[...]
    row = pl.program_id(0) if mod_row is None else mod_row
    g = mod_ref[pl.ds(row, 1), 2 * D_MODEL:] * (1.0 / DEEPNORM_ALPHA)
    w = w_ref[...].astype(BF16)
    sub = min(OUT_SUB, tm)
    for r in range(tm // sub):
        rs = slice(r * sub, (r + 1) * sub)
        conv = u_prev[rs] * cw[0:1] + u[rs] * cw[1:2] + u_next[rs] * cw[2:3] + cbias_ref[...]
        y_conv = gz_ref[0, rs, :].astype(F32) * conv
        y = jnp.concatenate([y_conv.astype(BF16), yr_ref[0, rs, :], yn_ref[0, rs, :]], axis=-1)
        z = x_ref[0, rs, :] + g * _dot(y, w)
        mu = jnp.mean(z, axis=-1, keepdims=True)
        dlt = z - mu
        var = jnp.mean(dlt * dlt, axis=-1, keepdims=True)
        o_ref[0, rs, :] = dlt * lax.rsqrt(var + LN_EPS / DEEPNORM_ALPHA ** 2) * lg_ref[...] + lb_ref[...]


def _output(x, mod, mod_row, p, y_ret, y_na, conv_w, conv_b, w_out, layer, ln_g, ln_b):
    b, n, _ = x.shape
    tm = min(OUT_TILE, n)
    halo = 16
    per = tm // halo
    nh = n // halo

    def conv(cb):
        return pl.BlockSpec((1, tm, CONV_W), lambda bi, i: (bi, i, cb))

    def before(cb):
        return pl.BlockSpec((1, halo, CONV_W), lambda bi, i: (bi, jnp.maximum(i * per - 1, 0), cb))

    def after(cb):
        return pl.BlockSpec((1, halo, CONV_W), lambda bi, i: (bi, jnp.minimum((i + 1) * per, nh - 1), cb))

    def const(shape):
        return pl.BlockSpec(shape, lambda bi, i: (0,) * len(shape))

    return pl.pallas_call(
        functools.partial(_out_kernel, mod_row=mod_row),
        grid=(b, n // tm),
        in_specs=[pl.BlockSpec((1, tm, D_MODEL), lambda bi, i: (bi, i, 0)),
                  pl.BlockSpec((None, 8, 3 * D_MODEL), lambda bi, i: (layer, 0, 0)),
                  conv(CB_U), conv(CB_GZ), before(CB_U), after(CB_U),
                  pl.BlockSpec((1, tm, RET_W), lambda bi, i: (bi, i, 0)),
                  pl.BlockSpec((1, tm, NA_W), lambda bi, i: (bi, i, 0)),
                  const((3, CONV_W)), const((1, CONV_W)),
                  pl.BlockSpec((None, MIX_W, D_MODEL), lambda bi, i: (layer, 0, 0), pipeline_mode=pl.Buffered(1)),
                  const((1, D_MODEL)), const((1, D_MODEL))],
        out_specs=pl.BlockSpec((1, tm, D_MODEL), lambda bi, i: (bi, i, 0)),
        out_shape=jax.ShapeDtypeStruct((b, n, D_MODEL), F32),
        compiler_params=pltpu.CompilerParams(dimension_semantics=("parallel", "parallel")),
        name="output",
    )(x, mod, p, p, p, p, y_ret, y_na, conv_w, conv_b.reshape(1, CONV_W), w_out,
      ln_g.reshape(1, D_MODEL), ln_b.reshape(1, D_MODEL))


def kernel(x, c, ctx, c_ctx, w_mod, b_mod, w_in, conv_w, conv_b, ret_decay, na_rpb, w_out, ln_g, ln_b):
    b, n, d = x.shape
    act_t = jnp.pad(jnp.concatenate([c, c_ctx[None]], axis=0).T, ((0, 0), (0, 5)))
    mod = _modulation(act_t, w_mod, b_mod)
    tables = _rope_tables(n)
    xc = ctx
    for l in range(DEPTH):
        need_ctx = l < DEPTH - 1
        p, pc = _projection(x, xc, mod, w_in, l, tables, PROJ_W if need_ctx else KV_W)
        y_ret = _retention(p, pc, ret_decay[l], init_state=True)
        y_na = _neighbourhood(p, pc, na_rpb[l])
        x_new = _output(x, mod, None, p, y_ret, y_na, conv_w[l], conv_b[l], w_out, l, ln_g[l], ln_b[l])
        if need_ctx:
            yc_ret, yc_na = _retention(pc, None, ret_decay[l], init_state=False, attend=True)
            xc = _output(xc, mod, b, pc, yc_ret, yc_na, conv_w[l], conv_b[l], w_out, l, ln_g[l], ln_b[l])
        x = x_new
    return x
```

```python
import functools

import numpy as np
import jax
import jax.numpy as jnp
from jax import lax
from jax.experimental import pallas as pl
from jax.experimental.pallas import tpu as pltpu

D_MODEL = 1024
DEPTH = 2
GRID_W = 64
HEAD_DIM = 64
CONV_W = 256
RET_HEADS = 6
RET_W = RET_HEADS * HEAD_DIM
NA_HEADS = 6
NA_W = NA_HEADS * HEAD_DIM
MIX_W = CONV_W + RET_W + NA_W
RET_CHUNK = 128
WIN_H = 8
WIN_W = 16
ROPE_BASE = 10000.0
LN_EPS = 1e-5
DEEPNORM_ALPHA = (2 * DEPTH) ** 0.25
PROJ_SPLITS = (RET_W, RET_W, NA_W, NA_W, RET_W, RET_W, NA_W, NA_W, CONV_W, CONV_W, CONV_W, CONV_W)
PROJ_W = sum(PROJ_SPLITS)
KV_W = 2 * RET_W + 2 * NA_W

LANES = 128
PAIRS = RET_HEADS // 2
CB_RK, CB_RV, CB_NK, CB_NV, CB_RQ, CB_RG, CB_NQ, CB_NG = (sum(PROJ_SPLITS[:j]) // LANES for j in range(8))
CONV_OFF = PROJ_W - 4 * CONV_W
P_W = CONV_OFF + 2 * CONV_W
CB_U, CB_GZ = CONV_OFF // CONV_W, CONV_OFF // CONV_W + 1
QK_SCALE = HEAD_DIM ** -0.5
NEG = -1e30
LOG2E = 1.4426950408889634
NA_ROWS = 4
NA_TOK = NA_ROWS * GRID_W
NA_STEP = 4
RET_GROUP = 64
ROW_TILE = 1024
OUT_TILE = 2048
OUT_SUB = 256
PROJ_CHUNK = 512

F32 = jnp.float32
BF16 = jnp.bfloat16


def _silu(v):
    return v * jax.nn.sigmoid(v)


def _dot(a, b):
    return jnp.dot(a, b, preferred_element_type=F32)


def _dot_nt(a, b):
    return lax.dot_general(a, b, (((1,), (1,)), ((), ())), preferred_element_type=F32)


def _dot_tn(a, b):
    return lax.dot_general(a, b, (((0,), (0,)), ((), ())), preferred_element_type=F32)


def _lane_is_head0(shape):
    return lax.broadcasted_iota(jnp.int32, shape, len(shape) - 1) < HEAD_DIM


def _mod_kernel(act_ref, w_ref, b_ref, o_ref):
    a = _silu(act_ref[...])
    w = w_ref[0]
    bias = b_ref[0]
    for r in range(3):
        o_ref[0, r:r + 1, :] = jnp.sum(a[:, r:r + 1] * w, axis=0, keepdims=True) + bias
    o_ref[0, 3:8, :] = jnp.zeros((5, w.shape[1]), F32)


def _modulation(act_t, w_mod, b_mod):
    tn = 1536
    n = w_mod.shape[-1]
    return pl.pallas_call(
        _mod_kernel,
        grid=(DEPTH, n // tn),
        in_specs=[pl.BlockSpec((D_MODEL, 8), lambda l, j: (0, 0)),
                  pl.BlockSpec((1, D_MODEL, tn), lambda l, j: (l, 0, j)),
                  pl.BlockSpec((1, 1, tn), lambda l, j: (l, 0, j))],
        out_specs=pl.BlockSpec((1, 8, tn), lambda l, j: (l, 0, j)),
        out_shape=jax.ShapeDtypeStruct((DEPTH, 8, n), F32),
        name="modulation",
    )(act_t, w_mod, b_mod.reshape(DEPTH, 1, n))


def _rope_tables(n):
    rows = n // GRID_W
    nf = HEAD_DIM // 4
    inv = ROPE_BASE ** (-np.arange(nf, dtype=np.float64) / nf)
    lane = np.arange(LANES)
    by_row = ((lane % HEAD_DIM) < HEAD_DIM // 2)[None]
    first = ((lane % (2 * nf)) < nf)[None]
    ang_r = np.tile(np.arange(rows, dtype=np.float64)[:, None] * inv, (1, LANES // nf))
    ang_c = np.tile(np.arange(GRID_W, dtype=np.float64)[:, None] * inv, (1, LANES // nf))
    parts = []
    for ang, own in ((ang_r, by_row), (ang_c, ~by_row)):
        cos, sin = np.cos(ang), np.sin(ang)
        parts += [np.where(own, cos, 0.0), np.where(own & first, -sin, 0.0), np.where(own & ~first, sin, 0.0)]
    return tuple(jnp.asarray(t, F32) for t in parts)


def _project(x, mod_row, w_ref, tables, o_ref, n_cols):
    shift, scale = mod_row[:, :D_MODEL], mod_row[:, D_MODEL:2 * D_MODEL]
    h = (x * (1.0 + scale) + shift).astype(BF16)
    rotated = tuple(range(CB_RK, CB_RK + PAIRS)) + tuple(range(CB_RQ, CB_RQ + PAIRS))
    col_scale = {cb: QK_SCALE for cb in range(CB_RK, CB_RK + PAIRS)}
    col_scale.update({cb: QK_SCALE * LOG2E for cb in range(CB_NQ, CB_NQ + PAIRS)})
    for off in range(0, min(n_cols, CONV_OFF), PROJ_CHUNK):
        acc = _dot(h, w_ref[:, off:off + PROJ_CHUNK].astype(BF16))
        tiles = []
        for j in range(PROJ_CHUNK // LANES):
            cb = off // LANES + j
            v = acc[:, j * LANES:(j + 1) * LANES]
            if tables is not None and cb in rotated:
                cos, up, dn = tables
                v = v * cos + pltpu.roll(v, LANES - 16, axis=1) * up + pltpu.roll(v, 16, axis=1) * dn
            if cb in col_scale:
                v = v * col_scale[cb]
            tiles.append(v.astype(BF16))
        o_ref[0, :, off:off + PROJ_CHUNK] = jnp.concatenate(tiles, axis=1)
    if n_cols > CONV_OFF:
        ch, cb, cc, cz = (_dot(h, w_ref[:, CONV_OFF + j * CONV_W:CONV_OFF + (j + 1) * CONV_W].astype(BF16))
                          for j in range(4))
        o_ref[0, :, CONV_OFF:CONV_OFF + CONV_W] = (cc * ch).astype(BF16)
        o_ref[0, :, CONV_OFF + CONV_W:CONV_OFF + 2 * CONV_W] = (cb * _silu(cz)).astype(BF16)


def _proj_kernel(x_ref, xc_ref, mod_ref, w_ref, *rest, ctx_row, n_cols_ctx):
    *table_refs, o_ref, oc_ref = rest
    tables = tuple(jnp.concatenate([r_ref[g:g + 1, :] + c_ref[...] for g in range(r_ref.shape[0])], axis=0)
                   for r_ref, c_ref in zip(table_refs[:3], table_refs[3:]))
    _project(x_ref[0], mod_ref[pl.ds(pl.program_id(0), 1), :], w_ref, tables, o_ref, PROJ_W)

    @pl.when(pl.program_id(1) == 0)
    def _():
        _project(xc_ref[0], mod_ref[ctx_row:ctx_row + 1, :], w_ref, None, oc_ref, n_cols_ctx)


def _projection(x, xc, mod, w_in, layer, tables, n_cols_ctx):
    b, n, _ = x.shape
    lc = xc.shape[1]
    tm = min(ROW_TILE, n)
    pc_w = P_W if n_cols_ctx > CONV_OFF else n_cols_ctx
    in_specs = ([pl.BlockSpec((1, tm, D_MODEL), lambda bi, i: (bi, i, 0)),
                 pl.BlockSpec((1, lc, D_MODEL), lambda bi, i: (bi, 0, 0)),
                 pl.BlockSpec((None, 8, 3 * D_MODEL), lambda bi, i: (layer, 0, 0)),
                 pl.BlockSpec((None, D_MODEL, PROJ_W), lambda bi, i: (layer, 0, 0), pipeline_mode=pl.Buffered(1))]
                + [pl.BlockSpec((tm // GRID_W, LANES), lambda bi, i: (i, 0))] * 3
                + [pl.BlockSpec((GRID_W, LANES), lambda bi, i: (0, 0))] * 3)
    return pl.pallas_call(
        functools.partial(_proj_kernel, ctx_row=b, n_cols_ctx=n_cols_ctx),
        grid=(b, n // tm),
        in_specs=in_specs,
        out_specs=[pl.BlockSpec((1, tm, P_W), lambda bi, i: (bi, i, 0)),
                   pl.BlockSpec((1, lc, pc_w), lambda bi, i: (bi, 0, 0))],
        out_shape=[jax.ShapeDtypeStruct((b, n, P_W), BF16), jax.ShapeDtypeStruct((b, lc, pc_w), BF16)],
        compiler_params=pltpu.CompilerParams(dimension_semantics=("parallel", "arbitrary")),
        name="projection",
    )(x, xc, mod, w_in, *tables)


def _log_sigmoid(v):
    return jnp.minimum(v, 0.0) - jnp.log1p(jnp.exp(-jnp.abs(v)))


def _block_diag(m):
    r = lax.broadcasted_iota(jnp.int32, m.shape, 0) < HEAD_DIM
    c = lax.broadcasted_iota(jnp.int32, m.shape, 1) < HEAD_DIM
    return jnp.where(r == c, m, 0.0)


def _head_mean(x):
    head0 = _lane_is_head0(x.shape)
    s0 = jnp.sum(jnp.where(head0, x, 0.0), axis=-1, keepdims=True)
    s1 = jnp.sum(jnp.where(head0, 0.0, x), axis=-1, keepdims=True)
    return jnp.where(head0, s0, s1) * (1.0 / HEAD_DIM)


def _attend(q_ref, k_ref, v_ref, g_ref, o_ref):
    q, k, v = q_ref[0], k_ref[0], v_ref[0]
    head0 = _lane_is_head0(q.shape)
    zero = jnp.zeros_like(q)
    outs = []
    for hh in range(2):
        qh = jnp.where(head0, q, zero) if hh == 0 else jnp.where(head0, zero, q)
        s = _dot_nt(qh, k)
        p = jnp.exp2(s - jnp.max(s, axis=-1, keepdims=True))
        outs.append(_dot(p.astype(BF16), v) / jnp.sum(p, axis=-1, keepdims=True))
    o = jnp.where(head0, outs[0], outs[1])
    o_ref[0] = (o * _silu(g_ref[0].astype(F32))).astype(o_ref.dtype)


def _ret_kernel(*refs, init_state, attend, group):
    refs = list(refs)
    dec_ref, q_ref, k_ref, v_ref, g_ref = refs[:5]
    refs = refs[5:]
    if init_state:
        kc_ref, vc_ref = refs[:2]
        refs = refs[2:]
    if attend:
        attn_refs, refs = refs[:4], refs[4:]
        o_ref, ao_ref, s_ref, u_ref, kt_ref, wq_ref, wkt_ref, d_ref = refs
        _attend(*attn_refs, ao_ref)
    else:
        o_ref, s_ref, u_ref, kt_ref, wq_ref, wkt_ref, d_ref = refs
    c = RET_CHUNK
    n_chunks = q_ref.shape[1] // c
    n_groups = n_chunks // group
    head0 = _lane_is_head0((c, LANES))
    lg_f = _log_sigmoid(dec_ref[0, 0])
    lg_b = _log_sigmoid(dec_ref[1, 0])
    i = lax.broadcasted_iota(jnp.int32, (c, LANES), 0).astype(F32)
    wq_ref[:, :LANES] = jnp.exp((i + 1.0) * lg_f)
    wq_ref[:, LANES:] = jnp.exp((c - i) * lg_b)
    wkt_ref[:LANES, :] = jnp.exp((c - 1.0 - i) * lg_f).T
    wkt_ref[LANES:, :] = jnp.exp(i * lg_b).T
    gc_f = jnp.exp(float(c) * lg_f)
    gc_b = jnp.exp(float(c) * lg_b)
    diff = (lax.broadcasted_iota(jnp.int32, (c, c), 0) - lax.broadcasted_iota(jnp.int32, (c, c), 1)).astype(F32)
    lower = diff >= 0
    for hh in range(2):
        lf = lg_f[:, hh * HEAD_DIM:hh * HEAD_DIM + 1]
        lb = lg_b[:, hh * HEAD_DIM:hh * HEAD_DIM + 1]
        d_ref[:, hh * c:(hh + 1) * c] = jnp.where(lower, jnp.exp(jnp.where(lower, diff, 0.0) * lf),
                                                  jnp.exp(jnp.where(lower, 0.0, -diff) * lb))
    if init_state:
        lc = kc_ref.shape[1]
        m = lax.broadcasted_iota(jnp.int32, (lc, LANES), 0).astype(F32)
        kcf = kc_ref[0].astype(F32)
        s_f0 = _block_diag(_dot_tn((kcf * jnp.exp((lc - 1.0 - m) * lg_f)).astype(BF16), vc_ref[0]))
        s_b0 = _block_diag(_dot_tn((kcf * jnp.exp(m * lg_b)).astype(BF16), vc_ref[0]))
    else:
        s_f0 = s_b0 = jnp.zeros((LANES, LANES), F32)
    r2 = lax.broadcasted_iota(jnp.int32, (2 * LANES, LANES), 0) % LANES < HEAD_DIM
    c2 = lax.broadcasted_iota(jnp.int32, (2 * LANES, LANES), 1) < HEAD_DIM
    diag2 = r2 == c2

    chan0 = lax.broadcasted_iota(jnp.int32, (LANES, c), 0) < HEAD_DIM

    def chunk(n):
        return pl.ds(pl.multiple_of(n * c, c), c)

    def increments(gi, carry):
        for j in range(group):
            n = gi * group + j
            ktb = k_ref[0, chunk(n), :].T
            kt = ktb.astype(F32)
            zero = jnp.zeros_like(ktb)
            kt_ref[n] = jnp.concatenate([jnp.where(chan0, ktb, zero), jnp.where(chan0, zero, ktb)], axis=1)
            lhs = jnp.concatenate([kt * wkt_ref[:LANES, :], kt * wkt_ref[LANES:, :]], axis=0).astype(BF16)
            u_ref[n] = jnp.where(diag2, _dot(lhs, v_ref[0, chunk(n), :]), 0.0)
        return carry

    lax.fori_loop(0, n_groups, increments, 0)

    def fwd_scan(gi, s):
        for j in range(group):
            n = gi * group + j
            s_ref[n, :LANES, :] = s.astype(BF16)
            s = s * gc_f + u_ref[n, :LANES, :]
        return s

    def bwd_scan(gi, s):
        for j in range(group):
            n = n_chunks - 1 - (gi * group + j)
            s_ref[n, LANES:, :] = s.astype(BF16)
            s = s * gc_b + u_ref[n, LANES:, :]
        return s

    lax.fori_loop(0, n_groups, fwd_scan, s_f0)
    lax.fori_loop(0, n_groups, bwd_scan, s_b0)

    def body(gi, carry):
        outs = []
        for j in range(group):
            n = gi * group + j
            rows = chunk(n)
            q, v = q_ref[0, rows, :], v_ref[0, rows, :]
            zero = jnp.zeros_like(v)
            vcat = jnp.concatenate([jnp.where(head0, v, zero), jnp.where(head0, zero, v)], axis=0)
            scores = _dot(q, kt_ref[n]) * d_ref[...]
            qf = q.astype(F32)
            qw = jnp.concatenate([qf * wq_ref[:, :LANES], qf * wq_ref[:, LANES:]], axis=1).astype(BF16)
            outs.append(_dot(scores.astype(BF16), vcat) + _dot(qw, s_ref[n]))
        o = jnp.concatenate(outs, axis=0)
        rows = pl.ds(pl.multiple_of(gi * (group * c), group * c), group * c)
        dlt = o - _head_mean(o)
        var = _head_mean(dlt * dlt)
        y = dlt * lax.rsqrt(var + LN_EPS) * _silu(g_ref[0, rows, :].astype(F32))
        o_ref[0, rows, :] = y.astype(o_ref.dtype)
        return carry

    lax.fori_loop(0, n_groups, body, 0)


def _retention(p, pc, ret_decay, init_state, attend=False):
    b, n, _ = p.shape
    dec = jnp.repeat(ret_decay.astype(F32), HEAD_DIM, axis=-1).reshape(2, PAIRS, 1, LANES)

    def col(cb):
        return pl.BlockSpec((1, n, LANES), lambda bi, pi: (bi, 0, cb + pi))

    in_specs = [pl.BlockSpec((2, 1, 1, LANES), lambda bi, pi: (0, pi, 0, 0)),
                col(CB_RQ), col(CB_RK), col(CB_RV), col(CB_RG)]
    args = [dec, p, p, p, p]
    if init_state:
        lc = pc.shape[1]
        in_specs += [pl.BlockSpec((1, lc, LANES), lambda bi, pi: (bi, 0, CB_RK + pi)),
                     pl.BlockSpec((1, lc, LANES), lambda bi, pi: (bi, 0, CB_RV + pi))]
        args += [pc, pc]
    out_specs, out_shape = col(0), jax.ShapeDtypeStruct((b, n, RET_W), BF16)
    if attend:
        in_specs += [col(CB_NQ), col(CB_NK), col(CB_NV), col(CB_NG)]
        args += [p, p, p, p]
        out_specs, out_shape = [out_specs, col(0)], [out_shape, jax.ShapeDtypeStruct((b, n, NA_W), BF16)]
    c = RET_CHUNK
    return pl.pallas_call(
        functools.partial(_ret_kernel, init_state=init_state, attend=attend, group=min(RET_GROUP, n // c)),
        grid=(b, PAIRS),
        in_specs=in_specs,
        out_specs=out_specs,
        out_shape=out_shape,
        scratch_shapes=[pltpu.VMEM((n // c, 2 * LANES, LANES), BF16),
                        pltpu.VMEM((n // c, 2 * LANES, LANES), F32),
                        pltpu.VMEM((n // c, LANES, 2 * c), BF16),
                        pltpu.VMEM((c, 2 * LANES), F32),
                        pltpu.VMEM((2 * LANES, c), F32),
                        pltpu.VMEM((c, 2 * c), F32)],
        compiler_params=pltpu.CompilerParams(dimension_semantics=("parallel", "parallel")),
        name="retention",
    )(*args)


def _na_bias_plan(rows):
    nblk = rows // NA_ROWS
    win_h = min(WIN_H, rows)
    plan = np.full((3, NA_ROWS, 3, NA_ROWS), -1, np.int64)
    for kind, blk in enumerate((0, 1, nblk - 1)):
        for qr in range(NA_ROWS):
            r = blk * NA_ROWS + qr
            rs = min(max(r - win_h // 2, 0), rows - win_h)
            for kb in range(3):
                if not 0 <= blk + kb - 1 <= nblk - 1:
                    continue
                for krl in range(NA_ROWS):
                    kr = (blk + kb - 1) * NA_ROWS + krl
                    if rs <= kr < rs + win_h:
                        plan[kind, qr, kb, krl] = kr - r + WIN_H - 1
    return plan


def _fill_na_bias(rpb_ref, bias_ref, t_ref, plan):
    n_dr = 2 * WIN_H - 1
    qc = lax.broadcasted_iota(jnp.int32, (GRID_W, LANES), 0)
    lane = lax.broadcasted_iota(jnp.int32, (GRID_W, LANES), 1)
    kc = lane % GRID_W
    cs = jnp.clip(qc - WIN_W // 2, 0, GRID_W - WIN_W)
    col_ok = (kc >= cs) & (kc < cs + WIN_W)
    first = lane < GRID_W
    neg = jnp.full((GRID_W, LANES), NEG, F32)
    for h in range(NA_HEADS):
        for dr in range(n_dr):
            row = jnp.broadcast_to(rpb_ref[h, dr:dr + 1, :] * LOG2E, (GRID_W, LANES))
            lo = pltpu.roll(row, LANES - (WIN_W - 1), 1, stride=1, stride_axis=0)
            hi = pltpu.roll(row, GRID_W - (WIN_W - 1), 1, stride=1, stride_axis=0)
            t_ref[dr] = jnp.where(col_ok, jnp.where(first, lo, hi), neg)
        for kind in range(3):
            for qr in range(NA_ROWS):
                for kb in range(3):
                    for pr in range(NA_ROWS // 2):
                        ia, ib = (int(plan[kind, qr, kb, 2 * pr + s]) for s in range(2))
                        a = t_ref[ia] if ia >= 0 else neg
                        b = t_ref[ib] if ib >= 0 else neg
                        col0 = kb * NA_TOK + pr * LANES
                        bias_ref[kind, h, qr * GRID_W:(qr + 1) * GRID_W, col0:col0 + LANES] = jnp.where(first, a, b)


def _fold_lanes(blocks, op):
    tiles = [blk[:, j:j + LANES] for blk in blocks for j in range(0, blk.shape[1], LANES)]
    acc = tiles[0]
    for t in tiles[1:]:
        acc = op(acc, t)
    return acc


def _na_kernel(*refs, plan):
    nkb = NA_STEP + 2
    q_ref, k_refs, v_refs = refs[0], refs[1:1 + nkb], refs[1 + nkb:1 + 2 * nkb]
    kx_ref, vx_ref, g_ref, rpb_ref, o_ref, bias_ref, t_ref = refs[1 + 2 * nkb:]
    i, last = pl.program_id(1), pl.num_programs(1) - 1

    @pl.when((pl.program_id(0) == 0) & (i == 0))
    def _():
        _fill_na_bias(rpb_ref, bias_ref, t_ref, plan)

    head0 = _lane_is_head0((NA_TOK, LANES))
    for j in range(NA_STEP):
        edge = 0 if j == 0 else 2 if j == NA_STEP - 1 else None
        kind = 1 if edge is None else jnp.where(i == (0 if edge == 0 else last), edge, 1)
        rows = slice(j * NA_TOK, (j + 1) * NA_TOK)
        for pi in range(PAIRS):
            cols = slice(pi * LANES, (pi + 1) * LANES)
            q = q_ref[0, rows, cols]
            zero = jnp.zeros_like(q)
            keys = [r[0, :, cols] for r in k_refs[j:j + 3]] + [kx_ref[0, :, cols]]
            vals = [r[0, :, cols] for r in v_refs[j:j + 3]] + [vx_ref[0, :, cols]]
            probs, inv = [], []
            for hh in range(2):
                qh = jnp.where(head0, q, zero) if hh == 0 else jnp.where(head0, zero, q)
                s = [_dot_nt(qh, keys[t]) + bias_ref[kind, 2 * pi + hh, :, t * NA_TOK:(t + 1) * NA_TOK] for t in range(3)]
                s.append(_dot_nt(qh, keys[3]))
                m = jnp.max(_fold_lanes(s, jnp.maximum), axis=-1, keepdims=True)
                p = [jnp.exp2(st - m) for st in s]
                inv.append(1.0 / jnp.sum(_fold_lanes(p, jnp.add), axis=-1, keepdims=True))
                probs += [pt.astype(BF16) for pt in p]
            v_cat = jnp.concatenate([jnp.where(head0, v, zero) for v in vals]
                                    + [jnp.where(head0, zero, v) for v in vals], axis=0)
            o = _dot(jnp.concatenate(probs, axis=1), v_cat) * jnp.where(head0, inv[0], inv[1])
            o_ref[0, rows, cols] = (o * _silu(g_ref[0, rows, cols].astype(F32))).astype(o_ref.dtype)


def _neighbourhood(p, pc, rpb):
    b, n, _ = p.shape
    nblk = n // NA_TOK
    nstep = nblk // NA_STEP
    assert NA_STEP >= 2 and nstep >= 2
    grp = NA_W // LANES

    def key_blk(cb, shift):
        return pl.BlockSpec((1, NA_TOK, NA_W),
                            lambda bi, i: (bi, jnp.clip(NA_STEP * i + shift, 0, nblk - 1), cb // grp))

    def step_blk(cb):
        return pl.BlockSpec((1, NA_STEP * NA_TOK, NA_W), lambda bi, i: (bi, i, cb // grp))

    lc = pc.shape[1]

    def ctx(cb):
        return pl.BlockSpec((1, lc, NA_W), lambda bi, i: (bi, 0, cb // grp))

    n_dr, n_dc = 2 * WIN_H - 1, 2 * WIN_W - 1
    rpb_rows = jnp.pad(rpb.astype(F32), ((0, 0), (0, 16 - n_dr), (0, LANES - n_dc)))
    shifts = range(-1, NA_STEP + 1)
    return pl.pallas_call(
        functools.partial(_na_kernel, plan=_na_bias_plan(n // GRID_W)),
        grid=(b, nstep),
        in_specs=([step_blk(CB_NQ)] + [key_blk(CB_NK, s) for s in shifts] + [key_blk(CB_NV, s) for s in shifts]
                  + [ctx(CB_NK), ctx(CB_NV), step_blk(CB_NG),
                     pl.BlockSpec((NA_HEADS, 16, LANES), lambda bi, i: (0, 0, 0))]),
        out_specs=pl.BlockSpec((1, NA_STEP * NA_TOK, NA_W), lambda bi, i: (bi, i, 0)),
        out_shape=jax.ShapeDtypeStruct((b, n, NA_W), BF16),
        scratch_shapes=[pltpu.VMEM((3, NA_HEADS, NA_TOK, 3 * NA_TOK), F32),
                        pltpu.VMEM((n_dr, GRID_W, LANES), F32)],
        compiler_params=pltpu.CompilerParams(dimension_semantics=("arbitrary", "arbitrary")),
        name="neighbourhood",
    )(p, *([p] * (2 * len(shifts))), pc, pc, p, rpb_rows)


def _mix_out(x_ref, g, u_ref, gz_ref, u_before, u_after, yr_ref, yn_ref, cw, cbias, w, lg, lb, o_ref):
    u = u_ref[0].astype(F32)
    tm = u.shape[0]
    row = lax.broadcasted_iota(jnp.int32, u.shape, 0)
    u_prev = jnp.where(row == 0, u_before, pltpu.roll(u, 1, axis=0))
    u_next = jnp.where(row == tm - 1, u_after, pltpu.roll(u, tm - 1, axis=0))
    sub = min(OUT_SUB, tm)
    for r in range(tm // sub):
        rs = slice(r * sub, (r + 1) * sub)
        conv = u_prev[rs] * cw[0:1] + u[rs] * cw[1:2] + u_next[rs] * cw[2:3] + cbias
        y_conv = gz_ref[0, rs, :].astype(F32) * conv
        y = jnp.concatenate([y_conv.astype(BF16), yr_ref[0, rs, :], yn_ref[0, rs, :]], axis=-1)
        z = x_ref[0, rs, :] + g * _dot(y, w)
        mu = jnp.mean(z, axis=-1, keepdims=True)
        dlt = z - mu
        var = jnp.mean(dlt * dlt, axis=-1, keepdims=True)
        o_ref[0, rs, :] = dlt * lax.rsqrt(var + LN_EPS / DEEPNORM_ALPHA ** 2) * lg + lb


def _out_kernel(*refs, ctx_row):
    x_ref, mod_ref, u_ref, gz_ref, ub_ref, ua_ref, yr_ref, yn_ref = refs[:8]
    n_out = 1 if ctx_row is None else 2
    cw_ref, cbias_ref, w_ref, lg_ref, lb_ref = refs[-5 - n_out:-n_out]
    i = pl.program_id(1)
    halo = ub_ref.shape[1]
    u_before = jnp.where(i == 0, 0.0, ub_ref[0, halo - 1:halo, :].astype(F32))
    u_after = jnp.where(i == pl.num_programs(1) - 1, 0.0, ua_ref[0, 0:1, :].astype(F32))
    w = w_ref[...].astype(BF16)
    consts = (cw_ref[...], cbias_ref[...], w, lg_ref[...], lb_ref[...])

    def gate(row):
        return mod_ref[pl.ds(row, 1), 2 * D_MODEL:] * (1.0 / DEEPNORM_ALPHA)

    if ctx_row is None:
        _mix_out(x_ref, gate(pl.program_id(0)), u_ref, gz_ref, u_before, u_after, yr_ref, yn_ref, *consts, refs[-1])
    else:
        xc_ref, uc_ref, gzc_ref, yrc_ref, ync_ref = refs[8:13]
        o_ref, oc_ref = refs[-2:]
        _mix_out(x_ref, gate(pl.program_id(0)), u_ref, gz_ref, u_before, u_after, yr_ref, yn_ref, *consts, o_ref)

        @pl.when(i == 0)
        def _():
            _mix_out(xc_ref, gate(ctx_row), uc_ref, gzc_ref, 0.0, 0.0, yrc_ref, ync_ref, *consts, oc_ref)


def _output(x, mod, p, y_ret, y_na, conv_w, conv_b, w_out, layer, ln_g, ln_b, ctx=None):
    b, n, _ = x.shape
    tm = min(OUT_TILE, n)
    halo = 16
    per = tm // halo
    nh = n // halo

    def rows(width, cb=0):
        return pl.BlockSpec((1, tm, width), lambda bi, i: (bi, i, cb))

    def before(cb):
        return pl.BlockSpec((1, halo, CONV_W), lambda bi, i: (bi, jnp.maximum(i * per - 1, 0), cb))

    def after(cb):
        return pl.BlockSpec((1, halo, CONV_W), lambda bi, i: (bi, jnp.minimum((i + 1) * per, nh - 1), cb))

    def const(shape):
        return pl.BlockSpec(shape, lambda bi, i: (0,) * len(shape))

    in_specs = [rows(D_MODEL), pl.BlockSpec((None, 8, 3 * D_MODEL), lambda bi, i: (layer, 0, 0)),
                rows(CONV_W, CB_U), rows(CONV_W, CB_GZ), before(CB_U), after(CB_U), rows(RET_W), rows(NA_W)]
    args = [x, mod, p, p, p, p, y_ret, y_na]
    out_specs, out_shape = rows(D_MODEL), jax.ShapeDtypeStruct((b, n, D_MODEL), F32)
    if ctx is not None:
        xc, pc, yc_ret, yc_na = ctx
        lc = xc.shape[1]

        def seq(width, cb=0):
            return pl.BlockSpec((1, lc, width), lambda bi, i: (bi, 0, cb))

        in_specs += [seq(D_MODEL), seq(CONV_W, CB_U), seq(CONV_W, CB_GZ), seq(RET_W), seq(NA_W)]
        args += [xc, pc, pc, yc_ret, yc_na]
        out_specs, out_shape = [out_specs, seq(D_MODEL)], [out_shape, jax.ShapeDtypeStruct((b, lc, D_MODEL), F32)]
    in_specs += [const((3, CONV_W)), const((1, CONV_W)),
                 pl.BlockSpec((None, MIX_W, D_MODEL), lambda bi, i: (layer, 0, 0), pipeline_mode=pl.Buffered(1)),
                 const((1, D_MODEL)), const((1, D_MODEL))]
    args += [conv_w, conv_b.reshape(1, CONV_W), w_out, ln_g.reshape(1, D_MODEL), ln_b.reshape(1, D_MODEL)]
    return pl.pallas_call(
        functools.partial(_out_kernel, ctx_row=None if ctx is None else b),
        grid=(b, n // tm),
        in_specs=in_specs,
        out_specs=out_specs,
        out_shape=out_shape,
        compiler_params=pltpu.CompilerParams(dimension_semantics=("parallel", "arbitrary")),
        name="output",
    )(*args)


def kernel(x, c, ctx, c_ctx, w_mod, b_mod, w_in, conv_w, conv_b, ret_decay, na_rpb, w_out, ln_g, ln_b):
    b, n, d = x.shape
    act_t = jnp.pad(jnp.concatenate([c, c_ctx[None]], axis=0).T, ((0, 0), (0, 5)))
    mod = _modulation(act_t, w_mod, b_mod)
    tables = _rope_tables(n)
    xc = ctx
    for l in range(DEPTH):
        need_ctx = l < DEPTH - 1
        p, pc = _projection(x, xc, mod, w_in, l, tables, PROJ_W if need_ctx else KV_W)
        y_ret = _retention(p, pc, ret_decay[l], init_state=True)
        y_na = _neighbourhood(p, pc, na_rpb[l])
        ctx_mix = (xc, pc) + tuple(_retention(pc, None, ret_decay[l], init_state=False, attend=True)) if need_ctx else None
        out = _output(x, mod, p, y_ret, y_na, conv_w[l], conv_b[l], w_out, l, ln_g[l], ln_b[l], ctx_mix)
        x, xc = out if need_ctx else (out, None)
    return x
```

```python
import functools

import numpy as np
import jax
import jax.numpy as jnp
from jax import lax
from jax.experimental import pallas as pl
from jax.experimental.pallas import tpu as pltpu

D_MODEL = 1024
DEPTH = 2
GRID_W = 64
HEAD_DIM = 64
CONV_W = 256
RET_HEADS = 6
RET_W = RET_HEADS * HEAD_DIM
NA_HEADS = 6
NA_W = NA_HEADS * HEAD_DIM
MIX_W = CONV_W + RET_W + NA_W
RET_CHUNK = 128
WIN_H = 8
WIN_W = 16
ROPE_BASE = 10000.0
LN_EPS = 1e-5
DEEPNORM_ALPHA = (2 * DEPTH) ** 0.25
PROJ_SPLITS = (RET_W, RET_W, NA_W, NA_W, RET_W, RET_W, NA_W, NA_W, CONV_W, CONV_W, CONV_W, CONV_W)
PROJ_W = sum(PROJ_SPLITS)
KV_W = 2 * RET_W + 2 * NA_W

LANES = 128
PAIRS = RET_HEADS // 2
CB_RK, CB_RV, CB_NK, CB_NV, CB_RQ, CB_RG, CB_NQ, CB_NG = (sum(PROJ_SPLITS[:j]) // LANES for j in range(8))
CONV_OFF = PROJ_W - 4 * CONV_W
P_W = CONV_OFF + 2 * CONV_W
CB_U, CB_GZ = CONV_OFF // CONV_W, CONV_OFF // CONV_W + 1
QK_SCALE = HEAD_DIM ** -0.5
NEG = -1e30
LOG2E = 1.4426950408889634
NA_ROWS = 4
NA_TOK = NA_ROWS * GRID_W
NA_STEP = 4
RET_GROUP = 64
ROW_TILE = 1024
OUT_TILE = 2048
OUT_SUB = 256
PROJ_CHUNK = 512

F32 = jnp.float32
BF16 = jnp.bfloat16


def _silu(v):
    return v * jax.nn.sigmoid(v)


def _dot(a, b):
    return jnp.dot(a, b, preferred_element_type=F32)


def _dot_nt(a, b):
    return lax.dot_general(a, b, (((1,), (1,)), ((), ())), preferred_element_type=F32)


def _dot_tn(a, b):
    return lax.dot_general(a, b, (((0,), (0,)), ((), ())), preferred_element_type=F32)


def _lane_is_head0(shape):
    return lax.broadcasted_iota(jnp.int32, shape, len(shape) - 1) < HEAD_DIM


def _mod_kernel(act_ref, w_ref, b_ref, o_ref):
    a = _silu(act_ref[...])
    w = w_ref[0]
    bias = b_ref[0]
    for r in range(3):
        o_ref[0, r:r + 1, :] = jnp.sum(a[:, r:r + 1] * w, axis=0, keepdims=True) + bias
    o_ref[0, 3:8, :] = jnp.zeros((5, w.shape[1]), F32)


def _modulation(act_t, w_mod, b_mod):
    tn = 1536
    n = w_mod.shape[-1]
    return pl.pallas_call(
        _mod_kernel,
        grid=(DEPTH, n // tn),
        in_specs=[pl.BlockSpec((D_MODEL, 8), lambda l, j: (0, 0)),
                  pl.BlockSpec((1, D_MODEL, tn), lambda l, j: (l, 0, j)),
                  pl.BlockSpec((1, 1, tn), lambda l, j: (l, 0, j))],
        out_specs=pl.BlockSpec((1, 8, tn), lambda l, j: (l, 0, j)),
        out_shape=jax.ShapeDtypeStruct((DEPTH, 8, n), F32),
        name="modulation",
    )(act_t, w_mod, b_mod.reshape(DEPTH, 1, n))


def _rope_tables(n):
    rows = n // GRID_W
    nf = HEAD_DIM // 4
    inv = ROPE_BASE ** (-np.arange(nf, dtype=np.float64) / nf)
    lane = np.arange(LANES)
    by_row = ((lane % HEAD_DIM) < HEAD_DIM // 2)[None]
    first = ((lane % (2 * nf)) < nf)[None]
    ang_r = np.tile(np.arange(rows, dtype=np.float64)[:, None] * inv, (1, LANES // nf))
    ang_c = np.tile(np.arange(GRID_W, dtype=np.float64)[:, None] * inv, (1, LANES // nf))
    parts = []
    for ang, own in ((ang_r, by_row), (ang_c, ~by_row)):
        cos, sin = np.cos(ang), np.sin(ang)
        parts += [np.where(own, cos, 0.0), np.where(own & first, -sin, 0.0), np.where(own & ~first, sin, 0.0)]
    return tuple(jnp.asarray(t, F32) for t in parts)


def _project(x, mod_row, w_ref, tables, o_ref, n_cols):
    shift, scale = mod_row[:, :D_MODEL], mod_row[:, D_MODEL:2 * D_MODEL]
    h = (x * (1.0 + scale) + shift).astype(BF16)
    rotated = tuple(range(CB_RK, CB_RK + PAIRS)) + tuple(range(CB_RQ, CB_RQ + PAIRS))
    col_scale = {cb: QK_SCALE for cb in range(CB_RK, CB_RK + PAIRS)}
    col_scale.update({cb: QK_SCALE * LOG2E for cb in range(CB_NQ, CB_NQ + PAIRS)})
    for off in range(0, min(n_cols, CONV_OFF), PROJ_CHUNK):
        acc = _dot(h, w_ref[:, off:off + PROJ_CHUNK].astype(BF16))
        tiles = []
        for j in range(PROJ_CHUNK // LANES):
            cb = off // LANES + j
            v = acc[:, j * LANES:(j + 1) * LANES]
            if tables is not None and cb in rotated:
                cos, up, dn = tables
                v = v * cos + pltpu.roll(v, LANES - 16, axis=1) * up + pltpu.roll(v, 16, axis=1) * dn
            if cb in col_scale:
                v = v * col_scale[cb]
            tiles.append(v.astype(BF16))
        o_ref[0, :, off:off + PROJ_CHUNK] = jnp.concatenate(tiles, axis=1)
    if n_cols > CONV_OFF:
        ch, cb, cc, cz = (_dot(h, w_ref[:, CONV_OFF + j * CONV_W:CONV_OFF + (j + 1) * CONV_W].astype(BF16))
                          for j in range(4))
        o_ref[0, :, CONV_OFF:CONV_OFF + CONV_W] = (cc * ch).astype(BF16)
        o_ref[0, :, CONV_OFF + CONV_W:CONV_OFF + 2 * CONV_W] = (cb * _silu(cz)).astype(BF16)


def _proj_kernel(x_ref, xc_ref, mod_ref, w_ref, *rest, ctx_row, n_cols_ctx):
    *table_refs, o_ref, oc_ref = rest
    tables = tuple(jnp.concatenate([r_ref[g:g + 1, :] + c_ref[...] for g in range(r_ref.shape[0])], axis=0)
                   for r_ref, c_ref in zip(table_refs[:3], table_refs[3:]))
    _project(x_ref[0], mod_ref[pl.ds(pl.program_id(0), 1), :], w_ref, tables, o_ref, PROJ_W)

    @pl.when(pl.program_id(1) == 0)
    def _():
        _project(xc_ref[0], mod_ref[ctx_row:ctx_row + 1, :], w_ref, None, oc_ref, n_cols_ctx)


def _projection(x, xc, mod, w_in, layer, tables, n_cols_ctx):
    b, n, _ = x.shape
    lc = xc.shape[1]
    tm = min(ROW_TILE, n)
    pc_w = P_W if n_cols_ctx > CONV_OFF else n_cols_ctx
    in_specs = ([pl.BlockSpec((1, tm, D_MODEL), lambda bi, i: (bi, i, 0)),
                 pl.BlockSpec((1, lc, D_MODEL), lambda bi, i: (bi, 0, 0)),
                 pl.BlockSpec((None, 8, 3 * D_MODEL), lambda bi, i: (layer, 0, 0)),
                 pl.BlockSpec((None, D_MODEL, PROJ_W), lambda bi, i: (layer, 0, 0), pipeline_mode=pl.Buffered(1))]
                + [pl.BlockSpec((tm // GRID_W, LANES), lambda bi, i: (i, 0))] * 3
                + [pl.BlockSpec((GRID_W, LANES), lambda bi, i: (0, 0))] * 3)
    return pl.pallas_call(
        functools.partial(_proj_kernel, ctx_row=b, n_cols_ctx=n_cols_ctx),
        grid=(b, n // tm),
        in_specs=in_specs,
        out_specs=[pl.BlockSpec((1, tm, P_W), lambda bi, i: (bi, i, 0)),
                   pl.BlockSpec((1, lc, pc_w), lambda bi, i: (bi, 0, 0))],
        out_shape=[jax.ShapeDtypeStruct((b, n, P_W), BF16), jax.ShapeDtypeStruct((b, lc, pc_w), BF16)],
        compiler_params=pltpu.CompilerParams(dimension_semantics=("parallel", "arbitrary")),
        name="projection",
    )(x, xc, mod, w_in, *tables)


def _log_sigmoid(v):
    return jnp.minimum(v, 0.0) - jnp.log1p(jnp.exp(-jnp.abs(v)))


def _block_diag(m):
    r = lax.broadcasted_iota(jnp.int32, m.shape, 0) < HEAD_DIM
    c = lax.broadcasted_iota(jnp.int32, m.shape, 1) < HEAD_DIM
    return jnp.where(r == c, m, 0.0)


def _head_mean(x):
    head0 = _lane_is_head0(x.shape)
    s0 = jnp.sum(jnp.where(head0, x, 0.0), axis=-1, keepdims=True)
    s1 = jnp.sum(jnp.where(head0, 0.0, x), axis=-1, keepdims=True)
    return jnp.where(head0, s0, s1) * (1.0 / HEAD_DIM)


def _attend(q_ref, k_ref, v_ref, g_ref, o_ref):
    q, k, v = q_ref[0], k_ref[0], v_ref[0]
    head0 = _lane_is_head0(q.shape)
    zero = jnp.zeros_like(q)
    outs = []
    for hh in range(2):
        qh = jnp.where(head0, q, zero) if hh == 0 else jnp.where(head0, zero, q)
        s = _dot_nt(qh, k)
        p = jnp.exp2(s - jnp.max(s, axis=-1, keepdims=True))
        outs.append(_dot(p.astype(BF16), v) / jnp.sum(p, axis=-1, keepdims=True))
    o = jnp.where(head0, outs[0], outs[1])
    o_ref[0] = (o * _silu(g_ref[0].astype(F32))).astype(o_ref.dtype)


def _ret_kernel(*refs, init_state, attend, group):
    refs = list(refs)
    dec_ref, q_ref, k_ref, v_ref, g_ref = refs[:5]
    refs = refs[5:]
    if init_state:
        kc_ref, vc_ref = refs[:2]
        refs = refs[2:]
    if attend:
        attn_refs, refs = refs[:4], refs[4:]
        o_ref, ao_ref, s_ref, u_ref, kt_ref, wq_ref, wkt_ref, d_ref = refs
        _attend(*attn_refs, ao_ref)
    else:
        o_ref, s_ref, u_ref, kt_ref, wq_ref, wkt_ref, d_ref = refs
    c = RET_CHUNK
    n_chunks = q_ref.shape[1] // c
    n_groups = n_chunks // group
    head0 = _lane_is_head0((c, LANES))
    lg_f = _log_sigmoid(dec_ref[0, 0])
    lg_b = _log_sigmoid(dec_ref[1, 0])
    i = lax.broadcasted_iota(jnp.int32, (c, LANES), 0).astype(F32)
    wq_ref[:, :LANES] = jnp.exp((i + 1.0) * lg_f)
    wq_ref[:, LANES:] = jnp.exp((c - i) * lg_b)
    wkt_ref[:LANES, :] = jnp.exp((c - 1.0 - i) * lg_f).T
    wkt_ref[LANES:, :] = jnp.exp(i * lg_b).T
    gc_f = jnp.exp(float(c) * lg_f)
    gc_b = jnp.exp(float(c) * lg_b)
    diff = (lax.broadcasted_iota(jnp.int32, (c, c), 0) - lax.broadcasted_iota(jnp.int32, (c, c), 1)).astype(F32)
    lower = diff >= 0
    for hh in range(2):
        lf = lg_f[:, hh * HEAD_DIM:hh * HEAD_DIM + 1]
        lb = lg_b[:, hh * HEAD_DIM:hh * HEAD_DIM + 1]
        d_ref[:, hh * c:(hh + 1) * c] = jnp.where(lower, jnp.exp(jnp.where(lower, diff, 0.0) * lf),
                                                  jnp.exp(jnp.where(lower, 0.0, -diff) * lb))
    if init_state:
        lc = kc_ref.shape[1]
        m = lax.broadcasted_iota(jnp.int32, (lc, LANES), 0).astype(F32)
        kcf = kc_ref[0].astype(F32)
        s_f0 = _block_diag(_dot_tn((kcf * jnp.exp((lc - 1.0 - m) * lg_f)).astype(BF16), vc_ref[0]))
        s_b0 = _block_diag(_dot_tn((kcf * jnp.exp(m * lg_b)).astype(BF16), vc_ref[0]))
    else:
        s_f0 = s_b0 = jnp.zeros((LANES, LANES), F32)
    r2 = lax.broadcasted_iota(jnp.int32, (2 * LANES, LANES), 0) % LANES < HEAD_DIM
    c2 = lax.broadcasted_iota(jnp.int32, (2 * LANES, LANES), 1) < HEAD_DIM
    diag2 = r2 == c2

    chan0 = lax.broadcasted_iota(jnp.int32, (LANES, c), 0) < HEAD_DIM

    def chunk(n):
        return pl.ds(pl.multiple_of(n * c, c), c)

    def increments(gi, carry):
        for j in range(group):
            n = gi * group + j
            ktb = k_ref[0, chunk(n), :].T
            kt = ktb.astype(F32)
            zero = jnp.zeros_like(ktb)
            kt_ref[n] = jnp.concatenate([jnp.where(chan0, ktb, zero), jnp.where(chan0, zero, ktb)], axis=1)
            lhs = jnp.concatenate([kt * wkt_ref[:LANES, :], kt * wkt_ref[LANES:, :]], axis=0).astype(BF16)
            u_ref[n] = jnp.where(diag2, _dot(lhs, v_ref[0, chunk(n), :]), 0.0)
        return carry

    lax.fori_loop(0, n_groups, increments, 0)

    def fwd_scan(gi, s):
        for j in range(group):
            n = gi * group + j
            s_ref[n, :LANES, :] = s.astype(BF16)
            s = s * gc_f + u_ref[n, :LANES, :]
        return s

    def bwd_scan(gi, s):
        for j in range(group):
            n = n_chunks - 1 - (gi * group + j)
            s_ref[n, LANES:, :] = s.astype(BF16)
            s = s * gc_b + u_ref[n, LANES:, :]
        return s

    lax.fori_loop(0, n_groups, fwd_scan, s_f0)
    lax.fori_loop(0, n_groups, bwd_scan, s_b0)

    def body(gi, carry):
        outs = []
        for j in range(group):
            n = gi * group + j
            rows = chunk(n)
            q, v = q_ref[0, rows, :], v_ref[0, rows, :]
            zero = jnp.zeros_like(v)
            vcat = jnp.concatenate([jnp.where(head0, v, zero), jnp.where(head0, zero, v)], axis=0)
            scores = _dot(q, kt_ref[n]) * d_ref[...]
            qf = q.astype(F32)
            qw = jnp.concatenate([qf * wq_ref[:, :LANES], qf * wq_ref[:, LANES:]], axis=1).astype(BF16)
            outs.append(_dot(scores.astype(BF16), vcat) + _dot(qw, s_ref[n]))
        o = jnp.concatenate(outs, axis=0)
        rows = pl.ds(pl.multiple_of(gi * (group * c), group * c), group * c)
        dlt = o - _head_mean(o)
        var = _head_mean(dlt * dlt)
        y = dlt * lax.rsqrt(var + LN_EPS) * _silu(g_ref[0, rows, :].astype(F32))
        o_ref[0, rows, :] = y.astype(o_ref.dtype)
        return carry

    lax.fori_loop(0, n_groups, body, 0)


def _retention(p, pc, ret_decay, init_state, attend=False):
    b, n, _ = p.shape
    dec = jnp.repeat(ret_decay.astype(F32), HEAD_DIM, axis=-1).reshape(2, PAIRS, 1, LANES)

    def col(cb):
        return pl.BlockSpec((1, n, LANES), lambda bi, pi: (bi, 0, cb + pi))

    in_specs = [pl.BlockSpec((2, 1, 1, LANES), lambda bi, pi: (0, pi, 0, 0)),
                col(CB_RQ), col(CB_RK), col(CB_RV), col(CB_RG)]
    args = [dec, p, p, p, p]
    if init_state:
        lc = pc.shape[1]
        in_specs += [pl.BlockSpec((1, lc, LANES), lambda bi, pi: (bi, 0, CB_RK + pi)),
                     pl.BlockSpec((1, lc, LANES), lambda bi, pi: (bi, 0, CB_RV + pi))]
        args += [pc, pc]
    out_specs, out_shape = col(0), jax.ShapeDtypeStruct((b, n, RET_W), BF16)
    if attend:
        in_specs += [col(CB_NQ), col(CB_NK), col(CB_NV), col(CB_NG)]
        args += [p, p, p, p]
        out_specs, out_shape = [out_specs, col(0)], [out_shape, jax.ShapeDtypeStruct((b, n, NA_W), BF16)]
    c = RET_CHUNK
    return pl.pallas_call(
        functools.partial(_ret_kernel, init_state=init_state, attend=attend, group=min(RET_GROUP, n // c)),
        grid=(b, PAIRS),
        in_specs=in_specs,
        out_specs=out_specs,
        out_shape=out_shape,
        scratch_shapes=[pltpu.VMEM((n // c, 2 * LANES, LANES), BF16),
                        pltpu.VMEM((n // c, 2 * LANES, LANES), F32),
                        pltpu.VMEM((n // c, LANES, 2 * c), BF16),
                        pltpu.VMEM((c, 2 * LANES), F32),
                        pltpu.VMEM((2 * LANES, c), F32),
                        pltpu.VMEM((c, 2 * c), F32)],
        compiler_params=pltpu.CompilerParams(dimension_semantics=("parallel", "parallel")),
        name="retention",
    )(*args)


def _na_bias_plan(rows):
    nblk = rows // NA_ROWS
    win_h = min(WIN_H, rows)
    plan = np.full((3, NA_ROWS, 3, NA_ROWS), -1, np.int64)
    for kind, blk in enumerate((0, 1, nblk - 1)):
        for qr in range(NA_ROWS):
            r = blk * NA_ROWS + qr
            rs = min(max(r - win_h // 2, 0), rows - win_h)
            for kb in range(3):
                if not 0 <= blk + kb - 1 <= nblk - 1:
                    continue
                for krl in range(NA_ROWS):
                    kr = (blk + kb - 1) * NA_ROWS + krl
                    if rs <= kr < rs + win_h:
                        plan[kind, qr, kb, krl] = kr - r + WIN_H - 1
    return plan


def _fill_na_bias(rpb_ref, bias_ref, t_ref, plan):
    n_dr = 2 * WIN_H - 1
    qc = lax.broadcasted_iota(jnp.int32, (GRID_W, LANES), 0)
    lane = lax.broadcasted_iota(jnp.int32, (GRID_W, LANES), 1)
    kc = lane % GRID_W
    cs = jnp.clip(qc - WIN_W // 2, 0, GRID_W - WIN_W)
    col_ok = (kc >= cs) & (kc < cs + WIN_W)
    first = lane < GRID_W
    neg = jnp.full((GRID_W, LANES), NEG, F32)
    for h in range(NA_HEADS):
        for dr in range(n_dr):
            row = jnp.broadcast_to(rpb_ref[h, dr:dr + 1, :] * LOG2E, (GRID_W, LANES))
            lo = pltpu.roll(row, LANES - (WIN_W - 1), 1, stride=1, stride_axis=0)
            hi = pltpu.roll(row, GRID_W - (WIN_W - 1), 1, stride=1, stride_axis=0)
            t_ref[dr] = jnp.where(col_ok, jnp.where(first, lo, hi), neg)
        for kind in range(3):
            for qr in range(NA_ROWS):
                for kb in range(3):
                    for pr in range(NA_ROWS // 2):
                        ia, ib = (int(plan[kind, qr, kb, 2 * pr + s]) for s in range(2))
                        a = t_ref[ia] if ia >= 0 else neg
                        b = t_ref[ib] if ib >= 0 else neg
                        col0 = kb * NA_TOK + pr * LANES
                        bias_ref[kind, h, qr * GRID_W:(qr + 1) * GRID_W, col0:col0 + LANES] = jnp.where(first, a, b)


def _fold_lanes(blocks, op):
    tiles = [blk[:, j:j + LANES] for blk in blocks for j in range(0, blk.shape[1], LANES)]
    acc = tiles[0]
    for t in tiles[1:]:
        acc = op(acc, t)
    return acc


def _na_kernel(*refs, plan):
    nkb = NA_STEP + 2
    q_ref, k_refs, v_refs = refs[0], refs[1:1 + nkb], refs[1 + nkb:1 + 2 * nkb]
    kx_ref, vx_ref, g_ref, rpb_ref, o_ref, bias_ref, t_ref = refs[1 + 2 * nkb:]
    i, last = pl.program_id(1), pl.num_programs(1) - 1

    @pl.when((pl.program_id(0) == 0) & (i == 0))
    def _():
        _fill_na_bias(rpb_ref, bias_ref, t_ref, plan)

    head0 = _lane_is_head0((NA_TOK, LANES))
    for pi in range(PAIRS):
        cols = slice(pi * LANES, (pi + 1) * LANES)
        keys = [r[0, :, cols] for r in k_refs] + [kx_ref[0, :, cols]]
        zero = jnp.zeros_like(keys[0])
        vals = [r[0, :, cols] for r in v_refs] + [vx_ref[0, :, cols]]
        v_h0 = [jnp.where(head0, v, zero) for v in vals]
        v_h1 = [jnp.where(head0, zero, v) for v in vals]
        for j in range(NA_STEP):
            edge = 0 if j == 0 else 2 if j == NA_STEP - 1 else None
            kind = 1 if edge is None else jnp.where(i == (0 if edge == 0 else last), edge, 1)
            rows = slice(j * NA_TOK, (j + 1) * NA_TOK)
            band = [j, j + 1, j + 2, nkb]
            q = q_ref[0, rows, cols]
            probs, inv = [], []
            for hh in range(2):
                qh = jnp.where(head0, q, zero) if hh == 0 else jnp.where(head0, zero, q)
                s = [_dot_nt(qh, keys[band[t]]) + bias_ref[kind, 2 * pi + hh, :, t * NA_TOK:(t + 1) * NA_TOK]
                     for t in range(3)]
                s.append(_dot_nt(qh, keys[nkb]))
                m = jnp.max(_fold_lanes(s, jnp.maximum), axis=-1, keepdims=True)
                p = [jnp.exp2(st - m) for st in s]
                inv.append(1.0 / jnp.sum(_fold_lanes(p, jnp.add), axis=-1, keepdims=True))
                probs += [pt.astype(BF16) for pt in p]
            v_cat = jnp.concatenate([v_h0[t] for t in band] + [v_h1[t] for t in band], axis=0)
            o = _dot(jnp.concatenate(probs, axis=1), v_cat) * jnp.where(head0, inv[0], inv[1])
            o_ref[0, rows, cols] = (o * _silu(g_ref[0, rows, cols].astype(F32))).astype(o_ref.dtype)


def _neighbourhood(p, pc, rpb):
    b, n, _ = p.shape
    nblk = n // NA_TOK
    nstep = nblk // NA_STEP
    assert NA_STEP >= 2 and nstep >= 2
    grp = NA_W // LANES

    def key_blk(cb, shift):
        return pl.BlockSpec((1, NA_TOK, NA_W),
                            lambda bi, i: (bi, jnp.clip(NA_STEP * i + shift, 0, nblk - 1), cb // grp))

    def step_blk(cb):
        return pl.BlockSpec((1, NA_STEP * NA_TOK, NA_W), lambda bi, i: (bi, i, cb // grp))

    lc = pc.shape[1]

    def ctx(cb):
        return pl.BlockSpec((1, lc, NA_W), lambda bi, i: (bi, 0, cb // grp))

    n_dr, n_dc = 2 * WIN_H - 1, 2 * WIN_W - 1
    rpb_rows = jnp.pad(rpb.astype(F32), ((0, 0), (0, 16 - n_dr), (0, LANES - n_dc)))
    shifts = range(-1, NA_STEP + 1)
    return pl.pallas_call(
        functools.partial(_na_kernel, plan=_na_bias_plan(n // GRID_W)),
        grid=(b, nstep),
        in_specs=([step_blk(CB_NQ)] + [key_blk(CB_NK, s) for s in shifts] + [key_blk(CB_NV, s) for s in shifts]
                  + [ctx(CB_NK), ctx(CB_NV), step_blk(CB_NG),
                     pl.BlockSpec((NA_HEADS, 16, LANES), lambda bi, i: (0, 0, 0))]),
        out_specs=pl.BlockSpec((1, NA_STEP * NA_TOK, NA_W), lambda bi, i: (bi, i, 0)),
        out_shape=jax.ShapeDtypeStruct((b, n, NA_W), BF16),
        scratch_shapes=[pltpu.VMEM((3, NA_HEADS, NA_TOK, 3 * NA_TOK), F32),
                        pltpu.VMEM((n_dr, GRID_W, LANES), F32)],
        compiler_params=pltpu.CompilerParams(dimension_semantics=("arbitrary", "arbitrary")),
        name="neighbourhood",
    )(p, *([p] * (2 * len(shifts))), pc, pc, p, rpb_rows)


def _mix_out(x_ref, g, u_ref, gz_ref, u_before, u_after, yr_ref, yn_ref, cw, cbias, w, lg, lb, o_ref):
    u = u_ref[0].astype(F32)
    tm = u.shape[0]
    row = lax.broadcasted_iota(jnp.int32, u.shape, 0)
    u_prev = jnp.where(row == 0, u_before, pltpu.roll(u, 1, axis=0))
    u_next = jnp.where(row == tm - 1, u_after, pltpu.roll(u, tm - 1, axis=0))
    sub = min(OUT_SUB, tm)
    for r in range(tm // sub):
        rs = slice(r * sub, (r + 1) * sub)
        conv = u_prev[rs] * cw[0:1] + u[rs] * cw[1:2] + u_next[rs] * cw[2:3] + cbias
        y_conv = gz_ref[0, rs, :].astype(F32) * conv
        y = jnp.concatenate([y_conv.astype(BF16), yr_ref[0, rs, :], yn_ref[0, rs, :]], axis=-1)
        z = x_ref[0, rs, :] + g * _dot(y, w)
        mu = jnp.mean(z, axis=-1, keepdims=True)
        dlt = z - mu
        var = jnp.mean(dlt * dlt, axis=-1, keepdims=True)
        o_ref[0, rs, :] = dlt * lax.rsqrt(var + LN_EPS / DEEPNORM_ALPHA ** 2) * lg + lb


def _out_kernel(*refs, ctx_row):
    x_ref, mod_ref, u_ref, gz_ref, ub_ref, ua_ref, yr_ref, yn_ref = refs[:8]
    n_out = 1 if ctx_row is None else 2
    cw_ref, cbias_ref, w_ref, lg_ref, lb_ref = refs[-5 - n_out:-n_out]
    i = pl.program_id(1)
    halo = ub_ref.shape[1]
    u_before = jnp.where(i == 0, 0.0, ub_ref[0, halo - 1:halo, :].astype(F32))
    u_after = jnp.where(i == pl.num_programs(1) - 1, 0.0, ua_ref[0, 0:1, :].astype(F32))
    w = w_ref[...].astype(BF16)
    consts = (cw_ref[...], cbias_ref[...], w, lg_ref[...], lb_ref[...])

    def gate(row):
        return mod_ref[pl.ds(row, 1), 2 * D_MODEL:] * (1.0 / DEEPNORM_ALPHA)

    if ctx_row is None:
        _mix_out(x_ref, gate(pl.program_id(0)), u_ref, gz_ref, u_before, u_after, yr_ref, yn_ref, *consts, refs[-1])
    else:
        xc_ref, uc_ref, gzc_ref, yrc_ref, ync_ref = refs[8:13]
        o_ref, oc_ref = refs[-2:]
        _mix_out(x_ref, gate(pl.program_id(0)), u_ref, gz_ref, u_before, u_after, yr_ref, yn_ref, *consts, o_ref)

        @pl.when(i == 0)
        def _():
            _mix_out(xc_ref, gate(ctx_row), uc_ref, gzc_ref, 0.0, 0.0, yrc_ref, ync_ref, *consts, oc_ref)


def _output(x, mod, p, y_ret, y_na, conv_w, conv_b, w_out, layer, ln_g, ln_b, ctx=None):
    b, n, _ = x.shape
    tm = min(OUT_TILE, n)
    halo = 16
    per = tm // halo
    nh = n // halo

    def rows(width, cb=0):
        return pl.BlockSpec((1, tm, width), lambda bi, i: (bi, i, cb))

    def before(cb):
        return pl.BlockSpec((1, halo, CONV_W), lambda bi, i: (bi, jnp.maximum(i * per - 1, 0), cb))

    def after(cb):
        return pl.BlockSpec((1, halo, CONV_W), lambda bi, i: (bi, jnp.minimum((i + 1) * per, nh - 1), cb))

    def const(shape):
        return pl.BlockSpec(shape, lambda bi, i: (0,) * len(shape))

    in_specs = [rows(D_MODEL), pl.BlockSpec((None, 8, 3 * D_MODEL), lambda bi, i: (layer, 0, 0)),
                rows(CONV_W, CB_U), rows(CONV_W, CB_GZ), before(CB_U), after(CB_U), rows(RET_W), rows(NA_W)]
    args = [x, mod, p, p, p, p, y_ret, y_na]
    out_specs, out_shape = rows(D_MODEL), jax.ShapeDtypeStruct((b, n, D_MODEL), F32)
    if ctx is not None:
        xc, pc, yc_ret, yc_na = ctx
        lc = xc.shape[1]

        def seq(width, cb=0):
            return pl.BlockSpec((1, lc, width), lambda bi, i: (bi, 0, cb))

        in_specs += [seq(D_MODEL), seq(CONV_W, CB_U), seq(CONV_W, CB_GZ), seq(RET_W), seq(NA_W)]
        args += [xc, pc, pc, yc_ret, yc_na]
        out_specs, out_shape = [out_specs, seq(D_MODEL)], [out_shape, jax.ShapeDtypeStruct((b, lc, D_MODEL), F32)]
    in_specs += [const((3, CONV_W)), const((1, CONV_W)),
                 pl.BlockSpec((None, MIX_W, D_MODEL), lambda bi, i: (layer, 0, 0), pipeline_mode=pl.Buffered(1)),
                 const((1, D_MODEL)), const((1, D_MODEL))]
    args += [conv_w, conv_b.reshape(1, CONV_W), w_out, ln_g.reshape(1, D_MODEL), ln_b.reshape(1, D_MODEL)]
    return pl.pallas_call(
        functools.partial(_out_kernel, ctx_row=None if ctx is None else b),
        grid=(b, n // tm),
        in_specs=in_specs,
        out_specs=out_specs,
        out_shape=out_shape,
        compiler_params=pltpu.CompilerParams(dimension_semantics=("parallel", "arbitrary")),
        name="output",
    )(*args)


def kernel(x, c, ctx, c_ctx, w_mod, b_mod, w_in, conv_w, conv_b, ret_decay, na_rpb, w_out, ln_g, ln_b):
    b, n, d = x.shape
    act_t = jnp.pad(jnp.concatenate([c, c_ctx[None]], axis=0).T, ((0, 0), (0, 5)))
    mod = _modulation(act_t, w_mod, b_mod)
    tables = _rope_tables(n)
    xc = ctx
    for l in range(DEPTH):
        need_ctx = l < DEPTH - 1
        p, pc = _projection(x, xc, mod, w_in, l, tables, PROJ_W if need_ctx else KV_W)
        y_ret = _retention(p, pc, ret_decay[l], init_state=True)
        y_na = _neighbourhood(p, pc, na_rpb[l])
        ctx_mix = (xc, pc) + tuple(_retention(pc, None, ret_decay[l], init_state=False, attend=True)) if need_ctx else None
        out = _output(x, mod, p, y_ret, y_na, conv_w[l], conv_b[l], w_out, l, ln_g[l], ln_b[l], ctx_mix)
        x, xc = out if need_ctx else (out, None)
    return x
```

```python
import functools

import numpy as np
import jax
import jax.numpy as jnp
from jax import lax
from jax.experimental import pallas as pl
from jax.experimental.pallas import tpu as pltpu

D_MODEL = 1024
DEPTH = 2
GRID_W = 64
HEAD_DIM = 64
CONV_W = 256
RET_HEADS = 6
RET_W = RET_HEADS * HEAD_DIM
NA_HEADS = 6
NA_W = NA_HEADS * HEAD_DIM
MIX_W = CONV_W + RET_W + NA_W
RET_CHUNK = 128
WIN_H = 8
WIN_W = 16
ROPE_BASE = 10000.0
LN_EPS = 1e-5
DEEPNORM_ALPHA = (2 * DEPTH) ** 0.25
PROJ_SPLITS = (RET_W, RET_W, NA_W, NA_W, RET_W, RET_W, NA_W, NA_W, CONV_W, CONV_W, CONV_W, CONV_W)
PROJ_W = sum(PROJ_SPLITS)
KV_W = 2 * RET_W + 2 * NA_W

LANES = 128
PAIRS = RET_HEADS // 2
CB_RK, CB_RV, CB_NK, CB_NV, CB_RQ, CB_RG, CB_NQ, CB_NG = (sum(PROJ_SPLITS[:j]) // LANES for j in range(8))
CONV_OFF = PROJ_W - 4 * CONV_W
P_W = CONV_OFF + 2 * CONV_W
CB_U, CB_GZ = CONV_OFF // CONV_W, CONV_OFF // CONV_W + 1
QK_SCALE = HEAD_DIM ** -0.5
NEG = -1e30
LOG2E = 1.4426950408889634
NA_ROWS = 4
NA_TOK = NA_ROWS * GRID_W
NA_STEP = 4
RET_GROUP = 64
ROW_TILE = 1024
OUT_TILE = 2048
OUT_SUB = 256
PROJ_CHUNK = 512

F32 = jnp.float32
BF16 = jnp.bfloat16


def _silu(v):
    return v * jax.nn.sigmoid(v)


def _dot(a, b):
    return jnp.dot(a, b, preferred_element_type=F32)


def _dot_nt(a, b):
    return lax.dot_general(a, b, (((1,), (1,)), ((), ())), preferred_element_type=F32)


def _dot_tn(a, b):
    return lax.dot_general(a, b, (((0,), (0,)), ((), ())), preferred_element_type=F32)


def _lane_is_head0(shape):
    return lax.broadcasted_iota(jnp.int32, shape, len(shape) - 1) < HEAD_DIM


def _mod_kernel(act_ref, w_ref, b_ref, o_ref):
    a = _silu(act_ref[...])
    w = w_ref[0]
    bias = b_ref[0]
    for r in range(3):
        o_ref[0, r:r + 1, :] = jnp.sum(a[:, r:r + 1] * w, axis=0, keepdims=True) + bias
    o_ref[0, 3:8, :] = jnp.zeros((5, w.shape[1]), F32)


def _modulation(act_t, w_mod, b_mod):
    tn = 3072
    n = w_mod.shape[-1]
    return pl.pallas_call(
        _mod_kernel,
        grid=(DEPTH, n // tn),
        in_specs=[pl.BlockSpec((D_MODEL, 8), lambda l, j: (0, 0)),
                  pl.BlockSpec((1, D_MODEL, tn), lambda l, j: (l, 0, j)),
                  pl.BlockSpec((1, 1, tn), lambda l, j: (l, 0, j))],
        out_specs=pl.BlockSpec((1, 8, tn), lambda l, j: (l, 0, j)),
        out_shape=jax.ShapeDtypeStruct((DEPTH, 8, n), F32),
        name="modulation",
    )(act_t, w_mod, b_mod.reshape(DEPTH, 1, n))


def _rope_tables(n):
    rows = n // GRID_W
    nf = HEAD_DIM // 4
    inv = ROPE_BASE ** (-np.arange(nf, dtype=np.float64) / nf)
    lane = np.arange(LANES)
    by_row = ((lane % HEAD_DIM) < HEAD_DIM // 2)[None]
    first = ((lane % (2 * nf)) < nf)[None]
    ang_r = np.tile(np.arange(rows, dtype=np.float64)[:, None] * inv, (1, LANES // nf))
    ang_c = np.tile(np.arange(GRID_W, dtype=np.float64)[:, None] * inv, (1, LANES // nf))
    parts = []
    for ang, own in ((ang_r, by_row), (ang_c, ~by_row)):
        cos, sin = np.cos(ang), np.sin(ang)
        parts += [np.where(own, cos, 0.0), np.where(own & first, -sin, 0.0), np.where(own & ~first, sin, 0.0)]
    return tuple(jnp.asarray(t, F32) for t in parts)


def _project(x, mod_row, w_ref, tables, o_ref, n_cols):
    shift, scale = mod_row[:, :D_MODEL], mod_row[:, D_MODEL:2 * D_MODEL]
    h = (x * (1.0 + scale) + shift).astype(BF16)
    rotated = tuple(range(CB_RK, CB_RK + PAIRS)) + tuple(range(CB_RQ, CB_RQ + PAIRS))
    col_scale = {cb: QK_SCALE for cb in range(CB_RK, CB_RK + PAIRS)}
    col_scale.update({cb: QK_SCALE * LOG2E for cb in range(CB_NQ, CB_NQ + PAIRS)})
    for off in range(0, min(n_cols, CONV_OFF), PROJ_CHUNK):
        acc = _dot(h, w_ref[:, off:off + PROJ_CHUNK].astype(BF16))
        tiles = []
        for j in range(PROJ_CHUNK // LANES):
            cb = off // LANES + j
            v = acc[:, j * LANES:(j + 1) * LANES]
            if tables is not None and cb in rotated:
                cos, up, dn = tables
                v = v * cos + pltpu.roll(v, LANES - 16, axis=1) * up + pltpu.roll(v, 16, axis=1) * dn
            if cb in col_scale:
                v = v * col_scale[cb]
            tiles.append(v.astype(BF16))
        o_ref[0, :, off:off + PROJ_CHUNK] = jnp.concatenate(tiles, axis=1)
    if n_cols > CONV_OFF:
        ch, cb, cc, cz = (_dot(h, w_ref[:, CONV_OFF + j * CONV_W:CONV_OFF + (j + 1) * CONV_W].astype(BF16))
                          for j in range(4))
        o_ref[0, :, CONV_OFF:CONV_OFF + CONV_W] = (cc * ch).astype(BF16)
        o_ref[0, :, CONV_OFF + CONV_W:CONV_OFF + 2 * CONV_W] = (cb * _silu(cz)).astype(BF16)


def _proj_kernel(x_ref, xc_ref, mod_ref, w_ref, *rest, ctx_row, n_cols_ctx):
    *table_refs, o_ref, oc_ref = rest
    tables = tuple(jnp.concatenate([r_ref[g:g + 1, :] + c_ref[...] for g in range(r_ref.shape[0])], axis=0)
                   for r_ref, c_ref in zip(table_refs[:3], table_refs[3:]))
    _project(x_ref[0], mod_ref[pl.ds(pl.program_id(0), 1), :], w_ref, tables, o_ref, PROJ_W)

    @pl.when(pl.program_id(1) == 0)
    def _():
        _project(xc_ref[0], mod_ref[ctx_row:ctx_row + 1, :], w_ref, None, oc_ref, n_cols_ctx)


def _projection(x, xc, mod, w_in, layer, tables, n_cols_ctx):
    b, n, _ = x.shape
    lc = xc.shape[1]
    tm = min(ROW_TILE, n)
    pc_w = P_W if n_cols_ctx > CONV_OFF else n_cols_ctx
    in_specs = ([pl.BlockSpec((1, tm, D_MODEL), lambda bi, i: (bi, i, 0)),
                 pl.BlockSpec((1, lc, D_MODEL), lambda bi, i: (bi, 0, 0)),
                 pl.BlockSpec((None, 8, 3 * D_MODEL), lambda bi, i: (layer, 0, 0)),
                 pl.BlockSpec((None, D_MODEL, PROJ_W), lambda bi, i: (layer, 0, 0), pipeline_mode=pl.Buffered(1))]
                + [pl.BlockSpec((tm // GRID_W, LANES), lambda bi, i: (i, 0))] * 3
                + [pl.BlockSpec((GRID_W, LANES), lambda bi, i: (0, 0))] * 3)
    return pl.pallas_call(
        functools.partial(_proj_kernel, ctx_row=b, n_cols_ctx=n_cols_ctx),
        grid=(b, n // tm),
        in_specs=in_specs,
        out_specs=[pl.BlockSpec((1, tm, P_W), lambda bi, i: (bi, i, 0)),
                   pl.BlockSpec((1, lc, pc_w), lambda bi, i: (bi, 0, 0))],
        out_shape=[jax.ShapeDtypeStruct((b, n, P_W), BF16), jax.ShapeDtypeStruct((b, lc, pc_w), BF16)],
        compiler_params=pltpu.CompilerParams(dimension_semantics=("parallel", "arbitrary")),
        name="projection",
    )(x, xc, mod, w_in, *tables)


def _log_sigmoid(v):
    return jnp.minimum(v, 0.0) - jnp.log1p(jnp.exp(-jnp.abs(v)))


def _block_diag(m):
    r = lax.broadcasted_iota(jnp.int32, m.shape, 0) < HEAD_DIM
    c = lax.broadcasted_iota(jnp.int32, m.shape, 1) < HEAD_DIM
    return jnp.where(r == c, m, 0.0)


def _head_mean(x):
    head0 = _lane_is_head0(x.shape)
    s0 = jnp.sum(jnp.where(head0, x, 0.0), axis=-1, keepdims=True)
    s1 = jnp.sum(jnp.where(head0, 0.0, x), axis=-1, keepdims=True)
    return jnp.where(head0, s0, s1) * (1.0 / HEAD_DIM)


def _attend(q_ref, k_ref, v_ref, g_ref, o_ref):
    q, k, v = q_ref[0], k_ref[0], v_ref[0]
    head0 = _lane_is_head0(q.shape)
    zero = jnp.zeros_like(q)
    outs = []
    for hh in range(2):
        qh = jnp.where(head0, q, zero) if hh == 0 else jnp.where(head0, zero, q)
        s = _dot_nt(qh, k)
        p = jnp.exp2(s - jnp.max(s, axis=-1, keepdims=True))
        outs.append(_dot(p.astype(BF16), v) / jnp.sum(p, axis=-1, keepdims=True))
    o = jnp.where(head0, outs[0], outs[1])
    o_ref[0] = (o * _silu(g_ref[0].astype(F32))).astype(o_ref.dtype)


def _ret_kernel(*refs, init_state, attend, group):
    refs = list(refs)
    dec_ref, q_ref, k_ref, v_ref, g_ref = refs[:5]
    refs = refs[5:]
    if init_state:
        kc_ref, vc_ref = refs[:2]
        refs = refs[2:]
    if attend:
        attn_refs, refs = refs[:4], refs[4:]
        o_ref, ao_ref, s_ref, u_ref, kt_ref, wq_ref, wkt_ref, d_ref = refs
        _attend(*attn_refs, ao_ref)
    else:
        o_ref, s_ref, u_ref, kt_ref, wq_ref, wkt_ref, d_ref = refs
    c = RET_CHUNK
    n_chunks = q_ref.shape[1] // c
    n_groups = n_chunks // group
    head0 = _lane_is_head0((c, LANES))
    lg_f = _log_sigmoid(dec_ref[0, 0])
    lg_b = _log_sigmoid(dec_ref[1, 0])
    i = lax.broadcasted_iota(jnp.int32, (c, LANES), 0).astype(F32)
    wq_ref[:, :LANES] = jnp.exp((i + 1.0) * lg_f)
    wq_ref[:, LANES:] = jnp.exp((c - i) * lg_b)
    wkt_ref[:LANES, :] = jnp.exp((c - 1.0 - i) * lg_f).T
    wkt_ref[LANES:, :] = jnp.exp(i * lg_b).T
    gc_f = jnp.exp(float(c) * lg_f)
    gc_b = jnp.exp(float(c) * lg_b)
    diff = (lax.broadcasted_iota(jnp.int32, (c, c), 0) - lax.broadcasted_iota(jnp.int32, (c, c), 1)).astype(F32)
    lower = diff >= 0
    for hh in range(2):
        lf = lg_f[:, hh * HEAD_DIM:hh * HEAD_DIM + 1]
        lb = lg_b[:, hh * HEAD_DIM:hh * HEAD_DIM + 1]
        d_ref[:, hh * c:(hh + 1) * c] = jnp.where(lower, jnp.exp(jnp.where(lower, diff, 0.0) * lf),
                                                  jnp.exp(jnp.where(lower, 0.0, -diff) * lb))
    if init_state:
        lc = kc_ref.shape[1]
        m = lax.broadcasted_iota(jnp.int32, (lc, LANES), 0).astype(F32)
        kcf = kc_ref[0].astype(F32)
        s_f0 = _block_diag(_dot_tn((kcf * jnp.exp((lc - 1.0 - m) * lg_f)).astype(BF16), vc_ref[0]))
        s_b0 = _block_diag(_dot_tn((kcf * jnp.exp(m * lg_b)).astype(BF16), vc_ref[0]))
    else:
        s_f0 = s_b0 = jnp.zeros((LANES, LANES), F32)
    r2 = lax.broadcasted_iota(jnp.int32, (2 * LANES, LANES), 0) % LANES < HEAD_DIM
    c2 = lax.broadcasted_iota(jnp.int32, (2 * LANES, LANES), 1) < HEAD_DIM
    diag2 = r2 == c2

    chan0 = lax.broadcasted_iota(jnp.int32, (LANES, c), 0) < HEAD_DIM

    def chunk(n):
        return pl.ds(pl.multiple_of(n * c, c), c)

    def increments(gi, carry):
        for j in range(group):
            n = gi * group + j
            ktb = k_ref[0, chunk(n), :].T
            kt = ktb.astype(F32)
            zero = jnp.zeros_like(ktb)
            kt_ref[n] = jnp.concatenate([jnp.where(chan0, ktb, zero), jnp.where(chan0, zero, ktb)], axis=1)
            lhs = jnp.concatenate([kt * wkt_ref[:LANES, :], kt * wkt_ref[LANES:, :]], axis=0).astype(BF16)
            u_ref[n] = jnp.where(diag2, _dot(lhs, v_ref[0, chunk(n), :]), 0.0)
        return carry

    lax.fori_loop(0, n_groups, increments, 0)

    def fwd_scan(gi, s):
        for j in range(group):
            n = gi * group + j
            s_ref[n, :LANES, :] = s.astype(BF16)
            s = s * gc_f + u_ref[n, :LANES, :]
        return s

    def bwd_scan(gi, s):
        for j in range(group):
            n = n_chunks - 1 - (gi * group + j)
            s_ref[n, LANES:, :] = s.astype(BF16)
            s = s * gc_b + u_ref[n, LANES:, :]
        return s

    lax.fori_loop(0, n_groups, fwd_scan, s_f0)
    lax.fori_loop(0, n_groups, bwd_scan, s_b0)

    def body(gi, carry):
        outs = []
        for j in range(group):
            n = gi * group + j
            rows = chunk(n)
            q, v = q_ref[0, rows, :], v_ref[0, rows, :]
            zero = jnp.zeros_like(v)
            vcat = jnp.concatenate([jnp.where(head0, v, zero), jnp.where(head0, zero, v)], axis=0)
            scores = _dot(q, kt_ref[n]) * d_ref[...]
            qf = q.astype(F32)
            qw = jnp.concatenate([qf * wq_ref[:, :LANES], qf * wq_ref[:, LANES:]], axis=1).astype(BF16)
            outs.append(_dot(scores.astype(BF16), vcat) + _dot(qw, s_ref[n]))
        o = jnp.concatenate(outs, axis=0)
        rows = pl.ds(pl.multiple_of(gi * (group * c), group * c), group * c)
        dlt = o - _head_mean(o)
        var = _head_mean(dlt * dlt)
        y = dlt * lax.rsqrt(var + LN_EPS) * _silu(g_ref[0, rows, :].astype(F32))
        o_ref[0, rows, :] = y.astype(o_ref.dtype)
        return carry

    lax.fori_loop(0, n_groups, body, 0)


def _retention(p, pc, ret_decay, init_state, attend=False):
    b, n, _ = p.shape
    dec = jnp.repeat(ret_decay.astype(F32), HEAD_DIM, axis=-1).reshape(2, PAIRS, 1, LANES)

    def col(cb):
        return pl.BlockSpec((1, n, LANES), lambda bi, pi: (bi, 0, cb + pi))

    in_specs = [pl.BlockSpec((2, 1, 1, LANES), lambda bi, pi: (0, pi, 0, 0)),
                col(CB_RQ), col(CB_RK), col(CB_RV), col(CB_RG)]
    args = [dec, p, p, p, p]
    if init_state:
        lc = pc.shape[1]
        in_specs += [pl.BlockSpec((1, lc, LANES), lambda bi, pi: (bi, 0, CB_RK + pi)),
                     pl.BlockSpec((1, lc, LANES), lambda bi, pi: (bi, 0, CB_RV + pi))]
        args += [pc, pc]
    out_specs, out_shape = col(0), jax.ShapeDtypeStruct((b, n, RET_W), BF16)
    if attend:
        in_specs += [col(CB_NQ), col(CB_NK), col(CB_NV), col(CB_NG)]
        args += [p, p, p, p]
        out_specs, out_shape = [out_specs, col(0)], [out_shape, jax.ShapeDtypeStruct((b, n, NA_W), BF16)]
    c = RET_CHUNK
    return pl.pallas_call(
        functools.partial(_ret_kernel, init_state=init_state, attend=attend, group=min(RET_GROUP, n // c)),
        grid=(b, PAIRS),
        in_specs=in_specs,
        out_specs=out_specs,
        out_shape=out_shape,
        scratch_shapes=[pltpu.VMEM((n // c, 2 * LANES, LANES), BF16),
                        pltpu.VMEM((n // c, 2 * LANES, LANES), F32),
                        pltpu.VMEM((n // c, LANES, 2 * c), BF16),
                        pltpu.VMEM((c, 2 * LANES), F32),
                        pltpu.VMEM((2 * LANES, c), F32),
                        pltpu.VMEM((c, 2 * c), F32)],
        compiler_params=pltpu.CompilerParams(dimension_semantics=("parallel", "parallel")),
        name="retention",
    )(*args)


def _na_bias_plan(rows):
    nblk = rows // NA_ROWS
    win_h = min(WIN_H, rows)
    plan = np.full((3, NA_ROWS, 3, NA_ROWS), -1, np.int64)
    for kind, blk in enumerate((0, 1, nblk - 1)):
        for qr in range(NA_ROWS):
            r = blk * NA_ROWS + qr
            rs = min(max(r - win_h // 2, 0), rows - win_h)
            for kb in range(3):
                if not 0 <= blk + kb - 1 <= nblk - 1:
                    continue
                for krl in range(NA_ROWS):
                    kr = (blk + kb - 1) * NA_ROWS + krl
                    if rs <= kr < rs + win_h:
                        plan[kind, qr, kb, krl] = kr - r + WIN_H - 1
    return plan


def _fill_na_bias(rpb_ref, bias_ref, t_ref, plan):
    n_dr = 2 * WIN_H - 1
    qc = lax.broadcasted_iota(jnp.int32, (GRID_W, LANES), 0)
    lane = lax.broadcasted_iota(jnp.int32, (GRID_W, LANES), 1)
    kc = lane % GRID_W
    cs = jnp.clip(qc - WIN_W // 2, 0, GRID_W - WIN_W)
    col_ok = (kc >= cs) & (kc < cs + WIN_W)
    first = lane < GRID_W
    neg = jnp.full((GRID_W, LANES), NEG, F32)
    for h in range(NA_HEADS):
        for dr in range(n_dr):
            row = jnp.broadcast_to(rpb_ref[h, dr:dr + 1, :] * LOG2E, (GRID_W, LANES))
            lo = pltpu.roll(row, LANES - (WIN_W - 1), 1, stride=1, stride_axis=0)
            hi = pltpu.roll(row, GRID_W - (WIN_W - 1), 1, stride=1, stride_axis=0)
            t_ref[dr] = jnp.where(col_ok, jnp.where(first, lo, hi), neg)
        for kind in range(3):
            for qr in range(NA_ROWS):
                for kb in range(3):
                    for pr in range(NA_ROWS // 2):
                        ia, ib = (int(plan[kind, qr, kb, 2 * pr + s]) for s in range(2))
                        a = t_ref[ia] if ia >= 0 else neg
                        b = t_ref[ib] if ib >= 0 else neg
                        col0 = kb * NA_TOK + pr * LANES
                        bias_ref[kind, h, qr * GRID_W:(qr + 1) * GRID_W, col0:col0 + LANES] = jnp.where(first, a, b)


def _fold_lanes(blocks, op):
    tiles = [blk[:, j:j + LANES] for blk in blocks for j in range(0, blk.shape[1], LANES)]
    acc = tiles[0]
    for t in tiles[1:]:
        acc = op(acc, t)
    return acc


def _na_kernel(*refs, plan):
    nkb = NA_STEP + 2
    q_ref, k_refs, v_refs = refs[0], refs[1:1 + nkb], refs[1 + nkb:1 + 2 * nkb]
    kx_ref, vx_ref, g_ref, rpb_ref, o_ref, bias_ref, t_ref = refs[1 + 2 * nkb:]
    i, last = pl.program_id(1), pl.num_programs(1) - 1

    @pl.when((pl.program_id(0) == 0) & (i == 0))
    def _():
        _fill_na_bias(rpb_ref, bias_ref, t_ref, plan)

    head0 = _lane_is_head0((NA_TOK, LANES))
    for pi in range(PAIRS):
        cols = slice(pi * LANES, (pi + 1) * LANES)
        keys = [r[0, :, cols] for r in k_refs] + [kx_ref[0, :, cols]]
        zero = jnp.zeros_like(keys[0])
        vals = [r[0, :, cols] for r in v_refs] + [vx_ref[0, :, cols]]
        v_h0 = [jnp.where(head0, v, zero) for v in vals]
        v_h1 = [jnp.where(head0, zero, v) for v in vals]
        for j in range(NA_STEP):
            edge = 0 if j == 0 else 2 if j == NA_STEP - 1 else None
            kind = 1 if edge is None else jnp.where(i == (0 if edge == 0 else last), edge, 1)
            rows = slice(j * NA_TOK, (j + 1) * NA_TOK)
            band = [j, j + 1, j + 2, nkb]
            q = q_ref[0, rows, cols]
            probs, inv = [], []
            for hh in range(2):
                qh = jnp.where(head0, q, zero) if hh == 0 else jnp.where(head0, zero, q)
                s = [_dot_nt(qh, keys[band[t]]) + bias_ref[kind, 2 * pi + hh, :, t * NA_TOK:(t + 1) * NA_TOK]
                     for t in range(3)]
                s.append(_dot_nt(qh, keys[nkb]))
                m = jnp.max(_fold_lanes(s, jnp.maximum), axis=-1, keepdims=True)
                p = [jnp.exp2(st - m) for st in s]
                inv.append(1.0 / jnp.sum(_fold_lanes(p, jnp.add), axis=-1, keepdims=True))
                probs += [pt.astype(BF16) for pt in p]
            v_cat = jnp.concatenate([v_h0[t] for t in band] + [v_h1[t] for t in band], axis=0)
            o = _dot(jnp.concatenate(probs, axis=1), v_cat) * jnp.where(head0, inv[0], inv[1])
            o_ref[0, rows, cols] = (o * _silu(g_ref[0, rows, cols].astype(F32))).astype(o_ref.dtype)


def _neighbourhood(p, pc, rpb):
    b, n, _ = p.shape
    nblk = n // NA_TOK
    nstep = nblk // NA_STEP
    assert NA_STEP >= 2 and nstep >= 2
    grp = NA_W // LANES

    def key_blk(cb, shift):
        return pl.BlockSpec((1, NA_TOK, NA_W),
                            lambda bi, i: (bi, jnp.clip(NA_STEP * i + shift, 0, nblk - 1), cb // grp))

    def step_blk(cb):
        return pl.BlockSpec((1, NA_STEP * NA_TOK, NA_W), lambda bi, i: (bi, i, cb // grp))

    lc = pc.shape[1]

    def ctx(cb):
        return pl.BlockSpec((1, lc, NA_W), lambda bi, i: (bi, 0, cb // grp))

    n_dr, n_dc = 2 * WIN_H - 1, 2 * WIN_W - 1
    rpb_rows = jnp.pad(rpb.astype(F32), ((0, 0), (0, 16 - n_dr), (0, LANES - n_dc)))
    shifts = range(-1, NA_STEP + 1)
    return pl.pallas_call(
        functools.partial(_na_kernel, plan=_na_bias_plan(n // GRID_W)),
        grid=(b, nstep),
        in_specs=([step_blk(CB_NQ)] + [key_blk(CB_NK, s) for s in shifts] + [key_blk(CB_NV, s) for s in shifts]
                  + [ctx(CB_NK), ctx(CB_NV), step_blk(CB_NG),
                     pl.BlockSpec((NA_HEADS, 16, LANES), lambda bi, i: (0, 0, 0))]),
        out_specs=pl.BlockSpec((1, NA_STEP * NA_TOK, NA_W), lambda bi, i: (bi, i, 0)),
        out_shape=jax.ShapeDtypeStruct((b, n, NA_W), BF16),
        scratch_shapes=[pltpu.VMEM((3, NA_HEADS, NA_TOK, 3 * NA_TOK), F32),
                        pltpu.VMEM((n_dr, GRID_W, LANES), F32)],
        compiler_params=pltpu.CompilerParams(dimension_semantics=("arbitrary", "arbitrary")),
        name="neighbourhood",
    )(p, *([p] * (2 * len(shifts))), pc, pc, p, rpb_rows)


def _mix_out(x_ref, g, u_ref, gz_ref, u_before, u_after, yr_ref, yn_ref, cw, cbias, w, lg, lb, o_ref):
    u = u_ref[0].astype(F32)
    tm = u.shape[0]
    row = lax.broadcasted_iota(jnp.int32, u.shape, 0)
    u_prev = jnp.where(row == 0, u_before, pltpu.roll(u, 1, axis=0))
    u_next = jnp.where(row == tm - 1, u_after, pltpu.roll(u, tm - 1, axis=0))
    sub = min(OUT_SUB, tm)
    for r in range(tm // sub):
        rs = slice(r * sub, (r + 1) * sub)
        conv = u_prev[rs] * cw[0:1] + u[rs] * cw[1:2] + u_next[rs] * cw[2:3] + cbias
        y_conv = gz_ref[0, rs, :].astype(F32) * conv
        y = jnp.concatenate([y_conv.astype(BF16), yr_ref[0, rs, :], yn_ref[0, rs, :]], axis=-1)
        z = x_ref[0, rs, :] + g * _dot(y, w)
        mu = jnp.mean(z, axis=-1, keepdims=True)
        dlt = z - mu
        var = jnp.mean(dlt * dlt, axis=-1, keepdims=True)
        o_ref[0, rs, :] = dlt * lax.rsqrt(var + LN_EPS / DEEPNORM_ALPHA ** 2) * lg + lb


def _out_kernel(*refs, ctx_row):
    x_ref, mod_ref, u_ref, gz_ref, ub_ref, ua_ref, yr_ref, yn_ref = refs[:8]
    n_out = 1 if ctx_row is None else 2
    cw_ref, cbias_ref, w_ref, lg_ref, lb_ref = refs[-5 - n_out:-n_out]
    i = pl.program_id(1)
    halo = ub_ref.shape[1]
    u_before = jnp.where(i == 0, 0.0, ub_ref[0, halo - 1:halo, :].astype(F32))
    u_after = jnp.where(i == pl.num_programs(1) - 1, 0.0, ua_ref[0, 0:1, :].astype(F32))
    w = w_ref[...].astype(BF16)
    consts = (cw_ref[...], cbias_ref[...], w, lg_ref[...], lb_ref[...])

    def gate(row):
        return mod_ref[pl.ds(row, 1), 2 * D_MODEL:] * (1.0 / DEEPNORM_ALPHA)

    if ctx_row is None:
        _mix_out(x_ref, gate(pl.program_id(0)), u_ref, gz_ref, u_before, u_after, yr_ref, yn_ref, *consts, refs[-1])
    else:
        xc_ref, uc_ref, gzc_ref, yrc_ref, ync_ref = refs[8:13]
        o_ref, oc_ref = refs[-2:]
        _mix_out(x_ref, gate(pl.program_id(0)), u_ref, gz_ref, u_before, u_after, yr_ref, yn_ref, *consts, o_ref)

        @pl.when(i == 0)
        def _():
            _mix_out(xc_ref, gate(ctx_row), uc_ref, gzc_ref, 0.0, 0.0, yrc_ref, ync_ref, *consts, oc_ref)


def _output(x, mod, p, y_ret, y_na, conv_w, conv_b, w_out, layer, ln_g, ln_b, ctx=None):
    b, n, _ = x.shape
    tm = min(OUT_TILE, n)
    halo = 16
    per = tm // halo
    nh = n // halo

    def rows(width, cb=0):
        return pl.BlockSpec((1, tm, width), lambda bi, i: (bi, i, cb))

    def before(cb):
        return pl.BlockSpec((1, halo, CONV_W), lambda bi, i: (bi, jnp.maximum(i * per - 1, 0), cb))

    def after(cb):
        return pl.BlockSpec((1, halo, CONV_W), lambda bi, i: (bi, jnp.minimum((i + 1) * per, nh - 1), cb))

    def const(shape):
        return pl.BlockSpec(shape, lambda bi, i: (0,) * len(shape))

    in_specs = [rows(D_MODEL), pl.BlockSpec((None, 8, 3 * D_MODEL), lambda bi, i: (layer, 0, 0)),
                rows(CONV_W, CB_U), rows(CONV_W, CB_GZ), before(CB_U), after(CB_U), rows(RET_W), rows(NA_W)]
    args = [x, mod, p, p, p, p, y_ret, y_na]
    out_specs, out_shape = rows(D_MODEL), jax.ShapeDtypeStruct((b, n, D_MODEL), F32)
    if ctx is not None:
        xc, pc, yc_ret, yc_na = ctx
        lc = xc.shape[1]

        def seq(width, cb=0):
            return pl.BlockSpec((1, lc, width), lambda bi, i: (bi, 0, cb))

        in_specs += [seq(D_MODEL), seq(CONV_W, CB_U), seq(CONV_W, CB_GZ), seq(RET_W), seq(NA_W)]
        args += [xc, pc, pc, yc_ret, yc_na]
        out_specs, out_shape = [out_specs, seq(D_MODEL)], [out_shape, jax.ShapeDtypeStruct((b, lc, D_MODEL), F32)]
    in_specs += [const((3, CONV_W)), const((1, CONV_W)),
                 pl.BlockSpec((None, MIX_W, D_MODEL), lambda bi, i: (layer, 0, 0), pipeline_mode=pl.Buffered(1)),
                 const((1, D_MODEL)), const((1, D_MODEL))]
    args += [conv_w, conv_b.reshape(1, CONV_W), w_out, ln_g.reshape(1, D_MODEL), ln_b.reshape(1, D_MODEL)]
    return pl.pallas_call(
        functools.partial(_out_kernel, ctx_row=None if ctx is None else b),
        grid=(b, n // tm),
        in_specs=in_specs,
        out_specs=out_specs,
        out_shape=out_shape,
        compiler_params=pltpu.CompilerParams(dimension_semantics=("parallel", "arbitrary")),
        name="output",
    )(*args)


def kernel(x, c, ctx, c_ctx, w_mod, b_mod, w_in, conv_w, conv_b, ret_decay, na_rpb, w_out, ln_g, ln_b):
    b, n, d = x.shape
    act_t = jnp.pad(jnp.concatenate([c, c_ctx[None]], axis=0).T, ((0, 0), (0, 5)))
    mod = _modulation(act_t, w_mod, b_mod)
    tables = _rope_tables(n)
    xc = ctx
    for l in range(DEPTH):
        need_ctx = l < DEPTH - 1
        p, pc = _projection(x, xc, mod, w_in, l, tables, PROJ_W if need_ctx else KV_W)
        y_ret = _retention(p, pc, ret_decay[l], init_state=True)
        y_na = _neighbourhood(p, pc, na_rpb[l])
        ctx_mix = (xc, pc) + tuple(_retention(pc, None, ret_decay[l], init_state=False, attend=True)) if need_ctx else None
        out = _output(x, mod, p, y_ret, y_na, conv_w[l], conv_b[l], w_out, l, ln_g[l], ln_b[l], ctx_mix)
        x, xc = out if need_ctx else (out, None)
    return x
```

```python
import functools

import numpy as np
import jax
import jax.numpy as jnp
from jax import lax
from jax.experimental import pallas as pl
from jax.experimental.pallas import tpu as pltpu

D_MODEL = 1024
DEPTH = 2
GRID_W = 64
HEAD_DIM = 64
CONV_W = 256
RET_HEADS = 6
RET_W = RET_HEADS * HEAD_DIM
NA_HEADS = 6
NA_W = NA_HEADS * HEAD_DIM
MIX_W = CONV_W + RET_W + NA_W
RET_CHUNK = 128
WIN_H = 8
WIN_W = 16
ROPE_BASE = 10000.0
LN_EPS = 1e-5
DEEPNORM_ALPHA = (2 * DEPTH) ** 0.25
PROJ_SPLITS = (RET_W, RET_W, NA_W, NA_W, RET_W, RET_W, NA_W, NA_W, CONV_W, CONV_W, CONV_W, CONV_W)
PROJ_W = sum(PROJ_SPLITS)
KV_W = 2 * RET_W + 2 * NA_W

LANES = 128
PAIRS = RET_HEADS // 2
CB_RK, CB_RV, CB_NK, CB_NV, CB_RQ, CB_RG, CB_NQ, CB_NG = (sum(PROJ_SPLITS[:j]) // LANES for j in range(8))
CONV_OFF = PROJ_W - 4 * CONV_W
P_W = CONV_OFF + 2 * CONV_W
CB_U, CB_GZ = CONV_OFF // CONV_W, CONV_OFF // CONV_W + 1
QK_SCALE = HEAD_DIM ** -0.5
NEG = -1e30
LOG2E = 1.4426950408889634
NA_ROWS = 4
NA_TOK = NA_ROWS * GRID_W
NA_STEP = 4
RET_GROUP = 64
ROW_TILE = 1024
OUT_TILE = 2048
OUT_SUB = 256
PROJ_CHUNK = 512

F32 = jnp.float32
BF16 = jnp.bfloat16


def _silu(v):
    return v * jax.nn.sigmoid(v)


def _dot(a, b):
    return jnp.dot(a, b, preferred_element_type=F32)


def _dot_nt(a, b):
    return lax.dot_general(a, b, (((1,), (1,)), ((), ())), preferred_element_type=F32)


def _dot_tn(a, b):
    return lax.dot_general(a, b, (((0,), (0,)), ((), ())), preferred_element_type=F32)


def _lane_is_head0(shape):
    return lax.broadcasted_iota(jnp.int32, shape, len(shape) - 1) < HEAD_DIM


def _mod_kernel(act_ref, w_ref, b_ref, o_ref):
    a = _silu(act_ref[...])
    w = w_ref[0]
    bias = b_ref[0]
    for r in range(3):
        o_ref[0, r:r + 1, :] = jnp.sum(a[:, r:r + 1] * w, axis=0, keepdims=True) + bias
    o_ref[0, 3:8, :] = jnp.zeros((5, w.shape[1]), F32)


def _modulation(act_t, w_mod, b_mod):
    tn = 1536
    n = w_mod.shape[-1]
    return pl.pallas_call(
        _mod_kernel,
        grid=(DEPTH, n // tn),
        in_specs=[pl.BlockSpec((D_MODEL, 8), lambda l, j: (0, 0)),
                  pl.BlockSpec((1, D_MODEL, tn), lambda l, j: (l, 0, j)),
                  pl.BlockSpec((1, 1, tn), lambda l, j: (l, 0, j))],
        out_specs=pl.BlockSpec((1, 8, tn), lambda l, j: (l, 0, j)),
        out_shape=jax.ShapeDtypeStruct((DEPTH, 8, n), F32),
        name="modulation",
    )(act_t, w_mod, b_mod.reshape(DEPTH, 1, n))


def _rope_tables(n):
    rows = n // GRID_W
    nf = HEAD_DIM // 4
    inv = ROPE_BASE ** (-np.arange(nf, dtype=np.float64) / nf)
    lane = np.arange(LANES)
    by_row = ((lane % HEAD_DIM) < HEAD_DIM // 2)[None]
    first = ((lane % (2 * nf)) < nf)[None]
    ang_r = np.tile(np.arange(rows, dtype=np.float64)[:, None] * inv, (1, LANES // nf))
    ang_c = np.tile(np.arange(GRID_W, dtype=np.float64)[:, None] * inv, (1, LANES // nf))
    parts = []
    for ang, own in ((ang_r, by_row), (ang_c, ~by_row)):
        cos, sin = np.cos(ang), np.sin(ang)
        parts += [np.where(own, cos, 0.0), np.where(own & first, -sin, 0.0), np.where(own & ~first, sin, 0.0)]
    return tuple(jnp.asarray(t, F32) for t in parts)


def _project(x, mod_row, w_ref, tables, o_ref, n_cols):
    shift, scale = mod_row[:, :D_MODEL], mod_row[:, D_MODEL:2 * D_MODEL]
    h = (x * (1.0 + scale) + shift).astype(BF16)
    rotated = tuple(range(CB_RK, CB_RK + PAIRS)) + tuple(range(CB_RQ, CB_RQ + PAIRS))
    col_scale = {cb: QK_SCALE for cb in range(CB_RK, CB_RK + PAIRS)}
    col_scale.update({cb: QK_SCALE * LOG2E for cb in range(CB_NQ, CB_NQ + PAIRS)})
    for off in range(0, min(n_cols, CONV_OFF), PROJ_CHUNK):
        acc = _dot(h, w_ref[:, off:off + PROJ_CHUNK].astype(BF16))
        tiles = []
        for j in range(PROJ_CHUNK // LANES):
            cb = off // LANES + j
            v = acc[:, j * LANES:(j + 1) * LANES]
            if tables is not None and cb in rotated:
                cos, up, dn = tables
                v = v * cos + pltpu.roll(v, LANES - 16, axis=1) * up + pltpu.roll(v, 16, axis=1) * dn
            if cb in col_scale:
                v = v * col_scale[cb]
            tiles.append(v.astype(BF16))
        o_ref[0, :, off:off + PROJ_CHUNK] = jnp.concatenate(tiles, axis=1)
    if n_cols > CONV_OFF:
        ch, cb, cc, cz = (_dot(h, w_ref[:, CONV_OFF + j * CONV_W:CONV_OFF + (j + 1) * CONV_W].astype(BF16))
                          for j in range(4))
        o_ref[0, :, CONV_OFF:CONV_OFF + CONV_W] = (cc * ch).astype(BF16)
        o_ref[0, :, CONV_OFF + CONV_W:CONV_OFF + 2 * CONV_W] = (cb * _silu(cz)).astype(BF16)


def _proj_kernel(x_ref, xc_ref, mod_ref, w_ref, *rest, ctx_row, n_cols_ctx):
    *table_refs, o_ref, oc_ref = rest
    tables = tuple(jnp.concatenate([r_ref[g:g + 1, :] + c_ref[...] for g in range(r_ref.shape[0])], axis=0)
                   for r_ref, c_ref in zip(table_refs[:3], table_refs[3:]))
    _project(x_ref[0], mod_ref[pl.ds(pl.program_id(0), 1), :], w_ref, tables, o_ref, PROJ_W)

    @pl.when(pl.program_id(1) == 0)
    def _():
        _project(xc_ref[0], mod_ref[ctx_row:ctx_row + 1, :], w_ref, None, oc_ref, n_cols_ctx)


def _projection(x, xc, mod, w_in, layer, tables, n_cols_ctx):
    b, n, _ = x.shape
    lc = xc.shape[1]
    tm = min(ROW_TILE, n)
    pc_w = P_W if n_cols_ctx > CONV_OFF else n_cols_ctx
    in_specs = ([pl.BlockSpec((1, tm, D_MODEL), lambda bi, i: (bi, i, 0)),
                 pl.BlockSpec((1, lc, D_MODEL), lambda bi, i: (bi, 0, 0)),
                 pl.BlockSpec((None, 8, 3 * D_MODEL), lambda bi, i: (layer, 0, 0)),
                 pl.BlockSpec((None, D_MODEL, PROJ_W), lambda bi, i: (layer, 0, 0), pipeline_mode=pl.Buffered(1))]
                + [pl.BlockSpec((tm // GRID_W, LANES), lambda bi, i: (i, 0))] * 3
                + [pl.BlockSpec((GRID_W, LANES), lambda bi, i: (0, 0))] * 3)
    return pl.pallas_call(
        functools.partial(_proj_kernel, ctx_row=b, n_cols_ctx=n_cols_ctx),
        grid=(b, n // tm),
        in_specs=in_specs,
        out_specs=[pl.BlockSpec((1, tm, P_W), lambda bi, i: (bi, i, 0)),
                   pl.BlockSpec((1, lc, pc_w), lambda bi, i: (bi, 0, 0))],
        out_shape=[jax.ShapeDtypeStruct((b, n, P_W), BF16), jax.ShapeDtypeStruct((b, lc, pc_w), BF16)],
        compiler_params=pltpu.CompilerParams(dimension_semantics=("parallel", "arbitrary")),
        name="projection",
    )(x, xc, mod, w_in, *tables)


def _log_sigmoid(v):
    return jnp.minimum(v, 0.0) - jnp.log1p(jnp.exp(-jnp.abs(v)))


def _block_diag(m):
    r = lax.broadcasted_iota(jnp.int32, m.shape, 0) < HEAD_DIM
    c = lax.broadcasted_iota(jnp.int32, m.shape, 1) < HEAD_DIM
    return jnp.where(r == c, m, 0.0)


def _head_mean(x):
    head0 = _lane_is_head0(x.shape)
    s0 = jnp.sum(jnp.where(head0, x, 0.0), axis=-1, keepdims=True)
    s1 = jnp.sum(jnp.where(head0, 0.0, x), axis=-1, keepdims=True)
    return jnp.where(head0, s0, s1) * (1.0 / HEAD_DIM)


def _attend(q_ref, k_ref, v_ref, g_ref, o_ref):
    q, k, v = q_ref[0], k_ref[0], v_ref[0]
    head0 = _lane_is_head0(q.shape)
    zero = jnp.zeros_like(q)
    outs = []
    for hh in range(2):
        qh = jnp.where(head0, q, zero) if hh == 0 else jnp.where(head0, zero, q)
        s = _dot_nt(qh, k)
        p = jnp.exp2(s - jnp.max(s, axis=-1, keepdims=True))
        outs.append(_dot(p.astype(BF16), v) / jnp.sum(p, axis=-1, keepdims=True))
    o = jnp.where(head0, outs[0], outs[1])
    o_ref[0] = (o * _silu(g_ref[0].astype(F32))).astype(o_ref.dtype)


def _ret_kernel(*refs, init_state, attend, group):
    refs = list(refs)
    dec_ref, q_ref, k_ref, v_ref, g_ref = refs[:5]
    refs = refs[5:]
    if init_state:
        kc_ref, vc_ref = refs[:2]
        refs = refs[2:]
    if attend:
        attn_refs, refs = refs[:4], refs[4:]
        o_ref, ao_ref, s_ref, u_ref, kt_ref, wq_ref, wkt_ref, d_ref = refs
        _attend(*attn_refs, ao_ref)
    else:
        o_ref, s_ref, u_ref, kt_ref, wq_ref, wkt_ref, d_ref = refs
    c = RET_CHUNK
    n_chunks = q_ref.shape[1] // c
    n_groups = n_chunks // group
    head0 = _lane_is_head0((c, LANES))
    lg_f = _log_sigmoid(dec_ref[0, 0])
    lg_b = _log_sigmoid(dec_ref[1, 0])
    i = lax.broadcasted_iota(jnp.int32, (c, LANES), 0).astype(F32)
    wq_ref[:, :LANES] = jnp.exp((i + 1.0) * lg_f)
    wq_ref[:, LANES:] = jnp.exp((c - i) * lg_b)
    wkt_ref[:LANES, :] = jnp.exp((c - 1.0 - i) * lg_f).T
    wkt_ref[LANES:, :] = jnp.exp(i * lg_b).T
    gc_f = jnp.exp(float(c) * lg_f)
    gc_b = jnp.exp(float(c) * lg_b)
    diff = (lax.broadcasted_iota(jnp.int32, (c, c), 0) - lax.broadcasted_iota(jnp.int32, (c, c), 1)).astype(F32)
    lower = diff >= 0
    for hh in range(2):
        lf = lg_f[:, hh * HEAD_DIM:hh * HEAD_DIM + 1]
        lb = lg_b[:, hh * HEAD_DIM:hh * HEAD_DIM + 1]
        d_ref[:, hh * c:(hh + 1) * c] = jnp.where(lower, jnp.exp(jnp.where(lower, diff, 0.0) * lf),
                                                  jnp.exp(jnp.where(lower, 0.0, -diff) * lb))
    if init_state:
        lc = kc_ref.shape[1]
        m = lax.broadcasted_iota(jnp.int32, (lc, LANES), 0).astype(F32)
        kcf = kc_ref[0].astype(F32)
        s_f0 = _block_diag(_dot_tn((kcf * jnp.exp((lc - 1.0 - m) * lg_f)).astype(BF16), vc_ref[0]))
        s_b0 = _block_diag(_dot_tn((kcf * jnp.exp(m * lg_b)).astype(BF16), vc_ref[0]))
    else:
        s_f0 = s_b0 = jnp.zeros((LANES, LANES), F32)
    r2 = lax.broadcasted_iota(jnp.int32, (2 * LANES, LANES), 0) % LANES < HEAD_DIM
    c2 = lax.broadcasted_iota(jnp.int32, (2 * LANES, LANES), 1) < HEAD_DIM
    diag2 = r2 == c2

    chan0 = lax.broadcasted_iota(jnp.int32, (LANES, c), 0) < HEAD_DIM

    def chunk(n):
        return pl.ds(pl.multiple_of(n * c, c), c)

    def increments(gi, carry):
        for j in range(group):
            n = gi * group + j
            ktb = k_ref[0, chunk(n), :].T
            kt = ktb.astype(F32)
            zero = jnp.zeros_like(ktb)
            kt_ref[n] = jnp.concatenate([jnp.where(chan0, ktb, zero), jnp.where(chan0, zero, ktb)], axis=1)
            lhs = jnp.concatenate([kt * wkt_ref[:LANES, :], kt * wkt_ref[LANES:, :]], axis=0).astype(BF16)
            u_ref[n] = jnp.where(diag2, _dot(lhs, v_ref[0, chunk(n), :]), 0.0)
        return carry

    lax.fori_loop(0, n_groups, increments, 0)

    def fwd_scan(gi, s):
        for j in range(group):
            n = gi * group + j
            s_ref[n, :LANES, :] = s.astype(BF16)
            s = s * gc_f + u_ref[n, :LANES, :]
        return s

    def bwd_scan(gi, s):
        for j in range(group):
            n = n_chunks - 1 - (gi * group + j)
            s_ref[n, LANES:, :] = s.astype(BF16)
            s = s * gc_b + u_ref[n, LANES:, :]
        return s

    lax.fori_loop(0, n_groups, fwd_scan, s_f0)
    lax.fori_loop(0, n_groups, bwd_scan, s_b0)

    def body(gi, carry):
        outs = []
        for j in range(group):
            n = gi * group + j
            rows = chunk(n)
            q, v = q_ref[0, rows, :], v_ref[0, rows, :]
            zero = jnp.zeros_like(v)
            vcat = jnp.concatenate([jnp.where(head0, v, zero), jnp.where(head0, zero, v)], axis=0)
            scores = _dot(q, kt_ref[n]) * d_ref[...]
            qf = q.astype(F32)
            qw = jnp.concatenate([qf * wq_ref[:, :LANES], qf * wq_ref[:, LANES:]], axis=1).astype(BF16)
            outs.append(_dot(scores.astype(BF16), vcat) + _dot(qw, s_ref[n]))
        o = jnp.concatenate(outs, axis=0)
        rows = pl.ds(pl.multiple_of(gi * (group * c), group * c), group * c)
        dlt = o - _head_mean(o)
        var = _head_mean(dlt * dlt)
        y = dlt * lax.rsqrt(var + LN_EPS) * _silu(g_ref[0, rows, :].astype(F32))
        o_ref[0, rows, :] = y.astype(o_ref.dtype)
        return carry

    lax.fori_loop(0, n_groups, body, 0)


def _retention(p, pc, ret_decay, init_state, attend=False):
    b, n, _ = p.shape
    dec = jnp.repeat(ret_decay.astype(F32), HEAD_DIM, axis=-1).reshape(2, PAIRS, 1, LANES)

    def col(cb):
        return pl.BlockSpec((1, n, LANES), lambda bi, pi: (bi, 0, cb + pi))

    in_specs = [pl.BlockSpec((2, 1, 1, LANES), lambda bi, pi: (0, pi, 0, 0)),
                col(CB_RQ), col(CB_RK), col(CB_RV), col(CB_RG)]
    args = [dec, p, p, p, p]
    if init_state:
        lc = pc.shape[1]
        in_specs += [pl.BlockSpec((1, lc, LANES), lambda bi, pi: (bi, 0, CB_RK + pi)),
                     pl.BlockSpec((1, lc, LANES), lambda bi, pi: (bi, 0, CB_RV + pi))]
        args += [pc, pc]
    out_specs, out_shape = col(0), jax.ShapeDtypeStruct((b, n, RET_W), BF16)
    if attend:
        in_specs += [col(CB_NQ), col(CB_NK), col(CB_NV), col(CB_NG)]
        args += [p, p, p, p]
        out_specs, out_shape = [out_specs, col(0)], [out_shape, jax.ShapeDtypeStruct((b, n, NA_W), BF16)]
    c = RET_CHUNK
    return pl.pallas_call(
        functools.partial(_ret_kernel, init_state=init_state, attend=attend, group=min(RET_GROUP, n // c)),
        grid=(b, PAIRS),
        in_specs=in_specs,
        out_specs=out_specs,
        out_shape=out_shape,
        scratch_shapes=[pltpu.VMEM((n // c, 2 * LANES, LANES), BF16),
                        pltpu.VMEM((n // c, 2 * LANES, LANES), F32),
                        pltpu.VMEM((n // c, LANES, 2 * c), BF16),
                        pltpu.VMEM((c, 2 * LANES), F32),
                        pltpu.VMEM((2 * LANES, c), F32),
                        pltpu.VMEM((c, 2 * c), F32)],
        compiler_params=pltpu.CompilerParams(dimension_semantics=("parallel", "parallel")),
        name="retention",
    )(*args)


def _na_bias_plan(rows):
    nblk = rows // NA_ROWS
    win_h = min(WIN_H, rows)
    plan = np.full((3, NA_ROWS, 3, NA_ROWS), -1, np.int64)
    for kind, blk in enumerate((0, 1, nblk - 1)):
        for qr in range(NA_ROWS):
            r = blk * NA_ROWS + qr
            rs = min(max(r - win_h // 2, 0), rows - win_h)
            for kb in range(3):
                if not 0 <= blk + kb - 1 <= nblk - 1:
                    continue
                for krl in range(NA_ROWS):
                    kr = (blk + kb - 1) * NA_ROWS + krl
                    if rs <= kr < rs + win_h:
                        plan[kind, qr, kb, krl] = kr - r + WIN_H - 1
    return plan


def _fill_na_bias(rpb_ref, bias_ref, t_ref, plan):
    n_dr = 2 * WIN_H - 1
    qc = lax.broadcasted_iota(jnp.int32, (GRID_W, LANES), 0)
    lane = lax.broadcasted_iota(jnp.int32, (GRID_W, LANES), 1)
    kc = lane % GRID_W
    cs = jnp.clip(qc - WIN_W // 2, 0, GRID_W - WIN_W)
    col_ok = (kc >= cs) & (kc < cs + WIN_W)
    first = lane < GRID_W
    neg = jnp.full((GRID_W, LANES), NEG, F32)
    for h in range(NA_HEADS):
        for dr in range(n_dr):
            row = jnp.broadcast_to(rpb_ref[h, dr:dr + 1, :] * LOG2E, (GRID_W, LANES))
            lo = pltpu.roll(row, LANES - (WIN_W - 1), 1, stride=1, stride_axis=0)
            hi = pltpu.roll(row, GRID_W - (WIN_W - 1), 1, stride=1, stride_axis=0)
            t_ref[dr] = jnp.where(col_ok, jnp.where(first, lo, hi), neg)
        for kind in range(3):
            for qr in range(NA_ROWS):
                for kb in range(3):
                    for pr in range(NA_ROWS // 2):
                        ia, ib = (int(plan[kind, qr, kb, 2 * pr + s]) for s in range(2))
                        a = t_ref[ia] if ia >= 0 else neg
                        b = t_ref[ib] if ib >= 0 else neg
                        col0 = kb * NA_TOK + pr * LANES
                        bias_ref[kind, h, qr * GRID_W:(qr + 1) * GRID_W, col0:col0 + LANES] = jnp.where(first, a, b)


def _fold_lanes(blocks, op):
    tiles = [blk[:, j:j + LANES] for blk in blocks for j in range(0, blk.shape[1], LANES)]
    acc = tiles[0]
    for t in tiles[1:]:
        acc = op(acc, t)
    return acc


def _na_kernel(*refs, plan):
    nkb = NA_STEP + 2
    q_ref, k_refs, v_refs = refs[0], refs[1:1 + nkb], refs[1 + nkb:1 + 2 * nkb]
    kx_ref, vx_ref, g_ref, rpb_ref, o_ref, bias_ref, t_ref = refs[1 + 2 * nkb:]
    i, last = pl.program_id(1), pl.num_programs(1) - 1

    @pl.when((pl.program_id(0) == 0) & (i == 0))
    def _():
        _fill_na_bias(rpb_ref, bias_ref, t_ref, plan)

    head0 = _lane_is_head0((NA_TOK, LANES))
    for pi in range(PAIRS):
        cols = slice(pi * LANES, (pi + 1) * LANES)
        keys = [r[0, :, cols] for r in k_refs] + [kx_ref[0, :, cols]]
        zero = jnp.zeros_like(keys[0])
        vals = [r[0, :, cols] for r in v_refs] + [vx_ref[0, :, cols]]
        v_h0 = [jnp.where(head0, v, zero) for v in vals]
        v_h1 = [jnp.where(head0, zero, v) for v in vals]
        for j in range(NA_STEP):
            edge = 0 if j == 0 else 2 if j == NA_STEP - 1 else None
            kind = 1 if edge is None else jnp.where(i == (0 if edge == 0 else last), edge, 1)
            rows = slice(j * NA_TOK, (j + 1) * NA_TOK)
            band = [j, j + 1, j + 2, nkb]
            q = q_ref[0, rows, cols]
            probs, inv = [], []
            for hh in range(2):
                qh = jnp.where(head0, q, zero) if hh == 0 else jnp.where(head0, zero, q)
                s = [_dot_nt(qh, keys[band[t]]) + bias_ref[kind, 2 * pi + hh, :, t * NA_TOK:(t + 1) * NA_TOK]
                     for t in range(3)]
                s.append(_dot_nt(qh, keys[nkb]))
                m = jnp.max(_fold_lanes(s, jnp.maximum), axis=-1, keepdims=True)
                p = [jnp.exp2(st - m) for st in s]
                inv.append(1.0 / jnp.sum(_fold_lanes(p, jnp.add), axis=-1, keepdims=True))
                probs += [pt.astype(BF16) for pt in p]
            v_cat = jnp.concatenate([v_h0[t] for t in band] + [v_h1[t] for t in band], axis=0)
            o = _dot(jnp.concatenate(probs, axis=1), v_cat) * jnp.where(head0, inv[0], inv[1])
            o_ref[0, rows, cols] = (o * _silu(g_ref[0, rows, cols].astype(F32))).astype(o_ref.dtype)


def _neighbourhood(p, pc, rpb):
    b, n, _ = p.shape
    nblk = n // NA_TOK
    nstep = nblk // NA_STEP
    assert NA_STEP >= 2 and nstep >= 2
    grp = NA_W // LANES

    def key_blk(cb, shift):
        return pl.BlockSpec((1, NA_TOK, NA_W),
                            lambda bi, i: (bi, jnp.clip(NA_STEP * i + shift, 0, nblk - 1), cb // grp))

    def step_blk(cb):
        return pl.BlockSpec((1, NA_STEP * NA_TOK, NA_W), lambda bi, i: (bi, i, cb // grp))

    lc = pc.shape[1]

    def ctx(cb):
        return pl.BlockSpec((1, lc, NA_W), lambda bi, i: (bi, 0, cb // grp))

    n_dr, n_dc = 2 * WIN_H - 1, 2 * WIN_W - 1
    rpb_rows = jnp.pad(rpb.astype(F32), ((0, 0), (0, 16 - n_dr), (0, LANES - n_dc)))
    shifts = range(-1, NA_STEP + 1)
    return pl.pallas_call(
        functools.partial(_na_kernel, plan=_na_bias_plan(n // GRID_W)),
        grid=(b, nstep),
        in_specs=([step_blk(CB_NQ)] + [key_blk(CB_NK, s) for s in shifts] + [key_blk(CB_NV, s) for s in shifts]
                  + [ctx(CB_NK), ctx(CB_NV), step_blk(CB_NG),
                     pl.BlockSpec((NA_HEADS, 16, LANES), lambda bi, i: (0, 0, 0))]),
        out_specs=pl.BlockSpec((1, NA_STEP * NA_TOK, NA_W), lambda bi, i: (bi, i, 0)),
        out_shape=jax.ShapeDtypeStruct((b, n, NA_W), BF16),
        scratch_shapes=[pltpu.VMEM((3, NA_HEADS, NA_TOK, 3 * NA_TOK), F32),
                        pltpu.VMEM((n_dr, GRID_W, LANES), F32)],
        compiler_params=pltpu.CompilerParams(dimension_semantics=("arbitrary", "arbitrary")),
        name="neighbourhood",
    )(p, *([p] * (2 * len(shifts))), pc, pc, p, rpb_rows)


def _mix_out(x_ref, g, u_ref, gz_ref, u_before, u_after, yr_ref, yn_ref, cw, cbias, w, lg, lb, o_ref):
    u = u_ref[0].astype(F32)
    tm = u.shape[0]
    row = lax.broadcasted_iota(jnp.int32, u.shape, 0)
    u_prev = jnp.where(row == 0, u_before, pltpu.roll(u, 1, axis=0))
    u_next = jnp.where(row == tm - 1, u_after, pltpu.roll(u, tm - 1, axis=0))
    sub = min(OUT_SUB, tm)
    for r in range(tm // sub):
        rs = slice(r * sub, (r + 1) * sub)
        conv = u_prev[rs] * cw[0:1] + u[rs] * cw[1:2] + u_next[rs] * cw[2:3] + cbias
        y_conv = gz_ref[0, rs, :].astype(F32) * conv
        y = jnp.concatenate([y_conv.astype(BF16), yr_ref[0, rs, :], yn_ref[0, rs, :]], axis=-1)
        z = x_ref[0, rs, :] + g * _dot(y, w)
        mu = jnp.mean(z, axis=-1, keepdims=True)
        dlt = z - mu
        var = jnp.mean(dlt * dlt, axis=-1, keepdims=True)
        o_ref[0, rs, :] = dlt * lax.rsqrt(var + LN_EPS / DEEPNORM_ALPHA ** 2) * lg + lb


def _out_kernel(*refs, ctx_row):
    x_ref, mod_ref, u_ref, gz_ref, ub_ref, ua_ref, yr_ref, yn_ref = refs[:8]
    n_out = 1 if ctx_row is None else 2
    cw_ref, cbias_ref, w_ref, lg_ref, lb_ref = refs[-5 - n_out:-n_out]
    i = pl.program_id(1)
    halo = ub_ref.shape[1]
    u_before = jnp.where(i == 0, 0.0, ub_ref[0, halo - 1:halo, :].astype(F32))
    u_after = jnp.where(i == pl.num_programs(1) - 1, 0.0, ua_ref[0, 0:1, :].astype(F32))
    w = w_ref[...].astype(BF16)
    consts = (cw_ref[...], cbias_ref[...], w, lg_ref[...], lb_ref[...])

    def gate(row):
        return mod_ref[pl.ds(row, 1), 2 * D_MODEL:] * (1.0 / DEEPNORM_ALPHA)

    if ctx_row is None:
        _mix_out(x_ref, gate(pl.program_id(0)), u_ref, gz_ref, u_before, u_after, yr_ref, yn_ref, *consts, refs[-1])
    else:
        xc_ref, uc_ref, gzc_ref, yrc_ref, ync_ref = refs[8:13]
        o_ref, oc_ref = refs[-2:]
        _mix_out(x_ref, gate(pl.program_id(0)), u_ref, gz_ref, u_before, u_after, yr_ref, yn_ref, *consts, o_ref)

        @pl.when(i == 0)
        def _():
            _mix_out(xc_ref, gate(ctx_row), uc_ref, gzc_ref, 0.0, 0.0, yrc_ref, ync_ref, *consts, oc_ref)


def _output(x, mod, p, y_ret, y_na, conv_w, conv_b, w_out, layer, ln_g, ln_b, ctx=None):
    b, n, _ = x.shape
    tm = min(OUT_TILE, n)
    halo = 16
    per = tm // halo
    nh = n // halo

    def rows(width, cb=0):
        return pl.BlockSpec((1, tm, width), lambda bi, i: (bi, i, cb))

    def before(cb):
        return pl.BlockSpec((1, halo, CONV_W), lambda bi, i: (bi, jnp.maximum(i * per - 1, 0), cb))

    def after(cb):
        return pl.BlockSpec((1, halo, CONV_W), lambda bi, i: (bi, jnp.minimum((i + 1) * per, nh - 1), cb))

    def const(shape):
        return pl.BlockSpec(shape, lambda bi, i: (0,) * len(shape))

    in_specs = [rows(D_MODEL), pl.BlockSpec((None, 8, 3 * D_MODEL), lambda bi, i: (layer, 0, 0)),
                rows(CONV_W, CB_U), rows(CONV_W, CB_GZ), before(CB_U), after(CB_U), rows(RET_W), rows(NA_W)]
    args = [x, mod, p, p, p, p, y_ret, y_na]
    out_specs, out_shape = rows(D_MODEL), jax.ShapeDtypeStruct((b, n, D_MODEL), F32)
    if ctx is not None:
        xc, pc, yc_ret, yc_na = ctx
        lc = xc.shape[1]

        def seq(width, cb=0):
            return pl.BlockSpec((1, lc, width), lambda bi, i: (bi, 0, cb))

        in_specs += [seq(D_MODEL), seq(CONV_W, CB_U), seq(CONV_W, CB_GZ), seq(RET_W), seq(NA_W)]
        args += [xc, pc, pc, yc_ret, yc_na]
        out_specs, out_shape = [out_specs, seq(D_MODEL)], [out_shape, jax.ShapeDtypeStruct((b, lc, D_MODEL), F32)]
    in_specs += [const((3, CONV_W)), const((1, CONV_W)),
                 pl.BlockSpec((None, MIX_W, D_MODEL), lambda bi, i: (layer, 0, 0), pipeline_mode=pl.Buffered(1)),
                 const((1, D_MODEL)), const((1, D_MODEL))]
    args += [conv_w, conv_b.reshape(1, CONV_W), w_out, ln_g.reshape(1, D_MODEL), ln_b.reshape(1, D_MODEL)]
    return pl.pallas_call(
        functools.partial(_out_kernel, ctx_row=None if ctx is None else b),
        grid=(b, n // tm),
        in_specs=in_specs,
        out_specs=out_specs,
        out_shape=out_shape,
        compiler_params=pltpu.CompilerParams(dimension_semantics=("parallel", "arbitrary")),
        name="output",
    )(*args)


def _out_piped_kernel(mod_ref, cw_ref, cbias_ref, w_ref, lg_ref, lb_ref, x_hbm, p_hbm, yr_hbm, yn_hbm, o_hbm,
                      *, layer, tm):
    b, n, _ = x_hbm.shape
    nt = n // tm
    halo = 16
    per, nh = tm // halo, n // halo
    consts = (cw_ref[...], cbias_ref[...], w_ref[layer].astype(BF16), lg_ref[...], lb_ref[...])

    def body(idx, x_ref, u_ref, gz_ref, ub_ref, ua_ref, yr_ref, yn_ref, o_ref):
        bi, i = idx
        g = mod_ref[layer, pl.ds(bi, 1), 2 * D_MODEL:] * (1.0 / DEEPNORM_ALPHA)
        u_before = jnp.where(i == 0, 0.0, ub_ref[0, halo - 1:halo, :].astype(F32))
        u_after = jnp.where(i == nt - 1, 0.0, ua_ref[0, 0:1, :].astype(F32))
        _mix_out(x_ref, g, u_ref, gz_ref, u_before, u_after, yr_ref, yn_ref, *consts, o_ref)

    def rows(width, cb=0):
        return pl.BlockSpec((1, tm, width), lambda bi, i: (bi, i, cb), pipeline_mode=pl.Buffered(3))

    before = pl.BlockSpec((1, halo, CONV_W), lambda bi, i: (bi, jnp.maximum(i * per - 1, 0), CB_U))
    after = pl.BlockSpec((1, halo, CONV_W), lambda bi, i: (bi, jnp.minimum((i + 1) * per, nh - 1), CB_U))
    pltpu.emit_pipeline(
        body, grid=(b, nt),
        in_specs=[rows(D_MODEL), rows(CONV_W, CB_U), rows(CONV_W, CB_GZ), before, after, rows(RET_W), rows(NA_W)],
        out_specs=[pl.BlockSpec((1, tm, D_MODEL), lambda bi, i: (bi, i, 0))],
        _explicit_indices=True,
    )(x_hbm, p_hbm, p_hbm, p_hbm, p_hbm, yr_hbm, yn_hbm, o_hbm)


def _output_piped(x, mod, p, y_ret, y_na, conv_w, conv_b, w_out, layer, ln_g, ln_b):
    b, n, _ = x.shape
    vmem = pl.BlockSpec(memory_space=pltpu.VMEM)
    hbm = pl.BlockSpec(memory_space=pl.ANY)
    return pl.pallas_call(
        functools.partial(_out_piped_kernel, layer=layer, tm=min(ROW_TILE, n)),
        in_specs=[vmem] * 6 + [hbm] * 4,
        out_specs=hbm,
        out_shape=jax.ShapeDtypeStruct((b, n, D_MODEL), F32),
        name="output_piped",
    )(mod, conv_w, conv_b.reshape(1, CONV_W), w_out, ln_g.reshape(1, D_MODEL), ln_b.reshape(1, D_MODEL),
      x, p, y_ret, y_na)


def kernel(x, c, ctx, c_ctx, w_mod, b_mod, w_in, conv_w, conv_b, ret_decay, na_rpb, w_out, ln_g, ln_b):
    b, n, d = x.shape
    act_t = jnp.pad(jnp.concatenate([c, c_ctx[None]], axis=0).T, ((0, 0), (0, 5)))
    mod = _modulation(act_t, w_mod, b_mod)
    tables = _rope_tables(n)
    xc = ctx
    for l in range(DEPTH):
        need_ctx = l < DEPTH - 1
        p, pc = _projection(x, xc, mod, w_in, l, tables, PROJ_W if need_ctx else KV_W)
        y_ret = _retention(p, pc, ret_decay[l], init_state=True)
        y_na = _neighbourhood(p, pc, na_rpb[l])
        ctx_mix = (xc, pc) + tuple(_retention(pc, None, ret_decay[l], init_state=False, attend=True)) if need_ctx else None
        if need_ctx:
            x, xc = _output(x, mod, p, y_ret, y_na, conv_w[l], conv_b[l], w_out, l, ln_g[l], ln_b[l], ctx_mix)
        else:
            x = _output_piped(x, mod, p, y_ret, y_na, conv_w[l], conv_b[l], w_out, l, ln_g[l], ln_b[l])
    return x
```

```python
import functools

import numpy as np
import jax
import jax.numpy as jnp
from jax import lax
from jax.experimental import pallas as pl
from jax.experimental.pallas import tpu as pltpu

D_MODEL = 1024
DEPTH = 2
GRID_W = 64
HEAD_DIM = 64
CONV_W = 256
RET_HEADS = 6
RET_W = RET_HEADS * HEAD_DIM
NA_HEADS = 6
NA_W = NA_HEADS * HEAD_DIM
MIX_W = CONV_W + RET_W + NA_W
RET_CHUNK = 128
WIN_H = 8
WIN_W = 16
ROPE_BASE = 10000.0
LN_EPS = 1e-5
DEEPNORM_ALPHA = (2 * DEPTH) ** 0.25
PROJ_SPLITS = (RET_W, RET_W, NA_W, NA_W, RET_W, RET_W, NA_W, NA_W, CONV_W, CONV_W, CONV_W, CONV_W)
PROJ_W = sum(PROJ_SPLITS)
KV_W = 2 * RET_W + 2 * NA_W

LANES = 128
PAIRS = RET_HEADS // 2
CB_RK, CB_RV, CB_NK, CB_NV, CB_RQ, CB_RG, CB_NQ, CB_NG = (sum(PROJ_SPLITS[:j]) // LANES for j in range(8))
CONV_OFF = PROJ_W - 4 * CONV_W
P_W = CONV_OFF + 2 * CONV_W
CB_U, CB_GZ = CONV_OFF // CONV_W, CONV_OFF // CONV_W + 1
QK_SCALE = HEAD_DIM ** -0.5
NEG = -1e30
LOG2E = 1.4426950408889634
NA_ROWS = 4
NA_TOK = NA_ROWS * GRID_W
NA_STEP = 4
RET_GROUP = 64
ROW_TILE = 1024
OUT_TILE = 2048
OUT_SUB = 256
PROJ_CHUNK = 512

F32 = jnp.float32
BF16 = jnp.bfloat16


def _silu(v):
    return v * jax.nn.sigmoid(v)


def _dot(a, b):
    return jnp.dot(a, b, preferred_element_type=F32)


def _dot_nt(a, b):
    return lax.dot_general(a, b, (((1,), (1,)), ((), ())), preferred_element_type=F32)


def _dot_tn(a, b):
    return lax.dot_general(a, b, (((0,), (0,)), ((), ())), preferred_element_type=F32)


def _lane_is_head0(shape):
    return lax.broadcasted_iota(jnp.int32, shape, len(shape) - 1) < HEAD_DIM


def _mod_kernel(act_ref, w_ref, b_ref, o_ref):
    a = _silu(act_ref[...])
    w = w_ref[0]
    bias = b_ref[0]
    for r in range(3):
        o_ref[0, r:r + 1, :] = jnp.sum(a[:, r:r + 1] * w, axis=0, keepdims=True) + bias
    o_ref[0, 3:8, :] = jnp.zeros((5, w.shape[1]), F32)


def _modulation(act_t, w_mod, b_mod):
    tn = 1536
    n = w_mod.shape[-1]
    return pl.pallas_call(
        _mod_kernel,
        grid=(DEPTH, n // tn),
        in_specs=[pl.BlockSpec((D_MODEL, 8), lambda l, j: (0, 0)),
                  pl.BlockSpec((1, D_MODEL, tn), lambda l, j: (l, 0, j)),
                  pl.BlockSpec((1, 1, tn), lambda l, j: (l, 0, j))],
        out_specs=pl.BlockSpec((1, 8, tn), lambda l, j: (l, 0, j)),
        out_shape=jax.ShapeDtypeStruct((DEPTH, 8, n), F32),
        name="modulation",
    )(act_t, w_mod, b_mod.reshape(DEPTH, 1, n))


def _rope_tables(n):
    rows = n // GRID_W
    nf = HEAD_DIM // 4
    inv = ROPE_BASE ** (-np.arange(nf, dtype=np.float64) / nf)
    lane = np.arange(LANES)
    by_row = ((lane % HEAD_DIM) < HEAD_DIM // 2)[None]
    first = ((lane % (2 * nf)) < nf)[None]
    ang_r = np.tile(np.arange(rows, dtype=np.float64)[:, None] * inv, (1, LANES // nf))
    ang_c = np.tile(np.arange(GRID_W, dtype=np.float64)[:, None] * inv, (1, LANES // nf))
    parts = []
    for ang, own in ((ang_r, by_row), (ang_c, ~by_row)):
        cos, sin = np.cos(ang), np.sin(ang)
        parts += [np.where(own, cos, 0.0), np.where(own & first, -sin, 0.0), np.where(own & ~first, sin, 0.0)]
    return tuple(jnp.asarray(t, F32) for t in parts)


def _project(x, mod_row, w_ref, tables, o_ref, n_cols):
    shift, scale = mod_row[:, :D_MODEL], mod_row[:, D_MODEL:2 * D_MODEL]
    h = (x * (1.0 + scale) + shift).astype(BF16)
    rotated = tuple(range(CB_RK, CB_RK + PAIRS)) + tuple(range(CB_RQ, CB_RQ + PAIRS))
    col_scale = {cb: QK_SCALE for cb in range(CB_RK, CB_RK + PAIRS)}
    col_scale.update({cb: QK_SCALE * LOG2E for cb in range(CB_NQ, CB_NQ + PAIRS)})
    for off in range(0, min(n_cols, CONV_OFF), PROJ_CHUNK):
        acc = _dot(h, w_ref[:, off:off + PROJ_CHUNK].astype(BF16))
        tiles = []
        for j in range(PROJ_CHUNK // LANES):
            cb = off // LANES + j
            v = acc[:, j * LANES:(j + 1) * LANES]
            if tables is not None and cb in rotated:
                cos, up, dn = tables
                v = v * cos + pltpu.roll(v, LANES - 16, axis=1) * up + pltpu.roll(v, 16, axis=1) * dn
            if cb in col_scale:
                v = v * col_scale[cb]
            tiles.append(v.astype(BF16))
        o_ref[0, :, off:off + PROJ_CHUNK] = jnp.concatenate(tiles, axis=1)
    if n_cols > CONV_OFF:
        ch, cb, cc, cz = (_dot(h, w_ref[:, CONV_OFF + j * CONV_W:CONV_OFF + (j + 1) * CONV_W].astype(BF16))
                          for j in range(4))
        o_ref[0, :, CONV_OFF:CONV_OFF + CONV_W] = (cc * ch).astype(BF16)
        o_ref[0, :, CONV_OFF + CONV_W:CONV_OFF + 2 * CONV_W] = (cb * _silu(cz)).astype(BF16)


def _proj_kernel(x_ref, xc_ref, mod_ref, w_ref, *rest, ctx_row, n_cols_ctx):
    *table_refs, o_ref, oc_ref = rest
    tables = tuple(jnp.concatenate([r_ref[g:g + 1, :] + c_ref[...] for g in range(r_ref.shape[0])], axis=0)
                   for r_ref, c_ref in zip(table_refs[:3], table_refs[3:]))
    _project(x_ref[0], mod_ref[pl.ds(pl.program_id(0), 1), :], w_ref, tables, o_ref, PROJ_W)

    @pl.when(pl.program_id(1) == 0)
    def _():
        _project(xc_ref[0], mod_ref[ctx_row:ctx_row + 1, :], w_ref, None, oc_ref, n_cols_ctx)


def _projection(x, xc, mod, w_in, layer, tables, n_cols_ctx):
    b, n, _ = x.shape
    lc = xc.shape[1]
    tm = min(ROW_TILE, n)
    pc_w = P_W if n_cols_ctx > CONV_OFF else n_cols_ctx
    in_specs = ([pl.BlockSpec((1, tm, D_MODEL), lambda bi, i: (bi, i, 0)),
                 pl.BlockSpec((1, lc, D_MODEL), lambda bi, i: (bi, 0, 0)),
                 pl.BlockSpec((None, 8, 3 * D_MODEL), lambda bi, i: (layer, 0, 0)),
                 pl.BlockSpec((None, D_MODEL, PROJ_W), lambda bi, i: (layer, 0, 0), pipeline_mode=pl.Buffered(1))]
                + [pl.BlockSpec((tm // GRID_W, LANES), lambda bi, i: (i, 0))] * 3
                + [pl.BlockSpec((GRID_W, LANES), lambda bi, i: (0, 0))] * 3)
    return pl.pallas_call(
        functools.partial(_proj_kernel, ctx_row=b, n_cols_ctx=n_cols_ctx),
        grid=(b, n // tm),
        in_specs=in_specs,
        out_specs=[pl.BlockSpec((1, tm, P_W), lambda bi, i: (bi, i, 0)),
                   pl.BlockSpec((1, lc, pc_w), lambda bi, i: (bi, 0, 0))],
        out_shape=[jax.ShapeDtypeStruct((b, n, P_W), BF16), jax.ShapeDtypeStruct((b, lc, pc_w), BF16)],
        compiler_params=pltpu.CompilerParams(dimension_semantics=("parallel", "arbitrary")),
        name="projection",
    )(x, xc, mod, w_in, *tables)


def _log_sigmoid(v):
    return jnp.minimum(v, 0.0) - jnp.log1p(jnp.exp(-jnp.abs(v)))


def _block_diag(m):
    r = lax.broadcasted_iota(jnp.int32, m.shape, 0) < HEAD_DIM
    c = lax.broadcasted_iota(jnp.int32, m.shape, 1) < HEAD_DIM
    return jnp.where(r == c, m, 0.0)


def _head_mean(x):
    head0 = _lane_is_head0(x.shape)
    s0 = jnp.sum(jnp.where(head0, x, 0.0), axis=-1, keepdims=True)
    s1 = jnp.sum(jnp.where(head0, 0.0, x), axis=-1, keepdims=True)
    return jnp.where(head0, s0, s1) * (1.0 / HEAD_DIM)


def _attend(q_ref, k_ref, v_ref, g_ref, o_ref):
    q, k, v = q_ref[0], k_ref[0], v_ref[0]
    head0 = _lane_is_head0(q.shape)
    zero = jnp.zeros_like(q)
    outs = []
    for hh in range(2):
        qh = jnp.where(head0, q, zero) if hh == 0 else jnp.where(head0, zero, q)
        s = _dot_nt(qh, k)
        p = jnp.exp2(s - jnp.max(s, axis=-1, keepdims=True))
        outs.append(_dot(p.astype(BF16), v) / jnp.sum(p, axis=-1, keepdims=True))
    o = jnp.where(head0, outs[0], outs[1])
    o_ref[0] = (o * _silu(g_ref[0].astype(F32))).astype(o_ref.dtype)


def _ret_kernel(*refs, init_state, attend, group):
    refs = list(refs)
    dec_ref, q_ref, k_ref, v_ref, g_ref = refs[:5]
    refs = refs[5:]
    if init_state:
        kc_ref, vc_ref = refs[:2]
        refs = refs[2:]
    if attend:
        attn_refs, refs = refs[:4], refs[4:]
        o_ref, ao_ref, s_ref, u_ref, kt_ref, wq_ref, wkt_ref, d_ref = refs
        _attend(*attn_refs, ao_ref)
    else:
        o_ref, s_ref, u_ref, kt_ref, wq_ref, wkt_ref, d_ref = refs
    c = RET_CHUNK
    n_chunks = q_ref.shape[1] // c
    n_groups = n_chunks // group
    head0 = _lane_is_head0((c, LANES))
    lg_f = _log_sigmoid(dec_ref[0, 0])
    lg_b = _log_sigmoid(dec_ref[1, 0])
    i = lax.broadcasted_iota(jnp.int32, (c, LANES), 0).astype(F32)
    wq_ref[:, :LANES] = jnp.exp((i + 1.0) * lg_f)
    wq_ref[:, LANES:] = jnp.exp((c - i) * lg_b)
    wkt_ref[:LANES, :] = jnp.exp((c - 1.0 - i) * lg_f).T
    wkt_ref[LANES:, :] = jnp.exp(i * lg_b).T
    gc_f = jnp.exp(float(c) * lg_f)
    gc_b = jnp.exp(float(c) * lg_b)
    diff = (lax.broadcasted_iota(jnp.int32, (c, c), 0) - lax.broadcasted_iota(jnp.int32, (c, c), 1)).astype(F32)
    lower = diff >= 0
    for hh in range(2):
        lf = lg_f[:, hh * HEAD_DIM:hh * HEAD_DIM + 1]
        lb = lg_b[:, hh * HEAD_DIM:hh * HEAD_DIM + 1]
        d_ref[:, hh * c:(hh + 1) * c] = jnp.where(lower, jnp.exp(jnp.where(lower, diff, 0.0) * lf),
                                                  jnp.exp(jnp.where(lower, 0.0, -diff) * lb))
    if init_state:
        lc = kc_ref.shape[1]
        m = lax.broadcasted_iota(jnp.int32, (lc, LANES), 0).astype(F32)
        kcf = kc_ref[0].astype(F32)
        s_f0 = _block_diag(_dot_tn((kcf * jnp.exp((lc - 1.0 - m) * lg_f)).astype(BF16), vc_ref[0]))
        s_b0 = _block_diag(_dot_tn((kcf * jnp.exp(m * lg_b)).astype(BF16), vc_ref[0]))
    else:
        s_f0 = s_b0 = jnp.zeros((LANES, LANES), F32)
    r2 = lax.broadcasted_iota(jnp.int32, (2 * LANES, LANES), 0) % LANES < HEAD_DIM
    c2 = lax.broadcasted_iota(jnp.int32, (2 * LANES, LANES), 1) < HEAD_DIM
    diag2 = r2 == c2

    chan0 = lax.broadcasted_iota(jnp.int32, (LANES, c), 0) < HEAD_DIM

    def chunk(n):
        return pl.ds(pl.multiple_of(n * c, c), c)

    def increments(gi, carry):
        for j in range(group):
            n = gi * group + j
            ktb = k_ref[0, chunk(n), :].T
            kt = ktb.astype(F32)
            zero = jnp.zeros_like(ktb)
            kt_ref[n] = jnp.concatenate([jnp.where(chan0, ktb, zero), jnp.where(chan0, zero, ktb)], axis=1)
            lhs = jnp.concatenate([kt * wkt_ref[:LANES, :], kt * wkt_ref[LANES:, :]], axis=0).astype(BF16)
            u_ref[n] = jnp.where(diag2, _dot(lhs, v_ref[0, chunk(n), :]), 0.0)
        return carry

    lax.fori_loop(0, n_groups, increments, 0)

    def fwd_scan(gi, s):
        for j in range(group):
            n = gi * group + j
            s_ref[n, :LANES, :] = s.astype(BF16)
            s = s * gc_f + u_ref[n, :LANES, :]
        return s

    def bwd_scan(gi, s):
        for j in range(group):
            n = n_chunks - 1 - (gi * group + j)
            s_ref[n, LANES:, :] = s.astype(BF16)
            s = s * gc_b + u_ref[n, LANES:, :]
        return s

    lax.fori_loop(0, n_groups, fwd_scan, s_f0)
    lax.fori_loop(0, n_groups, bwd_scan, s_b0)

    def body(gi, carry):
        outs = []
        for j in range(group):
            n = gi * group + j
            rows = chunk(n)
            q, v = q_ref[0, rows, :], v_ref[0, rows, :]
            zero = jnp.zeros_like(v)
            vcat = jnp.concatenate([jnp.where(head0, v, zero), jnp.where(head0, zero, v)], axis=0)
            scores = _dot(q, kt_ref[n]) * d_ref[...]
            qf = q.astype(F32)
            qw = jnp.concatenate([qf * wq_ref[:, :LANES], qf * wq_ref[:, LANES:]], axis=1).astype(BF16)
            outs.append(_dot(scores.astype(BF16), vcat) + _dot(qw, s_ref[n]))
        o = jnp.concatenate(outs, axis=0)
        rows = pl.ds(pl.multiple_of(gi * (group * c), group * c), group * c)
        dlt = o - _head_mean(o)
        var = _head_mean(dlt * dlt)
        y = dlt * lax.rsqrt(var + LN_EPS) * _silu(g_ref[0, rows, :].astype(F32))
        o_ref[0, rows, :] = y.astype(o_ref.dtype)
        return carry

    lax.fori_loop(0, n_groups, body, 0)


def _retention(p, pc, ret_decay, init_state, attend=False):
    b, n, _ = p.shape
    dec = jnp.repeat(ret_decay.astype(F32), HEAD_DIM, axis=-1).reshape(2, PAIRS, 1, LANES)

    def col(cb):
        return pl.BlockSpec((1, n, LANES), lambda bi, pi: (bi, 0, cb + pi))

    in_specs = [pl.BlockSpec((2, 1, 1, LANES), lambda bi, pi: (0, pi, 0, 0)),
                col(CB_RQ), col(CB_RK), col(CB_RV), col(CB_RG)]
    args = [dec, p, p, p, p]
    if init_state:
        lc = pc.shape[1]
        in_specs += [pl.BlockSpec((1, lc, LANES), lambda bi, pi: (bi, 0, CB_RK + pi)),
                     pl.BlockSpec((1, lc, LANES), lambda bi, pi: (bi, 0, CB_RV + pi))]
        args += [pc, pc]
    out_specs, out_shape = col(0), jax.ShapeDtypeStruct((b, n, RET_W), BF16)
    if attend:
        in_specs += [col(CB_NQ), col(CB_NK), col(CB_NV), col(CB_NG)]
        args += [p, p, p, p]
        out_specs, out_shape = [out_specs, col(0)], [out_shape, jax.ShapeDtypeStruct((b, n, NA_W), BF16)]
    c = RET_CHUNK
    return pl.pallas_call(
        functools.partial(_ret_kernel, init_state=init_state, attend=attend, group=min(RET_GROUP, n // c)),
        grid=(b, PAIRS),
        in_specs=in_specs,
        out_specs=out_specs,
        out_shape=out_shape,
        scratch_shapes=[pltpu.VMEM((n // c, 2 * LANES, LANES), BF16),
                        pltpu.VMEM((n // c, 2 * LANES, LANES), F32),
                        pltpu.VMEM((n // c, LANES, 2 * c), BF16),
                        pltpu.VMEM((c, 2 * LANES), F32),
                        pltpu.VMEM((2 * LANES, c), F32),
                        pltpu.VMEM((c, 2 * c), F32)],
        compiler_params=pltpu.CompilerParams(dimension_semantics=("parallel", "parallel")),
        name="retention",
    )(*args)


def _na_bias_plan(rows):
    nblk = rows // NA_ROWS
    win_h = min(WIN_H, rows)
    plan = np.full((3, NA_ROWS, 3, NA_ROWS), -1, np.int64)
    for kind, blk in enumerate((0, 1, nblk - 1)):
        for qr in range(NA_ROWS):
            r = blk * NA_ROWS + qr
            rs = min(max(r - win_h // 2, 0), rows - win_h)
            for kb in range(3):
                if not 0 <= blk + kb - 1 <= nblk - 1:
                    continue
                for krl in range(NA_ROWS):
                    kr = (blk + kb - 1) * NA_ROWS + krl
                    if rs <= kr < rs + win_h:
                        plan[kind, qr, kb, krl] = kr - r + WIN_H - 1
    return plan


def _fill_na_bias(rpb_ref, bias_ref, t_ref, plan):
    n_dr = 2 * WIN_H - 1
    qc = lax.broadcasted_iota(jnp.int32, (GRID_W, LANES), 0)
    lane = lax.broadcasted_iota(jnp.int32, (GRID_W, LANES), 1)
    kc = lane % GRID_W
    cs = jnp.clip(qc - WIN_W // 2, 0, GRID_W - WIN_W)
    col_ok = (kc >= cs) & (kc < cs + WIN_W)
    first = lane < GRID_W
    neg = jnp.full((GRID_W, LANES), NEG, F32)
    for h in range(NA_HEADS):
        for dr in range(n_dr):
            row = jnp.broadcast_to(rpb_ref[h, dr:dr + 1, :] * LOG2E, (GRID_W, LANES))
            lo = pltpu.roll(row, LANES - (WIN_W - 1), 1, stride=1, stride_axis=0)
            hi = pltpu.roll(row, GRID_W - (WIN_W - 1), 1, stride=1, stride_axis=0)
            t_ref[dr] = jnp.where(col_ok, jnp.where(first, lo, hi), neg)
        for kind in range(3):
            for qr in range(NA_ROWS):
                for kb in range(3):
                    for pr in range(NA_ROWS // 2):
                        ia, ib = (int(plan[kind, qr, kb, 2 * pr + s]) for s in range(2))
                        a = t_ref[ia] if ia >= 0 else neg
                        b = t_ref[ib] if ib >= 0 else neg
                        col0 = kb * NA_TOK + pr * LANES
                        bias_ref[kind, h, qr * GRID_W:(qr + 1) * GRID_W, col0:col0 + LANES] = jnp.where(first, a, b)


def _fold_lanes(blocks, op):
    tiles = [blk[:, j:j + LANES] for blk in blocks for j in range(0, blk.shape[1], LANES)]
    acc = tiles[0]
    for t in tiles[1:]:
        acc = op(acc, t)
    return acc


def _na_kernel(*refs, plan):
    nkb = NA_STEP + 2
    q_ref, k_refs, v_refs = refs[0], refs[1:1 + nkb], refs[1 + nkb:1 + 2 * nkb]
    kx_ref, vx_ref, g_ref, rpb_ref, o_ref, bias_ref, t_ref = refs[1 + 2 * nkb:]
    i, last = pl.program_id(1), pl.num_programs(1) - 1

    @pl.when((pl.program_id(0) == 0) & (i == 0))
    def _():
        _fill_na_bias(rpb_ref, bias_ref, t_ref, plan)

    head0 = _lane_is_head0((NA_TOK, LANES))
    for pi in range(PAIRS):
        cols = slice(pi * LANES, (pi + 1) * LANES)
        keys = [r[0, :, cols] for r in k_refs] + [kx_ref[0, :, cols]]
        zero = jnp.zeros_like(keys[0])
        vals = [r[0, :, cols] for r in v_refs] + [vx_ref[0, :, cols]]
        v_h0 = [jnp.where(head0, v, zero) for v in vals]
        v_h1 = [jnp.where(head0, zero, v) for v in vals]
        for j in range(NA_STEP):
            edge = 0 if j == 0 else 2 if j == NA_STEP - 1 else None
            kind = 1 if edge is None else jnp.where(i == (0 if edge == 0 else last), edge, 1)
            rows = slice(j * NA_TOK, (j + 1) * NA_TOK)
            band = [j, j + 1, j + 2, nkb]
            q = q_ref[0, rows, cols]
            probs, inv = [], []
            for hh in range(2):
                qh = jnp.where(head0, q, zero) if hh == 0 else jnp.where(head0, zero, q)
                s = [_dot_nt(qh, keys[band[t]]) + bias_ref[kind, 2 * pi + hh, :, t * NA_TOK:(t + 1) * NA_TOK]
                     for t in range(3)]
                s.append(_dot_nt(qh, keys[nkb]))
                m = jnp.max(_fold_lanes(s, jnp.maximum), axis=-1, keepdims=True)
                p = [jnp.exp2(st - m) for st in s]
                inv.append(1.0 / jnp.sum(_fold_lanes(p, jnp.add), axis=-1, keepdims=True))
                probs += [pt.astype(BF16) for pt in p]
            v_cat = jnp.concatenate([v_h0[t] for t in band] + [v_h1[t] for t in band], axis=0)
            o = _dot(jnp.concatenate(probs, axis=1), v_cat) * jnp.where(head0, inv[0], inv[1])
            o_ref[0, rows, cols] = (o * _silu(g_ref[0, rows, cols].astype(F32))).astype(o_ref.dtype)


def _neighbourhood(p, pc, rpb):
    b, n, _ = p.shape
    nblk = n // NA_TOK
    nstep = nblk // NA_STEP
    assert NA_STEP >= 2 and nstep >= 2
    grp = NA_W // LANES

    def key_blk(cb, shift):
        return pl.BlockSpec((1, NA_TOK, NA_W),
                            lambda bi, i: (bi, jnp.clip(NA_STEP * i + shift, 0, nblk - 1), cb // grp))

    def step_blk(cb):
        return pl.BlockSpec((1, NA_STEP * NA_TOK, NA_W), lambda bi, i: (bi, i, cb // grp))

    lc = pc.shape[1]

    def ctx(cb):
        return pl.BlockSpec((1, lc, NA_W), lambda bi, i: (bi, 0, cb // grp))

    n_dr, n_dc = 2 * WIN_H - 1, 2 * WIN_W - 1
    rpb_rows = jnp.pad(rpb.astype(F32), ((0, 0), (0, 16 - n_dr), (0, LANES - n_dc)))
    shifts = range(-1, NA_STEP + 1)
    return pl.pallas_call(
        functools.partial(_na_kernel, plan=_na_bias_plan(n // GRID_W)),
        grid=(b, nstep),
        in_specs=([step_blk(CB_NQ)] + [key_blk(CB_NK, s) for s in shifts] + [key_blk(CB_NV, s) for s in shifts]
                  + [ctx(CB_NK), ctx(CB_NV), step_blk(CB_NG),
                     pl.BlockSpec((NA_HEADS, 16, LANES), lambda bi, i: (0, 0, 0))]),
        out_specs=pl.BlockSpec((1, NA_STEP * NA_TOK, NA_W), lambda bi, i: (bi, i, 0)),
        out_shape=jax.ShapeDtypeStruct((b, n, NA_W), BF16),
        scratch_shapes=[pltpu.VMEM((3, NA_HEADS, NA_TOK, 3 * NA_TOK), F32),
                        pltpu.VMEM((n_dr, GRID_W, LANES), F32)],
        compiler_params=pltpu.CompilerParams(dimension_semantics=("arbitrary", "arbitrary")),
        name="neighbourhood",
    )(p, *([p] * (2 * len(shifts))), pc, pc, p, rpb_rows)


def _mix_out(x_ref, g, u_ref, gz_ref, u_before, u_after, yr_ref, yn_ref, cw, cbias, w, lg, lb, o_ref):
    u = u_ref[0].astype(F32)
    tm = u.shape[0]
    row = lax.broadcasted_iota(jnp.int32, u.shape, 0)
    u_prev = jnp.where(row == 0, u_before, pltpu.roll(u, 1, axis=0))
    u_next = jnp.where(row == tm - 1, u_after, pltpu.roll(u, tm - 1, axis=0))
    sub = min(OUT_SUB, tm)
    for r in range(tm // sub):
        rs = slice(r * sub, (r + 1) * sub)
        conv = u_prev[rs] * cw[0:1] + u[rs] * cw[1:2] + u_next[rs] * cw[2:3] + cbias
        y_conv = gz_ref[0, rs, :].astype(F32) * conv
        y = jnp.concatenate([y_conv.astype(BF16), yr_ref[0, rs, :], yn_ref[0, rs, :]], axis=-1)
        z = x_ref[0, rs, :] + g * _dot(y, w)
        mu = jnp.mean(z, axis=-1, keepdims=True)
        dlt = z - mu
        var = jnp.mean(dlt * dlt, axis=-1, keepdims=True)
        o_ref[0, rs, :] = dlt * lax.rsqrt(var + LN_EPS / DEEPNORM_ALPHA ** 2) * lg + lb


def _out_kernel(*refs, ctx_row):
    x_ref, mod_ref, u_ref, gz_ref, ub_ref, ua_ref, yr_ref, yn_ref = refs[:8]
    n_out = 1 if ctx_row is None else 2
    cw_ref, cbias_ref, w_ref, lg_ref, lb_ref = refs[-5 - n_out:-n_out]
    i = pl.program_id(1)
    halo = ub_ref.shape[1]
    u_before = jnp.where(i == 0, 0.0, ub_ref[0, halo - 1:halo, :].astype(F32))
    u_after = jnp.where(i == pl.num_programs(1) - 1, 0.0, ua_ref[0, 0:1, :].astype(F32))
    w = w_ref[...].astype(BF16)
    consts = (cw_ref[...], cbias_ref[...], w, lg_ref[...], lb_ref[...])

    def gate(row):
        return mod_ref[pl.ds(row, 1), 2 * D_MODEL:] * (1.0 / DEEPNORM_ALPHA)

    if ctx_row is None:
        _mix_out(x_ref, gate(pl.program_id(0)), u_ref, gz_ref, u_before, u_after, yr_ref, yn_ref, *consts, refs[-1])
    else:
        xc_ref, uc_ref, gzc_ref, yrc_ref, ync_ref = refs[8:13]
        o_ref, oc_ref = refs[-2:]
        _mix_out(x_ref, gate(pl.program_id(0)), u_ref, gz_ref, u_before, u_after, yr_ref, yn_ref, *consts, o_ref)

        @pl.when(i == 0)
        def _():
            _mix_out(xc_ref, gate(ctx_row), uc_ref, gzc_ref, 0.0, 0.0, yrc_ref, ync_ref, *consts, oc_ref)


def _output(x, mod, p, y_ret, y_na, conv_w, conv_b, w_out, layer, ln_g, ln_b, ctx=None):
    b, n, _ = x.shape
    tm = min(OUT_TILE, n)
    halo = 16
    per = tm // halo
    nh = n // halo

    def rows(width, cb=0):
        return pl.BlockSpec((1, tm, width), lambda bi, i: (bi, i, cb))

    def before(cb):
        return pl.BlockSpec((1, halo, CONV_W), lambda bi, i: (bi, jnp.maximum(i * per - 1, 0), cb))

    def after(cb):
        return pl.BlockSpec((1, halo, CONV_W), lambda bi, i: (bi, jnp.minimum((i + 1) * per, nh - 1), cb))

    def const(shape):
        return pl.BlockSpec(shape, lambda bi, i: (0,) * len(shape))

    in_specs = [rows(D_MODEL), pl.BlockSpec((None, 8, 3 * D_MODEL), lambda bi, i: (layer, 0, 0)),
                rows(CONV_W, CB_U), rows(CONV_W, CB_GZ), before(CB_U), after(CB_U), rows(RET_W), rows(NA_W)]
    args = [x, mod, p, p, p, p, y_ret, y_na]
    out_specs, out_shape = rows(D_MODEL), jax.ShapeDtypeStruct((b, n, D_MODEL), F32)
    if ctx is not None:
        xc, pc, yc_ret, yc_na = ctx
        lc = xc.shape[1]

        def seq(width, cb=0):
            return pl.BlockSpec((1, lc, width), lambda bi, i: (bi, 0, cb))

        in_specs += [seq(D_MODEL), seq(CONV_W, CB_U), seq(CONV_W, CB_GZ), seq(RET_W), seq(NA_W)]
        args += [xc, pc, pc, yc_ret, yc_na]
        out_specs, out_shape = [out_specs, seq(D_MODEL)], [out_shape, jax.ShapeDtypeStruct((b, lc, D_MODEL), F32)]
    in_specs += [const((3, CONV_W)), const((1, CONV_W)),
                 pl.BlockSpec((None, MIX_W, D_MODEL), lambda bi, i: (layer, 0, 0), pipeline_mode=pl.Buffered(1)),
                 const((1, D_MODEL)), const((1, D_MODEL))]
    args += [conv_w, conv_b.reshape(1, CONV_W), w_out, ln_g.reshape(1, D_MODEL), ln_b.reshape(1, D_MODEL)]
    return pl.pallas_call(
        functools.partial(_out_kernel, ctx_row=None if ctx is None else b),
        grid=(b, n // tm),
        in_specs=in_specs,
        out_specs=out_specs,
        out_shape=out_shape,
        compiler_params=pltpu.CompilerParams(dimension_semantics=("parallel", "arbitrary")),
        name="output",
    )(*args)


def _out_piped_kernel(mod_ref, cw_ref, cbias_ref, w_ref, lg_ref, lb_ref, *refs, layer, tm, with_ctx):
    if with_ctx:
        xc_ref, pc_ref, yrc_ref, ync_ref, x_hbm, p_hbm, yr_hbm, yn_hbm, o_hbm, oc_ref = refs
    else:
        x_hbm, p_hbm, yr_hbm, yn_hbm, o_hbm = refs
    b, n, _ = x_hbm.shape
    nt = n // tm
    halo = 16
    per, nh = tm // halo, n // halo
    consts = (cw_ref[...], cbias_ref[...], w_ref[layer].astype(BF16), lg_ref[...], lb_ref[...])

    def body(idx, x_ref, u_ref, gz_ref, ub_ref, ua_ref, yr_ref, yn_ref, o_ref):
        bi, i = idx
        g = mod_ref[layer, pl.ds(bi, 1), 2 * D_MODEL:] * (1.0 / DEEPNORM_ALPHA)
        u_before = jnp.where(i == 0, 0.0, ub_ref[0, halo - 1:halo, :].astype(F32))
        u_after = jnp.where(i == nt - 1, 0.0, ua_ref[0, 0:1, :].astype(F32))
        _mix_out(x_ref, g, u_ref, gz_ref, u_before, u_after, yr_ref, yn_ref, *consts, o_ref)

    def rows(width, cb=0):
        return pl.BlockSpec((1, tm, width), lambda bi, i: (bi, i, cb), pipeline_mode=pl.Buffered(3))

    before = pl.BlockSpec((1, halo, CONV_W), lambda bi, i: (bi, jnp.maximum(i * per - 1, 0), CB_U))
    after = pl.BlockSpec((1, halo, CONV_W), lambda bi, i: (bi, jnp.minimum((i + 1) * per, nh - 1), CB_U))
    pltpu.emit_pipeline(
        body, grid=(b, nt),
        in_specs=[rows(D_MODEL), rows(CONV_W, CB_U), rows(CONV_W, CB_GZ), before, after, rows(RET_W), rows(NA_W)],
        out_specs=[pl.BlockSpec((1, tm, D_MODEL), lambda bi, i: (bi, i, 0))],
        _explicit_indices=True,
    )(x_hbm, p_hbm, p_hbm, p_hbm, p_hbm, yr_hbm, yn_hbm, o_hbm)
    if with_ctx:
        g = mod_ref[layer, b:b + 1, 2 * D_MODEL:] * (1.0 / DEEPNORM_ALPHA)
        for bi in range(b):
            one = pl.ds(bi, 1)
            _mix_out(xc_ref.at[one], g, pc_ref.at[one, :, CONV_OFF:CONV_OFF + CONV_W],
                     pc_ref.at[one, :, CONV_OFF + CONV_W:P_W], 0.0, 0.0, yrc_ref.at[one], ync_ref.at[one],
                     *consts, oc_ref.at[one])


def _output_piped(x, mod, p, y_ret, y_na, conv_w, conv_b, w_out, layer, ln_g, ln_b, ctx=None):
    b, n, _ = x.shape
    vmem = pl.BlockSpec(memory_space=pltpu.VMEM)
    hbm = pl.BlockSpec(memory_space=pl.ANY)
    with_ctx = ctx is not None
    out_specs, out_shape = hbm, jax.ShapeDtypeStruct((b, n, D_MODEL), F32)
    if with_ctx:
        out_specs, out_shape = [hbm, vmem], [out_shape, jax.ShapeDtypeStruct(ctx[0].shape, F32)]
    return pl.pallas_call(
        functools.partial(_out_piped_kernel, layer=layer, tm=min(ROW_TILE, n), with_ctx=with_ctx),
        in_specs=[vmem] * (6 + (4 if with_ctx else 0)) + [hbm] * 4,
        out_specs=out_specs,
        out_shape=out_shape,
        name="output_piped",
    )(mod, conv_w, conv_b.reshape(1, CONV_W), w_out, ln_g.reshape(1, D_MODEL), ln_b.reshape(1, D_MODEL),
      *(ctx if with_ctx else ()), x, p, y_ret, y_na)


def kernel(x, c, ctx, c_ctx, w_mod, b_mod, w_in, conv_w, conv_b, ret_decay, na_rpb, w_out, ln_g, ln_b):
    b, n, d = x.shape
    act_t = jnp.pad(jnp.concatenate([c, c_ctx[None]], axis=0).T, ((0, 0), (0, 5)))
    mod = _modulation(act_t, w_mod, b_mod)
    tables = _rope_tables(n)
    xc = ctx
    for l in range(DEPTH):
        need_ctx = l < DEPTH - 1
        p, pc = _projection(x, xc, mod, w_in, l, tables, PROJ_W if need_ctx else KV_W)
        y_ret = _retention(p, pc, ret_decay[l], init_state=True)
        y_na = _neighbourhood(p, pc, na_rpb[l])
        ctx_mix = (xc, pc) + tuple(_retention(pc, None, ret_decay[l], init_state=False, attend=True)) if need_ctx else None
        out = _output_piped(x, mod, p, y_ret, y_na, conv_w[l], conv_b[l], w_out, l, ln_g[l], ln_b[l], ctx_mix)
        x, xc = out if need_ctx else (out, None)
    return x
```
